```python
import math
import jax
import jax.numpy as jnp
from jax import lax
import numpy as np

D_MODEL = 1024
BATCH = 16
SEQ = 256
DEPTH = 4
DEC_BATCH = 8
DEC_SEQ = 2048
PAST_LEN = 256

GRID_W = 64
BLOCK = 128
N_BRANCH = 4
BRANCH = 256
A_HEADS = 4
A_QK = 32
A_V = 64
B_HEADS = 4
B_KV = 2
B_DIM = 64
WINDOW = 128
C_HEADS = 4
C_P = 64
C_GROUPS = 2
C_N = 64
C_CONV = 3
CONV_DIM = C_HEADS * C_P + 2 * C_GROUPS * C_N
D_HEADS = 4
D_K = 32
D_V = 64
CHUNK = 128
N_EXP = 16
N_EXP_GROUPS = 4
EXP_PER_GROUP = N_EXP // N_EXP_GROUPS
GROUP_SCORE_TOP = 2
TOP_K = 2
D_FF_EXP = 512
MOE_BLOCK = 128
ROPE_BASE = 10000.0
ALPHA = (2 * DEPTH) ** 0.25
BETA = (8 * DEPTH) ** -0.25
EPS = 1e-5
ADA_DIM = 6 * D_MODEL
IN_SIZES = (
    A_HEADS * 2 * A_QK, A_HEADS * 2 * A_QK, A_HEADS * A_V,
    B_HEADS * B_DIM, B_KV * B_DIM, B_KV * B_DIM,
    C_HEADS * C_P, C_HEADS * C_P, C_GROUPS * C_N, C_GROUPS * C_N, 2 * C_HEADS,
    D_HEADS * D_K, D_HEADS * D_K, D_HEADS * D_V, D_HEADS * D_V,
    N_BRANCH * D_MODEL,
)
IN_DIM = sum(IN_SIZES)

kernel_name = 'hybrid_diffusion_step'


def _split_points(sizes):
    pts, acc = [], 0
    for s in sizes[:-1]:
        acc += s
        pts.append(acc)
    return pts


def _normalize(x, rms):
    xf = x.astype(jnp.float32)
    if not rms:
        xf = xf - jnp.mean(xf, axis=-1, keepdims=True)
    return xf * lax.rsqrt(jnp.mean(xf * xf, axis=-1, keepdims=True) + EPS)


def layer_norm(x, g, b):
    return (_normalize(x, False) * g + b).astype(x.dtype)


def axial_rope(x):
    L, dim = x.shape[1], x.shape[-1]
    rows = L // GRID_W
    nf = dim // 4
    t = jnp.arange(rows * GRID_W)
    pos = jnp.stack([t // GRID_W, t % GRID_W], axis=-1).astype(jnp.float32)
    inv = ROPE_BASE ** (-jnp.arange(nf, dtype=jnp.float32) / nf)
    ang = pos[:, :, None] * inv
    shape = (1, L) + (1,) * (x.ndim - 3) + (2, nf)
    cos, sin = jnp.cos(ang).reshape(shape), jnp.sin(ang).reshape(shape)
    xr = x.astype(jnp.float32).reshape(x.shape[:-1] + (2, 2, nf))
    x1, x2 = xr[..., 0, :], xr[..., 1, :]
    out = jnp.stack([x1 * cos - x2 * sin, x1 * sin + x2 * cos], axis=-2)
    return out.reshape(x.shape).astype(x.dtype)


def _sweep(fn, q):
    Bsz, L = q.shape[:2]
    qb = jnp.moveaxis(q.reshape((Bsz, L // BLOCK, BLOCK) + q.shape[2:]), 1, 0)
    out = jnp.moveaxis(lax.map(fn, qb), 0, 1)
    return out.reshape((Bsz, L) + out.shape[3:])


def diff_attention(q, k, v, lam):
    scale = A_QK ** -0.5

    def blk(qb):
        s = jnp.einsum('bqhmd,bkhmd->bhmqk', qb, k).astype(jnp.float32) * scale
        p = jax.nn.softmax(s, axis=-1)
        w = p[:, :, 0] - lam * p[:, :, 1]
        return jnp.einsum('bhqk,bkhd->bqhd', w.astype(v.dtype), v)
    return _sweep(blk, q)


def sink_attention(q, k, v, sink):
    Bsz = q.shape[0]
    r = B_HEADS // B_KV
    scale = B_DIM ** -0.5
    sk = sink.astype(jnp.float32).reshape(1, B_KV, r, 1, 1)

    def blk(qb):
        qg = qb.reshape(Bsz, BLOCK, B_KV, r, B_DIM)
        s = jnp.einsum('bqgrd,bkgd->bgrqk', qg, k).astype(jnp.float32) * scale
        s = jnp.concatenate([s, jnp.broadcast_to(sk, s.shape[:-1] + (1,))], axis=-1)
        p = jax.nn.softmax(s, axis=-1)[..., :-1].astype(v.dtype)
        return jnp.einsum('bgrqk,bkgd->bqgrd', p, v).reshape(Bsz, BLOCK, B_HEADS, B_DIM)
    return _sweep(blk, q)


def window_attention(q, k, v, ck, cv, sink):
    Bsz, L = q.shape[:2]
    nb, r = L // BLOCK, B_HEADS // B_KV
    scale = B_DIM ** -0.5
    qb = q.reshape(Bsz, nb, BLOCK, B_KV, r, B_DIM)
    pad = ((0, 0), (BLOCK, BLOCK), (0, 0), (0, 0))

    def band(a):
        ap = jnp.pad(a, pad)
        return jnp.concatenate(
            [ap[:, i * BLOCK:i * BLOCK + L].reshape(Bsz, nb, BLOCK, B_KV, B_DIM) for i in range(3)], axis=2)
    kb, vb = band(k), band(v)
    qpos = jnp.arange(nb)[:, None] * BLOCK + jnp.arange(BLOCK)
    kpos = jnp.arange(nb)[:, None] * BLOCK - BLOCK + jnp.arange(3 * BLOCK)
    mask = ((jnp.abs(qpos[:, :, None] - kpos[:, None, :]) <= WINDOW)
            & (kpos >= 0)[:, None, :] & (kpos < L)[:, None, :])
    s_loc = jnp.einsum('bnqgrd,bnkgd->bngrqk', qb, kb).astype(jnp.float32) * scale
    s_loc = jnp.where(mask[None, :, None, None], s_loc, -jnp.inf)
    s_ctx = jnp.einsum('bnqgrd,bkgd->bngrqk', qb, ck).astype(jnp.float32) * scale
    s_snk = jnp.broadcast_to(sink.astype(jnp.float32).reshape(1, 1, B_KV, r, 1, 1), s_loc.shape[:-1] + (1,))
    p = jax.nn.softmax(jnp.concatenate([s_loc, s_ctx, s_snk], axis=-1), axis=-1).astype(v.dtype)
    nl, t = 3 * BLOCK, ck.shape[1]
    o = (jnp.einsum('bngrqk,bnkgd->bnqgrd', p[..., :nl], vb)
         + jnp.einsum('bngrqk,bkgd->bnqgrd', p[..., nl:nl + t], cv))
    return o.reshape(Bsz, L, B_HEADS, B_DIM)


def centred_dwconv(u, w, b):
    out = lax.conv_general_dilated(u, w[:, None, :].astype(u.dtype), window_strides=(1,), padding='SAME',
                                   dimension_numbers=('NWC', 'WIO', 'NWC'), feature_group_count=u.shape[-1])
    return out + b.astype(u.dtype)


def chunked_scan(x, a, Bm, Cm, h0):
    f32 = jnp.float32
    Bsz, L, H, P = x.shape
    N = Bm.shape[-1]
    nc = L // CHUNK
    xc = x.astype(f32).reshape(Bsz, nc, CHUNK, H, P)
    bc = Bm.astype(f32).reshape(Bsz, nc, CHUNK, H, N)
    cc = Cm.astype(f32).reshape(Bsz, nc, CHUNK, H, N)
    acs = jnp.cumsum(a.astype(f32).reshape(Bsz, nc, CHUNK, H), axis=2)
    lower = jnp.tril(jnp.ones((CHUNK, CHUNK), bool))[None, None, :, :, None]
    seg = acs[:, :, :, None, :] - acs[:, :, None, :, :]
    decay = jnp.exp(jnp.where(lower, seg, -jnp.inf))
    scores = jnp.einsum('bcihn,bcjhn->bcijh', cc, bc) * decay
    y_diag = jnp.einsum('bcijh,bcjhp->bcihp', scores, xc)
    last = acs[:, :, -1:, :]
    states = jnp.einsum('bcjhn,bcjhp->bchpn', bc * jnp.exp(last - acs)[..., None], xc)
    chunk_decay = jnp.exp(last[:, :, 0])

    def step(h, inp):
        s, dcy = inp
        return h * dcy[:, :, None, None] + s, h
    h_last, h_in = lax.scan(step, h0.astype(f32),
                            (jnp.moveaxis(states, 1, 0), jnp.moveaxis(chunk_decay, 1, 0)))
    h_in = jnp.moveaxis(h_in, 0, 1)
    y_off = jnp.einsum('bcihn,bchpn->bcihp', cc, h_in) * jnp.exp(acs)[..., None]
    y = (y_diag + y_off).reshape(Bsz, L, H, P)
    return y.astype(x.dtype), h_last


def directional_scan(x, a, Bm, Cm, h0, reverse):
    if reverse:
        y, h = chunked_scan(x[:, ::-1], a[:, ::-1], Bm[:, ::-1], Cm[:, ::-1], h0)
        return y[:, ::-1], h
    return chunked_scan(x, a, Bm, Cm, h0)


def route(x, router_w, router_b):
    t = x.shape[0]
    s = jax.nn.sigmoid(jnp.dot(x, router_w).astype(jnp.float32))
    sel = (s + router_b.astype(jnp.float32)).reshape(t, N_EXP_GROUPS, EXP_PER_GROUP)
    grp = jnp.argmax(jnp.sum(lax.top_k(sel, GROUP_SCORE_TOP)[0], axis=-1), axis=-1)
    _, loc = lax.top_k(sel[jnp.arange(t), grp], TOP_K)
    idx = grp[:, None] * EXP_PER_GROUP + loc
    w = jnp.take_along_axis(s, idx, axis=-1)
    return idx, w / jnp.sum(w, axis=-1, keepdims=True)


def moe(h, router_w, router_b, w_gate, w_up, w_down):
    shp = h.shape
    x = h.reshape(-1, shp[-1])
    t = x.shape[0]
    idx, gate = route(x, router_w, router_b)
    n_assign = t * TOP_K
    flat_e = idx.reshape(-1)
    order = jnp.argsort(flat_e)
    e_sorted = flat_e[order]
    counts = jnp.bincount(flat_e, length=N_EXP)
    padded = (counts + MOE_BLOCK - 1) // MOE_BLOCK * MOE_BLOCK
    pad_end = jnp.cumsum(padded)
    pad_start = pad_end - padded
    start = jnp.cumsum(counts) - counts
    dest = pad_start[e_sorted] + jnp.arange(n_assign) - start[e_sorted]
    n_blk = -(-n_assign // MOE_BLOCK) + N_EXP
    tok_sorted = (order // TOP_K).astype(jnp.int32)
    slot_tok = jnp.full((n_blk * MOE_BLOCK,), t, jnp.int32).at[dest].set(tok_sorted)
    blk_exp = jnp.minimum(jnp.searchsorted(pad_end, jnp.arange(n_blk) * MOE_BLOCK, side='right'), N_EXP - 1)
    xb = jnp.concatenate([x, jnp.zeros((1, x.shape[-1]), x.dtype)], axis=0)[slot_tok]
    xb = xb.reshape(n_blk, MOE_BLOCK, x.shape[-1])

    def expert_block(args):
        xe, e = args
        return jnp.dot(jax.nn.silu(jnp.dot(xe, w_gate[e])) * jnp.dot(xe, w_up[e]), w_down[e])
    yb = lax.map(expert_block, (xb, blk_exp)).reshape(n_blk * MOE_BLOCK, -1)
    contrib = yb[dest] * gate.reshape(-1)[order][:, None].astype(yb.dtype)
    return jnp.zeros_like(x).at[tok_sorted].add(contrib.astype(x.dtype)).reshape(shp)


def token_mixers(h, layer, lp, ctx, latent):
    f32 = jnp.float32
    Bsz, L, _ = h.shape
    proj = jnp.einsum('bld,de->ble', h, lp['w_in'])
    (aq, ak, av, bq, bk, bv, cx, cz, cb, cc, cdt, dq, dk, dv, dg, gl) = jnp.split(
        proj, _split_points(IN_SIZES), axis=-1)

    lam_init = 0.8 - 0.6 * math.exp(-0.3 * layer)
    lv = lp['diff_lambda'].astype(f32)
    lam = jnp.exp(jnp.sum(lv[0] * lv[1])) - jnp.exp(jnp.sum(lv[2] * lv[3])) + lam_init
    aq = aq.reshape(Bsz, L, A_HEADS, 2, A_QK)
    ak = ak.reshape(Bsz, L, A_HEADS, 2, A_QK)
    av = av.reshape(Bsz, L, A_HEADS, A_V)
    if latent:
        keys = jnp.concatenate([axial_rope(ak), ctx['diff_k'].astype(ak.dtype)], axis=1)
        vals = jnp.concatenate([av, ctx['diff_v'].astype(av.dtype)], axis=1)
        oa = diff_attention(axial_rope(aq), keys, vals, lam)
    else:
        oa = diff_attention(aq, ak, av, lam)
    oa = (_normalize(oa, True) * lp['diff_norm_g'] * (1.0 - lam_init)).reshape(Bsz, L, BRANCH)

    bq = bq.reshape(Bsz, L, B_HEADS, B_DIM)
    bk = bk.reshape(Bsz, L, B_KV, B_DIM)
    bv = bv.reshape(Bsz, L, B_KV, B_DIM)
    if latent:
        ob = window_attention(axial_rope(bq), axial_rope(bk), bv, ctx['win_k'].astype(bk.dtype),
                              ctx['win_v'].astype(bv.dtype), lp['win_sink'])
    else:
        ob = sink_attention(bq, bk, bv, lp['win_sink'])
    ob = ob.reshape(Bsz, L, BRANCH)

    xbc = jax.nn.silu(centred_dwconv(jnp.concatenate([cx, cb, cc], axis=-1), lp['conv_w'], lp['conv_b']))
    cx, cb, cc = jnp.split(xbc, [C_HEADS * C_P, C_HEADS * C_P + C_GROUPS * C_N], axis=-1)
    cx = cx.reshape(Bsz, L, C_HEADS, C_P)
    cb = jnp.repeat(cb.reshape(Bsz, L, C_GROUPS, C_N), C_HEADS // C_GROUPS, axis=2)
    cc = jnp.repeat(cc.reshape(Bsz, L, C_GROUPS, C_N), C_HEADS // C_GROUPS, axis=2)
    dt = jax.nn.softplus(cdt.reshape(Bsz, L, 2, C_HEADS).astype(f32) + lp['ssd_dt_bias'].astype(f32))
    a_ssd = -jnp.exp(lp['ssd_A_log'].astype(f32))
    h0 = ctx['ssd'] if latent else jnp.zeros((Bsz, 2, C_HEADS, C_P, C_N), f32)
    yf, sf = directional_scan(cx * dt[:, :, 0, :, None], dt[:, :, 0] * a_ssd[0], cb, cc, h0[:, 0], False)
    yb, sb = directional_scan(cx * dt[:, :, 1, :, None], dt[:, :, 1] * a_ssd[1], cb, cc, h0[:, 1], True)
    yc = (yf + yb + cx * lp['ssd_D'][:, None]).reshape(Bsz, L, BRANCH) * jax.nn.silu(cz)
    yc = _normalize(yc.reshape(Bsz, L, C_GROUPS, BRANCH // C_GROUPS), True).reshape(Bsz, L, BRANCH)
    yc = yc * lp['ssd_norm_g']

    dq = dq.reshape(Bsz, L, D_HEADS, D_K)
    dk = dk.reshape(Bsz, L, D_HEADS, D_K) * (D_K ** -0.5)
    dv = dv.reshape(Bsz, L, D_HEADS, D_V)
    log_g = jax.nn.log_sigmoid(lp['ret_decay_logit'].astype(f32))
    r0 = ctx['ret'] if latent else jnp.zeros((Bsz, 2, D_HEADS, D_V, D_K), f32)
    of, rf = directional_scan(dv, jnp.broadcast_to(log_g[0], (Bsz, L, D_HEADS)), dk, dq, r0[:, 0], False)
    orv, rb = directional_scan(dv, jnp.broadcast_to(log_g[1], (Bsz, L, D_HEADS)), dk, dq, r0[:, 1], True)
    od = _normalize(of + orv, False).reshape(Bsz, L, BRANCH) * lp['ret_norm_g'] * jax.nn.silu(dg)

    branches = jnp.stack([oa, ob, yc, od], axis=2).astype(h.dtype)
    up = jnp.einsum('blkc,kcd->blkd', branches, lp['w_branch'])
    merged = jnp.sum(jax.nn.sigmoid(gl.reshape(Bsz, L, N_BRANCH, D_MODEL)) * up, axis=2)
    out = jnp.einsum('bld,de->ble', merged, lp['w_out']).astype(h.dtype)
    new_ctx = dict(diff_k=ak, diff_v=av, win_k=bk, win_v=bv,
                   ssd=jnp.stack([sf, sb], axis=1), ret=jnp.stack([rf, rb], axis=1))
    return out, new_ctx


def trunk_layer(x, mod, layer, lp, router_w, router_b, ctx, latent):
    sh1, sc1, g1, sh2, sc2, g2 = jnp.split(mod[:, None, :].astype(x.dtype), 6, axis=-1)
    mix, new_ctx = token_mixers(x * (1 + sc1) + sh1, layer, lp, ctx, latent)
    x = layer_norm(ALPHA * x + g1 * mix, lp['ln_g'][0], lp['ln_b'][0])
    ffn = moe(x * (1 + sc2) + sh2, router_w, router_b, lp['moe_w_gate'], lp['moe_w_up'], lp['moe_w_down'])
    x = layer_norm(ALPHA * x + g2 * ffn, lp['ln_g'][1], lp['ln_b'][1])
    return x, new_ctx


def setup_inputs(seed: int = 0) -> dict:
    key = jax.random.key(seed)
    ks = jax.random.split(key, 40)
    f32 = jnp.float32

    def nrm(i, shape, scale=1.0):
        return jax.random.normal(ks[i], shape, f32) * scale

    gam = 1.0 - 2.0 ** (-5.0 - jnp.arange(D_HEADS, dtype=f32))
    dt0 = jnp.exp(jax.random.uniform(ks[19], (DEPTH, 2, C_HEADS), f32, math.log(1e-3), math.log(1e-1)))
    return {
        'x_prompt': nrm(0, (BATCH, SEQ, D_MODEL)),
        'x_sample': nrm(1, (DEC_BATCH, DEC_SEQ, D_MODEL)),
        'cache_diff_k': nrm(2, (DEC_BATCH, DEPTH, PAST_LEN, A_HEADS, 2, A_QK)),
        'cache_diff_v': nrm(3, (DEC_BATCH, DEPTH, PAST_LEN, A_HEADS, A_V)),
        'cache_win_k': nrm(4, (DEC_BATCH, DEPTH, PAST_LEN, B_KV, B_DIM)),
        'cache_win_v': nrm(5, (DEC_BATCH, DEPTH, PAST_LEN, B_KV, B_DIM)),
        'state_ssd': nrm(6, (DEC_BATCH, DEPTH, 2, C_HEADS, C_P, C_N)),
        'state_ret': nrm(7, (DEC_BATCH, DEPTH, 2, D_HEADS, D_V, D_K)),
        'c': nrm(8, (DEC_BATCH, D_MODEL)),
        'c_ctx': nrm(9, (D_MODEL,)),
        'w_ada': nrm(10, (DEPTH, D_MODEL, ADA_DIM), 0.5 * D_MODEL ** -0.5),
        'b_ada': nrm(11, (DEPTH, ADA_DIM), 0.02),
        'w_in': nrm(12, (DEPTH, D_MODEL, IN_DIM), D_MODEL ** -0.5),
        'diff_lambda': nrm(13, (DEPTH, 4, A_QK), 0.1),
        'diff_norm_g': 1.0 + nrm(14, (DEPTH, A_V), 0.02),
        'win_sink': nrm(15, (DEPTH, B_HEADS)),
        'conv_w': nrm(16, (DEPTH, C_CONV, CONV_DIM), C_CONV ** -0.5),
        'conv_b': nrm(17, (DEPTH, CONV_DIM), 0.02),
        'ssd_A_log': jnp.log(jax.random.uniform(ks[18], (DEPTH, 2, C_HEADS), f32, 1.0, 16.0)),
        'ssd_dt_bias': dt0 + jnp.log(-jnp.expm1(-dt0)),
        'ssd_D': 1.0 + nrm(20, (DEPTH, C_HEADS), 0.02),
        'ssd_norm_g': 1.0 + nrm(21, (DEPTH, C_HEADS * C_P), 0.02),
        'ret_decay_logit': (jnp.broadcast_to(jnp.log(gam) - jnp.log1p(-gam), (DEPTH, 2, D_HEADS))
                            + nrm(22, (DEPTH, 2, D_HEADS), 0.05)),
        'ret_norm_g': 1.0 + nrm(23, (DEPTH, D_HEADS * D_V), 0.02),
        'w_branch': nrm(24, (DEPTH, N_BRANCH, BRANCH, D_MODEL), BRANCH ** -0.5),
        'w_out': nrm(25, (DEPTH, D_MODEL, D_MODEL), BETA * D_MODEL ** -0.5),
        'ln_g': 1.0 + nrm(26, (DEPTH, 2, D_MODEL), 0.02),
        'ln_b': nrm(27, (DEPTH, 2, D_MODEL), 0.02),
        'router_w': nrm(28, (D_MODEL, N_EXP), D_MODEL ** -0.5),
        'router_b': nrm(29, (N_EXP,), 0.01),
        'moe_w_gate': nrm(30, (DEPTH, N_EXP, D_MODEL, D_FF_EXP), D_MODEL ** -0.5),
        'moe_w_up': nrm(31, (DEPTH, N_EXP, D_MODEL, D_FF_EXP), D_MODEL ** -0.5),
        'moe_w_down': nrm(32, (DEPTH, N_EXP, D_FF_EXP, D_MODEL), BETA * D_FF_EXP ** -0.5),
    }


def reference(x_prompt, x_sample, cache_diff_k, cache_diff_v, cache_win_k, cache_win_v, state_ssd, state_ret,
              c, c_ctx, w_ada, b_ada, w_in, diff_lambda, diff_norm_g, win_sink, conv_w, conv_b,
              ssd_A_log, ssd_dt_bias, ssd_D, ssd_norm_g, ret_decay_logit, ret_norm_g, w_branch, w_out,
              ln_g, ln_b, router_w, router_b, moe_w_gate, moe_w_up, moe_w_down):
    def layer_params(l):
        return dict(w_in=w_in[l], diff_lambda=diff_lambda[l], diff_norm_g=diff_norm_g[l], win_sink=win_sink[l],
                    conv_w=conv_w[l], conv_b=conv_b[l], ssd_A_log=ssd_A_log[l], ssd_dt_bias=ssd_dt_bias[l],
                    ssd_D=ssd_D[l], ssd_norm_g=ssd_norm_g[l], ret_decay_logit=ret_decay_logit[l],
                    ret_norm_g=ret_norm_g[l], w_branch=w_branch[l], w_out=w_out[l], ln_g=ln_g[l], ln_b=ln_b[l],
                    moe_w_gate=moe_w_gate[l], moe_w_up=moe_w_up[l], moe_w_down=moe_w_down[l])

    mod_ctx = jnp.einsum('d,lde->le', jax.nn.silu(c_ctx), w_ada) + b_ada
    mod_lat = jnp.einsum('bd,lde->lbe', jax.nn.silu(c), w_ada) + b_ada[:, None]

    y_prompt = x_prompt
    ctx_out = []
    for l in range(DEPTH):
        y_prompt, nc = trunk_layer(y_prompt, mod_ctx[l][None], l, layer_params(l), router_w, router_b,
                                   None, False)
        ctx_out.append(nc)
    new_diff_k = jnp.stack([nc['diff_k'] for nc in ctx_out], axis=1)
    new_diff_v = jnp.stack([nc['diff_v'] for nc in ctx_out], axis=1)
    new_win_k = jnp.stack([nc['win_k'] for nc in ctx_out], axis=1)
    new_win_v = jnp.stack([nc['win_v'] for nc in ctx_out], axis=1)
    new_ssd = jnp.stack([nc['ssd'] for nc in ctx_out], axis=1)
    new_ret = jnp.stack([nc['ret'] for nc in ctx_out], axis=1)

    y_sample = x_sample
    for l in range(DEPTH):
        ctx = dict(diff_k=cache_diff_k[:, l], diff_v=cache_diff_v[:, l], win_k=cache_win_k[:, l],
                   win_v=cache_win_v[:, l], ssd=state_ssd[:, l], ret=state_ret[:, l])
        y_sample, _ = trunk_layer(y_sample, mod_lat[l], l, layer_params(l), router_w, router_b, ctx, True)

    return (y_prompt, y_sample, new_diff_k, new_diff_v, new_win_k, new_win_v, new_ssd, new_ret)
```

```python
import functools
import math

import jax
import jax.numpy as jnp
from jax import lax
from jax.experimental import pallas as pl
from jax.experimental.pallas import tpu as pltpu

F32 = jnp.float32
BF16 = jnp.bfloat16

D_MODEL = 1024
DEPTH = 4
GRID_W = 64
BLOCK = 128
WINDOW = 128
CHUNK = 128
A_HEADS, A_QK, A_V = 4, 32, 64
B_HEADS, B_KV, B_DIM = 4, 2, 64
C_HEADS, C_P, C_GROUPS, C_N = 4, 64, 2, 64
D_HEADS, D_K, D_V = 4, 32, 64
BRANCH = 256
N_BRANCH = 4
N_EXP = 16
N_EXP_GROUPS = 4
EXP_PER_GROUP = 4
D_FF_EXP = 512
ROPE_BASE = 10000.0
ALPHA = (2 * DEPTH) ** 0.25
EPS = 1e-5
ADA_DIM = 6 * D_MODEL
NEG = -1e30

N_SMALL = 23 * 128
GATE_OFF = 2824
CDT_OFF = 2048

VMEM_LIMIT = 52 * 1024 * 1024
MOE_ROWS = 256
TOK_TILE = 256

_NN = (((1,), (0,)), ((), ()))
_NT = (((1,), (1,)), ((), ()))
_TN = (((0,), (0,)), ((), ()))


def _params(*sem):
    return pltpu.CompilerParams(dimension_semantics=sem, vmem_limit_bytes=VMEM_LIMIT)


def _mm(a, b, dims=_NN):
    return lax.dot_general(a, b, dims, preferred_element_type=F32)


def _split(a):
    hi = a.astype(BF16)
    return hi, (a - hi.astype(F32)).astype(BF16)


def _mm_f32(a, b, dims=_NN):
    a_hi, a_lo = _split(a)
    b_hi, b_lo = _split(b)
    return (_mm(a_lo, b_hi, dims) + _mm(a_hi, b_lo, dims)) + _mm(a_hi, b_hi, dims)


def _mm_exact_lhs(m_bf, a):
    a1 = a.astype(BF16)
    r1 = a - a1.astype(F32)
    a2 = r1.astype(BF16)
    a3 = (r1 - a2.astype(F32)).astype(BF16)
    return (_mm(m_bf, a3) + _mm(m_bf, a2)) + _mm(m_bf, a1)


def _silu(x):
    return x * jax.nn.sigmoid(x)


def _rope(x, c, s_lo, s_hi, shift):
    n = x.shape[1]
    return x * c + pltpu.roll(x, n - shift, 1) * s_lo + pltpu.roll(x, shift, 1) * s_hi


def _ada_kernel(c_ref, w_ref, b_ref, o_ref):
    c = c_ref[...]
    o_ref[...] = _mm_f32(_silu(c), w_ref[...]) + b_ref[...]


def _ada(cvec, w_ada, b_ada):
    rows = cvec.shape[0]
    tn = 1024
    return pl.pallas_call(
        _ada_kernel,
        out_shape=jax.ShapeDtypeStruct((DEPTH, rows, ADA_DIM), F32),
        grid=(DEPTH, ADA_DIM // tn),
        in_specs=[
            pl.BlockSpec((rows, D_MODEL), lambda l, j: (0, 0)),
            pl.BlockSpec((None, D_MODEL, tn), lambda l, j: (l, 0, j)),
            pl.BlockSpec((None, 1, tn), lambda l, j: (l, 0, j)),
        ],
        out_specs=pl.BlockSpec((None, rows, tn), lambda l, j: (l, 0, j)),
        compiler_params=_params("arbitrary", "arbitrary"),
        name="ada",
    )(cvec, w_ada, b_ada.reshape(DEPTH, 1, ADA_DIM))


def _inproj_kernel(x_ref, mod_ref, w_ref, o_ref):
    d = x_ref.shape[1]
    h = x_ref[...] * (1.0 + mod_ref[:, d:2 * d]) + mod_ref[:, 0:d]
    o_ref[...] = _mm(h.astype(BF16), w_ref[...])


def _mod_group(i, tm, n_ctx, dec_seq):
    row = i * tm
    return jnp.where(row < n_ctx, 0, 1 + lax.div(jnp.maximum(row - n_ctx, 0), dec_seq))


def _inproj(x, mod_l, w_small, n_ctx, dec_seq):
    n_tok = x.shape[0]
    tm = TOK_TILE
    grp = functools.partial(_mod_group, tm=tm, n_ctx=n_ctx, dec_seq=dec_seq)
    return pl.pallas_call(
        _inproj_kernel,
        out_shape=jax.ShapeDtypeStruct((n_tok, N_SMALL), F32),
        grid=(n_tok // tm,),
        in_specs=[
            pl.BlockSpec((tm, D_MODEL), lambda i: (i, 0)),
            pl.BlockSpec((None, 1, ADA_DIM), lambda i: (grp(i), 0, 0)),
            pl.BlockSpec((D_MODEL, N_SMALL), lambda i: (0, 0)),
        ],
        out_specs=pl.BlockSpec((tm, N_SMALL), lambda i: (i, 0)),
        compiler_params=_params("arbitrary"),
        name="inproj",
    )(x, mod_l, w_small)


def _diff_attn_kernel(*refs, latent, tq, seq, past):
    if latent:
        (sc_ref, q_ref, k_ref, v_ref, g_ref, ck_ref, cv_ref, rc_ref, rlo_ref, rhi_ref,
         o_ref, kt_scr, v_scr) = refs
    else:
        sc_ref, q_ref, k_ref, v_ref, g_ref, o_ref, kt_scr, v_scr = refs
    qi = pl.program_id(1)
    shift = A_QK // 4

    @pl.when(qi == 0)
    def _():
        k = k_ref[...]
        if latent:
            k = _rope(k, rc_ref[...], rlo_ref[...], rhi_ref[...], shift)
        kt_scr[:, 0:seq] = k.T.astype(BF16)
        v_scr[0:seq, :] = v_ref[...].astype(BF16)
        if latent:
            kt_scr[:, seq:seq + past] = ck_ref[...].T.astype(BF16)
            v_scr[seq:seq + past, :] = cv_ref[...].astype(BF16)

    q = q_ref[...]
    if latent:
        r = pl.ds(pl.multiple_of(qi * tq, tq), tq)
        q = _rope(q, rc_ref[r, :], rlo_ref[r, :], rhi_ref[r, :], shift)
    qb = q.astype(BF16)
    lam = sc_ref[0]
    post = sc_ref[1]
    scale = A_QK ** -0.5
    outs = []
    for h in range(A_HEADS):
        probs = []
        for m in range(2):
            off = (h * 2 + m) * A_QK
            s = _mm(qb[:, off:off + A_QK], kt_scr[off:off + A_QK, :]) * scale
            p = jnp.exp(s - jnp.max(s, axis=1, keepdims=True))
            probs.append(p * (1.0 / jnp.sum(p, axis=1, keepdims=True)))
        w = probs[0] - lam * probs[1]
        o = _mm(w.astype(BF16), v_scr[:, h * A_V:(h + 1) * A_V])
        n = o * lax.rsqrt(jnp.mean(o * o, axis=1, keepdims=True) + EPS)
        outs.append((n * g_ref[...]) * post)
    o_ref[...] = jnp.concatenate(outs, axis=1).astype(BF16)


def _diff_attn(proj, scal, g, row0, nb, seq, tq, cache=None, rope=None):
    latent = cache is not None
    nq = seq // tq
    rb = row0 // seq
    qb0 = row0 // tq
    past = cache[0].shape[1] if latent else 0
    in_specs = [
        pl.BlockSpec(memory_space=pltpu.SMEM),
        pl.BlockSpec((tq, 256), lambda b, i: (qb0 + b * nq + i, 0)),
        pl.BlockSpec((seq, 256), lambda b, i: (rb + b, 1)),
        pl.BlockSpec((seq, 256), lambda b, i: (rb + b, 2)),
        pl.BlockSpec((1, A_V), lambda b, i: (0, 0)),
    ]
    args = [scal, proj, proj, proj, g]
    if latent:
        in_specs += [
            pl.BlockSpec((None, past, 256), lambda b, i: (b, 0, 0)),
            pl.BlockSpec((None, past, 256), lambda b, i: (b, 0, 0)),
        ] + [pl.BlockSpec((seq, 256), lambda b, i: (0, 0))] * 3
        args += [cache[0], cache[1], *rope]
    return pl.pallas_call(
        functools.partial(_diff_attn_kernel, latent=latent, tq=tq, seq=seq, past=past),
        out_shape=jax.ShapeDtypeStruct((nb * seq, BRANCH), BF16),
        grid=(nb, nq),
        in_specs=in_specs,
        out_specs=pl.BlockSpec((tq, BRANCH), lambda b, i: (b * nq + i, 0)),
        scratch_shapes=[pltpu.VMEM((256, seq + past), BF16), pltpu.VMEM((seq + past, 256), BF16)],
        compiler_params=_params("arbitrary", "arbitrary"),
        name="diff_attn_lat" if latent else "diff_attn_ctx",
    )(*args)


def _win_attn_kernel(*refs, latent, tq, seq):
    if latent:
        (sink_ref, q_ref, k_ref, v_ref, ck_ref, cv_ref, rc_ref, rlo_ref, rhi_ref,
         o_ref, k_scr, v_scr, ck_scr, cv_scr) = refs
    else:
        sink_ref, q_ref, k_ref, v_ref, o_ref, k_scr, v_scr = refs
    qi = pl.program_id(1)
    nq = seq // tq
    shift = B_DIM // 4
    kvw = B_KV * B_DIM

    @pl.when(qi == 0)
    def _():
        k = k_ref[...]
        if latent:
            k = _rope(k, rc_ref[:, 0:kvw], rlo_ref[:, 0:kvw], rhi_ref[:, 0:kvw], shift)
            ck_scr[...] = ck_ref[...].astype(BF16)
            cv_scr[...] = cv_ref[...].astype(BF16)
        k_scr[...] = k.astype(BF16)
        v_scr[...] = v_ref[...].astype(BF16)

    q = q_ref[...]
    if latent:
        r = pl.ds(pl.multiple_of(qi * tq, tq), tq)
        q = _rope(q, rc_ref[r, :], rlo_ref[r, :], rhi_ref[r, :], shift)
        rows = [pl.ds(pl.multiple_of(j * tq, tq), tq)
                for j in (jnp.maximum(qi - 1, 0), qi, jnp.minimum(qi + 1, nq - 1))]
        kl = jnp.concatenate([k_scr[rr, :] for rr in rows], axis=0)
        vl = jnp.concatenate([v_scr[rr, :] for rr in rows], axis=0)
        ii = lax.broadcasted_iota(jnp.int32, (tq, 3 * tq), 0)
        jj = lax.broadcasted_iota(jnp.int32, (tq, 3 * tq), 1)
        lo = jnp.where(qi > 0, 0, tq)
        hi = jnp.where(qi < nq - 1, 3 * tq, 2 * tq)
        valid = (jnp.abs(jj - tq - ii) <= WINDOW) & (jj >= lo) & (jj < hi)
    else:
        kl = k_scr[...]
        vl = v_scr[...]
    qb = q.astype(BF16)
    scale = B_DIM ** -0.5
    ratio = B_HEADS // B_KV
    outs = []
    for h in range(B_HEADS):
        gsl = slice((h // ratio) * B_DIM, (h // ratio + 1) * B_DIM)
        qh = qb[:, h * B_DIM:(h + 1) * B_DIM]
        snk = sink_ref[h]
        s = _mm(qh, kl[:, gsl], _NT) * scale
        if latent:
            s = jnp.where(valid, s, NEG)
            sc = _mm(qh, ck_scr[:, gsl], _NT) * scale
            m = jnp.maximum(jnp.maximum(jnp.max(s, axis=1, keepdims=True),
                                        jnp.max(sc, axis=1, keepdims=True)), snk)
            pc = jnp.exp(sc - m)
        else:
            m = jnp.maximum(jnp.max(s, axis=1, keepdims=True), snk)
        p = jnp.exp(s - m)
        den = jnp.sum(p, axis=1, keepdims=True) + jnp.exp(snk - m)
        if latent:
            den = den + jnp.sum(pc, axis=1, keepdims=True)
        inv = 1.0 / den
        o = _mm((p * inv).astype(BF16), vl[:, gsl])
        if latent:
            o = o + _mm((pc * inv).astype(BF16), cv_scr[:, gsl])
        outs.append(o)
    o_ref[...] = jnp.concatenate(outs, axis=1).astype(BF16)


def _win_attn(proj, sink, row0, nb, seq, tq, cache=None, rope=None):
    latent = cache is not None
    nq = seq // tq
    rb = row0 // seq
    qb0 = row0 // tq
    kvw = B_KV * B_DIM
    in_specs = [
        pl.BlockSpec(memory_space=pltpu.SMEM),
        pl.BlockSpec((tq, 256), lambda b, i: (qb0 + b * nq + i, 3)),
        pl.BlockSpec((seq, kvw), lambda b, i: (rb + b, 8)),
        pl.BlockSpec((seq, kvw), lambda b, i: (rb + b, 9)),
    ]
    args = [sink, proj, proj, proj]
    scratch = [pltpu.VMEM((seq, kvw), BF16), pltpu.VMEM((seq, kvw), BF16)]
    if latent:
        past = cache[0].shape[1]
        in_specs += [
            pl.BlockSpec((None, past, kvw), lambda b, i: (b, 0, 0)),
            pl.BlockSpec((None, past, kvw), lambda b, i: (b, 0, 0)),
        ] + [pl.BlockSpec((seq, 256), lambda b, i: (0, 0))] * 3
        args += [cache[0], cache[1], *rope]
        scratch += [pltpu.VMEM((past, kvw), BF16), pltpu.VMEM((past, kvw), BF16)]
    return pl.pallas_call(
        functools.partial(_win_attn_kernel, latent=latent, tq=tq, seq=seq),
        out_shape=jax.ShapeDtypeStruct((nb * seq, BRANCH), BF16),
        grid=(nb, nq),
        in_specs=in_specs,
        out_specs=pl.BlockSpec((tq, BRANCH), lambda b, i: (b * nq + i, 0)),
        scratch_shapes=scratch,
        compiler_params=_params("arbitrary", "arbitrary"),
        name="win_attn_lat" if latent else "win_attn_ctx",
    )(*args)


def _conv_silu(u, w, b):
    n = u.shape[0]
    rows = lax.broadcasted_iota(jnp.int32, u.shape, 0)
    up = jnp.where(rows == 0, 0.0, pltpu.roll(u, 1, 0))
    un = jnp.where(rows == n - 1, 0.0, pltpu.roll(u, n - 1, 0))
    return _silu(up * w[0:1, :] + u * w[1:2, :] + un * w[2:3, :] + b)


def _ssd_kernel(*refs, seq, has_h0):
    if has_h0:
        (cx_ref, cz_ref, cbc_ref, cdt_ref, cw_ref, cb_ref, dtb_ref, ac_ref, dv_ref, g_ref, h0_ref,
         y_ref, st_ref, xs, bcs, dts, ybuf, hs) = refs
    else:
        (cx_ref, cz_ref, cbc_ref, cdt_ref, cw_ref, cb_ref, dtb_ref, ac_ref, dv_ref, g_ref,
         y_ref, st_ref, xs, bcs, dts, ybuf, hs) = refs
    nc = seq // CHUNK
    xw = C_HEADS * C_P
    xs[...] = _conv_silu(cx_ref[...], cw_ref[:, 0:xw], cb_ref[:, 0:xw])
    bcs[...] = _conv_silu(cbc_ref[...], cw_ref[:, xw:2 * xw], cb_ref[:, xw:2 * xw])
    z = cdt_ref[...] + dtb_ref[...]
    dts[...] = jnp.maximum(z, 0.0) + jnp.log1p(jnp.exp(-jnp.abs(z)))
    if has_h0:
        hs[...] = h0_ref[...]
    else:
        hs[...] = jnp.zeros(hs.shape, F32)

    ri = lax.broadcasted_iota(jnp.int32, (CHUNK, CHUNK), 0)
    ci = lax.broadcasted_iota(jnp.int32, (CHUNK, CHUNK), 1)
    gw = C_GROUPS * C_N

    def chunk(c, d):
        r = pl.ds(pl.multiple_of(c * CHUNK, CHUNK), CHUNK)
        tri = (ri >= ci) if d == 0 else (ci >= ri)
        dt = dts[r, :]
        cs = _mm_exact_lhs(tri.astype(BF16), dt * ac_ref[...])
        cst = cs.T
        x = xs[r, :]
        bc = bcs[r, :]
        ys = []
        for g in range(C_GROUPS):
            bm = bc[:, g * C_N:(g + 1) * C_N]
            cb = bc[:, gw + g * C_N:gw + (g + 1) * C_N].astype(BF16)
            gram = _mm(cb, bm.astype(BF16), _NT)
            for hh in range(C_HEADS // C_GROUPS):
                h = g * (C_HEADS // C_GROUPS) + hh
                col = d * C_HEADS + h
                xb = (x[:, h * C_P:(h + 1) * C_P] * dt[:, col:col + 1]).astype(BF16)
                cc = cs[:, col:col + 1]
                dec = jnp.exp(jnp.where(tri, cc - cst[col:col + 1, :], NEG))
                tot = cc[CHUNK - 1:CHUNK, :] if d == 0 else cc[0:1, :]
                hin = hs[col]
                y = _mm((gram * dec).astype(BF16), xb)
                y = y + _mm(cb, hin.astype(BF16), _NT) * jnp.exp(cc)
                bd = (bm * jnp.exp(tot - cc)).astype(BF16)
                hs[col] = hin * jnp.exp(tot) + _mm(xb, bd, _TN)
                ys.append(y)
        return r, jnp.concatenate(ys, axis=1)

    def fwd(c, carry):
        r, y = chunk(c, 0)
        ybuf[r, :] = y
        return carry

    lax.fori_loop(0, nc, fwd, 0)

    def bwd(i, carry):
        r, y = chunk(nc - 1 - i, 1)
        y = (ybuf[r, :] + y) + xs[r, :] * dv_ref[...]
        y = y * _silu(cz_ref[r, :])
        gl = xw // C_GROUPS
        parts = []
        for g in range(C_GROUPS):
            seg = y[:, g * gl:(g + 1) * gl]
            parts.append(seg * lax.rsqrt(jnp.mean(seg * seg, axis=1, keepdims=True) + EPS))
        y_ref[r, :] = (jnp.concatenate(parts, axis=1) * g_ref[...]).astype(BF16)
        return carry

    lax.fori_loop(0, nc, bwd, 0)
    st_ref[...] = hs[...]


def _ssd(proj, conv_w, conv_b, dtb, acoef, dvec, g, row0, nb, seq, h0=None):
    has_h0 = h0 is not None
    rb = row0 // seq
    nst = 2 * C_HEADS
    in_specs = [
        pl.BlockSpec((seq, 256), lambda b: (rb + b, 5)),
        pl.BlockSpec((seq, 256), lambda b: (rb + b, 6)),
        pl.BlockSpec((seq, 256), lambda b: (rb + b, 7)),
        pl.BlockSpec((seq, 128), lambda b: (rb + b, 22)),
        pl.BlockSpec((3, 512), lambda b: (0, 0)),
        pl.BlockSpec((1, 512), lambda b: (0, 0)),
        pl.BlockSpec((1, 128), lambda b: (0, 0)),
        pl.BlockSpec((1, 128), lambda b: (0, 0)),
        pl.BlockSpec((1, 256), lambda b: (0, 0)),
        pl.BlockSpec((1, 256), lambda b: (0, 0)),
    ]
    args = [proj, proj, proj, proj, conv_w, conv_b, dtb, acoef, dvec, g]
    if has_h0:
        in_specs.append(pl.BlockSpec((None, nst, C_P, C_N), lambda b: (b, 0, 0, 0)))
        args.append(h0)
    return pl.pallas_call(
        functools.partial(_ssd_kernel, seq=seq, has_h0=has_h0),
        out_shape=(jax.ShapeDtypeStruct((nb * seq, BRANCH), BF16),
                   jax.ShapeDtypeStruct((nb, nst, C_P, C_N), F32)),
        grid=(nb,),
        in_specs=in_specs,
        out_specs=(pl.BlockSpec((seq, BRANCH), lambda b: (b, 0)),
                   pl.BlockSpec((None, nst, C_P, C_N), lambda b: (b, 0, 0, 0))),
        scratch_shapes=[pltpu.VMEM((seq, 256), F32), pltpu.VMEM((seq, 256), F32),
                        pltpu.VMEM((seq, 128), F32), pltpu.VMEM((seq, 256), F32),
                        pltpu.VMEM((nst, C_P, C_N), F32)],
        compiler_params=_params("arbitrary"),
        name="ssd_lat" if has_h0 else "ssd_ctx",
    )(*args)


def _ret_kernel(*refs, seq, has_h0):
    if has_h0:
        lg_ref, q_ref, k_ref, v_ref, gt_ref, g_ref, h0_ref, y_ref, st_ref, ybuf, hs = refs
    else:
        lg_ref, q_ref, k_ref, v_ref, gt_ref, g_ref, y_ref, st_ref, ybuf, hs = refs
    nc = seq // CHUNK
    if has_h0:
        hs[...] = h0_ref[...]
    else:
        hs[...] = jnp.zeros(hs.shape, F32)
    ri = lax.broadcasted_iota(jnp.int32, (CHUNK, CHUNK), 0)
    ci = lax.broadcasted_iota(jnp.int32, (CHUNK, CHUNK), 1)
    pos = lax.broadcasted_iota(jnp.int32, (CHUNK, 1), 0).astype(F32)
    kscale = D_K ** -0.5

    def chunk(c, d):
        r = pl.ds(pl.multiple_of(c * CHUNK, CHUNK), CHUNK)
        q = q_ref[r, :]
        k = k_ref[r, :] * kscale
        v = v_ref[r, :]
        if d == 0:
            tri, dist = ri >= ci, (ri - ci).astype(F32)
            steps_in, steps_out = pos + 1.0, (CHUNK - 1.0) - pos
        else:
            tri, dist = ci >= ri, (ci - ri).astype(F32)
            steps_in, steps_out = CHUNK - pos, pos
        ys = []
        for h in range(D_HEADS):
            col = d * D_HEADS + h
            lg = lg_ref[col]
            dec = jnp.exp(jnp.where(tri, dist * lg, NEG))
            e_in = jnp.exp(steps_in * lg)
            e_tot = e_in[CHUNK - 1:CHUNK, :] if d == 0 else e_in[0:1, :]
            qb = q[:, h * D_K:(h + 1) * D_K].astype(BF16)
            km = k[:, h * D_K:(h + 1) * D_K]
            vb = v[:, h * D_V:(h + 1) * D_V].astype(BF16)
            hin = hs[col]
            y = _mm((_mm(qb, km.astype(BF16), _NT) * dec).astype(BF16), vb)
            y = y + _mm(qb, hin.astype(BF16), _NT) * e_in
            bd = (km * jnp.exp(steps_out * lg)).astype(BF16)
            hs[col] = hin * e_tot + _mm(vb, bd, _TN)
            ys.append(y)
        return r, ys

    def fwd(c, carry):
        r, ys = chunk(c, 0)
        ybuf[r, :] = jnp.concatenate(ys, axis=1)
        return carry

    lax.fori_loop(0, nc, fwd, 0)

    def bwd(i, carry):
        r, ys = chunk(nc - 1 - i, 1)
        yf = ybuf[r, :]
        parts = []
        for h in range(D_HEADS):
            o = yf[:, h * D_V:(h + 1) * D_V] + ys[h]
            o = o - jnp.mean(o, axis=1, keepdims=True)
            parts.append(o * lax.rsqrt(jnp.mean(o * o, axis=1, keepdims=True) + EPS))
        y = (jnp.concatenate(parts, axis=1) * g_ref[...]) * _silu(gt_ref[r, :])
        y_ref[r, :] = y.astype(BF16)
        return carry

    lax.fori_loop(0, nc, bwd, 0)
    st_ref[...] = hs[...]


def _ret(proj, log_g, g, row0, nb, seq, h0=None):
    has_h0 = h0 is not None
    rb = row0 // seq
    nst = 2 * D_HEADS
    in_specs = [
        pl.BlockSpec(memory_space=pltpu.SMEM),
        pl.BlockSpec((seq, 128), lambda b: (rb + b, 16)),
        pl.BlockSpec((seq, 128), lambda b: (rb + b, 17)),
        pl.BlockSpec((seq, 256), lambda b: (rb + b, 9)),
        pl.BlockSpec((seq, 256), lambda b: (rb + b, 10)),
        pl.BlockSpec((1, 256), lambda b: (0, 0)),
    ]
    args = [log_g, proj, proj, proj, proj, g]
    if has_h0:
        in_specs.append(pl.BlockSpec((None, nst, D_V, D_K), lambda b: (b, 0, 0, 0)))
        args.append(h0)
    return pl.pallas_call(
        functools.partial(_ret_kernel, seq=seq, has_h0=has_h0),
        out_shape=(jax.ShapeDtypeStruct((nb * seq, BRANCH), BF16),
                   jax.ShapeDtypeStruct((nb, nst, D_V, D_K), F32)),
        grid=(nb,),
        in_specs=in_specs,
        out_specs=(pl.BlockSpec((seq, BRANCH), lambda b: (b, 0)),
                   pl.BlockSpec((None, nst, D_V, D_K), lambda b: (b, 0, 0, 0))),
        scratch_shapes=[pltpu.VMEM((seq, 256), F32), pltpu.VMEM((nst, D_V, D_K), F32)],
        compiler_params=_params("arbitrary"),
        name="ret_lat" if has_h0 else "ret_ctx",
    )(*args)


def _route(sel, s):
    row = lambda a, e: a[e:e + 1, :]
    best = None
    grp = None
    for g in range(N_EXP_GROUPS):
        vals = [row(sel, g * EXP_PER_GROUP + j) for j in range(EXP_PER_GROUP)]
        score = None
        for a in range(EXP_PER_GROUP):
            for b in range(a + 1, EXP_PER_GROUP):
                pair = vals[a] + vals[b]
                score = pair if score is None else jnp.maximum(score, pair)
        if best is None:
            best, grp = score, jnp.zeros(score.shape, jnp.int32)
        else:
            better = score > best
            best = jnp.where(better, score, best)
            grp = jnp.where(better, g, grp)

    def pick(a, j):
        out = row(a, j)
        for g in range(1, N_EXP_GROUPS):
            out = jnp.where(grp == g, row(a, g * EXP_PER_GROUP + j), out)
        return out

    cand = [pick(sel, j) for j in range(EXP_PER_GROUP)]
    aff = [pick(s, j) for j in range(EXP_PER_GROUP)]

    def arg_first_max(vals):
        top, idx = vals[0], jnp.zeros(vals[0].shape, jnp.int32)
        for j in range(1, len(vals)):
            better = vals[j] > top
            top = jnp.where(better, vals[j], top)
            idx = jnp.where(better, j, idx)
        return idx

    def take(vals, idx):
        out = vals[0]
        for j in range(1, len(vals)):
            out = jnp.where(idx == j, vals[j], out)
        return out

    i1 = arg_first_max(cand)
    i2 = arg_first_max([jnp.where(i1 == j, -jnp.inf, cand[j]) for j in range(EXP_PER_GROUP)])
    w1, w2 = take(aff, i1), take(aff, i2)
    tot = w1 + w2
    ids = jnp.concatenate([grp * EXP_PER_GROUP + i1, grp * EXP_PER_GROUP + i2], axis=0)
    gates = jnp.concatenate([w1 / tot, w2 / tot], axis=0)
    return ids, gates


def _merge_kernel(x_ref, mod_ref, br_ref, wgl_ref, wbr_ref, wout_ref, lng_ref, lnb_ref,
                  rw_ref, rb_ref, x1_ref, h2_ref, ids_ref, gates_ref):
    d = D_MODEL
    x = x_ref[...]
    hb = (x * (1.0 + mod_ref[:, d:2 * d]) + mod_ref[:, 0:d]).astype(BF16)
    br = br_ref[...]
    merged = None
    for k in range(N_BRANCH):
        gate = jax.nn.sigmoid(_mm(hb, wgl_ref[:, k * d:(k + 1) * d]))
        up = _mm(br[:, k * BRANCH:(k + 1) * BRANCH], wbr_ref[k * BRANCH:(k + 1) * BRANCH, :])
        merged = gate * up if merged is None else merged + gate * up
    mix = _mm(merged.astype(BF16), wout_ref[...])
    y = ALPHA * x + mod_ref[:, 2 * d:3 * d] * mix
    y = y - jnp.mean(y, axis=1, keepdims=True)
    x1 = (y * lax.rsqrt(jnp.mean(y * y, axis=1, keepdims=True) + EPS)) * lng_ref[...] + lnb_ref[...]
    x1_ref[...] = x1
    h2 = x1 * (1.0 + mod_ref[:, 4 * d:5 * d]) + mod_ref[:, 3 * d:4 * d]
    h2_ref[...] = h2.astype(BF16)
    s = jax.nn.sigmoid(_mm_f32(rw_ref[...], h2, _NT))
    ids, gates = _route(s + rb_ref[...], s)
    ids_ref[...] = ids
    gates_ref[...] = gates


def _merge(x, mod_l, br, w_gl, w_br, w_out, ln_g, ln_b, rw_t, rb, n_ctx, dec_seq):
    n_tok = x.shape[0]
    tm = TOK_TILE
    grp = functools.partial(_mod_group, tm=tm, n_ctx=n_ctx, dec_seq=dec_seq)
    full = lambda shape: pl.BlockSpec(shape, lambda i: (0,) * len(shape))
    return pl.pallas_call(
        _merge_kernel,
        out_shape=(jax.ShapeDtypeStruct((n_tok, D_MODEL), F32),
                   jax.ShapeDtypeStruct((n_tok, D_MODEL), BF16),
                   jax.ShapeDtypeStruct((2, n_tok), jnp.int32),
                   jax.ShapeDtypeStruct((2, n_tok), F32)),
        grid=(n_tok // tm,),
        in_specs=[
            pl.BlockSpec((tm, D_MODEL), lambda i: (i, 0)),
            pl.BlockSpec((None, 1, ADA_DIM), lambda i: (grp(i), 0, 0)),
            pl.BlockSpec((tm, N_BRANCH * BRANCH), lambda i: (i, 0)),
            full((D_MODEL, N_BRANCH * D_MODEL)),
            full((N_BRANCH * BRANCH, D_MODEL)),
            full((D_MODEL, D_MODEL)),
            full((1, D_MODEL)),
            full((1, D_MODEL)),
            full((N_EXP, D_MODEL)),
            full((N_EXP, 1)),
        ],
        out_specs=(pl.BlockSpec((tm, D_MODEL), lambda i: (i, 0)),
                   pl.BlockSpec((tm, D_MODEL), lambda i: (i, 0)),
                   pl.BlockSpec((2, tm), lambda i: (0, i)),
                   pl.BlockSpec((2, tm), lambda i: (0, i))),
        compiler_params=_params("arbitrary"),
        name="merge",
    )(x, mod_l, br, w_gl, w_br, w_out, ln_g, ln_b, rw_t, rb)


def _moe_kernel(be_ref, nu_ref, x_ref, wg_ref, wu_ref, wd_ref, o_ref):
    i = pl.program_id(0)

    @pl.when(i < nu_ref[0])
    def _():
        x = x_ref[...]
        act = _silu(_mm(x, wg_ref[...])) * _mm(x, wu_ref[...])
        o_ref[...] = _mm(act.astype(BF16), wd_ref[...])

    @pl.when(i >= nu_ref[0])
    def _():
        o_ref[...] = jnp.zeros(o_ref.shape, F32)


def _moe_experts(xb, blk_exp, n_used, wg, wu, wd):
    n_blk = blk_exp.shape[0]
    return pl.pallas_call(
        _moe_kernel,
        out_shape=jax.ShapeDtypeStruct((n_blk * MOE_ROWS, D_MODEL), F32),
        grid_spec=pltpu.PrefetchScalarGridSpec(
            num_scalar_prefetch=2,
            grid=(n_blk,),
            in_specs=[
                pl.BlockSpec((MOE_ROWS, D_MODEL), lambda i, be, nu: (i, 0)),
                pl.BlockSpec((None, D_MODEL, D_FF_EXP), lambda i, be, nu: (be[i], 0, 0)),
                pl.BlockSpec((None, D_MODEL, D_FF_EXP), lambda i, be, nu: (be[i], 0, 0)),
                pl.BlockSpec((None, D_FF_EXP, D_MODEL), lambda i, be, nu: (be[i], 0, 0)),
            ],
            out_specs=pl.BlockSpec((MOE_ROWS, D_MODEL), lambda i, be, nu: (i, 0)),
        ),
        compiler_params=_params("arbitrary"),
        name="moe_experts",
    )(blk_exp, n_used, xb, wg, wu, wd)


def _final_kernel(x1_ref, mod_ref, y0_ref, y1_ref, gt_ref, lng_ref, lnb_ref, o_ref):
    d = D_MODEL
    gt = gt_ref[...]
    ffn = y0_ref[...] * gt[:, 0:1] + y1_ref[...] * gt[:, 1:2]
    y = ALPHA * x1_ref[...] + mod_ref[:, 5 * d:6 * d] * ffn
    y = y - jnp.mean(y, axis=1, keepdims=True)
    o_ref[...] = (y * lax.rsqrt(jnp.mean(y * y, axis=1, keepdims=True) + EPS)) * lng_ref[...] + lnb_ref[...]


def _final(x1, mod_l, y0, y1, gates_t, ln_g, ln_b, n_ctx, dec_seq):
    n_tok = x1.shape[0]
    tm = TOK_TILE
    grp = functools.partial(_mod_group, tm=tm, n_ctx=n_ctx, dec_seq=dec_seq)
    tile = pl.BlockSpec((tm, D_MODEL), lambda i: (i, 0))
    vec = pl.BlockSpec((1, D_MODEL), lambda i: (0, 0))
    return pl.pallas_call(
        _final_kernel,
        out_shape=jax.ShapeDtypeStruct((n_tok, D_MODEL), F32),
        grid=(n_tok // tm,),
        in_specs=[tile, pl.BlockSpec((None, 1, ADA_DIM), lambda i: (grp(i), 0, 0)), tile, tile,
                  pl.BlockSpec((tm, 2), lambda i: (i, 0)), vec, vec],
        out_specs=tile,
        compiler_params=_params("arbitrary"),
        name="final_norm",
    )(x1, mod_l, y0, y1, gates_t, ln_g, ln_b)


def _rope_tables(seq, dim, width):
    nf = dim // 4
    t = jnp.arange(seq)
    pos = jnp.stack([t // GRID_W, t % GRID_W], axis=-1).astype(F32)
    inv = ROPE_BASE ** (-jnp.arange(nf, dtype=F32) / nf)
    ang = pos[:, :, None] * inv
    cos, sin = jnp.cos(ang), jnp.sin(ang)
    zero = jnp.zeros_like(sin)
    c = jnp.stack([cos, cos], axis=2).reshape(seq, dim)
    s_lo = jnp.stack([-sin, zero], axis=2).reshape(seq, dim)
    s_hi = jnp.stack([zero, sin], axis=2).reshape(seq, dim)
    rep = width // dim
    return tuple(jnp.tile(a, (1, rep)) for a in (c, s_lo, s_hi))


def _dispatch_plan(ids, n_tok):
    n_assign = 2 * n_tok
    flat_e = ids.reshape(n_assign)
    onehot = (flat_e[:, None] == jnp.arange(N_EXP, dtype=jnp.int32)[None, :]).astype(jnp.int32)
    csum = jnp.cumsum(onehot, axis=0)
    rank = jnp.sum(onehot * (csum - 1), axis=1)
    counts = csum[-1]
    padded = (counts + MOE_ROWS - 1) // MOE_ROWS * MOE_ROWS
    pad_end = jnp.cumsum(padded)
    pad_start = pad_end - padded
    dest = jnp.sum(onehot * pad_start[None, :], axis=1) + rank
    n_blk = n_assign // MOE_ROWS + N_EXP
    tok = jnp.arange(n_assign, dtype=jnp.int32) % n_tok
    slot_tok = jnp.full((n_blk * MOE_ROWS,), n_tok, jnp.int32).at[dest].set(tok)
    blk_start = jnp.arange(n_blk, dtype=jnp.int32) * MOE_ROWS
    blk_exp = jnp.minimum(jnp.searchsorted(pad_end, blk_start, side='right'), N_EXP - 1).astype(jnp.int32)
    n_used = (pad_end[-1] // MOE_ROWS).astype(jnp.int32).reshape(1)
    return dest, slot_tok, blk_exp, n_used


def kernel(x_prompt, x_sample, cache_diff_k, cache_diff_v, cache_win_k, cache_win_v, state_ssd, state_ret,
           c, c_ctx, w_ada, b_ada, w_in, diff_lambda, diff_norm_g, win_sink, conv_w, conv_b,
           ssd_A_log, ssd_dt_bias, ssd_D, ssd_norm_g, ret_decay_logit, ret_norm_g, w_branch, w_out,
           ln_g, ln_b, router_w, router_b, moe_w_gate, moe_w_up, moe_w_down):
    batch, seq, d = x_prompt.shape
    dec_batch, dec_seq, _ = x_sample.shape
    past = cache_diff_k.shape[2]
    n_ctx, n_lat = batch * seq, dec_batch * dec_seq
    n_tok = n_ctx + n_lat
    assert d == D_MODEL and n_ctx % dec_seq == 0 and seq % CHUNK == 0 and dec_seq % CHUNK == 0

    x = jnp.concatenate([x_prompt.reshape(n_ctx, d), x_sample.reshape(n_lat, d)], axis=0)

    n_mod = 1 + dec_batch
    n_mod_pad = -(-n_mod // 8) * 8
    cvec = jnp.concatenate([c_ctx[None, :], c, jnp.zeros((n_mod_pad - n_mod, d), F32)], axis=0)
    mod = _ada(cvec, w_ada, b_ada)

    rope_a = _rope_tables(dec_seq, A_QK, 256)
    rope_b = _rope_tables(dec_seq, B_DIM, 256)
    rw_t = router_w.T
    rb_col = router_b.reshape(N_EXP, 1)

    ctx_out = {k: [] for k in ('diff_k', 'diff_v', 'win_k', 'win_v', 'ssd', 'ret')}
    for l in range(DEPTH):
        mod_l = mod[l, :n_mod].reshape(n_mod, 1, ADA_DIM)
        wl = w_in[l]
        w_small = jnp.concatenate(
            [wl[:, :CDT_OFF], wl[:, CDT_OFF + 8:GATE_OFF], wl[:, CDT_OFF:CDT_OFF + 8],
             jnp.zeros((d, N_SMALL - GATE_OFF), F32)], axis=1).astype(BF16)
        w_gl = wl[:, GATE_OFF:].astype(BF16)
        proj = _inproj(x, mod_l, w_small, n_ctx, dec_seq)

        lam_init = 0.8 - 0.6 * math.exp(-0.3 * l)
        lv = diff_lambda[l]
        lam = jnp.exp(jnp.sum(lv[0] * lv[1])) - jnp.exp(jnp.sum(lv[2] * lv[3])) + lam_init
        diff_scal = jnp.stack([lam, jnp.asarray(1.0 - lam_init, F32)]).astype(F32)
        g_a = diff_norm_g[l].reshape(1, A_V)
        sink = win_sink[l]
        dtb = jnp.zeros((1, 128), F32).at[0, :8].set(ssd_dt_bias[l].reshape(8))
        acoef = jnp.zeros((1, 128), F32).at[0, :8].set(-jnp.exp(ssd_A_log[l]).reshape(8))
        dvec = jnp.repeat(ssd_D[l], C_P).reshape(1, 256)
        g_c = ssd_norm_g[l].reshape(1, 256)
        log_g = jax.nn.log_sigmoid(ret_decay_logit[l]).reshape(8)
        g_d = ret_norm_g[l].reshape(1, 256)
        cw = conv_w[l]
        cb = conv_b[l].reshape(1, 512)

        oa_c = _diff_attn(proj, diff_scal, g_a, 0, batch, seq, seq)
        ob_c = _win_attn(proj, sink, 0, batch, seq, seq)
        yc_c, st_c = _ssd(proj, cw, cb, dtb, acoef, dvec, g_c, 0, batch, seq)
        od_c, rt_c = _ret(proj, log_g, g_d, 0, batch, seq)
        cache_a = (cache_diff_k[:, l].reshape(dec_batch, past, 256), cache_diff_v[:, l].reshape(dec_batch, past, 256))
        cache_b = (cache_win_k[:, l].reshape(dec_batch, past, 128), cache_win_v[:, l].reshape(dec_batch, past, 128))
        oa_l = _diff_attn(proj, diff_scal, g_a, n_ctx, dec_batch, dec_seq, BLOCK, cache=cache_a, rope=rope_a)
        ob_l = _win_attn(proj, sink, n_ctx, dec_batch, dec_seq, BLOCK, cache=cache_b, rope=rope_b)
        yc_l, _ = _ssd(proj, cw, cb, dtb, acoef, dvec, g_c, n_ctx, dec_batch, dec_seq,
                       h0=state_ssd[:, l].reshape(dec_batch, 8, C_P, C_N))
        od_l, _ = _ret(proj, log_g, g_d, n_ctx, dec_batch, dec_seq,
                       h0=state_ret[:, l].reshape(dec_batch, 8, D_V, D_K))

        br = jnp.concatenate([jnp.concatenate([oa_c, ob_c, yc_c, od_c], axis=1),
                              jnp.concatenate([oa_l, ob_l, yc_l, od_l], axis=1)], axis=0)
        x1, h2, ids, gates = _merge(
            x, mod_l, br, w_gl, w_branch[l].reshape(N_BRANCH * BRANCH, d).astype(BF16),
            w_out[l].astype(BF16), ln_g[l, 0].reshape(1, d), ln_b[l, 0].reshape(1, d), rw_t, rb_col,
            n_ctx, dec_seq)

        dest, slot_tok, blk_exp, n_used = _dispatch_plan(ids, n_tok)
        xb = jnp.concatenate([h2, jnp.zeros((1, d), BF16)], axis=0)[slot_tok]
        yb = _moe_experts(xb, blk_exp, n_used, moe_w_gate[l].astype(BF16), moe_w_up[l].astype(BF16),
                          moe_w_down[l].astype(BF16))
        x = _final(x1, mod_l, yb[dest[:n_tok]], yb[dest[n_tok:]], gates.T,
                   ln_g[l, 1].reshape(1, d), ln_b[l, 1].reshape(1, d), n_ctx, dec_seq)

        pc = proj[:n_ctx]
        ctx_out['diff_k'].append(pc[:, 256:512].reshape(batch, seq, A_HEADS, 2, A_QK))
        ctx_out['diff_v'].append(pc[:, 512:768].reshape(batch, seq, A_HEADS, A_V))
        ctx_out['win_k'].append(pc[:, 1024:1152].reshape(batch, seq, B_KV, B_DIM))
        ctx_out['win_v'].append(pc[:, 1152:1280].reshape(batch, seq, B_KV, B_DIM))
        ctx_out['ssd'].append(st_c.reshape(batch, 2, C_HEADS, C_P, C_N))
        ctx_out['ret'].append(rt_c.reshape(batch, 2, D_HEADS, D_V, D_K))

    y_prompt = x[:n_ctx].reshape(batch, seq, d)
    y_sample = x[n_ctx:].reshape(dec_batch, dec_seq, d)
    stk = lambda k: jnp.stack(ctx_out[k], axis=1)
    return (y_prompt, y_sample, stk('diff_k'), stk('diff_v'), stk('win_k'), stk('win_v'), stk('ssd'), stk('ret'))
```

```python
import functools
import math

import jax
import jax.numpy as jnp
from jax import lax
from jax.experimental import pallas as pl
from jax.experimental.pallas import tpu as pltpu

F32 = jnp.float32
BF16 = jnp.bfloat16

D_MODEL = 1024
DEPTH = 4
GRID_W = 64
BLOCK = 128
WINDOW = 128
CHUNK = 128
A_HEADS, A_QK, A_V = 4, 32, 64
B_HEADS, B_KV, B_DIM = 4, 2, 64
C_HEADS, C_P, C_GROUPS, C_N = 4, 64, 2, 64
D_HEADS, D_K, D_V = 4, 32, 64
BRANCH = 256
N_BRANCH = 4
N_EXP = 16
N_EXP_GROUPS = 4
EXP_PER_GROUP = 4
D_FF_EXP = 512
ROPE_BASE = 10000.0
ALPHA = (2 * DEPTH) ** 0.25
EPS = 1e-5
ADA_DIM = 6 * D_MODEL
NEG = -1e30

N_SMALL = 23 * 128
GATE_OFF = 2824
CDT_OFF = 2048

VMEM_LIMIT = 52 * 1024 * 1024
MOE_ROWS = 256
TOK_TILE = 256

_NN = (((1,), (0,)), ((), ()))
_NT = (((1,), (1,)), ((), ()))
_TN = (((0,), (0,)), ((), ()))


def _params(*sem):
    return pltpu.CompilerParams(dimension_semantics=sem, vmem_limit_bytes=VMEM_LIMIT)


def _mm(a, b, dims=_NN):
    return lax.dot_general(a, b, dims, preferred_element_type=F32)


def _split(a):
    hi = a.astype(BF16)
    return hi, (a - hi.astype(F32)).astype(BF16)


def _mm_f32(a, b, dims=_NN):
    a_hi, a_lo = _split(a)
    b_hi, b_lo = _split(b)
    return (_mm(a_lo, b_hi, dims) + _mm(a_hi, b_lo, dims)) + _mm(a_hi, b_hi, dims)


def _mm_exact_lhs(m_bf, a):
    a1 = a.astype(BF16)
    r1 = a - a1.astype(F32)
    a2 = r1.astype(BF16)
    a3 = (r1 - a2.astype(F32)).astype(BF16)
    return (_mm(m_bf, a3) + _mm(m_bf, a2)) + _mm(m_bf, a1)


def _silu(x):
    return x * jax.nn.sigmoid(x)


def _rope(x, c, s_lo, s_hi, shift):
    n = x.shape[1]
    return x * c + pltpu.roll(x, n - shift, 1) * s_lo + pltpu.roll(x, shift, 1) * s_hi


def _ada_kernel(c_ref, w_ref, b_ref, o_ref):
    c = c_ref[...]
    o_ref[...] = _mm_f32(_silu(c), w_ref[...]) + b_ref[...]


def _ada(cvec, w_ada, b_ada):
    rows = cvec.shape[0]
    tn = 1024
    return pl.pallas_call(
        _ada_kernel,
        out_shape=jax.ShapeDtypeStruct((DEPTH, rows, ADA_DIM), F32),
        grid=(DEPTH, ADA_DIM // tn),
        in_specs=[
            pl.BlockSpec((rows, D_MODEL), lambda l, j: (0, 0)),
            pl.BlockSpec((None, D_MODEL, tn), lambda l, j: (l, 0, j)),
            pl.BlockSpec((None, 1, tn), lambda l, j: (l, 0, j)),
        ],
        out_specs=pl.BlockSpec((None, rows, tn), lambda l, j: (l, 0, j)),
        compiler_params=_params("arbitrary", "arbitrary"),
        name="ada",
    )(cvec, w_ada, b_ada.reshape(DEPTH, 1, ADA_DIM))


def _inproj_kernel(x_ref, mod_ref, w_ref, o_ref):
    d = x_ref.shape[1]
    h = x_ref[...] * (1.0 + mod_ref[:, d:2 * d]) + mod_ref[:, 0:d]
    o_ref[...] = _mm(h.astype(BF16), w_ref[...])


def _mod_group(i, tm, n_ctx, dec_seq):
    row = i * tm
    return jnp.where(row < n_ctx, 0, 1 + lax.div(jnp.maximum(row - n_ctx, 0), dec_seq))


def _inproj(x, mod_l, w_small, n_ctx, dec_seq):
    n_tok = x.shape[0]
    tm = TOK_TILE
    grp = functools.partial(_mod_group, tm=tm, n_ctx=n_ctx, dec_seq=dec_seq)
    return pl.pallas_call(
        _inproj_kernel,
        out_shape=jax.ShapeDtypeStruct((n_tok, N_SMALL), F32),
        grid=(n_tok // tm,),
        in_specs=[
            pl.BlockSpec((tm, D_MODEL), lambda i: (i, 0)),
            pl.BlockSpec((None, 1, ADA_DIM), lambda i: (grp(i), 0, 0)),
            pl.BlockSpec((D_MODEL, N_SMALL), lambda i: (0, 0)),
        ],
        out_specs=pl.BlockSpec((tm, N_SMALL), lambda i: (i, 0)),
        compiler_params=_params("arbitrary"),
        name="inproj",
    )(x, mod_l, w_small)


def _diff_attn_kernel(*refs, latent, tq, seq, past):
    if latent:
        (sc_ref, q_ref, k_ref, v_ref, g_ref, ck_ref, cv_ref, rc_ref, rlo_ref, rhi_ref,
         o_ref, kt_scr, v_scr) = refs
    else:
        sc_ref, q_ref, k_ref, v_ref, g_ref, o_ref, kt_scr, v_scr = refs
    qi = pl.program_id(1)
    shift = A_QK // 4

    @pl.when(qi == 0)
    def _():
        k = k_ref[...]
        if latent:
            k = _rope(k, rc_ref[...], rlo_ref[...], rhi_ref[...], shift)
        kt_scr[:, 0:seq] = k.T.astype(BF16)
        v_scr[0:seq, :] = v_ref[...].astype(BF16)
        if latent:
            kt_scr[:, seq:seq + past] = ck_ref[...].T.astype(BF16)
            v_scr[seq:seq + past, :] = cv_ref[...].astype(BF16)

    q = q_ref[...]
    if latent:
        r = pl.ds(pl.multiple_of(qi * tq, tq), tq)
        q = _rope(q, rc_ref[r, :], rlo_ref[r, :], rhi_ref[r, :], shift)
    qb = q.astype(BF16)
    lam = sc_ref[0]
    post = sc_ref[1]
    scale = A_QK ** -0.5
    outs = []
    for h in range(A_HEADS):
        probs = []
        for m in range(2):
            off = (h * 2 + m) * A_QK
            s = _mm(qb[:, off:off + A_QK], kt_scr[off:off + A_QK, :]) * scale
            p = jnp.exp(s - jnp.max(s, axis=1, keepdims=True))
            probs.append(p * (1.0 / jnp.sum(p, axis=1, keepdims=True)))
        w = probs[0] - lam * probs[1]
        o = _mm(w.astype(BF16), v_scr[:, h * A_V:(h + 1) * A_V])
        n = o * lax.rsqrt(jnp.mean(o * o, axis=1, keepdims=True) + EPS)
        outs.append((n * g_ref[...]) * post)
    o_ref[...] = jnp.concatenate(outs, axis=1).astype(BF16)


def _diff_attn(proj, scal, g, row0, nb, seq, tq, cache=None, rope=None):
    latent = cache is not None
    nq = seq // tq
    rb = row0 // seq
    qb0 = row0 // tq
    past = cache[0].shape[1] if latent else 0
    in_specs = [
        pl.BlockSpec(memory_space=pltpu.SMEM),
        pl.BlockSpec((tq, 256), lambda b, i: (qb0 + b * nq + i, 0)),
        pl.BlockSpec((seq, 256), lambda b, i: (rb + b, 1)),
        pl.BlockSpec((seq, 256), lambda b, i: (rb + b, 2)),
        pl.BlockSpec((1, A_V), lambda b, i: (0, 0)),
    ]
    args = [scal, proj, proj, proj, g]
    if latent:
        in_specs += [
            pl.BlockSpec((None, past, 256), lambda b, i: (b, 0, 0)),
            pl.BlockSpec((None, past, 256), lambda b, i: (b, 0, 0)),
        ] + [pl.BlockSpec((seq, 256), lambda b, i: (0, 0))] * 3
        args += [cache[0], cache[1], *rope]
    return pl.pallas_call(
        functools.partial(_diff_attn_kernel, latent=latent, tq=tq, seq=seq, past=past),
        out_shape=jax.ShapeDtypeStruct((nb * seq, BRANCH), BF16),
        grid=(nb, nq),
        in_specs=in_specs,
        out_specs=pl.BlockSpec((tq, BRANCH), lambda b, i: (b * nq + i, 0)),
        scratch_shapes=[pltpu.VMEM((256, seq + past), BF16), pltpu.VMEM((seq + past, 256), BF16)],
        compiler_params=_params("arbitrary", "arbitrary"),
        name="diff_attn_lat" if latent else "diff_attn_ctx",
    )(*args)


def _win_attn_kernel(*refs, latent, tq, seq):
    if latent:
        (sink_ref, q_ref, k_ref, v_ref, ck_ref, cv_ref, rc_ref, rlo_ref, rhi_ref,
         o_ref, k_scr, v_scr, ck_scr, cv_scr) = refs
    else:
        sink_ref, q_ref, k_ref, v_ref, o_ref, k_scr, v_scr = refs
    qi = pl.program_id(1)
    nq = seq // tq
    shift = B_DIM // 4
    kvw = B_KV * B_DIM

    @pl.when(qi == 0)
    def _():
        k = k_ref[...]
        if latent:
            k = _rope(k, rc_ref[:, 0:kvw], rlo_ref[:, 0:kvw], rhi_ref[:, 0:kvw], shift)
            ck_scr[...] = ck_ref[...].astype(BF16)
            cv_scr[...] = cv_ref[...].astype(BF16)
        k_scr[...] = k.astype(BF16)
        v_scr[...] = v_ref[...].astype(BF16)

    q = q_ref[...]
    if latent:
        r = pl.ds(pl.multiple_of(qi * tq, tq), tq)
        q = _rope(q, rc_ref[r, :], rlo_ref[r, :], rhi_ref[r, :], shift)
        rows = [pl.ds(pl.multiple_of(j * tq, tq), tq)
                for j in (jnp.maximum(qi - 1, 0), qi, jnp.minimum(qi + 1, nq - 1))]
        kl = jnp.concatenate([k_scr[rr, :] for rr in rows], axis=0)
        vl = jnp.concatenate([v_scr[rr, :] for rr in rows], axis=0)
        ii = lax.broadcasted_iota(jnp.int32, (tq, 3 * tq), 0)
        jj = lax.broadcasted_iota(jnp.int32, (tq, 3 * tq), 1)
        lo = jnp.where(qi > 0, 0, tq)
        hi = jnp.where(qi < nq - 1, 3 * tq, 2 * tq)
        valid = (jnp.abs(jj - tq - ii) <= WINDOW) & (jj >= lo) & (jj < hi)
    else:
        kl = k_scr[...]
        vl = v_scr[...]
    qb = q.astype(BF16)
    scale = B_DIM ** -0.5
    ratio = B_HEADS // B_KV
    outs = []
    for h in range(B_HEADS):
        gsl = slice((h // ratio) * B_DIM, (h // ratio + 1) * B_DIM)
        qh = qb[:, h * B_DIM:(h + 1) * B_DIM]
        snk = sink_ref[h]
        s = _mm(qh, kl[:, gsl], _NT) * scale
        if latent:
            s = jnp.where(valid, s, NEG)
            sc = _mm(qh, ck_scr[:, gsl], _NT) * scale
            m = jnp.maximum(jnp.maximum(jnp.max(s, axis=1, keepdims=True),
                                        jnp.max(sc, axis=1, keepdims=True)), snk)
            pc = jnp.exp(sc - m)
        else:
            m = jnp.maximum(jnp.max(s, axis=1, keepdims=True), snk)
        p = jnp.exp(s - m)
        den = jnp.sum(p, axis=1, keepdims=True) + jnp.exp(snk - m)
        if latent:
            den = den + jnp.sum(pc, axis=1, keepdims=True)
        inv = 1.0 / den
        o = _mm((p * inv).astype(BF16), vl[:, gsl])
        if latent:
            o = o + _mm((pc * inv).astype(BF16), cv_scr[:, gsl])
        outs.append(o)
    o_ref[...] = jnp.concatenate(outs, axis=1).astype(BF16)


def _win_attn(proj, sink, row0, nb, seq, tq, cache=None, rope=None):
    latent = cache is not None
    nq = seq // tq
    rb = row0 // seq
    qb0 = row0 // tq
    kvw = B_KV * B_DIM
    in_specs = [
        pl.BlockSpec(memory_space=pltpu.SMEM),
        pl.BlockSpec((tq, 256), lambda b, i: (qb0 + b * nq + i, 3)),
        pl.BlockSpec((seq, kvw), lambda b, i: (rb + b, 8)),
        pl.BlockSpec((seq, kvw), lambda b, i: (rb + b, 9)),
    ]
    args = [sink, proj, proj, proj]
    scratch = [pltpu.VMEM((seq, kvw), BF16), pltpu.VMEM((seq, kvw), BF16)]
    if latent:
        past = cache[0].shape[1]
        in_specs += [
            pl.BlockSpec((None, past, kvw), lambda b, i: (b, 0, 0)),
            pl.BlockSpec((None, past, kvw), lambda b, i: (b, 0, 0)),
        ] + [pl.BlockSpec((seq, 256), lambda b, i: (0, 0))] * 3
        args += [cache[0], cache[1], *rope]
        scratch += [pltpu.VMEM((past, kvw), BF16), pltpu.VMEM((past, kvw), BF16)]
    return pl.pallas_call(
        functools.partial(_win_attn_kernel, latent=latent, tq=tq, seq=seq),
        out_shape=jax.ShapeDtypeStruct((nb * seq, BRANCH), BF16),
        grid=(nb, nq),
        in_specs=in_specs,
        out_specs=pl.BlockSpec((tq, BRANCH), lambda b, i: (b * nq + i, 0)),
        scratch_shapes=scratch,
        compiler_params=_params("arbitrary", "arbitrary"),
        name="win_attn_lat" if latent else "win_attn_ctx",
    )(*args)


def _conv_silu(u, w, b):
    n = u.shape[0]
    rows = lax.broadcasted_iota(jnp.int32, u.shape, 0)
    up = jnp.where(rows == 0, 0.0, pltpu.roll(u, 1, 0))
    un = jnp.where(rows == n - 1, 0.0, pltpu.roll(u, n - 1, 0))
    return _silu(up * w[0:1, :] + u * w[1:2, :] + un * w[2:3, :] + b)


def _ssd_kernel(*refs, seq, has_h0):
    if has_h0:
        (cx_ref, cz_ref, cbc_ref, cdt_ref, cw_ref, cb_ref, dtb_ref, ac_ref, dv_ref, g_ref, h0_ref,
         y_ref, st_ref, xs, bcs, dts, ybuf, hs) = refs
    else:
        (cx_ref, cz_ref, cbc_ref, cdt_ref, cw_ref, cb_ref, dtb_ref, ac_ref, dv_ref, g_ref,
         y_ref, st_ref, xs, bcs, dts, ybuf, hs) = refs
    nc = seq // CHUNK
    xw = C_HEADS * C_P
    xs[...] = _conv_silu(cx_ref[...], cw_ref[:, 0:xw], cb_ref[:, 0:xw])
    bcs[...] = _conv_silu(cbc_ref[...], cw_ref[:, xw:2 * xw], cb_ref[:, xw:2 * xw])
    z = cdt_ref[...] + dtb_ref[...]
    dts[...] = jnp.maximum(z, 0.0) + jnp.log1p(jnp.exp(-jnp.abs(z)))
    if has_h0:
        hs[...] = h0_ref[...]
    else:
        hs[...] = jnp.zeros(hs.shape, F32)

    ri = lax.broadcasted_iota(jnp.int32, (CHUNK, CHUNK), 0)
    ci = lax.broadcasted_iota(jnp.int32, (CHUNK, CHUNK), 1)
    gw = C_GROUPS * C_N

    def chunk(c, d):
        r = pl.ds(pl.multiple_of(c * CHUNK, CHUNK), CHUNK)
        tri = (ri >= ci) if d == 0 else (ci >= ri)
        dt = dts[r, :]
        cs = _mm_exact_lhs(tri.astype(BF16), dt * ac_ref[...])
        cst = cs.T
        x = xs[r, :]
        bc = bcs[r, :]
        ys = []
        for g in range(C_GROUPS):
            bm = bc[:, g * C_N:(g + 1) * C_N]
            cb = bc[:, gw + g * C_N:gw + (g + 1) * C_N].astype(BF16)
            gram = _mm(cb, bm.astype(BF16), _NT)
            for hh in range(C_HEADS // C_GROUPS):
                h = g * (C_HEADS // C_GROUPS) + hh
                col = d * C_HEADS + h
                xb = (x[:, h * C_P:(h + 1) * C_P] * dt[:, col:col + 1]).astype(BF16)
                cc = cs[:, col:col + 1]
                dec = jnp.exp(jnp.where(tri, cc - cst[col:col + 1, :], NEG))
                tot = cc[CHUNK - 1:CHUNK, :] if d == 0 else cc[0:1, :]
                hin = hs[col]
                y = _mm((gram * dec).astype(BF16), xb)
                y = y + _mm(cb, hin.astype(BF16), _NT) * jnp.exp(cc)
                bd = (bm * jnp.exp(tot - cc)).astype(BF16)
                hs[col] = hin * jnp.exp(tot) + _mm(xb, bd, _TN)
                ys.append(y)
        return r, jnp.concatenate(ys, axis=1)

    def fwd(c, carry):
        r, y = chunk(c, 0)
        ybuf[r, :] = y
        return carry

    lax.fori_loop(0, nc, fwd, 0)

    def bwd(i, carry):
        r, y = chunk(nc - 1 - i, 1)
        y = (ybuf[r, :] + y) + xs[r, :] * dv_ref[...]
        y = y * _silu(cz_ref[r, :])
        gl = xw // C_GROUPS
        parts = []
        for g in range(C_GROUPS):
            seg = y[:, g * gl:(g + 1) * gl]
            parts.append(seg * lax.rsqrt(jnp.mean(seg * seg, axis=1, keepdims=True) + EPS))
        y_ref[r, :] = (jnp.concatenate(parts, axis=1) * g_ref[...]).astype(BF16)
        return carry

    lax.fori_loop(0, nc, bwd, 0)
    st_ref[...] = hs[...]


def _ssd(proj, conv_w, conv_b, dtb, acoef, dvec, g, row0, nb, seq, h0=None):
    has_h0 = h0 is not None
    rb = row0 // seq
    nst = 2 * C_HEADS
    in_specs = [
        pl.BlockSpec((seq, 256), lambda b: (rb + b, 5)),
        pl.BlockSpec((seq, 256), lambda b: (rb + b, 6)),
        pl.BlockSpec((seq, 256), lambda b: (rb + b, 7)),
        pl.BlockSpec((seq, 128), lambda b: (rb + b, 22)),
        pl.BlockSpec((3, 512), lambda b: (0, 0)),
        pl.BlockSpec((1, 512), lambda b: (0, 0)),
        pl.BlockSpec((1, 128), lambda b: (0, 0)),
        pl.BlockSpec((1, 128), lambda b: (0, 0)),
        pl.BlockSpec((1, 256), lambda b: (0, 0)),
        pl.BlockSpec((1, 256), lambda b: (0, 0)),
    ]
    args = [proj, proj, proj, proj, conv_w, conv_b, dtb, acoef, dvec, g]
    if has_h0:
        in_specs.append(pl.BlockSpec((None, nst, C_P, C_N), lambda b: (b, 0, 0, 0)))
        args.append(h0)
    return pl.pallas_call(
        functools.partial(_ssd_kernel, seq=seq, has_h0=has_h0),
        out_shape=(jax.ShapeDtypeStruct((nb * seq, BRANCH), BF16),
                   jax.ShapeDtypeStruct((nb, nst, C_P, C_N), F32)),
        grid=(nb,),
        in_specs=in_specs,
        out_specs=(pl.BlockSpec((seq, BRANCH), lambda b: (b, 0)),
                   pl.BlockSpec((None, nst, C_P, C_N), lambda b: (b, 0, 0, 0))),
        scratch_shapes=[pltpu.VMEM((seq, 256), F32), pltpu.VMEM((seq, 256), F32),
                        pltpu.VMEM((seq, 128), F32), pltpu.VMEM((seq, 256), F32),
                        pltpu.VMEM((nst, C_P, C_N), F32)],
        compiler_params=_params("arbitrary"),
        name="ssd_lat" if has_h0 else "ssd_ctx",
    )(*args)


def _ret_kernel(*refs, seq, has_h0):
    if has_h0:
        lg_ref, q_ref, k_ref, v_ref, gt_ref, g_ref, h0_ref, y_ref, st_ref, ybuf, hs = refs
    else:
        lg_ref, q_ref, k_ref, v_ref, gt_ref, g_ref, y_ref, st_ref, ybuf, hs = refs
    nc = seq // CHUNK
    if has_h0:
        hs[...] = h0_ref[...]
    else:
        hs[...] = jnp.zeros(hs.shape, F32)
    ri = lax.broadcasted_iota(jnp.int32, (CHUNK, CHUNK), 0)
    ci = lax.broadcasted_iota(jnp.int32, (CHUNK, CHUNK), 1)
    pos = lax.broadcasted_iota(jnp.int32, (CHUNK, 1), 0).astype(F32)
    kscale = D_K ** -0.5

    def chunk(c, d):
        r = pl.ds(pl.multiple_of(c * CHUNK, CHUNK), CHUNK)
        q = q_ref[r, :]
        k = k_ref[r, :] * kscale
        v = v_ref[r, :]
        if d == 0:
            tri, dist = ri >= ci, (ri - ci).astype(F32)
            steps_in, steps_out = pos + 1.0, (CHUNK - 1.0) - pos
        else:
            tri, dist = ci >= ri, (ci - ri).astype(F32)
            steps_in, steps_out = CHUNK - pos, pos
        ys = []
        for h in range(D_HEADS):
            col = d * D_HEADS + h
            lg = lg_ref[col]
            dec = jnp.exp(jnp.where(tri, dist * lg, NEG))
            e_in = jnp.exp(steps_in * lg)
            e_tot = e_in[CHUNK - 1:CHUNK, :] if d == 0 else e_in[0:1, :]
            qb = q[:, h * D_K:(h + 1) * D_K].astype(BF16)
            km = k[:, h * D_K:(h + 1) * D_K]
            vb = v[:, h * D_V:(h + 1) * D_V].astype(BF16)
            hin = hs[col]
            y = _mm((_mm(qb, km.astype(BF16), _NT) * dec).astype(BF16), vb)
            y = y + _mm(qb, hin.astype(BF16), _NT) * e_in
            bd = (km * jnp.exp(steps_out * lg)).astype(BF16)
            hs[col] = hin * e_tot + _mm(vb, bd, _TN)
            ys.append(y)
        return r, ys

    def fwd(c, carry):
        r, ys = chunk(c, 0)
        ybuf[r, :] = jnp.concatenate(ys, axis=1)
        return carry

    lax.fori_loop(0, nc, fwd, 0)

    def bwd(i, carry):
        r, ys = chunk(nc - 1 - i, 1)
        yf = ybuf[r, :]
        parts = []
        for h in range(D_HEADS):
            o = yf[:, h * D_V:(h + 1) * D_V] + ys[h]
            o = o - jnp.mean(o, axis=1, keepdims=True)
            parts.append(o * lax.rsqrt(jnp.mean(o * o, axis=1, keepdims=True) + EPS))
        y = (jnp.concatenate(parts, axis=1) * g_ref[...]) * _silu(gt_ref[r, :])
        y_ref[r, :] = y.astype(BF16)
        return carry

    lax.fori_loop(0, nc, bwd, 0)
    st_ref[...] = hs[...]


def _ret(proj, log_g, g, row0, nb, seq, h0=None):
    has_h0 = h0 is not None
    rb = row0 // seq
    nst = 2 * D_HEADS
    in_specs = [
        pl.BlockSpec(memory_space=pltpu.SMEM),
        pl.BlockSpec((seq, 128), lambda b: (rb + b, 16)),
        pl.BlockSpec((seq, 128), lambda b: (rb + b, 17)),
        pl.BlockSpec((seq, 256), lambda b: (rb + b, 9)),
        pl.BlockSpec((seq, 256), lambda b: (rb + b, 10)),
        pl.BlockSpec((1, 256), lambda b: (0, 0)),
    ]
    args = [log_g, proj, proj, proj, proj, g]
    if has_h0:
        in_specs.append(pl.BlockSpec((None, nst, D_V, D_K), lambda b: (b, 0, 0, 0)))
        args.append(h0)
    return pl.pallas_call(
        functools.partial(_ret_kernel, seq=seq, has_h0=has_h0),
        out_shape=(jax.ShapeDtypeStruct((nb * seq, BRANCH), BF16),
                   jax.ShapeDtypeStruct((nb, nst, D_V, D_K), F32)),
        grid=(nb,),
        in_specs=in_specs,
        out_specs=(pl.BlockSpec((seq, BRANCH), lambda b: (b, 0)),
                   pl.BlockSpec((None, nst, D_V, D_K), lambda b: (b, 0, 0, 0))),
        scratch_shapes=[pltpu.VMEM((seq, 256), F32), pltpu.VMEM((nst, D_V, D_K), F32)],
        compiler_params=_params("arbitrary"),
        name="ret_lat" if has_h0 else "ret_ctx",
    )(*args)


def _route(sel, s):
    row = lambda a, e: a[e:e + 1, :]
    best = None
    grp = None
    for g in range(N_EXP_GROUPS):
        vals = [row(sel, g * EXP_PER_GROUP + j) for j in range(EXP_PER_GROUP)]
        score = None
        for a in range(EXP_PER_GROUP):
            for b in range(a + 1, EXP_PER_GROUP):
                pair = vals[a] + vals[b]
                score = pair if score is None else jnp.maximum(score, pair)
        if best is None:
            best, grp = score, jnp.zeros(score.shape, jnp.int32)
        else:
            better = score > best
            best = jnp.where(better, score, best)
            grp = jnp.where(better, g, grp)

    def pick(a, j):
        out = row(a, j)
        for g in range(1, N_EXP_GROUPS):
            out = jnp.where(grp == g, row(a, g * EXP_PER_GROUP + j), out)
        return out

    cand = [pick(sel, j) for j in range(EXP_PER_GROUP)]
    aff = [pick(s, j) for j in range(EXP_PER_GROUP)]

    def arg_first_max(vals):
        top, idx = vals[0], jnp.zeros(vals[0].shape, jnp.int32)
        for j in range(1, len(vals)):
            better = vals[j] > top
            top = jnp.where(better, vals[j], top)
            idx = jnp.where(better, j, idx)
        return idx

    def take(vals, idx):
        out = vals[0]
        for j in range(1, len(vals)):
            out = jnp.where(idx == j, vals[j], out)
        return out

    i1 = arg_first_max(cand)
    i2 = arg_first_max([jnp.where(i1 == j, -jnp.inf, cand[j]) for j in range(EXP_PER_GROUP)])
    w1, w2 = take(aff, i1), take(aff, i2)
    tot = w1 + w2
    ids = jnp.concatenate([grp * EXP_PER_GROUP + i1, grp * EXP_PER_GROUP + i2], axis=0)
    gates = jnp.concatenate([w1 / tot, w2 / tot], axis=0)
    return ids, gates


def _merge_kernel(x_ref, mod_ref, br_ref, wgl_ref, wbr_ref, wout_ref, lng_ref, lnb_ref,
                  rw_ref, rb_ref, x1_ref, h2_ref, ids_ref, gates_ref):
    d = D_MODEL
    x = x_ref[...]
    hb = (x * (1.0 + mod_ref[:, d:2 * d]) + mod_ref[:, 0:d]).astype(BF16)
    br = br_ref[...]
    merged = None
    for k in range(N_BRANCH):
        gate = jax.nn.sigmoid(_mm(hb, wgl_ref[:, k * d:(k + 1) * d]))
        up = _mm(br[:, k * BRANCH:(k + 1) * BRANCH], wbr_ref[k * BRANCH:(k + 1) * BRANCH, :])
        merged = gate * up if merged is None else merged + gate * up
    mix = _mm(merged.astype(BF16), wout_ref[...])
    y = ALPHA * x + mod_ref[:, 2 * d:3 * d] * mix
    y = y - jnp.mean(y, axis=1, keepdims=True)
    x1 = (y * lax.rsqrt(jnp.mean(y * y, axis=1, keepdims=True) + EPS)) * lng_ref[...] + lnb_ref[...]
    x1_ref[...] = x1
    h2 = x1 * (1.0 + mod_ref[:, 4 * d:5 * d]) + mod_ref[:, 3 * d:4 * d]
    h2_ref[...] = h2
    s = jax.nn.sigmoid(_mm_f32(rw_ref[...], h2, _NT))
    ids, gates = _route(s + rb_ref[...], s)
    ids_ref[...] = ids
    gates_ref[...] = gates


def _merge(x, mod_l, br, w_gl, w_br, w_out, ln_g, ln_b, rw_t, rb, n_ctx, dec_seq):
    n_tok = x.shape[0]
    tm = TOK_TILE
    grp = functools.partial(_mod_group, tm=tm, n_ctx=n_ctx, dec_seq=dec_seq)
    full = lambda shape: pl.BlockSpec(shape, lambda i: (0,) * len(shape))
    return pl.pallas_call(
        _merge_kernel,
        out_shape=(jax.ShapeDtypeStruct((n_tok, D_MODEL), F32),
                   jax.ShapeDtypeStruct((n_tok, D_MODEL), F32),
                   jax.ShapeDtypeStruct((2, n_tok), jnp.int32),
                   jax.ShapeDtypeStruct((2, n_tok), F32)),
        grid=(n_tok // tm,),
        in_specs=[
            pl.BlockSpec((tm, D_MODEL), lambda i: (i, 0)),
            pl.BlockSpec((None, 1, ADA_DIM), lambda i: (grp(i), 0, 0)),
            pl.BlockSpec((tm, N_BRANCH * BRANCH), lambda i: (i, 0)),
            full((D_MODEL, N_BRANCH * D_MODEL)),
            full((N_BRANCH * BRANCH, D_MODEL)),
            full((D_MODEL, D_MODEL)),
            full((1, D_MODEL)),
            full((1, D_MODEL)),
            full((N_EXP, D_MODEL)),
            full((N_EXP, 1)),
        ],
        out_specs=(pl.BlockSpec((tm, D_MODEL), lambda i: (i, 0)),
                   pl.BlockSpec((tm, D_MODEL), lambda i: (i, 0)),
                   pl.BlockSpec((2, tm), lambda i: (0, i)),
                   pl.BlockSpec((2, tm), lambda i: (0, i))),
        compiler_params=_params("arbitrary"),
        name="merge",
    )(x, mod_l, br, w_gl, w_br, w_out, ln_g, ln_b, rw_t, rb)


ROW_CHUNK = 512
ROW_UNROLL = 8


def _row_move_kernel(*refs, n_rows, n_src, scatter):
    if scatter:
        idx_ref, src_hbm, _, dst_hbm, sem = refs
    else:
        idx_ref, src_hbm, dst_hbm, sem = refs

    def row_copy(s, d):
        return pltpu.make_async_copy(src_hbm.at[pl.ds(s, 1), :], dst_hbm.at[pl.ds(d, 1), :], sem)

    def issue(c):
        def body(j, carry):
            for u in range(ROW_UNROLL):
                a = c * ROW_CHUNK + j * ROW_UNROLL + u
                if scatter:
                    row_copy(jnp.where(a >= n_src, a - n_src, a), idx_ref[a]).start()
                else:
                    row_copy(idx_ref[a], a).start()
            return carry
        lax.fori_loop(0, ROW_CHUNK // ROW_UNROLL, body, 0)

    def drain():
        def body(j, carry):
            for u in range(ROW_UNROLL):
                row_copy(0, 0).wait()
            return carry
        lax.fori_loop(0, ROW_CHUNK // ROW_UNROLL, body, 0)

    issue(0)

    def step(c, carry):
        issue(c)
        drain()
        return carry

    lax.fori_loop(1, n_rows // ROW_CHUNK, step, 0)
    drain()


def _row_move(idx, src, n_out, scatter):
    n_rows = idx.shape[0]
    n_src, d = src.shape
    assert n_rows % ROW_CHUNK == 0 and (not scatter or n_rows == 2 * n_src)
    any_spec = pl.BlockSpec(memory_space=pl.ANY)
    args = [idx, src]
    if scatter:
        args.append(jnp.zeros((n_out, d), src.dtype))
    return pl.pallas_call(
        functools.partial(_row_move_kernel, n_rows=n_rows, n_src=n_src, scatter=scatter),
        out_shape=jax.ShapeDtypeStruct((n_out, d), src.dtype),
        grid_spec=pltpu.PrefetchScalarGridSpec(
            num_scalar_prefetch=1,
            grid=(1,),
            in_specs=[any_spec] * (len(args) - 1),
            out_specs=any_spec,
            scratch_shapes=[pltpu.SemaphoreType.DMA(())],
        ),
        input_output_aliases={2: 0} if scatter else {},
        compiler_params=_params("arbitrary"),
        name="moe_dispatch" if scatter else "moe_combine",
    )(*args)


def _moe_kernel(be_ref, nu_ref, x_ref, wg_ref, wu_ref, wd_ref, o_ref):
    i = pl.program_id(0)

    @pl.when(i < nu_ref[0])
    def _():
        x = x_ref[...].astype(BF16)
        act = _silu(_mm(x, wg_ref[...])) * _mm(x, wu_ref[...])
        o_ref[...] = _mm(act.astype(BF16), wd_ref[...])

    @pl.when(i >= nu_ref[0])
    def _():
        o_ref[...] = jnp.zeros(o_ref.shape, F32)


def _moe_experts(xb, blk_exp, n_used, wg, wu, wd):
    n_blk = blk_exp.shape[0]
    return pl.pallas_call(
        _moe_kernel,
        out_shape=jax.ShapeDtypeStruct((n_blk * MOE_ROWS, D_MODEL), F32),
        grid_spec=pltpu.PrefetchScalarGridSpec(
            num_scalar_prefetch=2,
            grid=(n_blk,),
            in_specs=[
                pl.BlockSpec((MOE_ROWS, D_MODEL), lambda i, be, nu: (i, 0)),
                pl.BlockSpec((None, D_MODEL, D_FF_EXP), lambda i, be, nu: (be[i], 0, 0)),
                pl.BlockSpec((None, D_MODEL, D_FF_EXP), lambda i, be, nu: (be[i], 0, 0)),
                pl.BlockSpec((None, D_FF_EXP, D_MODEL), lambda i, be, nu: (be[i], 0, 0)),
            ],
            out_specs=pl.BlockSpec((MOE_ROWS, D_MODEL), lambda i, be, nu: (i, 0)),
        ),
        compiler_params=_params("arbitrary"),
        name="moe_experts",
    )(blk_exp, n_used, xb, wg, wu, wd)


def _final_kernel(x1_ref, mod_ref, y0_ref, y1_ref, gt_ref, lng_ref, lnb_ref, o_ref):
    d = D_MODEL
    gt = gt_ref[...]
    ffn = y0_ref[...] * gt[:, 0:1] + y1_ref[...] * gt[:, 1:2]
    y = ALPHA * x1_ref[...] + mod_ref[:, 5 * d:6 * d] * ffn
    y = y - jnp.mean(y, axis=1, keepdims=True)
    o_ref[...] = (y * lax.rsqrt(jnp.mean(y * y, axis=1, keepdims=True) + EPS)) * lng_ref[...] + lnb_ref[...]


def _final(x1, mod_l, y_pair, gates_t, ln_g, ln_b, n_ctx, dec_seq):
    n_tok = x1.shape[0]
    tm = TOK_TILE
    nt = n_tok // tm
    grp = functools.partial(_mod_group, tm=tm, n_ctx=n_ctx, dec_seq=dec_seq)
    tile = pl.BlockSpec((tm, D_MODEL), lambda i: (i, 0))
    vec = pl.BlockSpec((1, D_MODEL), lambda i: (0, 0))
    return pl.pallas_call(
        _final_kernel,
        out_shape=jax.ShapeDtypeStruct((n_tok, D_MODEL), F32),
        grid=(nt,),
        in_specs=[tile, pl.BlockSpec((None, 1, ADA_DIM), lambda i: (grp(i), 0, 0)), tile,
                  pl.BlockSpec((tm, D_MODEL), lambda i: (nt + i, 0)),
                  pl.BlockSpec((tm, 2), lambda i: (i, 0)), vec, vec],
        out_specs=tile,
        compiler_params=_params("arbitrary"),
        name="final_norm",
    )(x1, mod_l, y_pair, y_pair, gates_t, ln_g, ln_b)


def _rope_tables(seq, dim, width):
    nf = dim // 4
    t = jnp.arange(seq)
    pos = jnp.stack([t // GRID_W, t % GRID_W], axis=-1).astype(F32)
    inv = ROPE_BASE ** (-jnp.arange(nf, dtype=F32) / nf)
    ang = pos[:, :, None] * inv
    cos, sin = jnp.cos(ang), jnp.sin(ang)
    zero = jnp.zeros_like(sin)
    c = jnp.stack([cos, cos], axis=2).reshape(seq, dim)
    s_lo = jnp.stack([-sin, zero], axis=2).reshape(seq, dim)
    s_hi = jnp.stack([zero, sin], axis=2).reshape(seq, dim)
    rep = width // dim
    return tuple(jnp.tile(a, (1, rep)) for a in (c, s_lo, s_hi))


def _dispatch_plan(ids, n_tok):
    n_assign = 2 * n_tok
    flat_e = ids.reshape(n_assign)
    onehot = (flat_e[:, None] == jnp.arange(N_EXP, dtype=jnp.int32)[None, :]).astype(jnp.int32)
    csum = jnp.cumsum(onehot, axis=0)
    rank = jnp.sum(onehot * (csum - 1), axis=1)
    counts = csum[-1]
    padded = (counts + MOE_ROWS - 1) // MOE_ROWS * MOE_ROWS
    pad_end = jnp.cumsum(padded)
    pad_start = pad_end - padded
    dest = (jnp.sum(onehot * pad_start[None, :], axis=1) + rank).astype(jnp.int32)
    n_blk = n_assign // MOE_ROWS + N_EXP
    blk_start = jnp.arange(n_blk, dtype=jnp.int32) * MOE_ROWS
    blk_exp = jnp.sum((blk_start[:, None] >= pad_end[None, :]).astype(jnp.int32), axis=1)
    blk_exp = jnp.minimum(blk_exp, N_EXP - 1).astype(jnp.int32)
    n_used = (pad_end[-1] // MOE_ROWS).astype(jnp.int32).reshape(1)
    return dest, blk_exp, n_used


def kernel(x_prompt, x_sample, cache_diff_k, cache_diff_v, cache_win_k, cache_win_v, state_ssd, state_ret,
           c, c_ctx, w_ada, b_ada, w_in, diff_lambda, diff_norm_g, win_sink, conv_w, conv_b,
           ssd_A_log, ssd_dt_bias, ssd_D, ssd_norm_g, ret_decay_logit, ret_norm_g, w_branch, w_out,
           ln_g, ln_b, router_w, router_b, moe_w_gate, moe_w_up, moe_w_down):
    batch, seq, d = x_prompt.shape
    dec_batch, dec_seq, _ = x_sample.shape
    past = cache_diff_k.shape[2]
    n_ctx, n_lat = batch * seq, dec_batch * dec_seq
    n_tok = n_ctx + n_lat
    assert d == D_MODEL and n_ctx % dec_seq == 0 and seq % CHUNK == 0 and dec_seq % CHUNK == 0

    x = jnp.concatenate([x_prompt.reshape(n_ctx, d), x_sample.reshape(n_lat, d)], axis=0)

    n_mod = 1 + dec_batch
    n_mod_pad = -(-n_mod // 8) * 8
    cvec = jnp.concatenate([c_ctx[None, :], c, jnp.zeros((n_mod_pad - n_mod, d), F32)], axis=0)
    mod = _ada(cvec, w_ada, b_ada)

    rope_a = _rope_tables(dec_seq, A_QK, 256)
    rope_b = _rope_tables(dec_seq, B_DIM, 256)
    rw_t = router_w.T
    rb_col = router_b.reshape(N_EXP, 1)

    ctx_out = {k: [] for k in ('diff_k', 'diff_v', 'win_k', 'win_v', 'ssd', 'ret')}
    for l in range(DEPTH):
        mod_l = mod[l, :n_mod].reshape(n_mod, 1, ADA_DIM)
        wl = w_in[l]
        w_small = jnp.concatenate(
            [wl[:, :CDT_OFF], wl[:, CDT_OFF + 8:GATE_OFF], wl[:, CDT_OFF:CDT_OFF + 8],
             jnp.zeros((d, N_SMALL - GATE_OFF), F32)], axis=1).astype(BF16)
        w_gl = wl[:, GATE_OFF:].astype(BF16)
        proj = _inproj(x, mod_l, w_small, n_ctx, dec_seq)

        lam_init = 0.8 - 0.6 * math.exp(-0.3 * l)
        lv = diff_lambda[l]
        lam = jnp.exp(jnp.sum(lv[0] * lv[1])) - jnp.exp(jnp.sum(lv[2] * lv[3])) + lam_init
        diff_scal = jnp.stack([lam, jnp.asarray(1.0 - lam_init, F32)]).astype(F32)
        g_a = diff_norm_g[l].reshape(1, A_V)
        sink = win_sink[l]
        dtb = jnp.zeros((1, 128), F32).at[0, :8].set(ssd_dt_bias[l].reshape(8))
        acoef = jnp.zeros((1, 128), F32).at[0, :8].set(-jnp.exp(ssd_A_log[l]).reshape(8))
        dvec = jnp.repeat(ssd_D[l], C_P).reshape(1, 256)
        g_c = ssd_norm_g[l].reshape(1, 256)
        log_g = jax.nn.log_sigmoid(ret_decay_logit[l]).reshape(8)
        g_d = ret_norm_g[l].reshape(1, 256)
        cw = conv_w[l]
        cb = conv_b[l].reshape(1, 512)

        oa_c = _diff_attn(proj, diff_scal, g_a, 0, batch, seq, seq)
        ob_c = _win_attn(proj, sink, 0, batch, seq, seq)
        yc_c, st_c = _ssd(proj, cw, cb, dtb, acoef, dvec, g_c, 0, batch, seq)
        od_c, rt_c = _ret(proj, log_g, g_d, 0, batch, seq)
        cache_a = (cache_diff_k[:, l].reshape(dec_batch, past, 256), cache_diff_v[:, l].reshape(dec_batch, past, 256))
        cache_b = (cache_win_k[:, l].reshape(dec_batch, past, 128), cache_win_v[:, l].reshape(dec_batch, past, 128))
        oa_l = _diff_attn(proj, diff_scal, g_a, n_ctx, dec_batch, dec_seq, BLOCK, cache=cache_a, rope=rope_a)
        ob_l = _win_attn(proj, sink, n_ctx, dec_batch, dec_seq, BLOCK, cache=cache_b, rope=rope_b)
        yc_l, _ = _ssd(proj, cw, cb, dtb, acoef, dvec, g_c, n_ctx, dec_batch, dec_seq,
                       h0=state_ssd[:, l].reshape(dec_batch, 8, C_P, C_N))
        od_l, _ = _ret(proj, log_g, g_d, n_ctx, dec_batch, dec_seq,
                       h0=state_ret[:, l].reshape(dec_batch, 8, D_V, D_K))

        br = jnp.concatenate([jnp.concatenate([oa_c, ob_c, yc_c, od_c], axis=1),
                              jnp.concatenate([oa_l, ob_l, yc_l, od_l], axis=1)], axis=0)
        x1, h2, ids, gates = _merge(
            x, mod_l, br, w_gl, w_branch[l].reshape(N_BRANCH * BRANCH, d).astype(BF16),
            w_out[l].astype(BF16), ln_g[l, 0].reshape(1, d), ln_b[l, 0].reshape(1, d), rw_t, rb_col,
            n_ctx, dec_seq)

        dest, blk_exp, n_used = _dispatch_plan(ids, n_tok)
        xb = _row_move(dest, h2, blk_exp.shape[0] * MOE_ROWS, scatter=True)
        yb = _moe_experts(xb, blk_exp, n_used, moe_w_gate[l].astype(BF16), moe_w_up[l].astype(BF16),
                          moe_w_down[l].astype(BF16))
        y_pair = _row_move(dest, yb, 2 * n_tok, scatter=False)
        x = _final(x1, mod_l, y_pair, gates.T,
                   ln_g[l, 1].reshape(1, d), ln_b[l, 1].reshape(1, d), n_ctx, dec_seq)

        pc = proj[:n_ctx]
        ctx_out['diff_k'].append(pc[:, 256:512].reshape(batch, seq, A_HEADS, 2, A_QK))
        ctx_out['diff_v'].append(pc[:, 512:768].reshape(batch, seq, A_HEADS, A_V))
        ctx_out['win_k'].append(pc[:, 1024:1152].reshape(batch, seq, B_KV, B_DIM))
        ctx_out['win_v'].append(pc[:, 1152:1280].reshape(batch, seq, B_KV, B_DIM))
        ctx_out['ssd'].append(st_c.reshape(batch, 2, C_HEADS, C_P, C_N))
        ctx_out['ret'].append(rt_c.reshape(batch, 2, D_HEADS, D_V, D_K))

    y_prompt = x[:n_ctx].reshape(batch, seq, d)
    y_sample = x[n_ctx:].reshape(dec_batch, dec_seq, d)
    stk = lambda k: jnp.stack(ctx_out[k], axis=1)
    return (y_prompt, y_sample, stk('diff_k'), stk('diff_v'), stk('win_k'), stk('win_v'), stk('ssd'), stk('ret'))
```

```python
import functools
import math

import jax
import jax.numpy as jnp
from jax import lax
from jax.experimental import pallas as pl
from jax.experimental.pallas import tpu as pltpu

F32 = jnp.float32
BF16 = jnp.bfloat16

D_MODEL = 1024
DEPTH = 4
GRID_W = 64
BLOCK = 128
WINDOW = 128
CHUNK = 128
A_HEADS, A_QK, A_V = 4, 32, 64
B_HEADS, B_KV, B_DIM = 4, 2, 64
C_HEADS, C_P, C_GROUPS, C_N = 4, 64, 2, 64
D_HEADS, D_K, D_V = 4, 32, 64
BRANCH = 256
N_BRANCH = 4
N_EXP = 16
N_EXP_GROUPS = 4
EXP_PER_GROUP = 4
D_FF_EXP = 512
ROPE_BASE = 10000.0
ALPHA = (2 * DEPTH) ** 0.25
EPS = 1e-5
ADA_DIM = 6 * D_MODEL
NEG = -1e30

N_SMALL = 23 * 128
GATE_OFF = 2824
CDT_OFF = 2048

VMEM_LIMIT = 52 * 1024 * 1024
MOE_ROWS = 256
TOK_TILE = 256

_NN = (((1,), (0,)), ((), ()))
_NT = (((1,), (1,)), ((), ()))
_TN = (((0,), (0,)), ((), ()))


def _params(*sem):
    return pltpu.CompilerParams(dimension_semantics=sem, vmem_limit_bytes=VMEM_LIMIT)


def _mm(a, b, dims=_NN):
    return lax.dot_general(a, b, dims, preferred_element_type=F32)


def _split(a):
    hi = a.astype(BF16)
    return hi, (a - hi.astype(F32)).astype(BF16)


def _mm_f32(a, b, dims=_NN):
    a_hi, a_lo = _split(a)
    b_hi, b_lo = _split(b)
    return (_mm(a_lo, b_hi, dims) + _mm(a_hi, b_lo, dims)) + _mm(a_hi, b_hi, dims)


def _mm_exact_lhs(m_bf, a):
    a1 = a.astype(BF16)
    r1 = a - a1.astype(F32)
    a2 = r1.astype(BF16)
    a3 = (r1 - a2.astype(F32)).astype(BF16)
    return (_mm(m_bf, a3) + _mm(m_bf, a2)) + _mm(m_bf, a1)


def _silu(x):
    return x * jax.nn.sigmoid(x)


LANES = 128
TOKEN_TILE_ROWS = D_MODEL // LANES


def _store_token_tiles(ref, val, lead=()):
    n = val.shape[0]
    for j in range(TOKEN_TILE_ROWS):
        ref[(*lead, pl.ds(j, n, stride=TOKEN_TILE_ROWS), slice(None))] = val[:, j * LANES:(j + 1) * LANES]


def _load_token_tiles(ref, n, lead=()):
    return jnp.concatenate(
        [ref[(*lead, pl.ds(j, n, stride=TOKEN_TILE_ROWS), slice(None))] for j in range(TOKEN_TILE_ROWS)], axis=1)


def _rope(x, c, s_lo, s_hi, shift):
    n = x.shape[1]
    return x * c + pltpu.roll(x, n - shift, 1) * s_lo + pltpu.roll(x, shift, 1) * s_hi


def _ada_kernel(c_ref, w_ref, b_ref, o_ref):
    c = c_ref[...]
    o_ref[...] = _mm_f32(_silu(c), w_ref[...]) + b_ref[...]


def _ada(cvec, w_ada, b_ada):
    rows = cvec.shape[0]
    tn = 1024
    return pl.pallas_call(
        _ada_kernel,
        out_shape=jax.ShapeDtypeStruct((DEPTH, rows, ADA_DIM), F32),
        grid=(DEPTH, ADA_DIM // tn),
        in_specs=[
            pl.BlockSpec((rows, D_MODEL), lambda l, j: (0, 0)),
            pl.BlockSpec((None, D_MODEL, tn), lambda l, j: (l, 0, j)),
            pl.BlockSpec((None, 1, tn), lambda l, j: (l, 0, j)),
        ],
        out_specs=pl.BlockSpec((None, rows, tn), lambda l, j: (l, 0, j)),
        compiler_params=_params("arbitrary", "arbitrary"),
        name="ada",
    )(cvec, w_ada, b_ada.reshape(DEPTH, 1, ADA_DIM))


def _inproj_kernel(x_ref, mod_ref, w_ref, o_ref):
    d = x_ref.shape[1]
    h = x_ref[...] * (1.0 + mod_ref[:, d:2 * d]) + mod_ref[:, 0:d]
    o_ref[...] = _mm(h.astype(BF16), w_ref[...])


def _mod_group(i, tm, n_ctx, dec_seq):
    row = i * tm
    return jnp.where(row < n_ctx, 0, 1 + lax.div(jnp.maximum(row - n_ctx, 0), dec_seq))


def _inproj(x, mod_l, w_small, n_ctx, dec_seq):
    n_tok = x.shape[0]
    tm = TOK_TILE
    grp = functools.partial(_mod_group, tm=tm, n_ctx=n_ctx, dec_seq=dec_seq)
    return pl.pallas_call(
        _inproj_kernel,
        out_shape=jax.ShapeDtypeStruct((n_tok, N_SMALL), F32),
        grid=(n_tok // tm,),
        in_specs=[
            pl.BlockSpec((tm, D_MODEL), lambda i: (i, 0)),
            pl.BlockSpec((None, 1, ADA_DIM), lambda i: (grp(i), 0, 0)),
            pl.BlockSpec((D_MODEL, N_SMALL), lambda i: (0, 0)),
        ],
        out_specs=pl.BlockSpec((tm, N_SMALL), lambda i: (i, 0)),
        compiler_params=_params("arbitrary"),
        name="inproj",
    )(x, mod_l, w_small)


def _diff_attn_kernel(*refs, latent, tq, seq, past):
    if latent:
        (sc_ref, q_ref, k_ref, v_ref, g_ref, ck_ref, cv_ref, rc_ref, rlo_ref, rhi_ref,
         o_ref, kt_scr, v_scr) = refs
    else:
        sc_ref, q_ref, k_ref, v_ref, g_ref, o_ref, kt_scr, v_scr = refs
    qi = pl.program_id(1)
    shift = A_QK // 4

    @pl.when(qi == 0)
    def _():
        k = k_ref[...]
        if latent:
            k = _rope(k, rc_ref[...], rlo_ref[...], rhi_ref[...], shift)
        kt_scr[:, 0:seq] = k.T.astype(BF16)
        v_scr[0:seq, :] = v_ref[...].astype(BF16)
        if latent:
            kt_scr[:, seq:seq + past] = ck_ref[...].T.astype(BF16)
            v_scr[seq:seq + past, :] = cv_ref[...].astype(BF16)

    q = q_ref[...]
    if latent:
        r = pl.ds(pl.multiple_of(qi * tq, tq), tq)
        q = _rope(q, rc_ref[r, :], rlo_ref[r, :], rhi_ref[r, :], shift)
    qb = q.astype(BF16)
    lam = sc_ref[0]
    post = sc_ref[1]
    scale = A_QK ** -0.5
    outs = []
    for h in range(A_HEADS):
        probs = []
        for m in range(2):
            off = (h * 2 + m) * A_QK
            s = _mm(qb[:, off:off + A_QK], kt_scr[off:off + A_QK, :]) * scale
            p = jnp.exp(s - jnp.max(s, axis=1, keepdims=True))
            probs.append(p * (1.0 / jnp.sum(p, axis=1, keepdims=True)))
        w = probs[0] - lam * probs[1]
        o = _mm(w.astype(BF16), v_scr[:, h * A_V:(h + 1) * A_V])
        n = o * lax.rsqrt(jnp.mean(o * o, axis=1, keepdims=True) + EPS)
        outs.append((n * g_ref[...]) * post)
    o_ref[...] = jnp.concatenate(outs, axis=1).astype(BF16)


def _diff_attn(proj, scal, g, row0, nb, seq, tq, cache=None, rope=None):
    latent = cache is not None
    nq = seq // tq
    rb = row0 // seq
    qb0 = row0 // tq
    past = cache[0].shape[1] if latent else 0
    in_specs = [
        pl.BlockSpec(memory_space=pltpu.SMEM),
        pl.BlockSpec((tq, 256), lambda b, i: (qb0 + b * nq + i, 0)),
        pl.BlockSpec((seq, 256), lambda b, i: (rb + b, 1)),
        pl.BlockSpec((seq, 256), lambda b, i: (rb + b, 2)),
        pl.BlockSpec((1, A_V), lambda b, i: (0, 0)),
    ]
    args = [scal, proj, proj, proj, g]
    if latent:
        in_specs += [
            pl.BlockSpec((None, past, 256), lambda b, i: (b, 0, 0)),
            pl.BlockSpec((None, past, 256), lambda b, i: (b, 0, 0)),
        ] + [pl.BlockSpec((seq, 256), lambda b, i: (0, 0))] * 3
        args += [cache[0], cache[1], *rope]
    return pl.pallas_call(
        functools.partial(_diff_attn_kernel, latent=latent, tq=tq, seq=seq, past=past),
        out_shape=jax.ShapeDtypeStruct((nb * seq, BRANCH), BF16),
        grid=(nb, nq),
        in_specs=in_specs,
        out_specs=pl.BlockSpec((tq, BRANCH), lambda b, i: (b * nq + i, 0)),
        scratch_shapes=[pltpu.VMEM((256, seq + past), BF16), pltpu.VMEM((seq + past, 256), BF16)],
        compiler_params=_params("arbitrary", "arbitrary"),
        name="diff_attn_lat" if latent else "diff_attn_ctx",
    )(*args)


def _win_attn_kernel(*refs, latent, tq, seq):
    if latent:
        (sink_ref, q_ref, k_ref, v_ref, ck_ref, cv_ref, rc_ref, rlo_ref, rhi_ref,
         o_ref, k_scr, v_scr, ck_scr, cv_scr) = refs
    else:
        sink_ref, q_ref, k_ref, v_ref, o_ref, k_scr, v_scr = refs
    qi = pl.program_id(1)
    nq = seq // tq
    shift = B_DIM // 4
    kvw = B_KV * B_DIM

    @pl.when(qi == 0)
    def _():
        k = k_ref[...]
        if latent:
            k = _rope(k, rc_ref[:, 0:kvw], rlo_ref[:, 0:kvw], rhi_ref[:, 0:kvw], shift)
            ck_scr[...] = ck_ref[...].astype(BF16)
            cv_scr[...] = cv_ref[...].astype(BF16)
        k_scr[...] = k.astype(BF16)
        v_scr[...] = v_ref[...].astype(BF16)

    q = q_ref[...]
    if latent:
        r = pl.ds(pl.multiple_of(qi * tq, tq), tq)
        q = _rope(q, rc_ref[r, :], rlo_ref[r, :], rhi_ref[r, :], shift)
        rows = [pl.ds(pl.multiple_of(j * tq, tq), tq)
                for j in (jnp.maximum(qi - 1, 0), qi, jnp.minimum(qi + 1, nq - 1))]
        kl = jnp.concatenate([k_scr[rr, :] for rr in rows], axis=0)
        vl = jnp.concatenate([v_scr[rr, :] for rr in rows], axis=0)
        ii = lax.broadcasted_iota(jnp.int32, (tq, 3 * tq), 0)
        jj = lax.broadcasted_iota(jnp.int32, (tq, 3 * tq), 1)
        lo = jnp.where(qi > 0, 0, tq)
        hi = jnp.where(qi < nq - 1, 3 * tq, 2 * tq)
        valid = (jnp.abs(jj - tq - ii) <= WINDOW) & (jj >= lo) & (jj < hi)
    else:
        kl = k_scr[...]
        vl = v_scr[...]
    qb = q.astype(BF16)
    scale = B_DIM ** -0.5
    ratio = B_HEADS // B_KV
    outs = []
    for h in range(B_HEADS):
        gsl = slice((h // ratio) * B_DIM, (h // ratio + 1) * B_DIM)
        qh = qb[:, h * B_DIM:(h + 1) * B_DIM]
        snk = sink_ref[h]
        s = _mm(qh, kl[:, gsl], _NT) * scale
        if latent:
            s = jnp.where(valid, s, NEG)
            sc = _mm(qh, ck_scr[:, gsl], _NT) * scale
            m = jnp.maximum(jnp.maximum(jnp.max(s, axis=1, keepdims=True),
                                        jnp.max(sc, axis=1, keepdims=True)), snk)
            pc = jnp.exp(sc - m)
        else:
            m = jnp.maximum(jnp.max(s, axis=1, keepdims=True), snk)
        p = jnp.exp(s - m)
        den = jnp.sum(p, axis=1, keepdims=True) + jnp.exp(snk - m)
        if latent:
            den = den + jnp.sum(pc, axis=1, keepdims=True)
        inv = 1.0 / den
        o = _mm((p * inv).astype(BF16), vl[:, gsl])
        if latent:
            o = o + _mm((pc * inv).astype(BF16), cv_scr[:, gsl])
        outs.append(o)
    o_ref[...] = jnp.concatenate(outs, axis=1).astype(BF16)


def _win_attn(proj, sink, row0, nb, seq, tq, cache=None, rope=None):
    latent = cache is not None
    nq = seq // tq
    rb = row0 // seq
    qb0 = row0 // tq
    kvw = B_KV * B_DIM
    in_specs = [
        pl.BlockSpec(memory_space=pltpu.SMEM),
        pl.BlockSpec((tq, 256), lambda b, i: (qb0 + b * nq + i, 3)),
        pl.BlockSpec((seq, kvw), lambda b, i: (rb + b, 8)),
        pl.BlockSpec((seq, kvw), lambda b, i: (rb + b, 9)),
    ]
    args = [sink, proj, proj, proj]
    scratch = [pltpu.VMEM((seq, kvw), BF16), pltpu.VMEM((seq, kvw), BF16)]
    if latent:
        past = cache[0].shape[1]
        in_specs += [
            pl.BlockSpec((None, past, kvw), lambda b, i: (b, 0, 0)),
            pl.BlockSpec((None, past, kvw), lambda b, i: (b, 0, 0)),
        ] + [pl.BlockSpec((seq, 256), lambda b, i: (0, 0))] * 3
        args += [cache[0], cache[1], *rope]
        scratch += [pltpu.VMEM((past, kvw), BF16), pltpu.VMEM((past, kvw), BF16)]
    return pl.pallas_call(
        functools.partial(_win_attn_kernel, latent=latent, tq=tq, seq=seq),
        out_shape=jax.ShapeDtypeStruct((nb * seq, BRANCH), BF16),
        grid=(nb, nq),
        in_specs=in_specs,
        out_specs=pl.BlockSpec((tq, BRANCH), lambda b, i: (b * nq + i, 0)),
        scratch_shapes=scratch,
        compiler_params=_params("arbitrary", "arbitrary"),
        name="win_attn_lat" if latent else "win_attn_ctx",
    )(*args)


def _conv_silu(u, w, b):
    n = u.shape[0]
    rows = lax.broadcasted_iota(jnp.int32, u.shape, 0)
    up = jnp.where(rows == 0, 0.0, pltpu.roll(u, 1, 0))
    un = jnp.where(rows == n - 1, 0.0, pltpu.roll(u, n - 1, 0))
    return _silu(up * w[0:1, :] + u * w[1:2, :] + un * w[2:3, :] + b)


def _ssd_kernel(*refs, seq, has_h0):
    if has_h0:
        (cx_ref, cz_ref, cbc_ref, cdt_ref, cw_ref, cb_ref, dtb_ref, ac_ref, dv_ref, g_ref, h0_ref,
         y_ref, st_ref, xs, bcs, dts, ybuf, hs) = refs
    else:
        (cx_ref, cz_ref, cbc_ref, cdt_ref, cw_ref, cb_ref, dtb_ref, ac_ref, dv_ref, g_ref,
         y_ref, st_ref, xs, bcs, dts, ybuf, hs) = refs
    nc = seq // CHUNK
    xw = C_HEADS * C_P
    xs[...] = _conv_silu(cx_ref[...], cw_ref[:, 0:xw], cb_ref[:, 0:xw])
    bcs[...] = _conv_silu(cbc_ref[...], cw_ref[:, xw:2 * xw], cb_ref[:, xw:2 * xw])
    z = cdt_ref[...] + dtb_ref[...]
    dts[...] = jnp.maximum(z, 0.0) + jnp.log1p(jnp.exp(-jnp.abs(z)))
    if has_h0:
        hs[...] = h0_ref[...]
    else:
        hs[...] = jnp.zeros(hs.shape, F32)

    ri = lax.broadcasted_iota(jnp.int32, (CHUNK, CHUNK), 0)
    ci = lax.broadcasted_iota(jnp.int32, (CHUNK, CHUNK), 1)
    gw = C_GROUPS * C_N

    def chunk(c, d):
        r = pl.ds(pl.multiple_of(c * CHUNK, CHUNK), CHUNK)
        tri = (ri >= ci) if d == 0 else (ci >= ri)
        dt = dts[r, :]
        cs = _mm_exact_lhs(tri.astype(BF16), dt * ac_ref[...])
        cst = cs.T
        x = xs[r, :]
        bc = bcs[r, :]
        ys = []
        for g in range(C_GROUPS):
            bm = bc[:, g * C_N:(g + 1) * C_N]
            cb = bc[:, gw + g * C_N:gw + (g + 1) * C_N].astype(BF16)
            gram = _mm(cb, bm.astype(BF16), _NT)
            for hh in range(C_HEADS // C_GROUPS):
                h = g * (C_HEADS // C_GROUPS) + hh
                col = d * C_HEADS + h
                xb = (x[:, h * C_P:(h + 1) * C_P] * dt[:, col:col + 1]).astype(BF16)
                cc = cs[:, col:col + 1]
                dec = jnp.exp(jnp.where(tri, cc - cst[col:col + 1, :], NEG))
                tot = cc[CHUNK - 1:CHUNK, :] if d == 0 else cc[0:1, :]
                hin = hs[col]
                y = _mm((gram * dec).astype(BF16), xb)
                y = y + _mm(cb, hin.astype(BF16), _NT) * jnp.exp(cc)
                bd = (bm * jnp.exp(tot - cc)).astype(BF16)
                hs[col] = hin * jnp.exp(tot) + _mm(xb, bd, _TN)
                ys.append(y)
        return r, jnp.concatenate(ys, axis=1)

    def fwd(c, carry):
        r, y = chunk(c, 0)
        ybuf[r, :] = y
        return carry

    lax.fori_loop(0, nc, fwd, 0)

    def bwd(i, carry):
        r, y = chunk(nc - 1 - i, 1)
        y = (ybuf[r, :] + y) + xs[r, :] * dv_ref[...]
        y = y * _silu(cz_ref[r, :])
        gl = xw // C_GROUPS
        parts = []
        for g in range(C_GROUPS):
            seg = y[:, g * gl:(g + 1) * gl]
            parts.append(seg * lax.rsqrt(jnp.mean(seg * seg, axis=1, keepdims=True) + EPS))
        y_ref[r, :] = (jnp.concatenate(parts, axis=1) * g_ref[...]).astype(BF16)
        return carry

    lax.fori_loop(0, nc, bwd, 0)
    st_ref[...] = hs[...]


def _ssd(proj, conv_w, conv_b, dtb, acoef, dvec, g, row0, nb, seq, h0=None):
    has_h0 = h0 is not None
    rb = row0 // seq
    nst = 2 * C_HEADS
    in_specs = [
        pl.BlockSpec((seq, 256), lambda b: (rb + b, 5)),
        pl.BlockSpec((seq, 256), lambda b: (rb + b, 6)),
        pl.BlockSpec((seq, 256), lambda b: (rb + b, 7)),
        pl.BlockSpec((seq, 128), lambda b: (rb + b, 22)),
        pl.BlockSpec((3, 512), lambda b: (0, 0)),
        pl.BlockSpec((1, 512), lambda b: (0, 0)),
        pl.BlockSpec((1, 128), lambda b: (0, 0)),
        pl.BlockSpec((1, 128), lambda b: (0, 0)),
        pl.BlockSpec((1, 256), lambda b: (0, 0)),
        pl.BlockSpec((1, 256), lambda b: (0, 0)),
    ]
    args = [proj, proj, proj, proj, conv_w, conv_b, dtb, acoef, dvec, g]
    if has_h0:
        in_specs.append(pl.BlockSpec((None, nst, C_P, C_N), lambda b: (b, 0, 0, 0)))
        args.append(h0)
    return pl.pallas_call(
        functools.partial(_ssd_kernel, seq=seq, has_h0=has_h0),
        out_shape=(jax.ShapeDtypeStruct((nb * seq, BRANCH), BF16),
                   jax.ShapeDtypeStruct((nb, nst, C_P, C_N), F32)),
        grid=(nb,),
        in_specs=in_specs,
        out_specs=(pl.BlockSpec((seq, BRANCH), lambda b: (b, 0)),
                   pl.BlockSpec((None, nst, C_P, C_N), lambda b: (b, 0, 0, 0))),
        scratch_shapes=[pltpu.VMEM((seq, 256), F32), pltpu.VMEM((seq, 256), F32),
                        pltpu.VMEM((seq, 128), F32), pltpu.VMEM((seq, 256), F32),
                        pltpu.VMEM((nst, C_P, C_N), F32)],
        compiler_params=_params("arbitrary"),
        name="ssd_lat" if has_h0 else "ssd_ctx",
    )(*args)


def _ret_kernel(*refs, seq, has_h0):
    if has_h0:
        lg_ref, q_ref, k_ref, v_ref, gt_ref, g_ref, h0_ref, y_ref, st_ref, ybuf, hs = refs
    else:
        lg_ref, q_ref, k_ref, v_ref, gt_ref, g_ref, y_ref, st_ref, ybuf, hs = refs
    nc = seq // CHUNK
    if has_h0:
        hs[...] = h0_ref[...]
    else:
        hs[...] = jnp.zeros(hs.shape, F32)
    ri = lax.broadcasted_iota(jnp.int32, (CHUNK, CHUNK), 0)
    ci = lax.broadcasted_iota(jnp.int32, (CHUNK, CHUNK), 1)
    pos = lax.broadcasted_iota(jnp.int32, (CHUNK, 1), 0).astype(F32)
    kscale = D_K ** -0.5

    def chunk(c, d):
        r = pl.ds(pl.multiple_of(c * CHUNK, CHUNK), CHUNK)
        q = q_ref[r, :]
        k = k_ref[r, :] * kscale
        v = v_ref[r, :]
        if d == 0:
            tri, dist = ri >= ci, (ri - ci).astype(F32)
            steps_in, steps_out = pos + 1.0, (CHUNK - 1.0) - pos
        else:
            tri, dist = ci >= ri, (ci - ri).astype(F32)
            steps_in, steps_out = CHUNK - pos, pos
        ys = []
        for h in range(D_HEADS):
            col = d * D_HEADS + h
            lg = lg_ref[col]
            dec = jnp.exp(jnp.where(tri, dist * lg, NEG))
            e_in = jnp.exp(steps_in * lg)
            e_tot = e_in[CHUNK - 1:CHUNK, :] if d == 0 else e_in[0:1, :]
            qb = q[:, h * D_K:(h + 1) * D_K].astype(BF16)
            km = k[:, h * D_K:(h + 1) * D_K]
            vb = v[:, h * D_V:(h + 1) * D_V].astype(BF16)
            hin = hs[col]
            y = _mm((_mm(qb, km.astype(BF16), _NT) * dec).astype(BF16), vb)
            y = y + _mm(qb, hin.astype(BF16), _NT) * e_in
            bd = (km * jnp.exp(steps_out * lg)).astype(BF16)
            hs[col] = hin * e_tot + _mm(vb, bd, _TN)
            ys.append(y)
        return r, ys

    def fwd(c, carry):
        r, ys = chunk(c, 0)
        ybuf[r, :] = jnp.concatenate(ys, axis=1)
        return carry

    lax.fori_loop(0, nc, fwd, 0)

    def bwd(i, carry):
        r, ys = chunk(nc - 1 - i, 1)
        yf = ybuf[r, :]
        parts = []
        for h in range(D_HEADS):
            o = yf[:, h * D_V:(h + 1) * D_V] + ys[h]
            o = o - jnp.mean(o, axis=1, keepdims=True)
            parts.append(o * lax.rsqrt(jnp.mean(o * o, axis=1, keepdims=True) + EPS))
        y = (jnp.concatenate(parts, axis=1) * g_ref[...]) * _silu(gt_ref[r, :])
        y_ref[r, :] = y.astype(BF16)
        return carry

    lax.fori_loop(0, nc, bwd, 0)
    st_ref[...] = hs[...]


def _ret(proj, log_g, g, row0, nb, seq, h0=None):
    has_h0 = h0 is not None
    rb = row0 // seq
    nst = 2 * D_HEADS
    in_specs = [
        pl.BlockSpec(memory_space=pltpu.SMEM),
        pl.BlockSpec((seq, 128), lambda b: (rb + b, 16)),
        pl.BlockSpec((seq, 128), lambda b: (rb + b, 17)),
        pl.BlockSpec((seq, 256), lambda b: (rb + b, 9)),
        pl.BlockSpec((seq, 256), lambda b: (rb + b, 10)),
        pl.BlockSpec((1, 256), lambda b: (0, 0)),
    ]
    args = [log_g, proj, proj, proj, proj, g]
    if has_h0:
        in_specs.append(pl.BlockSpec((None, nst, D_V, D_K), lambda b: (b, 0, 0, 0)))
        args.append(h0)
    return pl.pallas_call(
        functools.partial(_ret_kernel, seq=seq, has_h0=has_h0),
        out_shape=(jax.ShapeDtypeStruct((nb * seq, BRANCH), BF16),
                   jax.ShapeDtypeStruct((nb, nst, D_V, D_K), F32)),
        grid=(nb,),
        in_specs=in_specs,
        out_specs=(pl.BlockSpec((seq, BRANCH), lambda b: (b, 0)),
                   pl.BlockSpec((None, nst, D_V, D_K), lambda b: (b, 0, 0, 0))),
        scratch_shapes=[pltpu.VMEM((seq, 256), F32), pltpu.VMEM((nst, D_V, D_K), F32)],
        compiler_params=_params("arbitrary"),
        name="ret_lat" if has_h0 else "ret_ctx",
    )(*args)


def _route(sel, s):
    row = lambda a, e: a[e:e + 1, :]
    best = None
    grp = None
    for g in range(N_EXP_GROUPS):
        vals = [row(sel, g * EXP_PER_GROUP + j) for j in range(EXP_PER_GROUP)]
        score = None
        for a in range(EXP_PER_GROUP):
            for b in range(a + 1, EXP_PER_GROUP):
                pair = vals[a] + vals[b]
                score = pair if score is None else jnp.maximum(score, pair)
        if best is None:
            best, grp = score, jnp.zeros(score.shape, jnp.int32)
        else:
            better = score > best
            best = jnp.where(better, score, best)
            grp = jnp.where(better, g, grp)

    def pick(a, j):
        out = row(a, j)
        for g in range(1, N_EXP_GROUPS):
            out = jnp.where(grp == g, row(a, g * EXP_PER_GROUP + j), out)
        return out

    cand = [pick(sel, j) for j in range(EXP_PER_GROUP)]
    aff = [pick(s, j) for j in range(EXP_PER_GROUP)]

    def arg_first_max(vals):
        top, idx = vals[0], jnp.zeros(vals[0].shape, jnp.int32)
        for j in range(1, len(vals)):
            better = vals[j] > top
            top = jnp.where(better, vals[j], top)
            idx = jnp.where(better, j, idx)
        return idx

    def take(vals, idx):
        out = vals[0]
        for j in range(1, len(vals)):
            out = jnp.where(idx == j, vals[j], out)
        return out

    i1 = arg_first_max(cand)
    i2 = arg_first_max([jnp.where(i1 == j, -jnp.inf, cand[j]) for j in range(EXP_PER_GROUP)])
    w1, w2 = take(aff, i1), take(aff, i2)
    tot = w1 + w2
    ids = jnp.concatenate([grp * EXP_PER_GROUP + i1, grp * EXP_PER_GROUP + i2], axis=0)
    gates = jnp.concatenate([w1 / tot, w2 / tot], axis=0)
    return ids, gates


def _merge_kernel(x_ref, mod_ref, br_ref, wgl_ref, wbr_ref, wout_ref, lng_ref, lnb_ref,
                  rw_ref, rb_ref, x1_ref, h2_ref, ids_ref, gates_ref):
    d = D_MODEL
    x = x_ref[...]
    hb = (x * (1.0 + mod_ref[:, d:2 * d]) + mod_ref[:, 0:d]).astype(BF16)
    br = br_ref[...]
    merged = None
    for k in range(N_BRANCH):
        gate = jax.nn.sigmoid(_mm(hb, wgl_ref[:, k * d:(k + 1) * d]))
        up = _mm(br[:, k * BRANCH:(k + 1) * BRANCH], wbr_ref[k * BRANCH:(k + 1) * BRANCH, :])
        merged = gate * up if merged is None else merged + gate * up
    mix = _mm(merged.astype(BF16), wout_ref[...])
    y = ALPHA * x + mod_ref[:, 2 * d:3 * d] * mix
    y = y - jnp.mean(y, axis=1, keepdims=True)
    x1 = (y * lax.rsqrt(jnp.mean(y * y, axis=1, keepdims=True) + EPS)) * lng_ref[...] + lnb_ref[...]
    x1_ref[...] = x1
    h2 = x1 * (1.0 + mod_ref[:, 4 * d:5 * d]) + mod_ref[:, 3 * d:4 * d]
    _store_token_tiles(h2_ref, h2)
    s = jax.nn.sigmoid(_mm_f32(rw_ref[...], h2, _NT))
    ids, gates = _route(s + rb_ref[...], s)
    ids_ref[...] = ids
    gates_ref[...] = gates


def _merge(x, mod_l, br, w_gl, w_br, w_out, ln_g, ln_b, rw_t, rb, n_ctx, dec_seq):
    n_tok = x.shape[0]
    tm = TOK_TILE
    grp = functools.partial(_mod_group, tm=tm, n_ctx=n_ctx, dec_seq=dec_seq)
    full = lambda shape: pl.BlockSpec(shape, lambda i: (0,) * len(shape))
    return pl.pallas_call(
        _merge_kernel,
        out_shape=(jax.ShapeDtypeStruct((n_tok, D_MODEL), F32),
                   jax.ShapeDtypeStruct((n_tok * TOKEN_TILE_ROWS, LANES), F32),
                   jax.ShapeDtypeStruct((2, n_tok), jnp.int32),
                   jax.ShapeDtypeStruct((2, n_tok), F32)),
        grid=(n_tok // tm,),
        in_specs=[
            pl.BlockSpec((tm, D_MODEL), lambda i: (i, 0)),
            pl.BlockSpec((None, 1, ADA_DIM), lambda i: (grp(i), 0, 0)),
            pl.BlockSpec((tm, N_BRANCH * BRANCH), lambda i: (i, 0)),
            full((D_MODEL, N_BRANCH * D_MODEL)),
            full((N_BRANCH * BRANCH, D_MODEL)),
            full((D_MODEL, D_MODEL)),
            full((1, D_MODEL)),
            full((1, D_MODEL)),
            full((N_EXP, D_MODEL)),
            full((N_EXP, 1)),
        ],
        out_specs=(pl.BlockSpec((tm, D_MODEL), lambda i: (i, 0)),
                   pl.BlockSpec((tm * TOKEN_TILE_ROWS, LANES), lambda i: (i, 0)),
                   pl.BlockSpec((2, tm), lambda i: (0, i)),
                   pl.BlockSpec((2, tm), lambda i: (0, i))),
        compiler_params=_params("arbitrary"),
        name="merge",
    )(x, mod_l, br, w_gl, w_br, w_out, ln_g, ln_b, rw_t, rb)


def _moe_kernel(be_ref, nu_ref, src_ref, dst_ref, h_hbm, wg_ref, wu_ref, wd_ref, y_hbm,
                xbuf, ybuf, gsem, ssem):
    del be_ref
    i = pl.program_id(0)
    n_used = nu_ref[0]
    rows = MOE_ROWS
    tr = TOKEN_TILE_ROWS

    def gather_copy(tok, r, slot):
        return pltpu.make_async_copy(h_hbm.at[pl.ds(tok * tr, tr), :],
                                     xbuf.at[slot, pl.ds(r * tr, tr), :], gsem.at[slot])

    def scatter_copy(r, dst, slot):
        return pltpu.make_async_copy(ybuf.at[slot, pl.ds(r * tr, tr), :],
                                     y_hbm.at[pl.ds(dst * tr, tr), :], ssem.at[slot])

    def start_gather(blk, slot):
        for r in range(rows):
            gather_copy(src_ref[blk * rows + r], r, slot).start()

    def wait_gather(slot):
        for r in range(rows):
            gather_copy(0, r, slot).wait()

    def start_scatter(blk, slot):
        for r in range(rows):
            scatter_copy(r, dst_ref[blk * rows + r], slot).start()

    def wait_scatter(slot):
        for r in range(rows):
            scatter_copy(r, 0, slot).wait()

    @pl.when(i == 0)
    def _():
        start_gather(0, 0)

    @pl.when(i < n_used)
    def _():
        slot = lax.rem(i, 2)
        other = 1 - slot
        wait_gather(slot)

        @pl.when(i >= 2)
        def _():
            wait_scatter(slot)

        start_gather(jnp.minimum(i + 1, n_used - 1), other)
        x = _load_token_tiles(xbuf, rows, lead=(slot,)).astype(BF16)
        act = _silu(_mm(x, wg_ref[...])) * _mm(x, wu_ref[...])
        _store_token_tiles(ybuf, _mm(act.astype(BF16), wd_ref[...]), lead=(slot,))
        start_scatter(i, slot)

        @pl.when(i == n_used - 1)
        def _():
            wait_gather(other)
            wait_scatter(slot)

            @pl.when(i >= 1)
            def _():
                wait_scatter(other)


def _moe_experts(h2_tiles, slot_tok, slot_dst, blk_exp, n_used, wg, wu, wd, n_out_tok):
    n_blk = blk_exp.shape[0]
    tr = TOKEN_TILE_ROWS
    any_spec = pl.BlockSpec(memory_space=pl.ANY)
    return pl.pallas_call(
        _moe_kernel,
        out_shape=jax.ShapeDtypeStruct((n_out_tok * tr, LANES), F32),
        grid_spec=pltpu.PrefetchScalarGridSpec(
            num_scalar_prefetch=4,
            grid=(n_blk,),
            in_specs=[
                any_spec,
                pl.BlockSpec((None, D_MODEL, D_FF_EXP), lambda i, be, nu, s, t: (be[i], 0, 0)),
                pl.BlockSpec((None, D_MODEL, D_FF_EXP), lambda i, be, nu, s, t: (be[i], 0, 0)),
                pl.BlockSpec((None, D_FF_EXP, D_MODEL), lambda i, be, nu, s, t: (be[i], 0, 0)),
            ],
            out_specs=any_spec,
            scratch_shapes=[pltpu.VMEM((2, MOE_ROWS * tr, LANES), F32),
                            pltpu.VMEM((2, MOE_ROWS * tr, LANES), F32),
                            pltpu.SemaphoreType.DMA((2,)), pltpu.SemaphoreType.DMA((2,))],
        ),
        compiler_params=_params("arbitrary"),
        name="moe_experts",
    )(blk_exp, n_used, slot_tok, slot_dst, h2_tiles, wg, wu, wd)


def _final_kernel(x1_ref, mod_ref, y0_ref, y1_ref, gt_ref, lng_ref, lnb_ref, o_ref):
    d = D_MODEL
    gt = gt_ref[...]
    n = x1_ref.shape[0]
    ffn = _load_token_tiles(y0_ref, n) * gt[:, 0:1] + _load_token_tiles(y1_ref, n) * gt[:, 1:2]
    y = ALPHA * x1_ref[...] + mod_ref[:, 5 * d:6 * d] * ffn
    y = y - jnp.mean(y, axis=1, keepdims=True)
    o_ref[...] = (y * lax.rsqrt(jnp.mean(y * y, axis=1, keepdims=True) + EPS)) * lng_ref[...] + lnb_ref[...]


def _final(x1, mod_l, y_pair, gates_t, ln_g, ln_b, n_ctx, dec_seq):
    n_tok = x1.shape[0]
    tm = TOK_TILE
    nt = n_tok // tm
    grp = functools.partial(_mod_group, tm=tm, n_ctx=n_ctx, dec_seq=dec_seq)
    tile = pl.BlockSpec((tm, D_MODEL), lambda i: (i, 0))
    vec = pl.BlockSpec((1, D_MODEL), lambda i: (0, 0))
    return pl.pallas_call(
        _final_kernel,
        out_shape=jax.ShapeDtypeStruct((n_tok, D_MODEL), F32),
        grid=(nt,),
        in_specs=[tile, pl.BlockSpec((None, 1, ADA_DIM), lambda i: (grp(i), 0, 0)),
                  pl.BlockSpec((tm * TOKEN_TILE_ROWS, LANES), lambda i: (i, 0)),
                  pl.BlockSpec((tm * TOKEN_TILE_ROWS, LANES), lambda i: (nt + i, 0)),
                  pl.BlockSpec((tm, 2), lambda i: (i, 0)), vec, vec],
        out_specs=tile,
        compiler_params=_params("arbitrary"),
        name="final_norm",
    )(x1, mod_l, y_pair, y_pair, gates_t, ln_g, ln_b)


def _rope_tables(seq, dim, width):
    nf = dim // 4
    t = jnp.arange(seq)
    pos = jnp.stack([t // GRID_W, t % GRID_W], axis=-1).astype(F32)
    inv = ROPE_BASE ** (-jnp.arange(nf, dtype=F32) / nf)
    ang = pos[:, :, None] * inv
    cos, sin = jnp.cos(ang), jnp.sin(ang)
    zero = jnp.zeros_like(sin)
    c = jnp.stack([cos, cos], axis=2).reshape(seq, dim)
    s_lo = jnp.stack([-sin, zero], axis=2).reshape(seq, dim)
    s_hi = jnp.stack([zero, sin], axis=2).reshape(seq, dim)
    rep = width // dim
    return tuple(jnp.tile(a, (1, rep)) for a in (c, s_lo, s_hi))


def _dispatch_plan(ids, n_tok):
    n_assign = 2 * n_tok
    flat_e = ids.reshape(n_assign)
    onehot = (flat_e[:, None] == jnp.arange(N_EXP, dtype=jnp.int32)[None, :]).astype(jnp.int32)
    csum = jnp.cumsum(onehot, axis=0)
    rank = jnp.sum(onehot * (csum - 1), axis=1)
    counts = csum[-1]
    padded = (counts + MOE_ROWS - 1) // MOE_ROWS * MOE_ROWS
    pad_end = jnp.cumsum(padded)
    pad_start = pad_end - padded
    dest = (jnp.sum(onehot * pad_start[None, :], axis=1) + rank).astype(jnp.int32)
    n_blk = n_assign // MOE_ROWS + N_EXP
    blk_start = jnp.arange(n_blk, dtype=jnp.int32) * MOE_ROWS
    blk_exp = jnp.sum((blk_start[:, None] >= pad_end[None, :]).astype(jnp.int32), axis=1)
    blk_exp = jnp.minimum(blk_exp, N_EXP - 1).astype(jnp.int32)
    n_used = (pad_end[-1] // MOE_ROWS).astype(jnp.int32).reshape(1)
    n_slots = n_blk * MOE_ROWS
    slot_a = jnp.full((n_slots,), -1, jnp.int32).at[dest].set(jnp.arange(n_assign, dtype=jnp.int32))
    real = slot_a >= 0
    slot_tok = jnp.where(real, jnp.where(slot_a >= n_tok, slot_a - n_tok, slot_a), 0)
    slot_dst = jnp.where(real, slot_a, n_assign + jnp.arange(n_slots, dtype=jnp.int32))
    return slot_tok, slot_dst, blk_exp, n_used


def kernel(x_prompt, x_sample, cache_diff_k, cache_diff_v, cache_win_k, cache_win_v, state_ssd, state_ret,
           c, c_ctx, w_ada, b_ada, w_in, diff_lambda, diff_norm_g, win_sink, conv_w, conv_b,
           ssd_A_log, ssd_dt_bias, ssd_D, ssd_norm_g, ret_decay_logit, ret_norm_g, w_branch, w_out,
           ln_g, ln_b, router_w, router_b, moe_w_gate, moe_w_up, moe_w_down):
    batch, seq, d = x_prompt.shape
    dec_batch, dec_seq, _ = x_sample.shape
    past = cache_diff_k.shape[2]
    n_ctx, n_lat = batch * seq, dec_batch * dec_seq
    n_tok = n_ctx + n_lat
    assert d == D_MODEL and n_ctx % dec_seq == 0 and seq % CHUNK == 0 and dec_seq % CHUNK == 0

    x = jnp.concatenate([x_prompt.reshape(n_ctx, d), x_sample.reshape(n_lat, d)], axis=0)

    n_mod = 1 + dec_batch
    n_mod_pad = -(-n_mod // 8) * 8
    cvec = jnp.concatenate([c_ctx[None, :], c, jnp.zeros((n_mod_pad - n_mod, d), F32)], axis=0)
    mod = _ada(cvec, w_ada, b_ada)

    rope_a = _rope_tables(dec_seq, A_QK, 256)
    rope_b = _rope_tables(dec_seq, B_DIM, 256)
    rw_t = router_w.T
    rb_col = router_b.reshape(N_EXP, 1)

    ctx_out = {k: [] for k in ('diff_k', 'diff_v', 'win_k', 'win_v', 'ssd', 'ret')}
    for l in range(DEPTH):
        mod_l = mod[l, :n_mod].reshape(n_mod, 1, ADA_DIM)
        wl = w_in[l]
        w_small = jnp.concatenate(
            [wl[:, :CDT_OFF], wl[:, CDT_OFF + 8:GATE_OFF], wl[:, CDT_OFF:CDT_OFF + 8],
             jnp.zeros((d, N_SMALL - GATE_OFF), F32)], axis=1).astype(BF16)
        w_gl = wl[:, GATE_OFF:].astype(BF16)
        proj = _inproj(x, mod_l, w_small, n_ctx, dec_seq)

        lam_init = 0.8 - 0.6 * math.exp(-0.3 * l)
        lv = diff_lambda[l]
        lam = jnp.exp(jnp.sum(lv[0] * lv[1])) - jnp.exp(jnp.sum(lv[2] * lv[3])) + lam_init
        diff_scal = jnp.stack([lam, jnp.asarray(1.0 - lam_init, F32)]).astype(F32)
        g_a = diff_norm_g[l].reshape(1, A_V)
        sink = win_sink[l]
        dtb = jnp.zeros((1, 128), F32).at[0, :8].set(ssd_dt_bias[l].reshape(8))
        acoef = jnp.zeros((1, 128), F32).at[0, :8].set(-jnp.exp(ssd_A_log[l]).reshape(8))
        dvec = jnp.repeat(ssd_D[l], C_P).reshape(1, 256)
        g_c = ssd_norm_g[l].reshape(1, 256)
        log_g = jax.nn.log_sigmoid(ret_decay_logit[l]).reshape(8)
        g_d = ret_norm_g[l].reshape(1, 256)
        cw = conv_w[l]
        cb = conv_b[l].reshape(1, 512)

        oa_c = _diff_attn(proj, diff_scal, g_a, 0, batch, seq, seq)
        ob_c = _win_attn(proj, sink, 0, batch, seq, seq)
        yc_c, st_c = _ssd(proj, cw, cb, dtb, acoef, dvec, g_c, 0, batch, seq)
        od_c, rt_c = _ret(proj, log_g, g_d, 0, batch, seq)
        cache_a = (cache_diff_k[:, l].reshape(dec_batch, past, 256), cache_diff_v[:, l].reshape(dec_batch, past, 256))
        cache_b = (cache_win_k[:, l].reshape(dec_batch, past, 128), cache_win_v[:, l].reshape(dec_batch, past, 128))
        oa_l = _diff_attn(proj, diff_scal, g_a, n_ctx, dec_batch, dec_seq, BLOCK, cache=cache_a, rope=rope_a)
        ob_l = _win_attn(proj, sink, n_ctx, dec_batch, dec_seq, BLOCK, cache=cache_b, rope=rope_b)
        yc_l, _ = _ssd(proj, cw, cb, dtb, acoef, dvec, g_c, n_ctx, dec_batch, dec_seq,
                       h0=state_ssd[:, l].reshape(dec_batch, 8, C_P, C_N))
        od_l, _ = _ret(proj, log_g, g_d, n_ctx, dec_batch, dec_seq,
                       h0=state_ret[:, l].reshape(dec_batch, 8, D_V, D_K))

        br = jnp.concatenate([jnp.concatenate([oa_c, ob_c, yc_c, od_c], axis=1),
                              jnp.concatenate([oa_l, ob_l, yc_l, od_l], axis=1)], axis=0)
        x1, h2, ids, gates = _merge(
            x, mod_l, br, w_gl, w_branch[l].reshape(N_BRANCH * BRANCH, d).astype(BF16),
            w_out[l].astype(BF16), ln_g[l, 0].reshape(1, d), ln_b[l, 0].reshape(1, d), rw_t, rb_col,
            n_ctx, dec_seq)

        slot_tok, slot_dst, blk_exp, n_used = _dispatch_plan(ids, n_tok)
        y_pair = _moe_experts(h2, slot_tok, slot_dst, blk_exp, n_used, moe_w_gate[l].astype(BF16),
                              moe_w_up[l].astype(BF16), moe_w_down[l].astype(BF16),
                              2 * n_tok + slot_tok.shape[0])
        x = _final(x1, mod_l, y_pair, gates.T,
                   ln_g[l, 1].reshape(1, d), ln_b[l, 1].reshape(1, d), n_ctx, dec_seq)

        pc = proj[:n_ctx]
        ctx_out['diff_k'].append(pc[:, 256:512].reshape(batch, seq, A_HEADS, 2, A_QK))
        ctx_out['diff_v'].append(pc[:, 512:768].reshape(batch, seq, A_HEADS, A_V))
        ctx_out['win_k'].append(pc[:, 1024:1152].reshape(batch, seq, B_KV, B_DIM))
        ctx_out['win_v'].append(pc[:, 1152:1280].reshape(batch, seq, B_KV, B_DIM))
        ctx_out['ssd'].append(st_c.reshape(batch, 2, C_HEADS, C_P, C_N))
        ctx_out['ret'].append(rt_c.reshape(batch, 2, D_HEADS, D_V, D_K))

    y_prompt = x[:n_ctx].reshape(batch, seq, d)
    y_sample = x[n_ctx:].reshape(dec_batch, dec_seq, d)
    stk = lambda k: jnp.stack(ctx_out[k], axis=1)
    return (y_prompt, y_sample, stk('diff_k'), stk('diff_v'), stk('win_k'), stk('win_v'), stk('ssd'), stk('ret'))
```

```python
import functools
import math

import jax
import jax.numpy as jnp
from jax import lax
from jax.experimental import pallas as pl
from jax.experimental.pallas import tpu as pltpu

F32 = jnp.float32
BF16 = jnp.bfloat16

D_MODEL = 1024
DEPTH = 4
GRID_W = 64
BLOCK = 128
WINDOW = 128
CHUNK = 128
A_HEADS, A_QK, A_V = 4, 32, 64
B_HEADS, B_KV, B_DIM = 4, 2, 64
C_HEADS, C_P, C_GROUPS, C_N = 4, 64, 2, 64
D_HEADS, D_K, D_V = 4, 32, 64
BRANCH = 256
N_BRANCH = 4
N_EXP = 16
N_EXP_GROUPS = 4
EXP_PER_GROUP = 4
D_FF_EXP = 512
ROPE_BASE = 10000.0
ALPHA = (2 * DEPTH) ** 0.25
EPS = 1e-5
ADA_DIM = 6 * D_MODEL
NEG = -1e30

N_SMALL = 23 * 128
GATE_OFF = 2824
CDT_OFF = 2048

VMEM_LIMIT = 52 * 1024 * 1024
MOE_ROWS = 256
TOK_TILE = 256

_NN = (((1,), (0,)), ((), ()))
_NT = (((1,), (1,)), ((), ()))
_TN = (((0,), (0,)), ((), ()))


def _params(*sem):
    return pltpu.CompilerParams(dimension_semantics=sem, vmem_limit_bytes=VMEM_LIMIT)


def _mm(a, b, dims=_NN):
    return lax.dot_general(a, b, dims, preferred_element_type=F32)


def _split(a):
    hi = a.astype(BF16)
    return hi, (a - hi.astype(F32)).astype(BF16)


def _mm_f32(a, b, dims=_NN):
    a_hi, a_lo = _split(a)
    b_hi, b_lo = _split(b)
    return (_mm(a_lo, b_hi, dims) + _mm(a_hi, b_lo, dims)) + _mm(a_hi, b_hi, dims)


def _mm_exact_lhs(m_bf, a):
    a1 = a.astype(BF16)
    r1 = a - a1.astype(F32)
    a2 = r1.astype(BF16)
    a3 = (r1 - a2.astype(F32)).astype(BF16)
    return (_mm(m_bf, a3) + _mm(m_bf, a2)) + _mm(m_bf, a1)


def _silu(x):
    return x * jax.nn.sigmoid(x)


LANES = 128
TOKEN_TILE_ROWS = D_MODEL // LANES


def _store_token_tiles(ref, val, lead=()):
    n = val.shape[0]
    for j in range(TOKEN_TILE_ROWS):
        ref[(*lead, pl.ds(j, n, stride=TOKEN_TILE_ROWS), slice(None))] = val[:, j * LANES:(j + 1) * LANES]


def _load_token_tiles(ref, n, lead=()):
    return jnp.concatenate(
        [ref[(*lead, pl.ds(j, n, stride=TOKEN_TILE_ROWS), slice(None))] for j in range(TOKEN_TILE_ROWS)], axis=1)


def _rope(x, c, s_lo, s_hi, shift):
    n = x.shape[1]
    return x * c + pltpu.roll(x, n - shift, 1) * s_lo + pltpu.roll(x, shift, 1) * s_hi


def _ada_kernel(c_ref, w_ref, b_ref, o_ref):
    c = c_ref[...]
    o_ref[...] = _mm_f32(_silu(c), w_ref[...]) + b_ref[...]


def _ada(cvec, w_ada, b_ada):
    rows = cvec.shape[0]
    tn = 1024
    return pl.pallas_call(
        _ada_kernel,
        out_shape=jax.ShapeDtypeStruct((DEPTH, rows, ADA_DIM), F32),
        grid=(DEPTH, ADA_DIM // tn),
        in_specs=[
            pl.BlockSpec((rows, D_MODEL), lambda l, j: (0, 0)),
            pl.BlockSpec((None, D_MODEL, tn), lambda l, j: (l, 0, j)),
            pl.BlockSpec((None, 1, tn), lambda l, j: (l, 0, j)),
        ],
        out_specs=pl.BlockSpec((None, rows, tn), lambda l, j: (l, 0, j)),
        compiler_params=_params("arbitrary", "arbitrary"),
        name="ada",
    )(cvec, w_ada, b_ada.reshape(DEPTH, 1, ADA_DIM))


def _inproj_kernel(x_ref, mod_ref, w_ref, o_ref):
    d = x_ref.shape[1]
    h = x_ref[...] * (1.0 + mod_ref[:, d:2 * d]) + mod_ref[:, 0:d]
    o_ref[...] = _mm(h.astype(BF16), w_ref[...])


def _mod_group(i, tm, n_ctx, dec_seq):
    row = i * tm
    return jnp.where(row < n_ctx, 0, 1 + lax.div(jnp.maximum(row - n_ctx, 0), dec_seq))


def _inproj(x, mod_l, w_small, n_ctx, dec_seq):
    n_tok = x.shape[0]
    tm = TOK_TILE
    grp = functools.partial(_mod_group, tm=tm, n_ctx=n_ctx, dec_seq=dec_seq)
    return pl.pallas_call(
        _inproj_kernel,
        out_shape=jax.ShapeDtypeStruct((n_tok, N_SMALL), F32),
        grid=(n_tok // tm,),
        in_specs=[
            pl.BlockSpec((tm, D_MODEL), lambda i: (i, 0)),
            pl.BlockSpec((None, 1, ADA_DIM), lambda i: (grp(i), 0, 0)),
            pl.BlockSpec((D_MODEL, N_SMALL), lambda i: (0, 0)),
        ],
        out_specs=pl.BlockSpec((tm, N_SMALL), lambda i: (i, 0)),
        compiler_params=_params("arbitrary"),
        name="inproj",
    )(x, mod_l, w_small)


def _diff_attn_kernel(*refs, latent, tq, seq, past):
    if latent:
        (sc_ref, q_ref, k_ref, v_ref, g_ref, ck_ref, cv_ref, rc_ref, rlo_ref, rhi_ref,
         o_ref, kt_scr, v_scr) = refs
    else:
        sc_ref, q_ref, k_ref, v_ref, g_ref, o_ref, kt_scr, v_scr = refs
    qi = pl.program_id(1)
    shift = A_QK // 4

    @pl.when(qi == 0)
    def _():
        k = k_ref[...]
        if latent:
            k = _rope(k, rc_ref[...], rlo_ref[...], rhi_ref[...], shift)
        kt_scr[:, 0:seq] = k.T.astype(BF16)
        if latent:
            kt_scr[:, seq:seq + past] = ck_ref[...].T.astype(BF16)
        ones = jnp.ones((seq + past, LANES - A_V), BF16)
        for h in range(A_HEADS):
            hs = slice(h * A_V, (h + 1) * A_V)
            v_scr[0:seq, h * LANES:h * LANES + A_V] = v_ref[:, hs].astype(BF16)
            if latent:
                v_scr[seq:seq + past, h * LANES:h * LANES + A_V] = cv_ref[:, hs].astype(BF16)
            v_scr[:, h * LANES + A_V:(h + 1) * LANES] = ones

    q = q_ref[...]
    if latent:
        r = pl.ds(pl.multiple_of(qi * tq, tq), tq)
        q = _rope(q, rc_ref[r, :], rlo_ref[r, :], rhi_ref[r, :], shift)
    qb = q.astype(BF16)
    lam = sc_ref[0]
    post = sc_ref[1]
    c = (A_QK ** -0.5) * math.log2(math.e)
    outs = []
    for h in range(A_HEADS):
        maps = []
        for m in range(2):
            off = (h * 2 + m) * A_QK
            s = _mm(qb[:, off:off + A_QK], kt_scr[off:off + A_QK, :])
            p = jnp.exp2(s * c - jnp.max(s, axis=1, keepdims=True) * c)
            ov = _mm(p.astype(BF16), v_scr[:, h * LANES:(h + 1) * LANES])
            maps.append(ov[:, 0:A_V] * (1.0 / ov[:, A_V:A_V + 1]))
        o = maps[0] - lam * maps[1]
        n = o * lax.rsqrt(jnp.mean(o * o, axis=1, keepdims=True) + EPS)
        outs.append((n * g_ref[...]) * post)
    o_ref[...] = jnp.concatenate(outs, axis=1).astype(BF16)


def _diff_attn(proj, scal, g, row0, nb, seq, tq, cache=None, rope=None):
    latent = cache is not None
    nq = seq // tq
    rb = row0 // seq
    qb0 = row0 // tq
    past = cache[0].shape[1] if latent else 0
    in_specs = [
        pl.BlockSpec(memory_space=pltpu.SMEM),
        pl.BlockSpec((tq, 256), lambda b, i: (qb0 + b * nq + i, 0)),
        pl.BlockSpec((seq, 256), lambda b, i: (rb + b, 1)),
        pl.BlockSpec((seq, 256), lambda b, i: (rb + b, 2)),
        pl.BlockSpec((1, A_V), lambda b, i: (0, 0)),
    ]
    args = [scal, proj, proj, proj, g]
    if latent:
        in_specs += [
            pl.BlockSpec((None, past, 256), lambda b, i: (b, 0, 0)),
            pl.BlockSpec((None, past, 256), lambda b, i: (b, 0, 0)),
        ] + [pl.BlockSpec((seq, 256), lambda b, i: (0, 0))] * 3
        args += [cache[0], cache[1], *rope]
    return pl.pallas_call(
        functools.partial(_diff_attn_kernel, latent=latent, tq=tq, seq=seq, past=past),
        out_shape=jax.ShapeDtypeStruct((nb * seq, BRANCH), BF16),
        grid=(nb, nq),
        in_specs=in_specs,
        out_specs=pl.BlockSpec((tq, BRANCH), lambda b, i: (b * nq + i, 0)),
        scratch_shapes=[pltpu.VMEM((256, seq + past), BF16),
                        pltpu.VMEM((seq + past, A_HEADS * LANES), BF16)],
        compiler_params=_params("arbitrary", "arbitrary"),
        name="diff_attn_lat" if latent else "diff_attn_ctx",
    )(*args)


def _win_attn_kernel(*refs, latent, tq, seq):
    if latent:
        (sink_ref, q_ref, k_ref, v_ref, ck_ref, cv_ref, rc_ref, rlo_ref, rhi_ref,
         o_ref, k_scr, v_scr, ck_scr, cv_scr) = refs
    else:
        sink_ref, q_ref, k_ref, v_ref, o_ref, k_scr, v_scr = refs
    qi = pl.program_id(1)
    nq = seq // tq
    shift = B_DIM // 4
    kvw = B_KV * B_DIM

    @pl.when(qi == 0)
    def _():
        k = k_ref[...]
        if latent:
            k = _rope(k, rc_ref[:, 0:kvw], rlo_ref[:, 0:kvw], rhi_ref[:, 0:kvw], shift)
            ck_scr[...] = ck_ref[...].astype(BF16)
            cv_scr[...] = cv_ref[...].astype(BF16)
        k_scr[...] = k.astype(BF16)
        v_scr[...] = v_ref[...].astype(BF16)

    q = q_ref[...]
    if latent:
        r = pl.ds(pl.multiple_of(qi * tq, tq), tq)
        q = _rope(q, rc_ref[r, :], rlo_ref[r, :], rhi_ref[r, :], shift)
        rows = [pl.ds(pl.multiple_of(j * tq, tq), tq)
                for j in (jnp.maximum(qi - 1, 0), qi, jnp.minimum(qi + 1, nq - 1))]
        kl = jnp.concatenate([k_scr[rr, :] for rr in rows], axis=0)
        vl = jnp.concatenate([v_scr[rr, :] for rr in rows], axis=0)
        ii = lax.broadcasted_iota(jnp.int32, (tq, 3 * tq), 0)
        jj = lax.broadcasted_iota(jnp.int32, (tq, 3 * tq), 1)
        lo = jnp.where(qi > 0, 0, tq)
        hi = jnp.where(qi < nq - 1, 3 * tq, 2 * tq)
        valid = (jnp.abs(jj - tq - ii) <= WINDOW) & (jj >= lo) & (jj < hi)
    else:
        kl = k_scr[...]
        vl = v_scr[...]
    qb = q.astype(BF16)
    scale = B_DIM ** -0.5
    ratio = B_HEADS // B_KV
    outs = []
    for h in range(B_HEADS):
        gsl = slice((h // ratio) * B_DIM, (h // ratio + 1) * B_DIM)
        qh = qb[:, h * B_DIM:(h + 1) * B_DIM]
        snk = sink_ref[h]
        s = _mm(qh, kl[:, gsl], _NT) * scale
        if latent:
            s = jnp.where(valid, s, NEG)
            sc = _mm(qh, ck_scr[:, gsl], _NT) * scale
            m = jnp.maximum(jnp.maximum(jnp.max(s, axis=1, keepdims=True),
                                        jnp.max(sc, axis=1, keepdims=True)), snk)
            pc = jnp.exp(sc - m)
        else:
            m = jnp.maximum(jnp.max(s, axis=1, keepdims=True), snk)
        p = jnp.exp(s - m)
        den = jnp.sum(p, axis=1, keepdims=True) + jnp.exp(snk - m)
        if latent:
            den = den + jnp.sum(pc, axis=1, keepdims=True)
        inv = 1.0 / den
        o = _mm((p * inv).astype(BF16), vl[:, gsl])
        if latent:
            o = o + _mm((pc * inv).astype(BF16), cv_scr[:, gsl])
        outs.append(o)
    o_ref[...] = jnp.concatenate(outs, axis=1).astype(BF16)


def _win_attn(proj, sink, row0, nb, seq, tq, cache=None, rope=None):
    latent = cache is not None
    nq = seq // tq
    rb = row0 // seq
    qb0 = row0 // tq
    kvw = B_KV * B_DIM
    in_specs = [
        pl.BlockSpec(memory_space=pltpu.SMEM),
        pl.BlockSpec((tq, 256), lambda b, i: (qb0 + b * nq + i, 3)),
        pl.BlockSpec((seq, kvw), lambda b, i: (rb + b, 8)),
        pl.BlockSpec((seq, kvw), lambda b, i: (rb + b, 9)),
    ]
    args = [sink, proj, proj, proj]
    scratch = [pltpu.VMEM((seq, kvw), BF16), pltpu.VMEM((seq, kvw), BF16)]
    if latent:
        past = cache[0].shape[1]
        in_specs += [
            pl.BlockSpec((None, past, kvw), lambda b, i: (b, 0, 0)),
            pl.BlockSpec((None, past, kvw), lambda b, i: (b, 0, 0)),
        ] + [pl.BlockSpec((seq, 256), lambda b, i: (0, 0))] * 3
        args += [cache[0], cache[1], *rope]
        scratch += [pltpu.VMEM((past, kvw), BF16), pltpu.VMEM((past, kvw), BF16)]
    return pl.pallas_call(
        functools.partial(_win_attn_kernel, latent=latent, tq=tq, seq=seq),
        out_shape=jax.ShapeDtypeStruct((nb * seq, BRANCH), BF16),
        grid=(nb, nq),
        in_specs=in_specs,
        out_specs=pl.BlockSpec((tq, BRANCH), lambda b, i: (b * nq + i, 0)),
        scratch_shapes=scratch,
        compiler_params=_params("arbitrary", "arbitrary"),
        name="win_attn_lat" if latent else "win_attn_ctx",
    )(*args)


def _conv_silu(u, w, b):
    n = u.shape[0]
    rows = lax.broadcasted_iota(jnp.int32, u.shape, 0)
    up = jnp.where(rows == 0, 0.0, pltpu.roll(u, 1, 0))
    un = jnp.where(rows == n - 1, 0.0, pltpu.roll(u, n - 1, 0))
    return _silu(up * w[0:1, :] + u * w[1:2, :] + un * w[2:3, :] + b)


def _ssd_kernel(*refs, seq, has_h0):
    if has_h0:
        (cx_ref, cz_ref, cbc_ref, cdt_ref, cw_ref, cb_ref, dtb_ref, ac_ref, dv_ref, g_ref, h0_ref,
         y_ref, st_ref, xs, bcs, dts, ybuf, hs) = refs
    else:
        (cx_ref, cz_ref, cbc_ref, cdt_ref, cw_ref, cb_ref, dtb_ref, ac_ref, dv_ref, g_ref,
         y_ref, st_ref, xs, bcs, dts, ybuf, hs) = refs
    nc = seq // CHUNK
    xw = C_HEADS * C_P
    xs[...] = _conv_silu(cx_ref[...], cw_ref[:, 0:xw], cb_ref[:, 0:xw])
    bcs[...] = _conv_silu(cbc_ref[...], cw_ref[:, xw:2 * xw], cb_ref[:, xw:2 * xw])
    z = cdt_ref[...] + dtb_ref[...]
    dts[...] = jnp.maximum(z, 0.0) + jnp.log1p(jnp.exp(-jnp.abs(z)))
    if has_h0:
        hs[...] = h0_ref[...]
    else:
        hs[...] = jnp.zeros(hs.shape, F32)

    ri = lax.broadcasted_iota(jnp.int32, (CHUNK, CHUNK), 0)
    ci = lax.broadcasted_iota(jnp.int32, (CHUNK, CHUNK), 1)
    gw = C_GROUPS * C_N

    def chunk(c, d):
        r = pl.ds(pl.multiple_of(c * CHUNK, CHUNK), CHUNK)
        tri = (ri >= ci) if d == 0 else (ci >= ri)
        dt = dts[r, :]
        cs = _mm_exact_lhs(tri.astype(BF16), dt * ac_ref[...])
        cst = cs.T
        x = xs[r, :]
        bc = bcs[r, :]
        ys = []
        for g in range(C_GROUPS):
            bm = bc[:, g * C_N:(g + 1) * C_N]
            cb = bc[:, gw + g * C_N:gw + (g + 1) * C_N].astype(BF16)
            gram = _mm(cb, bm.astype(BF16), _NT)
            for hh in range(C_HEADS // C_GROUPS):
                h = g * (C_HEADS // C_GROUPS) + hh
                col = d * C_HEADS + h
                xb = (x[:, h * C_P:(h + 1) * C_P] * dt[:, col:col + 1]).astype(BF16)
                cc = cs[:, col:col + 1]
                dec = jnp.exp(jnp.where(tri, cc - cst[col:col + 1, :], NEG))
                tot = cc[CHUNK - 1:CHUNK, :] if d == 0 else cc[0:1, :]
                hin = hs[col]
                y = _mm((gram * dec).astype(BF16), xb)
                y = y + _mm(cb, hin.astype(BF16), _NT) * jnp.exp(cc)
                bd = (bm * jnp.exp(tot - cc)).astype(BF16)
                hs[col] = hin * jnp.exp(tot) + _mm(xb, bd, _TN)
                ys.append(y)
        return r, jnp.concatenate(ys, axis=1)

    def fwd(c, carry):
        r, y = chunk(c, 0)
        ybuf[r, :] = y
        return carry

    lax.fori_loop(0, nc, fwd, 0)

    def bwd(i, carry):
        r, y = chunk(nc - 1 - i, 1)
        y = (ybuf[r, :] + y) + xs[r, :] * dv_ref[...]
        y = y * _silu(cz_ref[r, :])
        gl = xw // C_GROUPS
        parts = []
        for g in range(C_GROUPS):
            seg = y[:, g * gl:(g + 1) * gl]
            parts.append(seg * lax.rsqrt(jnp.mean(seg * seg, axis=1, keepdims=True) + EPS))
        y_ref[r, :] = (jnp.concatenate(parts, axis=1) * g_ref[...]).astype(BF16)
        return carry

    lax.fori_loop(0, nc, bwd, 0)
    st_ref[...] = hs[...]


def _ssd(proj, conv_w, conv_b, dtb, acoef, dvec, g, row0, nb, seq, h0=None):
    has_h0 = h0 is not None
    rb = row0 // seq
    nst = 2 * C_HEADS
    in_specs = [
        pl.BlockSpec((seq, 256), lambda b: (rb + b, 5)),
        pl.BlockSpec((seq, 256), lambda b: (rb + b, 6)),
        pl.BlockSpec((seq, 256), lambda b: (rb + b, 7)),
        pl.BlockSpec((seq, 128), lambda b: (rb + b, 22)),
        pl.BlockSpec((3, 512), lambda b: (0, 0)),
        pl.BlockSpec((1, 512), lambda b: (0, 0)),
        pl.BlockSpec((1, 128), lambda b: (0, 0)),
        pl.BlockSpec((1, 128), lambda b: (0, 0)),
        pl.BlockSpec((1, 256), lambda b: (0, 0)),
        pl.BlockSpec((1, 256), lambda b: (0, 0)),
    ]
    args = [proj, proj, proj, proj, conv_w, conv_b, dtb, acoef, dvec, g]
    if has_h0:
        in_specs.append(pl.BlockSpec((None, nst, C_P, C_N), lambda b: (b, 0, 0, 0)))
        args.append(h0)
    return pl.pallas_call(
        functools.partial(_ssd_kernel, seq=seq, has_h0=has_h0),
        out_shape=(jax.ShapeDtypeStruct((nb * seq, BRANCH), BF16),
                   jax.ShapeDtypeStruct((nb, nst, C_P, C_N), F32)),
        grid=(nb,),
        in_specs=in_specs,
        out_specs=(pl.BlockSpec((seq, BRANCH), lambda b: (b, 0)),
                   pl.BlockSpec((None, nst, C_P, C_N), lambda b: (b, 0, 0, 0))),
        scratch_shapes=[pltpu.VMEM((seq, 256), F32), pltpu.VMEM((seq, 256), F32),
                        pltpu.VMEM((seq, 128), F32), pltpu.VMEM((seq, 256), F32),
                        pltpu.VMEM((nst, C_P, C_N), F32)],
        compiler_params=_params("arbitrary"),
        name="ssd_lat" if has_h0 else "ssd_ctx",
    )(*args)


def _ret_kernel(*refs, seq, has_h0):
    if has_h0:
        lg_ref, q_ref, k_ref, v_ref, gt_ref, g_ref, h0_ref, y_ref, st_ref, ybuf, hs = refs
    else:
        lg_ref, q_ref, k_ref, v_ref, gt_ref, g_ref, y_ref, st_ref, ybuf, hs = refs
    nc = seq // CHUNK
    if has_h0:
        hs[...] = h0_ref[...]
    else:
        hs[...] = jnp.zeros(hs.shape, F32)
    ri = lax.broadcasted_iota(jnp.int32, (CHUNK, CHUNK), 0)
    ci = lax.broadcasted_iota(jnp.int32, (CHUNK, CHUNK), 1)
    pos = lax.broadcasted_iota(jnp.int32, (CHUNK, 1), 0).astype(F32)
    kscale = D_K ** -0.5

    def chunk(c, d):
        r = pl.ds(pl.multiple_of(c * CHUNK, CHUNK), CHUNK)
        q = q_ref[r, :]
        k = k_ref[r, :] * kscale
        v = v_ref[r, :]
        if d == 0:
            tri, dist = ri >= ci, (ri - ci).astype(F32)
            steps_in, steps_out = pos + 1.0, (CHUNK - 1.0) - pos
        else:
            tri, dist = ci >= ri, (ci - ri).astype(F32)
            steps_in, steps_out = CHUNK - pos, pos
        ys = []
        for h in range(D_HEADS):
            col = d * D_HEADS + h
            lg = lg_ref[col]
            dec = jnp.exp(jnp.where(tri, dist * lg, NEG))
            e_in = jnp.exp(steps_in * lg)
            e_tot = e_in[CHUNK - 1:CHUNK, :] if d == 0 else e_in[0:1, :]
            qb = q[:, h * D_K:(h + 1) * D_K].astype(BF16)
            km = k[:, h * D_K:(h + 1) * D_K]
            vb = v[:, h * D_V:(h + 1) * D_V].astype(BF16)
            hin = hs[col]
            y = _mm((_mm(qb, km.astype(BF16), _NT) * dec).astype(BF16), vb)
            y = y + _mm(qb, hin.astype(BF16), _NT) * e_in
            bd = (km * jnp.exp(steps_out * lg)).astype(BF16)
            hs[col] = hin * e_tot + _mm(vb, bd, _TN)
            ys.append(y)
        return r, ys

    def fwd(c, carry):
        r, ys = chunk(c, 0)
        ybuf[r, :] = jnp.concatenate(ys, axis=1)
        return carry

    lax.fori_loop(0, nc, fwd, 0)

    def bwd(i, carry):
        r, ys = chunk(nc - 1 - i, 1)
        yf = ybuf[r, :]
        parts = []
        for h in range(D_HEADS):
            o = yf[:, h * D_V:(h + 1) * D_V] + ys[h]
            o = o - jnp.mean(o, axis=1, keepdims=True)
            parts.append(o * lax.rsqrt(jnp.mean(o * o, axis=1, keepdims=True) + EPS))
        y = (jnp.concatenate(parts, axis=1) * g_ref[...]) * _silu(gt_ref[r, :])
        y_ref[r, :] = y.astype(BF16)
        return carry

    lax.fori_loop(0, nc, bwd, 0)
    st_ref[...] = hs[...]


def _ret(proj, log_g, g, row0, nb, seq, h0=None):
    has_h0 = h0 is not None
    rb = row0 // seq
    nst = 2 * D_HEADS
    in_specs = [
        pl.BlockSpec(memory_space=pltpu.SMEM),
        pl.BlockSpec((seq, 128), lambda b: (rb + b, 16)),
        pl.BlockSpec((seq, 128), lambda b: (rb + b, 17)),
        pl.BlockSpec((seq, 256), lambda b: (rb + b, 9)),
        pl.BlockSpec((seq, 256), lambda b: (rb + b, 10)),
        pl.BlockSpec((1, 256), lambda b: (0, 0)),
    ]
    args = [log_g, proj, proj, proj, proj, g]
    if has_h0:
        in_specs.append(pl.BlockSpec((None, nst, D_V, D_K), lambda b: (b, 0, 0, 0)))
        args.append(h0)
    return pl.pallas_call(
        functools.partial(_ret_kernel, seq=seq, has_h0=has_h0),
        out_shape=(jax.ShapeDtypeStruct((nb * seq, BRANCH), BF16),
                   jax.ShapeDtypeStruct((nb, nst, D_V, D_K), F32)),
        grid=(nb,),
        in_specs=in_specs,
        out_specs=(pl.BlockSpec((seq, BRANCH), lambda b: (b, 0)),
                   pl.BlockSpec((None, nst, D_V, D_K), lambda b: (b, 0, 0, 0))),
        scratch_shapes=[pltpu.VMEM((seq, 256), F32), pltpu.VMEM((nst, D_V, D_K), F32)],
        compiler_params=_params("arbitrary"),
        name="ret_lat" if has_h0 else "ret_ctx",
    )(*args)


def _route(sel, s):
    row = lambda a, e: a[e:e + 1, :]
    best = None
    grp = None
    for g in range(N_EXP_GROUPS):
        vals = [row(sel, g * EXP_PER_GROUP + j) for j in range(EXP_PER_GROUP)]
        score = None
        for a in range(EXP_PER_GROUP):
            for b in range(a + 1, EXP_PER_GROUP):
                pair = vals[a] + vals[b]
                score = pair if score is None else jnp.maximum(score, pair)
        if best is None:
            best, grp = score, jnp.zeros(score.shape, jnp.int32)
        else:
            better = score > best
            best = jnp.where(better, score, best)
            grp = jnp.where(better, g, grp)

    def pick(a, j):
        out = row(a, j)
        for g in range(1, N_EXP_GROUPS):
            out = jnp.where(grp == g, row(a, g * EXP_PER_GROUP + j), out)
        return out

    cand = [pick(sel, j) for j in range(EXP_PER_GROUP)]
    aff = [pick(s, j) for j in range(EXP_PER_GROUP)]

    def arg_first_max(vals):
        top, idx = vals[0], jnp.zeros(vals[0].shape, jnp.int32)
        for j in range(1, len(vals)):
            better = vals[j] > top
            top = jnp.where(better, vals[j], top)
            idx = jnp.where(better, j, idx)
        return idx

    def take(vals, idx):
        out = vals[0]
        for j in range(1, len(vals)):
            out = jnp.where(idx == j, vals[j], out)
        return out

    i1 = arg_first_max(cand)
    i2 = arg_first_max([jnp.where(i1 == j, -jnp.inf, cand[j]) for j in range(EXP_PER_GROUP)])
    w1, w2 = take(aff, i1), take(aff, i2)
    tot = w1 + w2
    ids = jnp.concatenate([grp * EXP_PER_GROUP + i1, grp * EXP_PER_GROUP + i2], axis=0)
    gates = jnp.concatenate([w1 / tot, w2 / tot], axis=0)
    return ids, gates


def _merge_kernel(x_ref, mod_ref, br_ref, wgl_ref, wbr_ref, wout_ref, lng_ref, lnb_ref,
                  rw_ref, rb_ref, tri_ref, x1_ref, h2_ref, ids_ref, gates_ref, rank_ref, cnt_ref):
    d = D_MODEL

    @pl.when(pl.program_id(0) == 0)
    def _():
        cnt_ref[...] = jnp.zeros(cnt_ref.shape, F32)

    x = x_ref[...]
    hb = (x * (1.0 + mod_ref[:, d:2 * d]) + mod_ref[:, 0:d]).astype(BF16)
    br = br_ref[...]
    merged = None
    for k in range(N_BRANCH):
        gate = jax.nn.sigmoid(_mm(hb, wgl_ref[:, k * d:(k + 1) * d]))
        up = _mm(br[:, k * BRANCH:(k + 1) * BRANCH], wbr_ref[k * BRANCH:(k + 1) * BRANCH, :])
        merged = gate * up if merged is None else merged + gate * up
    mix = _mm(merged.astype(BF16), wout_ref[...])
    y = ALPHA * x + mod_ref[:, 2 * d:3 * d] * mix
    y = y - jnp.mean(y, axis=1, keepdims=True)
    x1 = (y * lax.rsqrt(jnp.mean(y * y, axis=1, keepdims=True) + EPS)) * lng_ref[...] + lnb_ref[...]
    x1_ref[...] = x1
    h2 = x1 * (1.0 + mod_ref[:, 4 * d:5 * d]) + mod_ref[:, 3 * d:4 * d]
    _store_token_tiles(h2_ref, h2)
    s = jax.nn.sigmoid(_mm_f32(rw_ref[...], h2, _NT))
    ids, gates = _route(s + rb_ref[...], s)
    ids_ref[...] = ids
    gates_ref[...] = gates
    expert = lax.broadcasted_iota(jnp.int32, (N_EXP, ids.shape[1]), 0)
    hot = [(expert == ids[k:k + 1, :]).astype(F32) for k in range(2)]
    both = hot[0] + hot[1]
    incl = _mm(both.astype(BF16), tri_ref[...])
    before = cnt_ref[...] + (incl - both)
    rank_ref[...] = jnp.concatenate(
        [jnp.sum(hk * before, axis=0, keepdims=True) for hk in hot], axis=0).astype(jnp.int32)
    cnt_ref[...] = cnt_ref[...] + incl[:, ids.shape[1] - 1:ids.shape[1]]


def _merge(x, mod_l, br, w_gl, w_br, w_out, ln_g, ln_b, rw_t, rb, n_ctx, dec_seq):
    n_tok = x.shape[0]
    tm = TOK_TILE
    grp = functools.partial(_mod_group, tm=tm, n_ctx=n_ctx, dec_seq=dec_seq)
    full = lambda shape: pl.BlockSpec(shape, lambda i: (0,) * len(shape))
    pos = jnp.arange(tm)
    tri = (pos[:, None] <= pos[None, :]).astype(BF16)
    return pl.pallas_call(
        _merge_kernel,
        out_shape=(jax.ShapeDtypeStruct((n_tok, D_MODEL), F32),
                   jax.ShapeDtypeStruct((n_tok * TOKEN_TILE_ROWS, LANES), F32),
                   jax.ShapeDtypeStruct((2, n_tok), jnp.int32),
                   jax.ShapeDtypeStruct((2, n_tok), F32),
                   jax.ShapeDtypeStruct((2, n_tok), jnp.int32),
                   jax.ShapeDtypeStruct((N_EXP, 1), F32)),
        grid=(n_tok // tm,),
        in_specs=[
            pl.BlockSpec((tm, D_MODEL), lambda i: (i, 0)),
            pl.BlockSpec((None, 1, ADA_DIM), lambda i: (grp(i), 0, 0)),
            pl.BlockSpec((tm, N_BRANCH * BRANCH), lambda i: (i, 0)),
            full((D_MODEL, N_BRANCH * D_MODEL)),
            full((N_BRANCH * BRANCH, D_MODEL)),
            full((D_MODEL, D_MODEL)),
            full((1, D_MODEL)),
            full((1, D_MODEL)),
            full((N_EXP, D_MODEL)),
            full((N_EXP, 1)),
            full((tm, tm)),
        ],
        out_specs=(pl.BlockSpec((tm, D_MODEL), lambda i: (i, 0)),
                   pl.BlockSpec((tm * TOKEN_TILE_ROWS, LANES), lambda i: (i, 0)),
                   pl.BlockSpec((2, tm), lambda i: (0, i)),
                   pl.BlockSpec((2, tm), lambda i: (0, i)),
                   pl.BlockSpec((2, tm), lambda i: (0, i)),
                   full((N_EXP, 1))),
        compiler_params=_params("arbitrary"),
        name="merge",
    )(x, mod_l, br, w_gl, w_br, w_out, ln_g, ln_b, rw_t, rb, tri)


def _moe_kernel(be_ref, nu_ref, src_ref, dst_ref, h_hbm, wg_ref, wu_ref, wd_ref, y_hbm,
                xbuf, ybuf, gsem, ssem):
    del be_ref
    i = pl.program_id(0)
    n_used = nu_ref[0]
    rows = MOE_ROWS
    tr = TOKEN_TILE_ROWS

    def gather_copy(tok, r, slot):
        return pltpu.make_async_copy(h_hbm.at[pl.ds(tok * tr, tr), :],
                                     xbuf.at[slot, pl.ds(r * tr, tr), :], gsem.at[slot])

    def scatter_copy(r, dst, slot):
        return pltpu.make_async_copy(ybuf.at[slot, pl.ds(r * tr, tr), :],
                                     y_hbm.at[pl.ds(dst * tr, tr), :], ssem.at[slot])

    def start_gather(blk, slot):
        for r in range(rows):
            gather_copy(src_ref[blk * rows + r], r, slot).start()

    def wait_gather(slot):
        for r in range(rows):
            gather_copy(0, r, slot).wait()

    def start_scatter(blk, slot):
        for r in range(rows):
            scatter_copy(r, dst_ref[blk * rows + r], slot).start()

    def wait_scatter(slot):
        for r in range(rows):
            scatter_copy(r, 0, slot).wait()

    @pl.when(i == 0)
    def _():
        start_gather(0, 0)

    @pl.when(i < n_used)
    def _():
        slot = lax.rem(i, 2)
        other = 1 - slot
        wait_gather(slot)

        @pl.when(i >= 2)
        def _():
            wait_scatter(slot)

        start_gather(jnp.minimum(i + 1, n_used - 1), other)
        x = _load_token_tiles(xbuf, rows, lead=(slot,)).astype(BF16)
        act = _silu(_mm(x, wg_ref[...])) * _mm(x, wu_ref[...])
        _store_token_tiles(ybuf, _mm(act.astype(BF16), wd_ref[...]), lead=(slot,))
        start_scatter(i, slot)

        @pl.when(i == n_used - 1)
        def _():
            wait_gather(other)
            wait_scatter(slot)

            @pl.when(i >= 1)
            def _():
                wait_scatter(other)


def _moe_experts(h2_tiles, slot_tok, slot_dst, blk_exp, n_used, wg, wu, wd, n_out_tok):
    n_blk = blk_exp.shape[0]
    tr = TOKEN_TILE_ROWS
    any_spec = pl.BlockSpec(memory_space=pl.ANY)
    return pl.pallas_call(
        _moe_kernel,
        out_shape=jax.ShapeDtypeStruct((n_out_tok * tr, LANES), F32),
        grid_spec=pltpu.PrefetchScalarGridSpec(
            num_scalar_prefetch=4,
            grid=(n_blk,),
            in_specs=[
                any_spec,
                pl.BlockSpec((None, D_MODEL, D_FF_EXP), lambda i, be, nu, s, t: (be[i], 0, 0)),
                pl.BlockSpec((None, D_MODEL, D_FF_EXP), lambda i, be, nu, s, t: (be[i], 0, 0)),
                pl.BlockSpec((None, D_FF_EXP, D_MODEL), lambda i, be, nu, s, t: (be[i], 0, 0)),
            ],
            out_specs=any_spec,
            scratch_shapes=[pltpu.VMEM((2, MOE_ROWS * tr, LANES), F32),
                            pltpu.VMEM((2, MOE_ROWS * tr, LANES), F32),
                            pltpu.SemaphoreType.DMA((2,)), pltpu.SemaphoreType.DMA((2,))],
        ),
        compiler_params=_params("arbitrary"),
        name="moe_experts",
    )(blk_exp, n_used, slot_tok, slot_dst, h2_tiles, wg, wu, wd)


def _final_kernel(x1_ref, mod_ref, y0_ref, y1_ref, gt_ref, lng_ref, lnb_ref, o_ref):
    d = D_MODEL
    gt = gt_ref[...]
    n = x1_ref.shape[0]
    ffn = _load_token_tiles(y0_ref, n) * gt[:, 0:1] + _load_token_tiles(y1_ref, n) * gt[:, 1:2]
    y = ALPHA * x1_ref[...] + mod_ref[:, 5 * d:6 * d] * ffn
    y = y - jnp.mean(y, axis=1, keepdims=True)
    o_ref[...] = (y * lax.rsqrt(jnp.mean(y * y, axis=1, keepdims=True) + EPS)) * lng_ref[...] + lnb_ref[...]


def _final(x1, mod_l, y_pair, gates_t, ln_g, ln_b, n_ctx, dec_seq):
    n_tok = x1.shape[0]
    tm = TOK_TILE
    nt = n_tok // tm
    grp = functools.partial(_mod_group, tm=tm, n_ctx=n_ctx, dec_seq=dec_seq)
    tile = pl.BlockSpec((tm, D_MODEL), lambda i: (i, 0))
    vec = pl.BlockSpec((1, D_MODEL), lambda i: (0, 0))
    return pl.pallas_call(
        _final_kernel,
        out_shape=jax.ShapeDtypeStruct((n_tok, D_MODEL), F32),
        grid=(nt,),
        in_specs=[tile, pl.BlockSpec((None, 1, ADA_DIM), lambda i: (grp(i), 0, 0)),
                  pl.BlockSpec((tm * TOKEN_TILE_ROWS, LANES), lambda i: (i, 0)),
                  pl.BlockSpec((tm * TOKEN_TILE_ROWS, LANES), lambda i: (nt + i, 0)),
                  pl.BlockSpec((tm, 2), lambda i: (i, 0)), vec, vec],
        out_specs=tile,
        compiler_params=_params("arbitrary"),
        name="final_norm",
    )(x1, mod_l, y_pair, y_pair, gates_t, ln_g, ln_b)


def _rope_tables(seq, dim, width):
    nf = dim // 4
    t = jnp.arange(seq)
    pos = jnp.stack([t // GRID_W, t % GRID_W], axis=-1).astype(F32)
    inv = ROPE_BASE ** (-jnp.arange(nf, dtype=F32) / nf)
    ang = pos[:, :, None] * inv
    cos, sin = jnp.cos(ang), jnp.sin(ang)
    zero = jnp.zeros_like(sin)
    c = jnp.stack([cos, cos], axis=2).reshape(seq, dim)
    s_lo = jnp.stack([-sin, zero], axis=2).reshape(seq, dim)
    s_hi = jnp.stack([zero, sin], axis=2).reshape(seq, dim)
    rep = width // dim
    return tuple(jnp.tile(a, (1, rep)) for a in (c, s_lo, s_hi))


def _dispatch_plan(ids, rank, counts, n_tok):
    n_assign = 2 * n_tok
    flat_e = ids.reshape(n_assign)
    onehot = (flat_e[:, None] == jnp.arange(N_EXP, dtype=jnp.int32)[None, :]).astype(jnp.int32)
    counts = counts.reshape(N_EXP).astype(jnp.int32)
    padded = (counts + MOE_ROWS - 1) // MOE_ROWS * MOE_ROWS
    pad_end = jnp.cumsum(padded)
    pad_start = pad_end - padded
    dest = (jnp.sum(onehot * pad_start[None, :], axis=1) + rank.reshape(n_assign)).astype(jnp.int32)
    n_blk = n_assign // MOE_ROWS + N_EXP
    blk_start = jnp.arange(n_blk, dtype=jnp.int32) * MOE_ROWS
    blk_exp = jnp.sum((blk_start[:, None] >= pad_end[None, :]).astype(jnp.int32), axis=1)
    blk_exp = jnp.minimum(blk_exp, N_EXP - 1).astype(jnp.int32)
    n_used = (pad_end[-1] // MOE_ROWS).astype(jnp.int32).reshape(1)
    n_slots = n_blk * MOE_ROWS
    slot_a = jnp.full((n_slots,), -1, jnp.int32).at[dest].set(jnp.arange(n_assign, dtype=jnp.int32))
    real = slot_a >= 0
    slot_tok = jnp.where(real, jnp.where(slot_a >= n_tok, slot_a - n_tok, slot_a), 0)
    slot_dst = jnp.where(real, slot_a, n_assign + jnp.arange(n_slots, dtype=jnp.int32))
    return slot_tok, slot_dst, blk_exp, n_used


def kernel(x_prompt, x_sample, cache_diff_k, cache_diff_v, cache_win_k, cache_win_v, state_ssd, state_ret,
           c, c_ctx, w_ada, b_ada, w_in, diff_lambda, diff_norm_g, win_sink, conv_w, conv_b,
           ssd_A_log, ssd_dt_bias, ssd_D, ssd_norm_g, ret_decay_logit, ret_norm_g, w_branch, w_out,
           ln_g, ln_b, router_w, router_b, moe_w_gate, moe_w_up, moe_w_down):
    batch, seq, d = x_prompt.shape
    dec_batch, dec_seq, _ = x_sample.shape
    past = cache_diff_k.shape[2]
    n_ctx, n_lat = batch * seq, dec_batch * dec_seq
    n_tok = n_ctx + n_lat
    assert d == D_MODEL and n_ctx % dec_seq == 0 and seq % CHUNK == 0 and dec_seq % CHUNK == 0

    x = jnp.concatenate([x_prompt.reshape(n_ctx, d), x_sample.reshape(n_lat, d)], axis=0)

    n_mod = 1 + dec_batch
    n_mod_pad = -(-n_mod // 8) * 8
    cvec = jnp.concatenate([c_ctx[None, :], c, jnp.zeros((n_mod_pad - n_mod, d), F32)], axis=0)
    mod = _ada(cvec, w_ada, b_ada)

    rope_a = _rope_tables(dec_seq, A_QK, 256)
    rope_b = _rope_tables(dec_seq, B_DIM, 256)
    rw_t = router_w.T
    rb_col = router_b.reshape(N_EXP, 1)

    ctx_out = {k: [] for k in ('diff_k', 'diff_v', 'win_k', 'win_v', 'ssd', 'ret')}
    for l in range(DEPTH):
        mod_l = mod[l, :n_mod].reshape(n_mod, 1, ADA_DIM)
        wl = w_in[l]
        w_small = jnp.concatenate(
            [wl[:, :CDT_OFF], wl[:, CDT_OFF + 8:GATE_OFF], wl[:, CDT_OFF:CDT_OFF + 8],
             jnp.zeros((d, N_SMALL - GATE_OFF), F32)], axis=1).astype(BF16)
        w_gl = wl[:, GATE_OFF:].astype(BF16)
        proj = _inproj(x, mod_l, w_small, n_ctx, dec_seq)

        lam_init = 0.8 - 0.6 * math.exp(-0.3 * l)
        lv = diff_lambda[l]
        lam = jnp.exp(jnp.sum(lv[0] * lv[1])) - jnp.exp(jnp.sum(lv[2] * lv[3])) + lam_init
        diff_scal = jnp.stack([lam, jnp.asarray(1.0 - lam_init, F32)]).astype(F32)
        g_a = diff_norm_g[l].reshape(1, A_V)
        sink = win_sink[l]
        dtb = jnp.zeros((1, 128), F32).at[0, :8].set(ssd_dt_bias[l].reshape(8))
        acoef = jnp.zeros((1, 128), F32).at[0, :8].set(-jnp.exp(ssd_A_log[l]).reshape(8))
        dvec = jnp.repeat(ssd_D[l], C_P).reshape(1, 256)
        g_c = ssd_norm_g[l].reshape(1, 256)
        log_g = jax.nn.log_sigmoid(ret_decay_logit[l]).reshape(8)
        g_d = ret_norm_g[l].reshape(1, 256)
        cw = conv_w[l]
        cb = conv_b[l].reshape(1, 512)

        oa_c = _diff_attn(proj, diff_scal, g_a, 0, batch, seq, seq)
        ob_c = _win_attn(proj, sink, 0, batch, seq, seq)
        yc_c, st_c = _ssd(proj, cw, cb, dtb, acoef, dvec, g_c, 0, batch, seq)
        od_c, rt_c = _ret(proj, log_g, g_d, 0, batch, seq)
        cache_a = (cache_diff_k[:, l].reshape(dec_batch, past, 256), cache_diff_v[:, l].reshape(dec_batch, past, 256))
        cache_b = (cache_win_k[:, l].reshape(dec_batch, past, 128), cache_win_v[:, l].reshape(dec_batch, past, 128))
        oa_l = _diff_attn(proj, diff_scal, g_a, n_ctx, dec_batch, dec_seq, BLOCK, cache=cache_a, rope=rope_a)
        ob_l = _win_attn(proj, sink, n_ctx, dec_batch, dec_seq, BLOCK, cache=cache_b, rope=rope_b)
        yc_l, _ = _ssd(proj, cw, cb, dtb, acoef, dvec, g_c, n_ctx, dec_batch, dec_seq,
                       h0=state_ssd[:, l].reshape(dec_batch, 8, C_P, C_N))
        od_l, _ = _ret(proj, log_g, g_d, n_ctx, dec_batch, dec_seq,
                       h0=state_ret[:, l].reshape(dec_batch, 8, D_V, D_K))

        br = jnp.concatenate([jnp.concatenate([oa_c, ob_c, yc_c, od_c], axis=1),
                              jnp.concatenate([oa_l, ob_l, yc_l, od_l], axis=1)], axis=0)
        x1, h2, ids, gates, rank, counts = _merge(
            x, mod_l, br, w_gl, w_branch[l].reshape(N_BRANCH * BRANCH, d).astype(BF16),
            w_out[l].astype(BF16), ln_g[l, 0].reshape(1, d), ln_b[l, 0].reshape(1, d), rw_t, rb_col,
            n_ctx, dec_seq)

        slot_tok, slot_dst, blk_exp, n_used = _dispatch_plan(ids, rank, counts, n_tok)
        y_pair = _moe_experts(h2, slot_tok, slot_dst, blk_exp, n_used, moe_w_gate[l].astype(BF16),
                              moe_w_up[l].astype(BF16), moe_w_down[l].astype(BF16),
                              2 * n_tok + slot_tok.shape[0])
        x = _final(x1, mod_l, y_pair, gates.T,
                   ln_g[l, 1].reshape(1, d), ln_b[l, 1].reshape(1, d), n_ctx, dec_seq)

        pc = proj[:n_ctx]
        ctx_out['diff_k'].append(pc[:, 256:512].reshape(batch, seq, A_HEADS, 2, A_QK))
        ctx_out['diff_v'].append(pc[:, 512:768].reshape(batch, seq, A_HEADS, A_V))
        ctx_out['win_k'].append(pc[:, 1024:1152].reshape(batch, seq, B_KV, B_DIM))
        ctx_out['win_v'].append(pc[:, 1152:1280].reshape(batch, seq, B_KV, B_DIM))
        ctx_out['ssd'].append(st_c.reshape(batch, 2, C_HEADS, C_P, C_N))
        ctx_out['ret'].append(rt_c.reshape(batch, 2, D_HEADS, D_V, D_K))

    y_prompt = x[:n_ctx].reshape(batch, seq, d)
    y_sample = x[n_ctx:].reshape(dec_batch, dec_seq, d)
    stk = lambda k: jnp.stack(ctx_out[k], axis=1)
    return (y_prompt, y_sample, stk('diff_k'), stk('diff_v'), stk('win_k'), stk('win_v'), stk('ssd'), stk('ret'))
```

```python
import functools
import math

import jax
import jax.numpy as jnp
from jax import lax
from jax.experimental import pallas as pl
from jax.experimental.pallas import tpu as pltpu

F32 = jnp.float32
BF16 = jnp.bfloat16

D_MODEL = 1024
DEPTH = 4
GRID_W = 64
BLOCK = 128
WINDOW = 128
CHUNK = 128
A_HEADS, A_QK, A_V = 4, 32, 64
B_HEADS, B_KV, B_DIM = 4, 2, 64
C_HEADS, C_P, C_GROUPS, C_N = 4, 64, 2, 64
D_HEADS, D_K, D_V = 4, 32, 64
BRANCH = 256
N_BRANCH = 4
N_EXP = 16
N_EXP_GROUPS = 4
EXP_PER_GROUP = 4
D_FF_EXP = 512
ROPE_BASE = 10000.0
ALPHA = (2 * DEPTH) ** 0.25
EPS = 1e-5
ADA_DIM = 6 * D_MODEL
NEG = -1e30

N_SMALL = 23 * 128
GATE_OFF = 2824
CDT_OFF = 2048

VMEM_LIMIT = 52 * 1024 * 1024
MOE_ROWS = 256
TOK_TILE = 256

_NN = (((1,), (0,)), ((), ()))
_NT = (((1,), (1,)), ((), ()))
_TN = (((0,), (0,)), ((), ()))


def _params(*sem):
    return pltpu.CompilerParams(dimension_semantics=sem, vmem_limit_bytes=VMEM_LIMIT)


def _mm(a, b, dims=_NN):
    return lax.dot_general(a, b, dims, preferred_element_type=F32)


def _split(a):
    hi = a.astype(BF16)
    return hi, (a - hi.astype(F32)).astype(BF16)


def _mm_f32(a, b, dims=_NN):
    a_hi, a_lo = _split(a)
    b_hi, b_lo = _split(b)
    return (_mm(a_lo, b_hi, dims) + _mm(a_hi, b_lo, dims)) + _mm(a_hi, b_hi, dims)


def _mm_exact_lhs(m_bf, a):
    a1 = a.astype(BF16)
    r1 = a - a1.astype(F32)
    a2 = r1.astype(BF16)
    a3 = (r1 - a2.astype(F32)).astype(BF16)
    return (_mm(m_bf, a3) + _mm(m_bf, a2)) + _mm(m_bf, a1)


def _silu(x):
    return x * jax.nn.sigmoid(x)


LANES = 128
TOKEN_TILE_ROWS = D_MODEL // LANES


def _store_token_tiles(ref, val, lead=()):
    n = val.shape[0]
    for j in range(TOKEN_TILE_ROWS):
        ref[(*lead, pl.ds(j, n, stride=TOKEN_TILE_ROWS), slice(None))] = val[:, j * LANES:(j + 1) * LANES]


def _load_token_tiles(ref, n, lead=()):
    return jnp.concatenate(
        [ref[(*lead, pl.ds(j, n, stride=TOKEN_TILE_ROWS), slice(None))] for j in range(TOKEN_TILE_ROWS)], axis=1)


def _rope(x, c, s_lo, s_hi, shift):
    n = x.shape[1]
    return x * c + pltpu.roll(x, n - shift, 1) * s_lo + pltpu.roll(x, shift, 1) * s_hi


def _ada_kernel(c_ref, w_ref, b_ref, o_ref):
    c = c_ref[...]
    o_ref[...] = _mm_f32(_silu(c), w_ref[...]) + b_ref[...]


def _ada(cvec, w_ada, b_ada):
    rows = cvec.shape[0]
    tn = 1024
    return pl.pallas_call(
        _ada_kernel,
        out_shape=jax.ShapeDtypeStruct((DEPTH, rows, ADA_DIM), F32),
        grid=(DEPTH, ADA_DIM // tn),
        in_specs=[
            pl.BlockSpec((rows, D_MODEL), lambda l, j: (0, 0)),
            pl.BlockSpec((None, D_MODEL, tn), lambda l, j: (l, 0, j)),
            pl.BlockSpec((None, 1, tn), lambda l, j: (l, 0, j)),
        ],
        out_specs=pl.BlockSpec((None, rows, tn), lambda l, j: (l, 0, j)),
        compiler_params=_params("arbitrary", "arbitrary"),
        name="ada",
    )(cvec, w_ada, b_ada.reshape(DEPTH, 1, ADA_DIM))


def _inproj_kernel(x_ref, mod_ref, w_ref, o_ref):
    d = x_ref.shape[1]
    h = x_ref[...] * (1.0 + mod_ref[:, d:2 * d]) + mod_ref[:, 0:d]
    o_ref[...] = _mm(h.astype(BF16), w_ref[...])


def _mod_group(i, tm, n_ctx, dec_seq):
    row = i * tm
    return jnp.where(row < n_ctx, 0, 1 + lax.div(jnp.maximum(row - n_ctx, 0), dec_seq))


def _inproj(x, mod_l, w_small, n_ctx, dec_seq):
    n_tok = x.shape[0]
    tm = TOK_TILE
    grp = functools.partial(_mod_group, tm=tm, n_ctx=n_ctx, dec_seq=dec_seq)
    return pl.pallas_call(
        _inproj_kernel,
        out_shape=jax.ShapeDtypeStruct((n_tok, N_SMALL), F32),
        grid=(n_tok // tm,),
        in_specs=[
            pl.BlockSpec((tm, D_MODEL), lambda i: (i, 0)),
            pl.BlockSpec((None, 1, ADA_DIM), lambda i: (grp(i), 0, 0)),
            pl.BlockSpec((D_MODEL, N_SMALL), lambda i: (0, 0)),
        ],
        out_specs=pl.BlockSpec((tm, N_SMALL), lambda i: (i, 0)),
        compiler_params=_params("arbitrary"),
        name="inproj",
    )(x, mod_l, w_small)


def _diff_attn_kernel(*refs, latent, tq, seq, past):
    if latent:
        (sc_ref, q_ref, k_ref, v_ref, g_ref, ck_ref, cv_ref, rc_ref, rlo_ref, rhi_ref,
         o_ref, kt_scr, v_scr) = refs
    else:
        sc_ref, q_ref, k_ref, v_ref, g_ref, o_ref, kt_scr, v_scr = refs
    qi = pl.program_id(1)
    shift = A_QK // 4

    @pl.when(qi == 0)
    def _():
        k = k_ref[...]
        if latent:
            k = _rope(k, rc_ref[...], rlo_ref[...], rhi_ref[...], shift)
        kt_scr[:, 0:seq] = k.T.astype(BF16)
        if latent:
            kt_scr[:, seq:seq + past] = ck_ref[...].T.astype(BF16)
        ones = jnp.ones((seq + past, LANES - A_V), BF16)
        for h in range(A_HEADS):
            hs = slice(h * A_V, (h + 1) * A_V)
            v_scr[0:seq, h * LANES:h * LANES + A_V] = v_ref[:, hs].astype(BF16)
            if latent:
                v_scr[seq:seq + past, h * LANES:h * LANES + A_V] = cv_ref[:, hs].astype(BF16)
            v_scr[:, h * LANES + A_V:(h + 1) * LANES] = ones

    q = q_ref[...]
    if latent:
        r = pl.ds(pl.multiple_of(qi * tq, tq), tq)
        q = _rope(q, rc_ref[r, :], rlo_ref[r, :], rhi_ref[r, :], shift)
    qb = q.astype(BF16)
    lam = sc_ref[0]
    post = sc_ref[1]
    c = (A_QK ** -0.5) * math.log2(math.e)
    outs = []
    for h in range(A_HEADS):
        maps = []
        for m in range(2):
            off = (h * 2 + m) * A_QK
            s = _mm(qb[:, off:off + A_QK], kt_scr[off:off + A_QK, :])
            p = jnp.exp2(s * c - jnp.max(s, axis=1, keepdims=True) * c)
            ov = _mm(p.astype(BF16), v_scr[:, h * LANES:(h + 1) * LANES])
            maps.append(ov[:, 0:A_V] * (1.0 / ov[:, A_V:A_V + 1]))
        o = maps[0] - lam * maps[1]
        n = o * lax.rsqrt(jnp.mean(o * o, axis=1, keepdims=True) + EPS)
        outs.append((n * g_ref[...]) * post)
    o_ref[...] = jnp.concatenate(outs, axis=1).astype(BF16)


def _diff_attn(proj, scal, g, row0, nb, seq, tq, cache=None, rope=None):
    latent = cache is not None
    nq = seq // tq
    rb = row0 // seq
    qb0 = row0 // tq
    past = cache[0].shape[1] if latent else 0
    in_specs = [
        pl.BlockSpec(memory_space=pltpu.SMEM),
        pl.BlockSpec((tq, 256), lambda b, i: (qb0 + b * nq + i, 0)),
        pl.BlockSpec((seq, 256), lambda b, i: (rb + b, 1)),
        pl.BlockSpec((seq, 256), lambda b, i: (rb + b, 2)),
        pl.BlockSpec((1, A_V), lambda b, i: (0, 0)),
    ]
    args = [scal, proj, proj, proj, g]
    if latent:
        in_specs += [
            pl.BlockSpec((None, past, 256), lambda b, i: (b, 0, 0)),
            pl.BlockSpec((None, past, 256), lambda b, i: (b, 0, 0)),
        ] + [pl.BlockSpec((seq, 256), lambda b, i: (0, 0))] * 3
        args += [cache[0], cache[1], *rope]
    return pl.pallas_call(
        functools.partial(_diff_attn_kernel, latent=latent, tq=tq, seq=seq, past=past),
        out_shape=jax.ShapeDtypeStruct((nb * seq, BRANCH), BF16),
        grid=(nb, nq),
        in_specs=in_specs,
        out_specs=pl.BlockSpec((tq, BRANCH), lambda b, i: (b * nq + i, 0)),
        scratch_shapes=[pltpu.VMEM((256, seq + past), BF16),
                        pltpu.VMEM((seq + past, A_HEADS * LANES), BF16)],
        compiler_params=_params("arbitrary", "arbitrary"),
        name="diff_attn_lat" if latent else "diff_attn_ctx",
    )(*args)


def _win_attn_kernel(*refs, latent, tq, seq):
    if latent:
        (sink_ref, q_ref, k_ref, v_ref, ck_ref, cv_ref, rc_ref, rlo_ref, rhi_ref,
         o_ref, k_scr, v_scr, ck_scr, cv_scr) = refs
    else:
        sink_ref, q_ref, k_ref, v_ref, o_ref, k_scr, v_scr = refs
    qi = pl.program_id(1)
    nq = seq // tq
    shift = B_DIM // 4
    kvw = B_KV * B_DIM

    @pl.when(qi == 0)
    def _():
        k = k_ref[...]
        if latent:
            k = _rope(k, rc_ref[:, 0:kvw], rlo_ref[:, 0:kvw], rhi_ref[:, 0:kvw], shift)
            ck_scr[...] = ck_ref[...].T.astype(BF16)
            cv_scr[...] = cv_ref[...].astype(BF16)
        kt = k.T.astype(BF16)
        for j in range(nq):
            k_scr[j] = kt[:, j * tq:(j + 1) * tq]
        v_scr[...] = v_ref[...].astype(BF16)

    q = q_ref[...]
    if latent:
        r = pl.ds(pl.multiple_of(qi * tq, tq), tq)
        q = _rope(q, rc_ref[r, :], rlo_ref[r, :], rhi_ref[r, :], shift)
        near = (jnp.maximum(qi - 1, 0), qi, jnp.minimum(qi + 1, nq - 1))
        kl = jnp.concatenate([k_scr[j] for j in near], axis=1)
        vl = jnp.concatenate([v_scr[pl.ds(pl.multiple_of(j * tq, tq), tq), :] for j in near], axis=0)
        ii = lax.broadcasted_iota(jnp.int32, (tq, 3 * tq), 0)
        jj = lax.broadcasted_iota(jnp.int32, (tq, 3 * tq), 1)
        lo = jnp.where(qi > 0, 0, tq)
        hi = jnp.where(qi < nq - 1, 3 * tq, 2 * tq)
        valid = (jnp.abs(jj - tq - ii) <= WINDOW) & (jj >= lo) & (jj < hi)
    else:
        kl = k_scr[0]
        vl = v_scr[...]
    qb = q.astype(BF16)
    scale = B_DIM ** -0.5
    ratio = B_HEADS // B_KV
    outs = []
    for h in range(B_HEADS):
        gsl = slice((h // ratio) * B_DIM, (h // ratio + 1) * B_DIM)
        qh = qb[:, h * B_DIM:(h + 1) * B_DIM]
        snk = sink_ref[h]
        s = _mm(qh, kl[gsl, :]) * scale
        if latent:
            s = jnp.where(valid, s, NEG)
            sc = _mm(qh, ck_scr[gsl, :]) * scale
            m = jnp.maximum(jnp.maximum(jnp.max(s, axis=1, keepdims=True),
                                        jnp.max(sc, axis=1, keepdims=True)), snk)
            pc = jnp.exp(sc - m)
        else:
            m = jnp.maximum(jnp.max(s, axis=1, keepdims=True), snk)
        p = jnp.exp(s - m)
        den = jnp.sum(p, axis=1, keepdims=True) + jnp.exp(snk - m)
        if latent:
            den = den + jnp.sum(pc, axis=1, keepdims=True)
        inv = 1.0 / den
        o = _mm((p * inv).astype(BF16), vl[:, gsl])
        if latent:
            o = o + _mm((pc * inv).astype(BF16), cv_scr[:, gsl])
        outs.append(o)
    o_ref[...] = jnp.concatenate(outs, axis=1).astype(BF16)


def _win_attn(proj, sink, row0, nb, seq, tq, cache=None, rope=None):
    latent = cache is not None
    nq = seq // tq
    rb = row0 // seq
    qb0 = row0 // tq
    kvw = B_KV * B_DIM
    in_specs = [
        pl.BlockSpec(memory_space=pltpu.SMEM),
        pl.BlockSpec((tq, 256), lambda b, i: (qb0 + b * nq + i, 3)),
        pl.BlockSpec((seq, kvw), lambda b, i: (rb + b, 8)),
        pl.BlockSpec((seq, kvw), lambda b, i: (rb + b, 9)),
    ]
    args = [sink, proj, proj, proj]
    scratch = [pltpu.VMEM((nq, kvw, tq), BF16), pltpu.VMEM((seq, kvw), BF16)]
    if latent:
        past = cache[0].shape[1]
        in_specs += [
            pl.BlockSpec((None, past, kvw), lambda b, i: (b, 0, 0)),
            pl.BlockSpec((None, past, kvw), lambda b, i: (b, 0, 0)),
        ] + [pl.BlockSpec((seq, 256), lambda b, i: (0, 0))] * 3
        args += [cache[0], cache[1], *rope]
        scratch += [pltpu.VMEM((kvw, past), BF16), pltpu.VMEM((past, kvw), BF16)]
    return pl.pallas_call(
        functools.partial(_win_attn_kernel, latent=latent, tq=tq, seq=seq),
        out_shape=jax.ShapeDtypeStruct((nb * seq, BRANCH), BF16),
        grid=(nb, nq),
        in_specs=in_specs,
        out_specs=pl.BlockSpec((tq, BRANCH), lambda b, i: (b * nq + i, 0)),
        scratch_shapes=scratch,
        compiler_params=_params("arbitrary", "arbitrary"),
        name="win_attn_lat" if latent else "win_attn_ctx",
    )(*args)


def _conv_silu(u, w, b):
    n = u.shape[0]
    rows = lax.broadcasted_iota(jnp.int32, u.shape, 0)
    up = jnp.where(rows == 0, 0.0, pltpu.roll(u, 1, 0))
    un = jnp.where(rows == n - 1, 0.0, pltpu.roll(u, n - 1, 0))
    return _silu(up * w[0:1, :] + u * w[1:2, :] + un * w[2:3, :] + b)


def _ssd_kernel(*refs, seq, has_h0):
    if has_h0:
        (cx_ref, cz_ref, cbc_ref, cdt_ref, cw_ref, cb_ref, dtb_ref, ac_ref, dv_ref, g_ref, h0_ref,
         y_ref, st_ref, xs, bcs, dts, ybuf, ybuf_b, hs) = refs
    else:
        (cx_ref, cz_ref, cbc_ref, cdt_ref, cw_ref, cb_ref, dtb_ref, ac_ref, dv_ref, g_ref,
         y_ref, st_ref, xs, bcs, dts, ybuf, ybuf_b, hs) = refs
    nc = seq // CHUNK
    xw = C_HEADS * C_P
    xs[...] = _conv_silu(cx_ref[...], cw_ref[:, 0:xw], cb_ref[:, 0:xw])
    bcs[...] = _conv_silu(cbc_ref[...], cw_ref[:, xw:2 * xw], cb_ref[:, xw:2 * xw])
    z = cdt_ref[...] + dtb_ref[...]
    dts[...] = jnp.maximum(z, 0.0) + jnp.log1p(jnp.exp(-jnp.abs(z)))
    if has_h0:
        hs[...] = h0_ref[...]
    else:
        hs[...] = jnp.zeros(hs.shape, F32)

    ri = lax.broadcasted_iota(jnp.int32, (CHUNK, CHUNK), 0)
    ci = lax.broadcasted_iota(jnp.int32, (CHUNK, CHUNK), 1)
    gw = C_GROUPS * C_N

    def chunk(c, d):
        r = pl.ds(pl.multiple_of(c * CHUNK, CHUNK), CHUNK)
        tri = (ri >= ci) if d == 0 else (ci >= ri)
        dt = dts[r, :]
        cs = _mm_exact_lhs(tri.astype(BF16), dt * ac_ref[...])
        cst = cs.T
        x = xs[r, :]
        bc = bcs[r, :]
        ys = []
        for g in range(C_GROUPS):
            bm = bc[:, g * C_N:(g + 1) * C_N]
            cb = bc[:, gw + g * C_N:gw + (g + 1) * C_N].astype(BF16)
            gram = _mm(cb, bm.astype(BF16), _NT)
            for hh in range(C_HEADS // C_GROUPS):
                h = g * (C_HEADS // C_GROUPS) + hh
                col = d * C_HEADS + h
                xb = (x[:, h * C_P:(h + 1) * C_P] * dt[:, col:col + 1]).astype(BF16)
                cc = cs[:, col:col + 1]
                dec = jnp.exp(jnp.where(tri, cc - cst[col:col + 1, :], NEG))
                tot = cc[CHUNK - 1:CHUNK, :] if d == 0 else cc[0:1, :]
                hin = hs[col]
                y = _mm((gram * dec).astype(BF16), xb)
                y = y + _mm(cb, hin.astype(BF16), _NT) * jnp.exp(cc)
                bd = (bm * jnp.exp(tot - cc)).astype(BF16)
                hs[col] = hin * jnp.exp(tot) + _mm(xb, bd, _TN)
                ys.append(y)
        return r, jnp.concatenate(ys, axis=1)

    def scan(t, carry):
        r, y = chunk(t, 0)
        ybuf[r, :] = y
        r, y = chunk(nc - 1 - t, 1)
        ybuf_b[r, :] = y
        return carry

    lax.fori_loop(0, nc, scan, 0)

    def finish(c, carry):
        r = pl.ds(pl.multiple_of(c * CHUNK, CHUNK), CHUNK)
        y = (ybuf[r, :] + ybuf_b[r, :]) + xs[r, :] * dv_ref[...]
        y = y * _silu(cz_ref[r, :])
        gl = xw // C_GROUPS
        parts = []
        for g in range(C_GROUPS):
            seg = y[:, g * gl:(g + 1) * gl]
            parts.append(seg * lax.rsqrt(jnp.mean(seg * seg, axis=1, keepdims=True) + EPS))
        y_ref[r, :] = (jnp.concatenate(parts, axis=1) * g_ref[...]).astype(BF16)
        return carry

    lax.fori_loop(0, nc, finish, 0)
    st_ref[...] = hs[...]


def _ssd(proj, conv_w, conv_b, dtb, acoef, dvec, g, row0, nb, seq, h0=None):
    has_h0 = h0 is not None
    rb = row0 // seq
    nst = 2 * C_HEADS
    in_specs = [
        pl.BlockSpec((seq, 256), lambda b: (rb + b, 5)),
        pl.BlockSpec((seq, 256), lambda b: (rb + b, 6)),
        pl.BlockSpec((seq, 256), lambda b: (rb + b, 7)),
        pl.BlockSpec((seq, 128), lambda b: (rb + b, 22)),
        pl.BlockSpec((3, 512), lambda b: (0, 0)),
        pl.BlockSpec((1, 512), lambda b: (0, 0)),
        pl.BlockSpec((1, 128), lambda b: (0, 0)),
        pl.BlockSpec((1, 128), lambda b: (0, 0)),
        pl.BlockSpec((1, 256), lambda b: (0, 0)),
        pl.BlockSpec((1, 256), lambda b: (0, 0)),
    ]
    args = [proj, proj, proj, proj, conv_w, conv_b, dtb, acoef, dvec, g]
    if has_h0:
        in_specs.append(pl.BlockSpec((None, nst, C_P, C_N), lambda b: (b, 0, 0, 0)))
        args.append(h0)
    return pl.pallas_call(
        functools.partial(_ssd_kernel, seq=seq, has_h0=has_h0),
        out_shape=(jax.ShapeDtypeStruct((nb * seq, BRANCH), BF16),
                   jax.ShapeDtypeStruct((nb, nst, C_P, C_N), F32)),
        grid=(nb,),
        in_specs=in_specs,
        out_specs=(pl.BlockSpec((seq, BRANCH), lambda b: (b, 0)),
                   pl.BlockSpec((None, nst, C_P, C_N), lambda b: (b, 0, 0, 0))),
        scratch_shapes=[pltpu.VMEM((seq, 256), F32), pltpu.VMEM((seq, 256), F32),
                        pltpu.VMEM((seq, 128), F32), pltpu.VMEM((seq, 256), F32),
                        pltpu.VMEM((seq, 256), F32), pltpu.VMEM((nst, C_P, C_N), F32)],
        compiler_params=_params("arbitrary"),
        name="ssd_lat" if has_h0 else "ssd_ctx",
    )(*args)


def _ret_kernel(*refs, seq, has_h0):
    if has_h0:
        lg_ref, q_ref, k_ref, v_ref, gt_ref, g_ref, h0_ref, y_ref, st_ref, ybuf, ybuf_b, hs = refs
    else:
        lg_ref, q_ref, k_ref, v_ref, gt_ref, g_ref, y_ref, st_ref, ybuf, ybuf_b, hs = refs
    nc = seq // CHUNK
    if has_h0:
        hs[...] = h0_ref[...]
    else:
        hs[...] = jnp.zeros(hs.shape, F32)
    ri = lax.broadcasted_iota(jnp.int32, (CHUNK, CHUNK), 0)
    ci = lax.broadcasted_iota(jnp.int32, (CHUNK, CHUNK), 1)
    pos = lax.broadcasted_iota(jnp.int32, (CHUNK, 1), 0).astype(F32)
    kscale = D_K ** -0.5

    def chunk(c, d):
        r = pl.ds(pl.multiple_of(c * CHUNK, CHUNK), CHUNK)
        q = q_ref[r, :]
        k = k_ref[r, :] * kscale
        v = v_ref[r, :]
        if d == 0:
            tri, dist = ri >= ci, (ri - ci).astype(F32)
            steps_in, steps_out = pos + 1.0, (CHUNK - 1.0) - pos
        else:
            tri, dist = ci >= ri, (ci - ri).astype(F32)
            steps_in, steps_out = CHUNK - pos, pos
        ys = []
        for h in range(D_HEADS):
            col = d * D_HEADS + h
            lg = lg_ref[col]
            dec = jnp.exp(jnp.where(tri, dist * lg, NEG))
            e_in = jnp.exp(steps_in * lg)
            e_tot = e_in[CHUNK - 1:CHUNK, :] if d == 0 else e_in[0:1, :]
            qb = q[:, h * D_K:(h + 1) * D_K].astype(BF16)
            km = k[:, h * D_K:(h + 1) * D_K]
            vb = v[:, h * D_V:(h + 1) * D_V].astype(BF16)
            hin = hs[col]
            y = _mm((_mm(qb, km.astype(BF16), _NT) * dec).astype(BF16), vb)
            y = y + _mm(qb, hin.astype(BF16), _NT) * e_in
            bd = (km * jnp.exp(steps_out * lg)).astype(BF16)
            hs[col] = hin * e_tot + _mm(vb, bd, _TN)
            ys.append(y)
        return r, ys

    def scan(t, carry):
        r, ys = chunk(t, 0)
        ybuf[r, :] = jnp.concatenate(ys, axis=1)
        r, ys = chunk(nc - 1 - t, 1)
        ybuf_b[r, :] = jnp.concatenate(ys, axis=1)
        return carry

    lax.fori_loop(0, nc, scan, 0)

    def finish(c, carry):
        r = pl.ds(pl.multiple_of(c * CHUNK, CHUNK), CHUNK)
        yf = ybuf[r, :] + ybuf_b[r, :]
        parts = []
        for h in range(D_HEADS):
            o = yf[:, h * D_V:(h + 1) * D_V]
            o = o - jnp.mean(o, axis=1, keepdims=True)
            parts.append(o * lax.rsqrt(jnp.mean(o * o, axis=1, keepdims=True) + EPS))
        y = (jnp.concatenate(parts, axis=1) * g_ref[...]) * _silu(gt_ref[r, :])
        y_ref[r, :] = y.astype(BF16)
        return carry

    lax.fori_loop(0, nc, finish, 0)
    st_ref[...] = hs[...]


def _ret(proj, log_g, g, row0, nb, seq, h0=None):
    has_h0 = h0 is not None
    rb = row0 // seq
    nst = 2 * D_HEADS
    in_specs = [
        pl.BlockSpec(memory_space=pltpu.SMEM),
        pl.BlockSpec((seq, 128), lambda b: (rb + b, 16)),
        pl.BlockSpec((seq, 128), lambda b: (rb + b, 17)),
        pl.BlockSpec((seq, 256), lambda b: (rb + b, 9)),
        pl.BlockSpec((seq, 256), lambda b: (rb + b, 10)),
        pl.BlockSpec((1, 256), lambda b: (0, 0)),
    ]
    args = [log_g, proj, proj, proj, proj, g]
    if has_h0:
        in_specs.append(pl.BlockSpec((None, nst, D_V, D_K), lambda b: (b, 0, 0, 0)))
        args.append(h0)
    return pl.pallas_call(
        functools.partial(_ret_kernel, seq=seq, has_h0=has_h0),
        out_shape=(jax.ShapeDtypeStruct((nb * seq, BRANCH), BF16),
                   jax.ShapeDtypeStruct((nb, nst, D_V, D_K), F32)),
        grid=(nb,),
        in_specs=in_specs,
        out_specs=(pl.BlockSpec((seq, BRANCH), lambda b: (b, 0)),
                   pl.BlockSpec((None, nst, D_V, D_K), lambda b: (b, 0, 0, 0))),
        scratch_shapes=[pltpu.VMEM((seq, 256), F32), pltpu.VMEM((seq, 256), F32),
                        pltpu.VMEM((nst, D_V, D_K), F32)],
        compiler_params=_params("arbitrary"),
        name="ret_lat" if has_h0 else "ret_ctx",
    )(*args)


def _route(sel, s):
    row = lambda a, e: a[e:e + 1, :]
    best = None
    grp = None
    for g in range(N_EXP_GROUPS):
        vals = [row(sel, g * EXP_PER_GROUP + j) for j in range(EXP_PER_GROUP)]
        score = None
        for a in range(EXP_PER_GROUP):
            for b in range(a + 1, EXP_PER_GROUP):
                pair = vals[a] + vals[b]
                score = pair if score is None else jnp.maximum(score, pair)
        if best is None:
            best, grp = score, jnp.zeros(score.shape, jnp.int32)
        else:
            better = score > best
            best = jnp.where(better, score, best)
            grp = jnp.where(better, g, grp)

    def pick(a, j):
        out = row(a, j)
        for g in range(1, N_EXP_GROUPS):
            out = jnp.where(grp == g, row(a, g * EXP_PER_GROUP + j), out)
        return out

    cand = [pick(sel, j) for j in range(EXP_PER_GROUP)]
    aff = [pick(s, j) for j in range(EXP_PER_GROUP)]

    def arg_first_max(vals):
        top, idx = vals[0], jnp.zeros(vals[0].shape, jnp.int32)
        for j in range(1, len(vals)):
            better = vals[j] > top
            top = jnp.where(better, vals[j], top)
            idx = jnp.where(better, j, idx)
        return idx

    def take(vals, idx):
        out = vals[0]
        for j in range(1, len(vals)):
            out = jnp.where(idx == j, vals[j], out)
        return out

    i1 = arg_first_max(cand)
    i2 = arg_first_max([jnp.where(i1 == j, -jnp.inf, cand[j]) for j in range(EXP_PER_GROUP)])
    w1, w2 = take(aff, i1), take(aff, i2)
    tot = w1 + w2
    ids = jnp.concatenate([grp * EXP_PER_GROUP + i1, grp * EXP_PER_GROUP + i2], axis=0)
    gates = jnp.concatenate([w1 / tot, w2 / tot], axis=0)
    return ids, gates


def _merge_kernel(*refs, n_ctx_tiles):
    (x_ref, mod_ref, wgl_ref, wbr_ref, wout_ref, lng_ref, lnb_ref, rw_ref, rb_ref, tri_ref) = refs[:10]
    br_refs = refs[10:10 + 2 * N_BRANCH]
    x1_ref, h2_ref, ids_ref, gates_ref, rank_ref, cnt_ref = refs[10 + 2 * N_BRANCH:]
    d = D_MODEL

    @pl.when(pl.program_id(0) == 0)
    def _():
        cnt_ref[...] = jnp.zeros(cnt_ref.shape, F32)

    is_ctx = pl.program_id(0) < n_ctx_tiles
    x = x_ref[...]
    hb = (x * (1.0 + mod_ref[:, d:2 * d]) + mod_ref[:, 0:d]).astype(BF16)
    merged = None
    for k in range(N_BRANCH):
        gate = jax.nn.sigmoid(_mm(hb, wgl_ref[:, k * d:(k + 1) * d]))
        br_k = jnp.where(is_ctx, br_refs[2 * k][...], br_refs[2 * k + 1][...])
        up = _mm(br_k, wbr_ref[k * BRANCH:(k + 1) * BRANCH, :])
        merged = gate * up if merged is None else merged + gate * up
    mix = _mm(merged.astype(BF16), wout_ref[...])
    y = ALPHA * x + mod_ref[:, 2 * d:3 * d] * mix
    y = y - jnp.mean(y, axis=1, keepdims=True)
    x1 = (y * lax.rsqrt(jnp.mean(y * y, axis=1, keepdims=True) + EPS)) * lng_ref[...] + lnb_ref[...]
    x1_ref[...] = x1
    h2 = x1 * (1.0 + mod_ref[:, 4 * d:5 * d]) + mod_ref[:, 3 * d:4 * d]
    _store_token_tiles(h2_ref, h2)
    s = jax.nn.sigmoid(_mm_f32(rw_ref[...], h2, _NT))
    ids, gates = _route(s + rb_ref[...], s)
    ids_ref[...] = ids
    gates_ref[...] = gates
    expert = lax.broadcasted_iota(jnp.int32, (N_EXP, ids.shape[1]), 0)
    hot = [(expert == ids[k:k + 1, :]).astype(F32) for k in range(2)]
    both = hot[0] + hot[1]
    incl = _mm(both.astype(BF16), tri_ref[...])
    before = cnt_ref[...] + (incl - both)
    rank_ref[...] = jnp.concatenate(
        [jnp.sum(hk * before, axis=0, keepdims=True) for hk in hot], axis=0).astype(jnp.int32)
    cnt_ref[...] = cnt_ref[...] + incl[:, ids.shape[1] - 1:ids.shape[1]]


def _merge(x, mod_l, branches, w_gl, w_br, w_out, ln_g, ln_b, rw_t, rb, n_ctx, dec_seq):
    n_tok = x.shape[0]
    tm = TOK_TILE
    nct = n_ctx // tm
    grp = functools.partial(_mod_group, tm=tm, n_ctx=n_ctx, dec_seq=dec_seq)
    full = lambda shape: pl.BlockSpec(shape, lambda i: (0,) * len(shape))
    pos = jnp.arange(tm)
    tri = (pos[:, None] <= pos[None, :]).astype(BF16)
    br_specs = [pl.BlockSpec((tm, BRANCH), lambda i: (jnp.minimum(i, nct - 1), 0)),
                pl.BlockSpec((tm, BRANCH), lambda i: (jnp.maximum(i - nct, 0), 0))] * N_BRANCH
    br_args = [a for pair in branches for a in pair]
    return pl.pallas_call(
        functools.partial(_merge_kernel, n_ctx_tiles=nct),
        out_shape=(jax.ShapeDtypeStruct((n_tok, D_MODEL), F32),
                   jax.ShapeDtypeStruct((n_tok * TOKEN_TILE_ROWS, LANES), F32),
                   jax.ShapeDtypeStruct((2, n_tok), jnp.int32),
                   jax.ShapeDtypeStruct((2, n_tok), F32),
                   jax.ShapeDtypeStruct((2, n_tok), jnp.int32),
                   jax.ShapeDtypeStruct((N_EXP, 1), F32)),
        grid=(n_tok // tm,),
        in_specs=[
            pl.BlockSpec((tm, D_MODEL), lambda i: (i, 0)),
            pl.BlockSpec((None, 1, ADA_DIM), lambda i: (grp(i), 0, 0)),
            full((D_MODEL, N_BRANCH * D_MODEL)),
            full((N_BRANCH * BRANCH, D_MODEL)),
            full((D_MODEL, D_MODEL)),
            full((1, D_MODEL)),
            full((1, D_MODEL)),
            full((N_EXP, D_MODEL)),
            full((N_EXP, 1)),
            full((tm, tm)),
        ] + br_specs,
        out_specs=(pl.BlockSpec((tm, D_MODEL), lambda i: (i, 0)),
                   pl.BlockSpec((tm * TOKEN_TILE_ROWS, LANES), lambda i: (i, 0)),
                   pl.BlockSpec((2, tm), lambda i: (0, i)),
                   pl.BlockSpec((2, tm), lambda i: (0, i)),
                   pl.BlockSpec((2, tm), lambda i: (0, i)),
                   full((N_EXP, 1))),
        compiler_params=_params("arbitrary"),
        name="merge",
    )(x, mod_l, w_gl, w_br, w_out, ln_g, ln_b, rw_t, rb, tri, *br_args)


def _moe_kernel(be_ref, nu_ref, src_ref, dst_ref, h_hbm, wg_ref, wu_ref, wd_ref, y_hbm,
                xbuf, ybuf, gsem, ssem):
    del be_ref
    i = pl.program_id(0)
    n_used = nu_ref[0]
    rows = MOE_ROWS
    tr = TOKEN_TILE_ROWS

    def gather_copy(tok, r, slot):
        return pltpu.make_async_copy(h_hbm.at[pl.ds(tok * tr, tr), :],
                                     xbuf.at[slot, pl.ds(r * tr, tr), :], gsem.at[slot])

    def scatter_copy(r, dst, slot):
        return pltpu.make_async_copy(ybuf.at[slot, pl.ds(r * tr, tr), :],
                                     y_hbm.at[pl.ds(dst * tr, tr), :], ssem.at[slot])

    def start_gather(blk, slot, lo=0, hi=rows):
        for r in range(lo, hi):
            gather_copy(src_ref[blk * rows + r], r, slot).start()

    def wait_gather(slot):
        for r in range(rows):
            gather_copy(0, r, slot).wait()

    def start_scatter(blk, slot, lo=0, hi=rows):
        for r in range(lo, hi):
            scatter_copy(r, dst_ref[blk * rows + r], slot).start()

    def wait_scatter(slot):
        for r in range(rows):
            scatter_copy(r, 0, slot).wait()

    @pl.when(i == 0)
    def _():
        start_gather(0, 0)

    @pl.when(i < n_used)
    def _():
        slot = lax.rem(i, 2)
        other = 1 - slot
        wait_gather(slot)

        @pl.when(i >= 2)
        def _():
            wait_scatter(slot)

        start_gather(jnp.minimum(i + 1, n_used - 1), other)
        x = _load_token_tiles(xbuf, rows, lead=(slot,)).astype(BF16)
        act = _silu(_mm(x, wg_ref[...])) * _mm(x, wu_ref[...])
        _store_token_tiles(ybuf, _mm(act.astype(BF16), wd_ref[...]), lead=(slot,))
        start_scatter(i, slot)

        @pl.when(i == n_used - 1)
        def _():
            wait_gather(other)
            wait_scatter(slot)

            @pl.when(i >= 1)
            def _():
                wait_scatter(other)


def _moe_experts(h2_tiles, slot_tok, slot_dst, blk_exp, n_used, wg, wu, wd, n_out_tok):
    n_blk = blk_exp.shape[0]
    tr = TOKEN_TILE_ROWS
    any_spec = pl.BlockSpec(memory_space=pl.ANY)
    return pl.pallas_call(
        _moe_kernel,
        out_shape=jax.ShapeDtypeStruct((n_out_tok * tr, LANES), F32),
        grid_spec=pltpu.PrefetchScalarGridSpec(
            num_scalar_prefetch=4,
            grid=(n_blk,),
            in_specs=[
                any_spec,
                pl.BlockSpec((None, D_MODEL, D_FF_EXP), lambda i, be, nu, s, t: (be[i], 0, 0)),
                pl.BlockSpec((None, D_MODEL, D_FF_EXP), lambda i, be, nu, s, t: (be[i], 0, 0)),
                pl.BlockSpec((None, D_FF_EXP, D_MODEL), lambda i, be, nu, s, t: (be[i], 0, 0)),
            ],
            out_specs=any_spec,
            scratch_shapes=[pltpu.VMEM((2, MOE_ROWS * tr, LANES), F32),
                            pltpu.VMEM((2, MOE_ROWS * tr, LANES), F32),
                            pltpu.SemaphoreType.DMA((2,)), pltpu.SemaphoreType.DMA((2,))],
        ),
        compiler_params=_params("arbitrary"),
        name="moe_experts",
    )(blk_exp, n_used, slot_tok, slot_dst, h2_tiles, wg, wu, wd)


def _final_kernel(x1_ref, mod_ref, y0_ref, y1_ref, gt_ref, lng_ref, lnb_ref, o_ref):
    d = D_MODEL
    gt = gt_ref[...]
    n = x1_ref.shape[0]
    ffn = _load_token_tiles(y0_ref, n) * gt[:, 0:1] + _load_token_tiles(y1_ref, n) * gt[:, 1:2]
    y = ALPHA * x1_ref[...] + mod_ref[:, 5 * d:6 * d] * ffn
    y = y - jnp.mean(y, axis=1, keepdims=True)
    o_ref[...] = (y * lax.rsqrt(jnp.mean(y * y, axis=1, keepdims=True) + EPS)) * lng_ref[...] + lnb_ref[...]


def _final(x1, mod_l, y_pair, gates_t, ln_g, ln_b, n_ctx, dec_seq):
    n_tok = x1.shape[0]
    tm = TOK_TILE
    nt = n_tok // tm
    grp = functools.partial(_mod_group, tm=tm, n_ctx=n_ctx, dec_seq=dec_seq)
    tile = pl.BlockSpec((tm, D_MODEL), lambda i: (i, 0))
    vec = pl.BlockSpec((1, D_MODEL), lambda i: (0, 0))
    return pl.pallas_call(
        _final_kernel,
        out_shape=jax.ShapeDtypeStruct((n_tok, D_MODEL), F32),
        grid=(nt,),
        in_specs=[tile, pl.BlockSpec((None, 1, ADA_DIM), lambda i: (grp(i), 0, 0)),
                  pl.BlockSpec((tm * TOKEN_TILE_ROWS, LANES), lambda i: (i, 0)),
                  pl.BlockSpec((tm * TOKEN_TILE_ROWS, LANES), lambda i: (nt + i, 0)),
                  pl.BlockSpec((tm, 2), lambda i: (i, 0)), vec, vec],
        out_specs=tile,
        compiler_params=_params("arbitrary"),
        name="final_norm",
    )(x1, mod_l, y_pair, y_pair, gates_t, ln_g, ln_b)


def _rope_tables(seq, dim, width):
    nf = dim // 4
    t = jnp.arange(seq)
    pos = jnp.stack([t // GRID_W, t % GRID_W], axis=-1).astype(F32)
    inv = ROPE_BASE ** (-jnp.arange(nf, dtype=F32) / nf)
    ang = pos[:, :, None] * inv
    cos, sin = jnp.cos(ang), jnp.sin(ang)
    zero = jnp.zeros_like(sin)
    c = jnp.stack([cos, cos], axis=2).reshape(seq, dim)
    s_lo = jnp.stack([-sin, zero], axis=2).reshape(seq, dim)
    s_hi = jnp.stack([zero, sin], axis=2).reshape(seq, dim)
    rep = width // dim
    return tuple(jnp.tile(a, (1, rep)) for a in (c, s_lo, s_hi))


def _dispatch_plan(ids, rank, counts, n_tok):
    n_assign = 2 * n_tok
    flat_e = ids.reshape(n_assign)
    onehot = (flat_e[:, None] == jnp.arange(N_EXP, dtype=jnp.int32)[None, :]).astype(jnp.int32)
    counts = counts.reshape(N_EXP).astype(jnp.int32)
    padded = (counts + MOE_ROWS - 1) // MOE_ROWS * MOE_ROWS
    pad_end = jnp.cumsum(padded)
    pad_start = pad_end - padded
    dest = (jnp.sum(onehot * pad_start[None, :], axis=1) + rank.reshape(n_assign)).astype(jnp.int32)
    n_blk = n_assign // MOE_ROWS + N_EXP
    blk_start = jnp.arange(n_blk, dtype=jnp.int32) * MOE_ROWS
    blk_exp = jnp.sum((blk_start[:, None] >= pad_end[None, :]).astype(jnp.int32), axis=1)
    blk_exp = jnp.minimum(blk_exp, N_EXP - 1).astype(jnp.int32)
    n_used = (pad_end[-1] // MOE_ROWS).astype(jnp.int32).reshape(1)
    n_slots = n_blk * MOE_ROWS
    slot_a = jnp.full((n_slots,), -1, jnp.int32).at[dest].set(jnp.arange(n_assign, dtype=jnp.int32))
    real = slot_a >= 0
    slot_tok = jnp.where(real, jnp.where(slot_a >= n_tok, slot_a - n_tok, slot_a), 0)
    slot_dst = jnp.where(real, slot_a, n_assign + jnp.arange(n_slots, dtype=jnp.int32))
    return slot_tok, slot_dst, blk_exp, n_used


def kernel(x_prompt, x_sample, cache_diff_k, cache_diff_v, cache_win_k, cache_win_v, state_ssd, state_ret,
           c, c_ctx, w_ada, b_ada, w_in, diff_lambda, diff_norm_g, win_sink, conv_w, conv_b,
           ssd_A_log, ssd_dt_bias, ssd_D, ssd_norm_g, ret_decay_logit, ret_norm_g, w_branch, w_out,
           ln_g, ln_b, router_w, router_b, moe_w_gate, moe_w_up, moe_w_down):
    batch, seq, d = x_prompt.shape
    dec_batch, dec_seq, _ = x_sample.shape
    past = cache_diff_k.shape[2]
    n_ctx, n_lat = batch * seq, dec_batch * dec_seq
    n_tok = n_ctx + n_lat
    assert d == D_MODEL and n_ctx % dec_seq == 0 and seq % CHUNK == 0 and dec_seq % CHUNK == 0

    x = jnp.concatenate([x_prompt.reshape(n_ctx, d), x_sample.reshape(n_lat, d)], axis=0)

    n_mod = 1 + dec_batch
    n_mod_pad = -(-n_mod // 8) * 8
    cvec = jnp.concatenate([c_ctx[None, :], c, jnp.zeros((n_mod_pad - n_mod, d), F32)], axis=0)
    mod = _ada(cvec, w_ada, b_ada)

    rope_a = _rope_tables(dec_seq, A_QK, 256)
    rope_b = _rope_tables(dec_seq, B_DIM, 256)
    rw_t = router_w.T
    rb_col = router_b.reshape(N_EXP, 1)

    ctx_out = {k: [] for k in ('diff_k', 'diff_v', 'win_k', 'win_v', 'ssd', 'ret')}
    for l in range(DEPTH):
        mod_l = mod[l, :n_mod].reshape(n_mod, 1, ADA_DIM)
        wl = w_in[l]
        w_small = jnp.concatenate(
            [wl[:, :CDT_OFF], wl[:, CDT_OFF + 8:GATE_OFF], wl[:, CDT_OFF:CDT_OFF + 8],
             jnp.zeros((d, N_SMALL - GATE_OFF), F32)], axis=1).astype(BF16)
        w_gl = wl[:, GATE_OFF:].astype(BF16)
        proj = _inproj(x, mod_l, w_small, n_ctx, dec_seq)

        lam_init = 0.8 - 0.6 * math.exp(-0.3 * l)
        lv = diff_lambda[l]
        lam = jnp.exp(jnp.sum(lv[0] * lv[1])) - jnp.exp(jnp.sum(lv[2] * lv[3])) + lam_init
        diff_scal = jnp.stack([lam, jnp.asarray(1.0 - lam_init, F32)]).astype(F32)
        g_a = diff_norm_g[l].reshape(1, A_V)
        sink = win_sink[l]
        dtb = jnp.zeros((1, 128), F32).at[0, :8].set(ssd_dt_bias[l].reshape(8))
        acoef = jnp.zeros((1, 128), F32).at[0, :8].set(-jnp.exp(ssd_A_log[l]).reshape(8))
        dvec = jnp.repeat(ssd_D[l], C_P).reshape(1, 256)
        g_c = ssd_norm_g[l].reshape(1, 256)
        log_g = jax.nn.log_sigmoid(ret_decay_logit[l]).reshape(8)
        g_d = ret_norm_g[l].reshape(1, 256)
        cw = conv_w[l]
        cb = conv_b[l].reshape(1, 512)

        oa_c = _diff_attn(proj, diff_scal, g_a, 0, batch, seq, seq)
        ob_c = _win_attn(proj, sink, 0, batch, seq, seq)
        yc_c, st_c = _ssd(proj, cw, cb, dtb, acoef, dvec, g_c, 0, batch, seq)
        od_c, rt_c = _ret(proj, log_g, g_d, 0, batch, seq)
        cache_a = (cache_diff_k[:, l].reshape(dec_batch, past, 256), cache_diff_v[:, l].reshape(dec_batch, past, 256))
        cache_b = (cache_win_k[:, l].reshape(dec_batch, past, 128), cache_win_v[:, l].reshape(dec_batch, past, 128))
        oa_l = _diff_attn(proj, diff_scal, g_a, n_ctx, dec_batch, dec_seq, BLOCK, cache=cache_a, rope=rope_a)
        ob_l = _win_attn(proj, sink, n_ctx, dec_batch, dec_seq, BLOCK, cache=cache_b, rope=rope_b)
        yc_l, _ = _ssd(proj, cw, cb, dtb, acoef, dvec, g_c, n_ctx, dec_batch, dec_seq,
                       h0=state_ssd[:, l].reshape(dec_batch, 8, C_P, C_N))
        od_l, _ = _ret(proj, log_g, g_d, n_ctx, dec_batch, dec_seq,
                       h0=state_ret[:, l].reshape(dec_batch, 8, D_V, D_K))

        x1, h2, ids, gates, rank, counts = _merge(
            x, mod_l, ((oa_c, oa_l), (ob_c, ob_l), (yc_c, yc_l), (od_c, od_l)), w_gl, w_branch[l].reshape(N_BRANCH * BRANCH, d).astype(BF16),
            w_out[l].astype(BF16), ln_g[l, 0].reshape(1, d), ln_b[l, 0].reshape(1, d), rw_t, rb_col,
            n_ctx, dec_seq)

        slot_tok, slot_dst, blk_exp, n_used = _dispatch_plan(ids, rank, counts, n_tok)
        y_pair = _moe_experts(h2, slot_tok, slot_dst, blk_exp, n_used, moe_w_gate[l].astype(BF16),
                              moe_w_up[l].astype(BF16), moe_w_down[l].astype(BF16),
                              2 * n_tok + slot_tok.shape[0])
        x = _final(x1, mod_l, y_pair, gates.T,
                   ln_g[l, 1].reshape(1, d), ln_b[l, 1].reshape(1, d), n_ctx, dec_seq)

        pc = proj[:n_ctx]
        ctx_out['diff_k'].append(pc[:, 256:512].reshape(batch, seq, A_HEADS, 2, A_QK))
        ctx_out['diff_v'].append(pc[:, 512:768].reshape(batch, seq, A_HEADS, A_V))
        ctx_out['win_k'].append(pc[:, 1024:1152].reshape(batch, seq, B_KV, B_DIM))
        ctx_out['win_v'].append(pc[:, 1152:1280].reshape(batch, seq, B_KV, B_DIM))
        ctx_out['ssd'].append(st_c.reshape(batch, 2, C_HEADS, C_P, C_N))
        ctx_out['ret'].append(rt_c.reshape(batch, 2, D_HEADS, D_V, D_K))

    y_prompt = x[:n_ctx].reshape(batch, seq, d)
    y_sample = x[n_ctx:].reshape(dec_batch, dec_seq, d)
    stk = lambda k: jnp.stack(ctx_out[k], axis=1)
    return (y_prompt, y_sample, stk('diff_k'), stk('diff_v'), stk('win_k'), stk('win_v'), stk('ssd'), stk('ret'))
```

```python
import functools
import math

import jax
import jax.numpy as jnp
from jax import lax
from jax.experimental import pallas as pl
from jax.experimental.pallas import tpu as pltpu

F32 = jnp.float32
BF16 = jnp.bfloat16

D_MODEL = 1024
DEPTH = 4
GRID_W = 64
BLOCK = 128
WINDOW = 128
CHUNK = 128
A_HEADS, A_QK, A_V = 4, 32, 64
B_HEADS, B_KV, B_DIM = 4, 2, 64
C_HEADS, C_P, C_GROUPS, C_N = 4, 64, 2, 64
D_HEADS, D_K, D_V = 4, 32, 64
BRANCH = 256
N_BRANCH = 4
N_EXP = 16
N_EXP_GROUPS = 4
EXP_PER_GROUP = 4
D_FF_EXP = 512
ROPE_BASE = 10000.0
ALPHA = (2 * DEPTH) ** 0.25
EPS = 1e-5
ADA_DIM = 6 * D_MODEL
NEG = -1e30

N_SMALL = 23 * 128
GATE_OFF = 2824
CDT_OFF = 2048

VMEM_LIMIT = 52 * 1024 * 1024
MOE_ROWS = 256
TOK_TILE = 256
MERGE_TILE = 2 * TOK_TILE

_NN = (((1,), (0,)), ((), ()))
_NT = (((1,), (1,)), ((), ()))
_TN = (((0,), (0,)), ((), ()))


def _params(*sem):
    return pltpu.CompilerParams(dimension_semantics=sem, vmem_limit_bytes=VMEM_LIMIT)


def _mm(a, b, dims=_NN):
    return lax.dot_general(a, b, dims, preferred_element_type=F32)


def _split(a):
    hi = a.astype(BF16)
    return hi, (a - hi.astype(F32)).astype(BF16)


def _mm_f32(a, b, dims=_NN):
    a_hi, a_lo = _split(a)
    b_hi, b_lo = _split(b)
    return (_mm(a_lo, b_hi, dims) + _mm(a_hi, b_lo, dims)) + _mm(a_hi, b_hi, dims)


def _mm_exact_lhs(m_bf, a):
    a1 = a.astype(BF16)
    r1 = a - a1.astype(F32)
    a2 = r1.astype(BF16)
    a3 = (r1 - a2.astype(F32)).astype(BF16)
    return (_mm(m_bf, a3) + _mm(m_bf, a2)) + _mm(m_bf, a1)


def _silu(x):
    return x * jax.nn.sigmoid(x)


LANES = 128
TOKEN_TILE_ROWS = D_MODEL // LANES


def _store_token_tiles(ref, val, lead=(), base=0):
    n = val.shape[0]
    for j in range(TOKEN_TILE_ROWS):
        ref[(*lead, pl.ds(base + j, n, stride=TOKEN_TILE_ROWS), slice(None))] = val[:, j * LANES:(j + 1) * LANES]


def _load_token_tiles(ref, n, lead=()):
    return jnp.concatenate(
        [ref[(*lead, pl.ds(j, n, stride=TOKEN_TILE_ROWS), slice(None))] for j in range(TOKEN_TILE_ROWS)], axis=1)


def _rope(x, c, s_lo, s_hi, shift):
    n = x.shape[1]
    return x * c + pltpu.roll(x, n - shift, 1) * s_lo + pltpu.roll(x, shift, 1) * s_hi


def _ada_kernel(c_ref, w_ref, b_ref, o_ref):
    c = c_ref[...]
    o_ref[...] = _mm_f32(_silu(c), w_ref[...]) + b_ref[...]


def _ada(cvec, w_ada, b_ada):
    rows = cvec.shape[0]
    tn = 1024
    return pl.pallas_call(
        _ada_kernel,
        out_shape=jax.ShapeDtypeStruct((DEPTH, rows, ADA_DIM), F32),
        grid=(DEPTH, ADA_DIM // tn),
        in_specs=[
            pl.BlockSpec((rows, D_MODEL), lambda l, j: (0, 0)),
            pl.BlockSpec((None, D_MODEL, tn), lambda l, j: (l, 0, j)),
            pl.BlockSpec((None, 1, tn), lambda l, j: (l, 0, j)),
        ],
        out_specs=pl.BlockSpec((None, rows, tn), lambda l, j: (l, 0, j)),
        compiler_params=_params("arbitrary", "arbitrary"),
        name="ada",
    )(cvec, w_ada, b_ada.reshape(DEPTH, 1, ADA_DIM))


def _inproj_kernel(x_ref, mod_ref, w_ref, o_ref):
    d = x_ref.shape[1]
    h = x_ref[...] * (1.0 + mod_ref[:, d:2 * d]) + mod_ref[:, 0:d]
    o_ref[...] = _mm(h.astype(BF16), w_ref[...])


def _mod_group(i, tm, n_ctx, dec_seq):
    row = i * tm
    return jnp.where(row < n_ctx, 0, 1 + lax.div(jnp.maximum(row - n_ctx, 0), dec_seq))


def _inproj(x, mod_l, w_small, n_ctx, dec_seq):
    n_tok = x.shape[0]
    tm = TOK_TILE
    grp = functools.partial(_mod_group, tm=tm, n_ctx=n_ctx, dec_seq=dec_seq)
    return pl.pallas_call(
        _inproj_kernel,
        out_shape=jax.ShapeDtypeStruct((n_tok, N_SMALL), F32),
        grid=(n_tok // tm,),
        in_specs=[
            pl.BlockSpec((tm, D_MODEL), lambda i: (i, 0)),
            pl.BlockSpec((None, 1, ADA_DIM), lambda i: (grp(i), 0, 0)),
            pl.BlockSpec((D_MODEL, N_SMALL), lambda i: (0, 0)),
        ],
        out_specs=pl.BlockSpec((tm, N_SMALL), lambda i: (i, 0)),
        compiler_params=_params("arbitrary"),
        name="inproj",
    )(x, mod_l, w_small)


def _diff_attn_kernel(*refs, latent, tq, seq, past):
    if latent:
        (sc_ref, q_ref, k_ref, v_ref, g_ref, ck_ref, cv_ref, rc_ref, rlo_ref, rhi_ref,
         o_ref, kt_scr, v_scr) = refs
    else:
        sc_ref, q_ref, k_ref, v_ref, g_ref, o_ref, kt_scr, v_scr = refs
    qi = pl.program_id(1)
    shift = A_QK // 4

    @pl.when(qi == 0)
    def _():
        k = k_ref[...]
        if latent:
            k = _rope(k, rc_ref[...], rlo_ref[...], rhi_ref[...], shift)
        kt_scr[:, 0:seq] = k.T.astype(BF16)
        if latent:
            kt_scr[:, seq:seq + past] = ck_ref[...].T.astype(BF16)
        ones = jnp.ones((seq + past, LANES - A_V), BF16)
        for h in range(A_HEADS):
            hs = slice(h * A_V, (h + 1) * A_V)
            v_scr[0:seq, h * LANES:h * LANES + A_V] = v_ref[:, hs].astype(BF16)
            if latent:
                v_scr[seq:seq + past, h * LANES:h * LANES + A_V] = cv_ref[:, hs].astype(BF16)
            v_scr[:, h * LANES + A_V:(h + 1) * LANES] = ones

    q = q_ref[...]
    if latent:
        r = pl.ds(pl.multiple_of(qi * tq, tq), tq)
        q = _rope(q, rc_ref[r, :], rlo_ref[r, :], rhi_ref[r, :], shift)
    qb = q.astype(BF16)
    lam = sc_ref[0]
    post = sc_ref[1]
    c = (A_QK ** -0.5) * math.log2(math.e)
    outs = []
    for h in range(A_HEADS):
        maps = []
        for m in range(2):
            off = (h * 2 + m) * A_QK
            s = _mm(qb[:, off:off + A_QK], kt_scr[off:off + A_QK, :])
            p = jnp.exp2(s * c - jnp.max(s, axis=1, keepdims=True) * c)
            ov = _mm(p.astype(BF16), v_scr[:, h * LANES:(h + 1) * LANES])
            maps.append(ov[:, 0:A_V] * (1.0 / ov[:, A_V:A_V + 1]))
        o = maps[0] - lam * maps[1]
        n = o * lax.rsqrt(jnp.mean(o * o, axis=1, keepdims=True) + EPS)
        outs.append((n * g_ref[...]) * post)
    o_ref[...] = jnp.concatenate(outs, axis=1).astype(BF16)


def _diff_attn(proj, scal, g, row0, nb, seq, tq, cache=None, rope=None):
    latent = cache is not None
    nq = seq // tq
    rb = row0 // seq
    qb0 = row0 // tq
    past = cache[0].shape[1] if latent else 0
    in_specs = [
        pl.BlockSpec(memory_space=pltpu.SMEM),
        pl.BlockSpec((tq, 256), lambda b, i: (qb0 + b * nq + i, 0)),
        pl.BlockSpec((seq, 256), lambda b, i: (rb + b, 1)),
        pl.BlockSpec((seq, 256), lambda b, i: (rb + b, 2)),
        pl.BlockSpec((1, A_V), lambda b, i: (0, 0)),
    ]
    args = [scal, proj, proj, proj, g]
    if latent:
        in_specs += [
            pl.BlockSpec((None, past, 256), lambda b, i: (b, 0, 0)),
            pl.BlockSpec((None, past, 256), lambda b, i: (b, 0, 0)),
        ] + [pl.BlockSpec((seq, 256), lambda b, i: (0, 0))] * 3
        args += [cache[0], cache[1], *rope]
    return pl.pallas_call(
        functools.partial(_diff_attn_kernel, latent=latent, tq=tq, seq=seq, past=past),
        out_shape=jax.ShapeDtypeStruct((nb * seq, BRANCH), BF16),
        grid=(nb, nq),
        in_specs=in_specs,
        out_specs=pl.BlockSpec((tq, BRANCH), lambda b, i: (b * nq + i, 0)),
        scratch_shapes=[pltpu.VMEM((256, seq + past), BF16),
                        pltpu.VMEM((seq + past, A_HEADS * LANES), BF16)],
        compiler_params=_params("arbitrary", "arbitrary"),
        name="diff_attn_lat" if latent else "diff_attn_ctx",
    )(*args)


def _win_attn_kernel(*refs, latent, tq, seq):
    if latent:
        (sink_ref, q_ref, k_ref, v_ref, ck_ref, cv_ref, rc_ref, rlo_ref, rhi_ref,
         o_ref, k_scr, v_scr, ck_scr, cv_scr) = refs
    else:
        sink_ref, q_ref, k_ref, v_ref, o_ref, k_scr, v_scr = refs
    qi = pl.program_id(1)
    nq = seq // tq
    shift = B_DIM // 4
    kvw = B_KV * B_DIM

    @pl.when(qi == 0)
    def _():
        k = k_ref[...]
        if latent:
            k = _rope(k, rc_ref[:, 0:kvw], rlo_ref[:, 0:kvw], rhi_ref[:, 0:kvw], shift)
            ck_scr[...] = ck_ref[...].T.astype(BF16)
            cv_scr[...] = cv_ref[...].astype(BF16)
        kt = k.T.astype(BF16)
        for j in range(nq):
            k_scr[j] = kt[:, j * tq:(j + 1) * tq]
        v_scr[...] = v_ref[...].astype(BF16)

    q = q_ref[...]
    if latent:
        r = pl.ds(pl.multiple_of(qi * tq, tq), tq)
        q = _rope(q, rc_ref[r, :], rlo_ref[r, :], rhi_ref[r, :], shift)
        near = (jnp.maximum(qi - 1, 0), qi, jnp.minimum(qi + 1, nq - 1))
        kl = jnp.concatenate([k_scr[j] for j in near], axis=1)
        vl = jnp.concatenate([v_scr[pl.ds(pl.multiple_of(j * tq, tq), tq), :] for j in near], axis=0)
        ii = lax.broadcasted_iota(jnp.int32, (tq, 3 * tq), 0)
        jj = lax.broadcasted_iota(jnp.int32, (tq, 3 * tq), 1)
        lo = jnp.where(qi > 0, 0, tq)
        hi = jnp.where(qi < nq - 1, 3 * tq, 2 * tq)
        valid = (jnp.abs(jj - tq - ii) <= WINDOW) & (jj >= lo) & (jj < hi)
    else:
        kl = k_scr[0]
        vl = v_scr[...]
    qb = q.astype(BF16)
    scale = B_DIM ** -0.5
    ratio = B_HEADS // B_KV
    outs = []
    for h in range(B_HEADS):
        gsl = slice((h // ratio) * B_DIM, (h // ratio + 1) * B_DIM)
        qh = qb[:, h * B_DIM:(h + 1) * B_DIM]
        snk = sink_ref[h]
        s = _mm(qh, kl[gsl, :]) * scale
        if latent:
            s = jnp.where(valid, s, NEG)
            sc = _mm(qh, ck_scr[gsl, :]) * scale
            m = jnp.maximum(jnp.maximum(jnp.max(s, axis=1, keepdims=True),
                                        jnp.max(sc, axis=1, keepdims=True)), snk)
            pc = jnp.exp(sc - m)
        else:
            m = jnp.maximum(jnp.max(s, axis=1, keepdims=True), snk)
        p = jnp.exp(s - m)
        den = jnp.sum(p, axis=1, keepdims=True) + jnp.exp(snk - m)
        if latent:
            den = den + jnp.sum(pc, axis=1, keepdims=True)
        inv = 1.0 / den
        o = _mm((p * inv).astype(BF16), vl[:, gsl])
        if latent:
            o = o + _mm((pc * inv).astype(BF16), cv_scr[:, gsl])
        outs.append(o)
    o_ref[...] = jnp.concatenate(outs, axis=1).astype(BF16)


def _win_attn(proj, sink, row0, nb, seq, tq, cache=None, rope=None):
    latent = cache is not None
    nq = seq // tq
    rb = row0 // seq
    qb0 = row0 // tq
    kvw = B_KV * B_DIM
    in_specs = [
        pl.BlockSpec(memory_space=pltpu.SMEM),
        pl.BlockSpec((tq, 256), lambda b, i: (qb0 + b * nq + i, 3)),
        pl.BlockSpec((seq, kvw), lambda b, i: (rb + b, 8)),
        pl.BlockSpec((seq, kvw), lambda b, i: (rb + b, 9)),
    ]
    args = [sink, proj, proj, proj]
    scratch = [pltpu.VMEM((nq, kvw, tq), BF16), pltpu.VMEM((seq, kvw), BF16)]
    if latent:
        past = cache[0].shape[1]
        in_specs += [
            pl.BlockSpec((None, past, kvw), lambda b, i: (b, 0, 0)),
            pl.BlockSpec((None, past, kvw), lambda b, i: (b, 0, 0)),
        ] + [pl.BlockSpec((seq, 256), lambda b, i: (0, 0))] * 3
        args += [cache[0], cache[1], *rope]
        scratch += [pltpu.VMEM((kvw, past), BF16), pltpu.VMEM((past, kvw), BF16)]
    return pl.pallas_call(
        functools.partial(_win_attn_kernel, latent=latent, tq=tq, seq=seq),
        out_shape=jax.ShapeDtypeStruct((nb * seq, BRANCH), BF16),
        grid=(nb, nq),
        in_specs=in_specs,
        out_specs=pl.BlockSpec((tq, BRANCH), lambda b, i: (b * nq + i, 0)),
        scratch_shapes=scratch,
        compiler_params=_params("arbitrary", "arbitrary"),
        name="win_attn_lat" if latent else "win_attn_ctx",
    )(*args)


def _conv_silu(u, w, b):
    n = u.shape[0]
    rows = lax.broadcasted_iota(jnp.int32, u.shape, 0)
    up = jnp.where(rows == 0, 0.0, pltpu.roll(u, 1, 0))
    un = jnp.where(rows == n - 1, 0.0, pltpu.roll(u, n - 1, 0))
    return _silu(up * w[0:1, :] + u * w[1:2, :] + un * w[2:3, :] + b)


def _ssd_kernel(*refs, seq, has_h0):
    if has_h0:
        (cx_ref, cz_ref, cbc_ref, cdt_ref, cw_ref, cb_ref, dtb_ref, ac_ref, dv_ref, g_ref, h0_ref,
         y_ref, st_ref, xs, bcs, dts, ybuf, ybuf_b, hs) = refs
    else:
        (cx_ref, cz_ref, cbc_ref, cdt_ref, cw_ref, cb_ref, dtb_ref, ac_ref, dv_ref, g_ref,
         y_ref, st_ref, xs, bcs, dts, ybuf, ybuf_b, hs) = refs
    nc = seq // CHUNK
    xw = C_HEADS * C_P
    xs[...] = _conv_silu(cx_ref[...], cw_ref[:, 0:xw], cb_ref[:, 0:xw])
    bcs[...] = _conv_silu(cbc_ref[...], cw_ref[:, xw:2 * xw], cb_ref[:, xw:2 * xw])
    z = cdt_ref[...] + dtb_ref[...]
    dts[...] = jnp.maximum(z, 0.0) + jnp.log1p(jnp.exp(-jnp.abs(z)))
    if has_h0:
        hs[...] = h0_ref[...]
    else:
        hs[...] = jnp.zeros(hs.shape, F32)

    ri = lax.broadcasted_iota(jnp.int32, (CHUNK, CHUNK), 0)
    ci = lax.broadcasted_iota(jnp.int32, (CHUNK, CHUNK), 1)
    gw = C_GROUPS * C_N

    def chunk(c, d):
        r = pl.ds(pl.multiple_of(c * CHUNK, CHUNK), CHUNK)
        tri = (ri >= ci) if d == 0 else (ci >= ri)
        dt = dts[r, :]
        cs = _mm_exact_lhs(tri.astype(BF16), dt * ac_ref[...])
        cst = cs.T
        x = xs[r, :]
        bc = bcs[r, :]
        ys = []
        for g in range(C_GROUPS):
            bm = bc[:, g * C_N:(g + 1) * C_N]
            cb = bc[:, gw + g * C_N:gw + (g + 1) * C_N].astype(BF16)
            gram = _mm(cb, bm.astype(BF16), _NT)
            for hh in range(C_HEADS // C_GROUPS):
                h = g * (C_HEADS // C_GROUPS) + hh
                col = d * C_HEADS + h
                xb = (x[:, h * C_P:(h + 1) * C_P] * dt[:, col:col + 1]).astype(BF16)
                cc = cs[:, col:col + 1]
                dec = jnp.exp(jnp.where(tri, cc - cst[col:col + 1, :], NEG))
                tot = cc[CHUNK - 1:CHUNK, :] if d == 0 else cc[0:1, :]
                hin = hs[col]
                y = _mm((gram * dec).astype(BF16), xb)
                y = y + _mm(cb, hin.astype(BF16), _NT) * jnp.exp(cc)
                bd = (bm * jnp.exp(tot - cc)).astype(BF16)
                hs[col] = hin * jnp.exp(tot) + _mm(xb, bd, _TN)
                ys.append(y)
        return r, jnp.concatenate(ys, axis=1)

    def scan(t, carry):
        r, y = chunk(t, 0)
        ybuf[r, :] = y
        r, y = chunk(nc - 1 - t, 1)
        ybuf_b[r, :] = y
        return carry

    lax.fori_loop(0, nc, scan, 0)

    def finish(c, carry):
        r = pl.ds(pl.multiple_of(c * CHUNK, CHUNK), CHUNK)
        y = (ybuf[r, :] + ybuf_b[r, :]) + xs[r, :] * dv_ref[...]
        y = y * _silu(cz_ref[r, :])
        gl = xw // C_GROUPS
        parts = []
        for g in range(C_GROUPS):
            seg = y[:, g * gl:(g + 1) * gl]
            parts.append(seg * lax.rsqrt(jnp.mean(seg * seg, axis=1, keepdims=True) + EPS))
        y_ref[r, :] = (jnp.concatenate(parts, axis=1) * g_ref[...]).astype(BF16)
        return carry

    lax.fori_loop(0, nc, finish, 0)
    st_ref[...] = hs[...]


def _ssd(proj, conv_w, conv_b, dtb, acoef, dvec, g, row0, nb, seq, h0=None):
    has_h0 = h0 is not None
    rb = row0 // seq
    nst = 2 * C_HEADS
    in_specs = [
        pl.BlockSpec((seq, 256), lambda b: (rb + b, 5)),
        pl.BlockSpec((seq, 256), lambda b: (rb + b, 6)),
        pl.BlockSpec((seq, 256), lambda b: (rb + b, 7)),
        pl.BlockSpec((seq, 128), lambda b: (rb + b, 22)),
        pl.BlockSpec((3, 512), lambda b: (0, 0)),
        pl.BlockSpec((1, 512), lambda b: (0, 0)),
        pl.BlockSpec((1, 128), lambda b: (0, 0)),
        pl.BlockSpec((1, 128), lambda b: (0, 0)),
        pl.BlockSpec((1, 256), lambda b: (0, 0)),
        pl.BlockSpec((1, 256), lambda b: (0, 0)),
    ]
    args = [proj, proj, proj, proj, conv_w, conv_b, dtb, acoef, dvec, g]
    if has_h0:
        in_specs.append(pl.BlockSpec((None, nst, C_P, C_N), lambda b: (b, 0, 0, 0)))
        args.append(h0)
    return pl.pallas_call(
        functools.partial(_ssd_kernel, seq=seq, has_h0=has_h0),
        out_shape=(jax.ShapeDtypeStruct((nb * seq, BRANCH), BF16),
                   jax.ShapeDtypeStruct((nb, nst, C_P, C_N), F32)),
        grid=(nb,),
        in_specs=in_specs,
        out_specs=(pl.BlockSpec((seq, BRANCH), lambda b: (b, 0)),
                   pl.BlockSpec((None, nst, C_P, C_N), lambda b: (b, 0, 0, 0))),
        scratch_shapes=[pltpu.VMEM((seq, 256), F32), pltpu.VMEM((seq, 256), F32),
                        pltpu.VMEM((seq, 128), F32), pltpu.VMEM((seq, 256), F32),
                        pltpu.VMEM((seq, 256), F32), pltpu.VMEM((nst, C_P, C_N), F32)],
        compiler_params=_params("arbitrary"),
        name="ssd_lat" if has_h0 else "ssd_ctx",
    )(*args)


def _ret_kernel(*refs, seq, has_h0):
    if has_h0:
        lg_ref, q_ref, k_ref, v_ref, gt_ref, g_ref, h0_ref, y_ref, st_ref, ybuf, ybuf_b, hs = refs
    else:
        lg_ref, q_ref, k_ref, v_ref, gt_ref, g_ref, y_ref, st_ref, ybuf, ybuf_b, hs = refs
    nc = seq // CHUNK
    if has_h0:
        hs[...] = h0_ref[...]
    else:
        hs[...] = jnp.zeros(hs.shape, F32)
    ri = lax.broadcasted_iota(jnp.int32, (CHUNK, CHUNK), 0)
    ci = lax.broadcasted_iota(jnp.int32, (CHUNK, CHUNK), 1)
    pos = lax.broadcasted_iota(jnp.int32, (CHUNK, 1), 0).astype(F32)
    kscale = D_K ** -0.5

    def chunk(c, d):
        r = pl.ds(pl.multiple_of(c * CHUNK, CHUNK), CHUNK)
        q = q_ref[r, :]
        k = k_ref[r, :] * kscale
        v = v_ref[r, :]
        if d == 0:
            tri, dist = ri >= ci, (ri - ci).astype(F32)
            steps_in, steps_out = pos + 1.0, (CHUNK - 1.0) - pos
        else:
            tri, dist = ci >= ri, (ci - ri).astype(F32)
            steps_in, steps_out = CHUNK - pos, pos
        ys = []
        for h in range(D_HEADS):
            col = d * D_HEADS + h
            lg = lg_ref[col]
            dec = jnp.exp(jnp.where(tri, dist * lg, NEG))
            e_in = jnp.exp(steps_in * lg)
            e_tot = e_in[CHUNK - 1:CHUNK, :] if d == 0 else e_in[0:1, :]
            qb = q[:, h * D_K:(h + 1) * D_K].astype(BF16)
            km = k[:, h * D_K:(h + 1) * D_K]
            vb = v[:, h * D_V:(h + 1) * D_V].astype(BF16)
            hin = hs[col]
            y = _mm((_mm(qb, km.astype(BF16), _NT) * dec).astype(BF16), vb)
            y = y + _mm(qb, hin.astype(BF16), _NT) * e_in
            bd = (km * jnp.exp(steps_out * lg)).astype(BF16)
            hs[col] = hin * e_tot + _mm(vb, bd, _TN)
            ys.append(y)
        return r, ys

    def scan(t, carry):
        r, ys = chunk(t, 0)
        ybuf[r, :] = jnp.concatenate(ys, axis=1)
        r, ys = chunk(nc - 1 - t, 1)
        ybuf_b[r, :] = jnp.concatenate(ys, axis=1)
        return carry

    lax.fori_loop(0, nc, scan, 0)

    def finish(c, carry):
        r = pl.ds(pl.multiple_of(c * CHUNK, CHUNK), CHUNK)
        yf = ybuf[r, :] + ybuf_b[r, :]
        parts = []
        for h in range(D_HEADS):
            o = yf[:, h * D_V:(h + 1) * D_V]
            o = o - jnp.mean(o, axis=1, keepdims=True)
            parts.append(o * lax.rsqrt(jnp.mean(o * o, axis=1, keepdims=True) + EPS))
        y = (jnp.concatenate(parts, axis=1) * g_ref[...]) * _silu(gt_ref[r, :])
        y_ref[r, :] = y.astype(BF16)
        return carry

    lax.fori_loop(0, nc, finish, 0)
    st_ref[...] = hs[...]


def _ret(proj, log_g, g, row0, nb, seq, h0=None):
    has_h0 = h0 is not None
    rb = row0 // seq
    nst = 2 * D_HEADS
    in_specs = [
        pl.BlockSpec(memory_space=pltpu.SMEM),
        pl.BlockSpec((seq, 128), lambda b: (rb + b, 16)),
        pl.BlockSpec((seq, 128), lambda b: (rb + b, 17)),
        pl.BlockSpec((seq, 256), lambda b: (rb + b, 9)),
        pl.BlockSpec((seq, 256), lambda b: (rb + b, 10)),
        pl.BlockSpec((1, 256), lambda b: (0, 0)),
    ]
    args = [log_g, proj, proj, proj, proj, g]
    if has_h0:
        in_specs.append(pl.BlockSpec((None, nst, D_V, D_K), lambda b: (b, 0, 0, 0)))
        args.append(h0)
    return pl.pallas_call(
        functools.partial(_ret_kernel, seq=seq, has_h0=has_h0),
        out_shape=(jax.ShapeDtypeStruct((nb * seq, BRANCH), BF16),
                   jax.ShapeDtypeStruct((nb, nst, D_V, D_K), F32)),
        grid=(nb,),
        in_specs=in_specs,
        out_specs=(pl.BlockSpec((seq, BRANCH), lambda b: (b, 0)),
                   pl.BlockSpec((None, nst, D_V, D_K), lambda b: (b, 0, 0, 0))),
        scratch_shapes=[pltpu.VMEM((seq, 256), F32), pltpu.VMEM((seq, 256), F32),
                        pltpu.VMEM((nst, D_V, D_K), F32)],
        compiler_params=_params("arbitrary"),
        name="ret_lat" if has_h0 else "ret_ctx",
    )(*args)


def _route(sel, s):
    row = lambda a, e: a[e:e + 1, :]
    best = None
    grp = None
    for g in range(N_EXP_GROUPS):
        vals = [row(sel, g * EXP_PER_GROUP + j) for j in range(EXP_PER_GROUP)]
        score = None
        for a in range(EXP_PER_GROUP):
            for b in range(a + 1, EXP_PER_GROUP):
                pair = vals[a] + vals[b]
                score = pair if score is None else jnp.maximum(score, pair)
        if best is None:
            best, grp = score, jnp.zeros(score.shape, jnp.int32)
        else:
            better = score > best
            best = jnp.where(better, score, best)
            grp = jnp.where(better, g, grp)

    def pick(a, j):
        out = row(a, j)
        for g in range(1, N_EXP_GROUPS):
            out = jnp.where(grp == g, row(a, g * EXP_PER_GROUP + j), out)
        return out

    cand = [pick(sel, j) for j in range(EXP_PER_GROUP)]
    aff = [pick(s, j) for j in range(EXP_PER_GROUP)]

    def arg_first_max(vals):
        top, idx = vals[0], jnp.zeros(vals[0].shape, jnp.int32)
        for j in range(1, len(vals)):
            better = vals[j] > top
            top = jnp.where(better, vals[j], top)
            idx = jnp.where(better, j, idx)
        return idx

    def take(vals, idx):
        out = vals[0]
        for j in range(1, len(vals)):
            out = jnp.where(idx == j, vals[j], out)
        return out

    i1 = arg_first_max(cand)
    i2 = arg_first_max([jnp.where(i1 == j, -jnp.inf, cand[j]) for j in range(EXP_PER_GROUP)])
    w1, w2 = take(aff, i1), take(aff, i2)
    tot = w1 + w2
    ids = jnp.concatenate([grp * EXP_PER_GROUP + i1, grp * EXP_PER_GROUP + i2], axis=0)
    gates = jnp.concatenate([w1 / tot, w2 / tot], axis=0)
    return ids, gates


def _merge_kernel(*refs, n_ctx_tiles):
    (x_ref, mod_ref, wgl_ref, wbr_ref, wout_ref, lng_ref, lnb_ref, rw_ref, rb_ref, tri_ref) = refs[:10]
    br_refs = refs[10:10 + 2 * N_BRANCH]
    x1_ref, h2_ref, ids_ref, gates_ref, rank_ref, cnt_ref = refs[10 + 2 * N_BRANCH:]
    d = D_MODEL

    @pl.when(pl.program_id(0) == 0)
    def _():
        cnt_ref[...] = jnp.zeros(cnt_ref.shape, F32)

    is_ctx = pl.program_id(0) < n_ctx_tiles
    sub = tri_ref.shape[0]
    for part in range(x_ref.shape[0] // sub):
        rs = slice(part * sub, (part + 1) * sub)
        x = x_ref[rs, :]
        hb = (x * (1.0 + mod_ref[:, d:2 * d]) + mod_ref[:, 0:d]).astype(BF16)
        merged = None
        for k in range(N_BRANCH):
            gate = jax.nn.sigmoid(_mm(hb, wgl_ref[:, k * d:(k + 1) * d]))
            br_k = jnp.where(is_ctx, br_refs[2 * k][rs, :], br_refs[2 * k + 1][rs, :])
            up = _mm(br_k, wbr_ref[k * BRANCH:(k + 1) * BRANCH, :])
            merged = gate * up if merged is None else merged + gate * up
        mix = _mm(merged.astype(BF16), wout_ref[...])
        y = ALPHA * x + mod_ref[:, 2 * d:3 * d] * mix
        y = y - jnp.mean(y, axis=1, keepdims=True)
        x1 = (y * lax.rsqrt(jnp.mean(y * y, axis=1, keepdims=True) + EPS)) * lng_ref[...] + lnb_ref[...]
        x1_ref[rs, :] = x1
        h2 = x1 * (1.0 + mod_ref[:, 4 * d:5 * d]) + mod_ref[:, 3 * d:4 * d]
        _store_token_tiles(h2_ref, h2, base=part * sub * TOKEN_TILE_ROWS)
        s = jax.nn.sigmoid(_mm_f32(rw_ref[...], h2, _NT))
        ids, gates = _route(s + rb_ref[...], s)
        ids_ref[:, rs] = ids
        gates_ref[:, rs] = gates
        expert = lax.broadcasted_iota(jnp.int32, (N_EXP, sub), 0)
        hot = [(expert == ids[k:k + 1, :]).astype(F32) for k in range(2)]
        both = hot[0] + hot[1]
        incl = _mm(both.astype(BF16), tri_ref[...])
        before = cnt_ref[...] + (incl - both)
        rank_ref[:, rs] = jnp.concatenate(
            [jnp.sum(hk * before, axis=0, keepdims=True) for hk in hot], axis=0).astype(jnp.int32)
        cnt_ref[...] = cnt_ref[...] + incl[:, sub - 1:sub]


def _merge(x, mod_l, branches, w_gl, w_br, w_out, ln_g, ln_b, rw_t, rb, n_ctx, dec_seq):
    n_tok = x.shape[0]
    tm = MERGE_TILE
    nct = n_ctx // tm
    grp = functools.partial(_mod_group, tm=tm, n_ctx=n_ctx, dec_seq=dec_seq)
    full = lambda shape: pl.BlockSpec(shape, lambda i: (0,) * len(shape))
    pos = jnp.arange(TOK_TILE)
    tri = (pos[:, None] <= pos[None, :]).astype(BF16)
    br_specs = [pl.BlockSpec((tm, BRANCH), lambda i: (jnp.minimum(i, nct - 1), 0)),
                pl.BlockSpec((tm, BRANCH), lambda i: (jnp.maximum(i - nct, 0), 0))] * N_BRANCH
    br_args = [a for pair in branches for a in pair]
    return pl.pallas_call(
        functools.partial(_merge_kernel, n_ctx_tiles=nct),
        out_shape=(jax.ShapeDtypeStruct((n_tok, D_MODEL), F32),
                   jax.ShapeDtypeStruct((n_tok * TOKEN_TILE_ROWS, LANES), F32),
                   jax.ShapeDtypeStruct((2, n_tok), jnp.int32),
                   jax.ShapeDtypeStruct((2, n_tok), F32),
                   jax.ShapeDtypeStruct((2, n_tok), jnp.int32),
                   jax.ShapeDtypeStruct((N_EXP, 1), F32)),
        grid=(n_tok // tm,),
        in_specs=[
            pl.BlockSpec((tm, D_MODEL), lambda i: (i, 0)),
            pl.BlockSpec((None, 1, ADA_DIM), lambda i: (grp(i), 0, 0)),
            full((D_MODEL, N_BRANCH * D_MODEL)),
            full((N_BRANCH * BRANCH, D_MODEL)),
            full((D_MODEL, D_MODEL)),
            full((1, D_MODEL)),
            full((1, D_MODEL)),
            full((N_EXP, D_MODEL)),
            full((N_EXP, 1)),
            full((TOK_TILE, TOK_TILE)),
        ] + br_specs,
        out_specs=(pl.BlockSpec((tm, D_MODEL), lambda i: (i, 0)),
                   pl.BlockSpec((tm * TOKEN_TILE_ROWS, LANES), lambda i: (i, 0)),
                   pl.BlockSpec((2, tm), lambda i: (0, i)),
                   pl.BlockSpec((2, tm), lambda i: (0, i)),
                   pl.BlockSpec((2, tm), lambda i: (0, i)),
                   full((N_EXP, 1))),
        compiler_params=_params("arbitrary"),
        name="merge",
    )(x, mod_l, w_gl, w_br, w_out, ln_g, ln_b, rw_t, rb, tri, *br_args)


def _dispatch_kernel(dest_ref, h_ref, zero_hbm, x_hbm, sem, *, n_tok):
    del zero_hbm
    tr = TOKEN_TILE_ROWS
    base = pl.program_id(0) * TOK_TILE

    def slot_copy(r, dst):
        return pltpu.make_async_copy(h_ref.at[pl.ds(r * tr, tr), :], x_hbm.at[pl.ds(dst * tr, tr), :], sem)

    for r in range(TOK_TILE):
        slot_copy(r, dest_ref[base + r]).start()
        slot_copy(r, dest_ref[n_tok + base + r]).start()
    for r in range(2 * TOK_TILE):
        slot_copy(0, 0).wait()


def _moe_dispatch(h2_tiles, dest, n_slots):
    tr = TOKEN_TILE_ROWS
    n_tok = h2_tiles.shape[0] // tr
    any_spec = pl.BlockSpec(memory_space=pl.ANY)
    return pl.pallas_call(
        functools.partial(_dispatch_kernel, n_tok=n_tok),
        out_shape=jax.ShapeDtypeStruct((n_slots * tr, LANES), F32),
        grid_spec=pltpu.PrefetchScalarGridSpec(
            num_scalar_prefetch=1,
            grid=(n_tok // TOK_TILE,),
            in_specs=[pl.BlockSpec((TOK_TILE * tr, LANES), lambda i, d: (i, 0)), any_spec],
            out_specs=any_spec,
            scratch_shapes=[pltpu.SemaphoreType.DMA(())],
        ),
        input_output_aliases={2: 0},
        compiler_params=_params("arbitrary"),
        name="moe_dispatch",
    )(dest, h2_tiles, jnp.zeros((n_slots * tr, LANES), F32))


def _moe_kernel(be_ref, nu_ref, x_ref, wg_ref, wu_ref, wd_ref, o_ref):
    del be_ref
    i = pl.program_id(0)

    @pl.when(i < nu_ref[0])
    def _():
        x = _load_token_tiles(x_ref, MOE_ROWS).astype(BF16)
        act = _silu(_mm(x, wg_ref[...].astype(BF16))) * _mm(x, wu_ref[...].astype(BF16))
        _store_token_tiles(o_ref, _mm(act.astype(BF16), wd_ref[...].astype(BF16)))

    @pl.when(i >= nu_ref[0])
    def _():
        o_ref[...] = jnp.zeros(o_ref.shape, F32)


def _moe_experts(x_tiles, blk_exp, n_used, wg, wu, wd, layer):
    n_blk = blk_exp.shape[0]
    blk = pl.BlockSpec((MOE_ROWS * TOKEN_TILE_ROWS, LANES), lambda i, be, nu: (i, 0))
    return pl.pallas_call(
        _moe_kernel,
        out_shape=jax.ShapeDtypeStruct(x_tiles.shape, F32),
        grid_spec=pltpu.PrefetchScalarGridSpec(
            num_scalar_prefetch=2,
            grid=(n_blk,),
            in_specs=[
                blk,
                pl.BlockSpec((None, None, D_MODEL, D_FF_EXP), lambda i, be, nu: (layer, be[i], 0, 0)),
                pl.BlockSpec((None, None, D_MODEL, D_FF_EXP), lambda i, be, nu: (layer, be[i], 0, 0)),
                pl.BlockSpec((None, None, D_FF_EXP, D_MODEL), lambda i, be, nu: (layer, be[i], 0, 0)),
            ],
            out_specs=blk,
        ),
        compiler_params=_params("arbitrary"),
        name="moe_experts",
    )(blk_exp, n_used, x_tiles, wg, wu, wd)


def _final_kernel(dest_ref, x1_ref, mod_ref, gt_ref, lng_ref, lnb_ref, y_hbm, *rest, n_ctx_tiles, n_tok):
    *o_refs, ybuf, sem = rest
    d = D_MODEL
    n = x1_ref.shape[0]
    tr = TOKEN_TILE_ROWS
    i = pl.program_id(0)
    slot = lax.rem(i, 2)

    def fetch_copy(src, k, r, buf):
        return pltpu.make_async_copy(y_hbm.at[pl.ds(src * tr, tr), :],
                                     ybuf.at[buf, k, pl.ds(r * tr, tr), :], sem.at[buf])

    def start_fetch(tile, buf):
        for k in range(2):
            for r in range(n):
                fetch_copy(dest_ref[k * n_tok + tile * n + r], k, r, buf).start()

    @pl.when(i == 0)
    def _():
        start_fetch(0, 0)

    for k in range(2):
        for r in range(n):
            fetch_copy(0, k, r, slot).wait()

    @pl.when(i + 1 < pl.num_programs(0))
    def _():
        start_fetch(i + 1, 1 - slot)

    gt = gt_ref[...]
    ffn = (_load_token_tiles(ybuf, n, lead=(slot, 0)) * gt[:, 0:1]
           + _load_token_tiles(ybuf, n, lead=(slot, 1)) * gt[:, 1:2])
    y = ALPHA * x1_ref[...] + mod_ref[:, 5 * d:6 * d] * ffn
    y = y - jnp.mean(y, axis=1, keepdims=True)
    out = (y * lax.rsqrt(jnp.mean(y * y, axis=1, keepdims=True) + EPS)) * lng_ref[...] + lnb_ref[...]
    if len(o_refs) == 1:
        o_refs[0][...] = out
    else:
        @pl.when(pl.program_id(0) < n_ctx_tiles)
        def _():
            o_refs[0][...] = out

        @pl.when(pl.program_id(0) >= n_ctx_tiles)
        def _():
            o_refs[1][...] = out


def _final(x1, mod_l, y_slots, dest, gates_t, ln_g, ln_b, n_ctx, dec_seq, split=False):
    n_tok = x1.shape[0]
    tm = TOK_TILE
    nt = n_tok // tm
    nct = n_ctx // tm
    grp = functools.partial(_mod_group, tm=tm, n_ctx=n_ctx, dec_seq=dec_seq)
    tile = pl.BlockSpec((tm, D_MODEL), lambda i, dst: (i, 0))
    vec = pl.BlockSpec((1, D_MODEL), lambda i, dst: (0, 0))
    if split:
        out_shape = (jax.ShapeDtypeStruct((n_ctx, D_MODEL), F32), jax.ShapeDtypeStruct((n_tok - n_ctx, D_MODEL), F32))
        out_specs = (pl.BlockSpec((tm, D_MODEL), lambda i, dst: (jnp.minimum(i, nct - 1), 0)),
                     pl.BlockSpec((tm, D_MODEL), lambda i, dst: (jnp.maximum(i - nct, 0), 0)))
    else:
        out_shape = jax.ShapeDtypeStruct((n_tok, D_MODEL), F32)
        out_specs = tile
    return pl.pallas_call(
        functools.partial(_final_kernel, n_ctx_tiles=nct, n_tok=n_tok),
        out_shape=out_shape,
        grid_spec=pltpu.PrefetchScalarGridSpec(
            num_scalar_prefetch=1,
            grid=(nt,),
            in_specs=[tile, pl.BlockSpec((None, 1, ADA_DIM), lambda i, dst: (grp(i), 0, 0)),
                      pl.BlockSpec((tm, 2), lambda i, dst: (i, 0)), vec, vec,
                      pl.BlockSpec(memory_space=pl.ANY)],
            out_specs=out_specs,
            scratch_shapes=[pltpu.VMEM((2, 2, tm * TOKEN_TILE_ROWS, LANES), F32),
                            pltpu.SemaphoreType.DMA((2,))],
        ),
        compiler_params=_params("arbitrary"),
        name="final_norm",
    )(dest, x1, mod_l, gates_t, ln_g, ln_b, y_slots)


def _rope_tables(seq, dim, width):
    nf = dim // 4
    t = jnp.arange(seq)
    pos = jnp.stack([t // GRID_W, t % GRID_W], axis=-1).astype(F32)
    inv = ROPE_BASE ** (-jnp.arange(nf, dtype=F32) / nf)
    ang = pos[:, :, None] * inv
    cos, sin = jnp.cos(ang), jnp.sin(ang)
    zero = jnp.zeros_like(sin)
    c = jnp.stack([cos, cos], axis=2).reshape(seq, dim)
    s_lo = jnp.stack([-sin, zero], axis=2).reshape(seq, dim)
    s_hi = jnp.stack([zero, sin], axis=2).reshape(seq, dim)
    rep = width // dim
    return tuple(jnp.tile(a, (1, rep)) for a in (c, s_lo, s_hi))


def _dispatch_plan(ids, rank, counts, n_tok):
    n_assign = 2 * n_tok
    flat_e = ids.reshape(n_assign)
    onehot = (flat_e[:, None] == jnp.arange(N_EXP, dtype=jnp.int32)[None, :]).astype(jnp.int32)
    counts = counts.reshape(N_EXP).astype(jnp.int32)
    padded = (counts + MOE_ROWS - 1) // MOE_ROWS * MOE_ROWS
    pad_end = jnp.cumsum(padded)
    pad_start = pad_end - padded
    dest = (jnp.sum(onehot * pad_start[None, :], axis=1) + rank.reshape(n_assign)).astype(jnp.int32)
    n_blk = n_assign // MOE_ROWS + N_EXP
    blk_start = jnp.arange(n_blk, dtype=jnp.int32) * MOE_ROWS
    blk_exp = jnp.sum((blk_start[:, None] >= pad_end[None, :]).astype(jnp.int32), axis=1)
    blk_exp = jnp.minimum(blk_exp, N_EXP - 1).astype(jnp.int32)
    n_used = (pad_end[-1] // MOE_ROWS).astype(jnp.int32).reshape(1)
    return dest, blk_exp, n_used


def kernel(x_prompt, x_sample, cache_diff_k, cache_diff_v, cache_win_k, cache_win_v, state_ssd, state_ret,
           c, c_ctx, w_ada, b_ada, w_in, diff_lambda, diff_norm_g, win_sink, conv_w, conv_b,
           ssd_A_log, ssd_dt_bias, ssd_D, ssd_norm_g, ret_decay_logit, ret_norm_g, w_branch, w_out,
           ln_g, ln_b, router_w, router_b, moe_w_gate, moe_w_up, moe_w_down):
    batch, seq, d = x_prompt.shape
    dec_batch, dec_seq, _ = x_sample.shape
    past = cache_diff_k.shape[2]
    n_ctx, n_lat = batch * seq, dec_batch * dec_seq
    n_tok = n_ctx + n_lat
    assert d == D_MODEL and n_ctx % dec_seq == 0 and seq % CHUNK == 0 and dec_seq % CHUNK == 0

    x = jnp.concatenate([x_prompt.reshape(n_ctx, d), x_sample.reshape(n_lat, d)], axis=0)

    n_mod = 1 + dec_batch
    n_mod_pad = -(-n_mod // 8) * 8
    cvec = jnp.concatenate([c_ctx[None, :], c, jnp.zeros((n_mod_pad - n_mod, d), F32)], axis=0)
    mod = _ada(cvec, w_ada, b_ada)

    rope_a = _rope_tables(dec_seq, A_QK, 256)
    rope_b = _rope_tables(dec_seq, B_DIM, 256)
    rw_t = router_w.T
    rb_col = router_b.reshape(N_EXP, 1)

    ctx_out = {k: [] for k in ('diff_k', 'diff_v', 'win_k', 'win_v', 'ssd', 'ret')}
    for l in range(DEPTH):
        mod_l = mod[l, :n_mod].reshape(n_mod, 1, ADA_DIM)
        wl = w_in[l]
        w_small = jnp.concatenate(
            [wl[:, :CDT_OFF], wl[:, CDT_OFF + 8:GATE_OFF], wl[:, CDT_OFF:CDT_OFF + 8],
             jnp.zeros((d, N_SMALL - GATE_OFF), F32)], axis=1).astype(BF16)
        w_gl = wl[:, GATE_OFF:].astype(BF16)
        proj = _inproj(x, mod_l, w_small, n_ctx, dec_seq)

        lam_init = 0.8 - 0.6 * math.exp(-0.3 * l)
        lv = diff_lambda[l]
        lam = jnp.exp(jnp.sum(lv[0] * lv[1])) - jnp.exp(jnp.sum(lv[2] * lv[3])) + lam_init
        diff_scal = jnp.stack([lam, jnp.asarray(1.0 - lam_init, F32)]).astype(F32)
        g_a = diff_norm_g[l].reshape(1, A_V)
        sink = win_sink[l]
        dtb = jnp.zeros((1, 128), F32).at[0, :8].set(ssd_dt_bias[l].reshape(8))
        acoef = jnp.zeros((1, 128), F32).at[0, :8].set(-jnp.exp(ssd_A_log[l]).reshape(8))
        dvec = jnp.repeat(ssd_D[l], C_P).reshape(1, 256)
        g_c = ssd_norm_g[l].reshape(1, 256)
        log_g = jax.nn.log_sigmoid(ret_decay_logit[l]).reshape(8)
        g_d = ret_norm_g[l].reshape(1, 256)
        cw = conv_w[l]
        cb = conv_b[l].reshape(1, 512)

        oa_c = _diff_attn(proj, diff_scal, g_a, 0, batch, seq, seq)
        ob_c = _win_attn(proj, sink, 0, batch, seq, seq)
        yc_c, st_c = _ssd(proj, cw, cb, dtb, acoef, dvec, g_c, 0, batch, seq)
        od_c, rt_c = _ret(proj, log_g, g_d, 0, batch, seq)
        cache_a = (cache_diff_k[:, l].reshape(dec_batch, past, 256), cache_diff_v[:, l].reshape(dec_batch, past, 256))
        cache_b = (cache_win_k[:, l].reshape(dec_batch, past, 128), cache_win_v[:, l].reshape(dec_batch, past, 128))
        oa_l = _diff_attn(proj, diff_scal, g_a, n_ctx, dec_batch, dec_seq, BLOCK, cache=cache_a, rope=rope_a)
        ob_l = _win_attn(proj, sink, n_ctx, dec_batch, dec_seq, BLOCK, cache=cache_b, rope=rope_b)
        yc_l, _ = _ssd(proj, cw, cb, dtb, acoef, dvec, g_c, n_ctx, dec_batch, dec_seq,
                       h0=state_ssd[:, l].reshape(dec_batch, 8, C_P, C_N))
        od_l, _ = _ret(proj, log_g, g_d, n_ctx, dec_batch, dec_seq,
                       h0=state_ret[:, l].reshape(dec_batch, 8, D_V, D_K))

        x1, h2, ids, gates, rank, counts = _merge(
            x, mod_l, ((oa_c, oa_l), (ob_c, ob_l), (yc_c, yc_l), (od_c, od_l)), w_gl, w_branch[l].reshape(N_BRANCH * BRANCH, d).astype(BF16),
            w_out[l].astype(BF16), ln_g[l, 0].reshape(1, d), ln_b[l, 0].reshape(1, d), rw_t, rb_col,
            n_ctx, dec_seq)

        dest, blk_exp, n_used = _dispatch_plan(ids, rank, counts, n_tok)
        x_slots = _moe_dispatch(h2, dest, blk_exp.shape[0] * MOE_ROWS)
        y_slots = _moe_experts(x_slots, blk_exp, n_used, moe_w_gate, moe_w_up, moe_w_down, l)
        x = _final(x1, mod_l, y_slots, dest, gates.T,
                   ln_g[l, 1].reshape(1, d), ln_b[l, 1].reshape(1, d), n_ctx, dec_seq,
                   split=(l == DEPTH - 1))

        pc = proj[:n_ctx]
        ctx_out['diff_k'].append(pc[:, 256:512].reshape(batch, seq, A_HEADS, 2, A_QK))
        ctx_out['diff_v'].append(pc[:, 512:768].reshape(batch, seq, A_HEADS, A_V))
        ctx_out['win_k'].append(pc[:, 1024:1152].reshape(batch, seq, B_KV, B_DIM))
        ctx_out['win_v'].append(pc[:, 1152:1280].reshape(batch, seq, B_KV, B_DIM))
        ctx_out['ssd'].append(st_c.reshape(batch, 2, C_HEADS, C_P, C_N))
        ctx_out['ret'].append(rt_c.reshape(batch, 2, D_HEADS, D_V, D_K))

    y_prompt = x[0].reshape(batch, seq, d)
    y_sample = x[1].reshape(dec_batch, dec_seq, d)
    stk = lambda k: jnp.stack(ctx_out[k], axis=1)
    return (y_prompt, y_sample, stk('diff_k'), stk('diff_v'), stk('win_k'), stk('win_v'), stk('ssd'), stk('ret'))
```

```python
import functools
import math

import jax
import jax.numpy as jnp
from jax import lax
from jax.experimental import pallas as pl
from jax.experimental.pallas import tpu as pltpu

F32 = jnp.float32
BF16 = jnp.bfloat16

D_MODEL = 1024
DEPTH = 4
GRID_W = 64
BLOCK = 128
WINDOW = 128
CHUNK = 128
A_HEADS, A_QK, A_V = 4, 32, 64
B_HEADS, B_KV, B_DIM = 4, 2, 64
C_HEADS, C_P, C_GROUPS, C_N = 4, 64, 2, 64
D_HEADS, D_K, D_V = 4, 32, 64
BRANCH = 256
N_BRANCH = 4
N_EXP = 16
N_EXP_GROUPS = 4
EXP_PER_GROUP = 4
D_FF_EXP = 512
ROPE_BASE = 10000.0
ALPHA = (2 * DEPTH) ** 0.25
EPS = 1e-5
ADA_DIM = 6 * D_MODEL
NEG = -1e30

N_SMALL = 23 * 128
GATE_OFF = 2824
CDT_OFF = 2048

VMEM_LIMIT = 52 * 1024 * 1024
MOE_ROWS = 256
TOK_TILE = 256
MERGE_TILE = 2 * TOK_TILE

_NN = (((1,), (0,)), ((), ()))
_NT = (((1,), (1,)), ((), ()))
_TN = (((0,), (0,)), ((), ()))


def _params(*sem):
    return pltpu.CompilerParams(dimension_semantics=sem, vmem_limit_bytes=VMEM_LIMIT)


def _mm(a, b, dims=_NN):
    return lax.dot_general(a, b, dims, preferred_element_type=F32)


def _split(a):
    hi = a.astype(BF16)
    return hi, (a - hi.astype(F32)).astype(BF16)


def _mm_f32(a, b, dims=_NN):
    a_hi, a_lo = _split(a)
    b_hi, b_lo = _split(b)
    return (_mm(a_lo, b_hi, dims) + _mm(a_hi, b_lo, dims)) + _mm(a_hi, b_hi, dims)


def _mm_exact_lhs(m_bf, a):
    a1 = a.astype(BF16)
    r1 = a - a1.astype(F32)
    a2 = r1.astype(BF16)
    a3 = (r1 - a2.astype(F32)).astype(BF16)
    return (_mm(m_bf, a3) + _mm(m_bf, a2)) + _mm(m_bf, a1)


def _silu(x):
    return x * jax.nn.sigmoid(x)


LANES = 128
TOKEN_TILE_ROWS = D_MODEL // LANES


def _store_token_tiles(ref, val, lead=(), base=0):
    n = val.shape[0]
    for j in range(TOKEN_TILE_ROWS):
        ref[(*lead, pl.ds(base + j, n, stride=TOKEN_TILE_ROWS), slice(None))] = val[:, j * LANES:(j + 1) * LANES]


def _load_token_tiles(ref, n, lead=()):
    return jnp.concatenate(
        [ref[(*lead, pl.ds(j, n, stride=TOKEN_TILE_ROWS), slice(None))] for j in range(TOKEN_TILE_ROWS)], axis=1)


def _rope(x, c, s_lo, s_hi, shift):
    n = x.shape[1]
    return x * c + pltpu.roll(x, n - shift, 1) * s_lo + pltpu.roll(x, shift, 1) * s_hi


def _ada_kernel(c_ref, w_ref, b_ref, o_ref):
    c = c_ref[...]
    o_ref[...] = _mm_f32(_silu(c), w_ref[...]) + b_ref[...]


def _ada(cvec, w_ada, b_ada):
    rows = cvec.shape[0]
    tn = 1024
    return pl.pallas_call(
        _ada_kernel,
        out_shape=jax.ShapeDtypeStruct((DEPTH, rows, ADA_DIM), F32),
        grid=(DEPTH, ADA_DIM // tn),
        in_specs=[
            pl.BlockSpec((rows, D_MODEL), lambda l, j: (0, 0)),
            pl.BlockSpec((None, D_MODEL, tn), lambda l, j: (l, 0, j)),
            pl.BlockSpec((None, 1, tn), lambda l, j: (l, 0, j)),
        ],
        out_specs=pl.BlockSpec((None, rows, tn), lambda l, j: (l, 0, j)),
        compiler_params=_params("arbitrary", "arbitrary"),
        name="ada",
    )(cvec, w_ada, b_ada.reshape(DEPTH, 1, ADA_DIM))


def _inproj_kernel(x_ref, mod_ref, w_ref, o_ref):
    d = x_ref.shape[1]
    h = x_ref[...] * (1.0 + mod_ref[:, d:2 * d]) + mod_ref[:, 0:d]
    o_ref[...] = _mm(h.astype(BF16), w_ref[...])


def _mod_group(i, tm, n_ctx, dec_seq):
    row = i * tm
    return jnp.where(row < n_ctx, 0, 1 + lax.div(jnp.maximum(row - n_ctx, 0), dec_seq))


def _inproj(x, mod_l, w_small, n_ctx, dec_seq):
    n_tok = x.shape[0]
    tm = TOK_TILE
    grp = functools.partial(_mod_group, tm=tm, n_ctx=n_ctx, dec_seq=dec_seq)
    return pl.pallas_call(
        _inproj_kernel,
        out_shape=jax.ShapeDtypeStruct((n_tok, N_SMALL), F32),
        grid=(n_tok // tm,),
        in_specs=[
            pl.BlockSpec((tm, D_MODEL), lambda i: (i, 0)),
            pl.BlockSpec((None, 1, ADA_DIM), lambda i: (grp(i), 0, 0)),
            pl.BlockSpec((D_MODEL, N_SMALL), lambda i: (0, 0)),
        ],
        out_specs=pl.BlockSpec((tm, N_SMALL), lambda i: (i, 0)),
        compiler_params=_params("arbitrary"),
        name="inproj",
    )(x, mod_l, w_small)


def _diff_attn_kernel(*refs, latent, tq, seq, past):
    if latent:
        (sc_ref, q_ref, k_ref, v_ref, g_ref, ck_ref, cv_ref, rc_ref, rlo_ref, rhi_ref,
         o_ref, kt_scr, v_scr) = refs
    else:
        sc_ref, q_ref, k_ref, v_ref, g_ref, o_ref, kt_scr, v_scr = refs
    qi = pl.program_id(1)
    shift = A_QK // 4

    @pl.when(qi == 0)
    def _():
        k = k_ref[...]
        if latent:
            k = _rope(k, rc_ref[...], rlo_ref[...], rhi_ref[...], shift)
        kt_scr[:, 0:seq] = k.T.astype(BF16)
        if latent:
            kt_scr[:, seq:seq + past] = ck_ref[...].T.astype(BF16)
        ones = jnp.ones((seq + past, LANES - A_V), BF16)
        for h in range(A_HEADS):
            hs = slice(h * A_V, (h + 1) * A_V)
            v_scr[0:seq, h * LANES:h * LANES + A_V] = v_ref[:, hs].astype(BF16)
            if latent:
                v_scr[seq:seq + past, h * LANES:h * LANES + A_V] = cv_ref[:, hs].astype(BF16)
            v_scr[:, h * LANES + A_V:(h + 1) * LANES] = ones

    q = q_ref[...]
    if latent:
        r = pl.ds(pl.multiple_of(qi * tq, tq), tq)
        q = _rope(q, rc_ref[r, :], rlo_ref[r, :], rhi_ref[r, :], shift)
    qb = q.astype(BF16)
    lam = sc_ref[0]
    post = sc_ref[1]
    c = (A_QK ** -0.5) * math.log2(math.e)
    outs = []
    for h in range(A_HEADS):
        probs = []
        for m in range(2):
            off = (h * 2 + m) * A_QK
            s = _mm(qb[:, off:off + A_QK], kt_scr[off:off + A_QK, :])
            probs.append(jnp.exp2(s * c - jnp.max(s, axis=1, keepdims=True) * c).astype(BF16))
        ov = _mm(jnp.concatenate(probs, axis=0), v_scr[:, h * LANES:(h + 1) * LANES])
        maps = [ov[m * tq:(m + 1) * tq, 0:A_V] * (1.0 / ov[m * tq:(m + 1) * tq, A_V:A_V + 1]) for m in range(2)]
        o = maps[0] - lam * maps[1]
        n = o * lax.rsqrt(jnp.mean(o * o, axis=1, keepdims=True) + EPS)
        outs.append((n * g_ref[...]) * post)
    o_ref[...] = jnp.concatenate(outs, axis=1).astype(BF16)


def _diff_attn(proj, scal, g, row0, nb, seq, tq, cache=None, rope=None):
    latent = cache is not None
    nq = seq // tq
    rb = row0 // seq
    qb0 = row0 // tq
    past = cache[0].shape[1] if latent else 0
    in_specs = [
        pl.BlockSpec(memory_space=pltpu.SMEM),
        pl.BlockSpec((tq, 256), lambda b, i: (qb0 + b * nq + i, 0)),
        pl.BlockSpec((seq, 256), lambda b, i: (rb + b, 1)),
        pl.BlockSpec((seq, 256), lambda b, i: (rb + b, 2)),
        pl.BlockSpec((1, A_V), lambda b, i: (0, 0)),
    ]
    args = [scal, proj, proj, proj, g]
    if latent:
        in_specs += [
            pl.BlockSpec((None, past, 256), lambda b, i: (b, 0, 0)),
            pl.BlockSpec((None, past, 256), lambda b, i: (b, 0, 0)),
        ] + [pl.BlockSpec((seq, 256), lambda b, i: (0, 0))] * 3
        args += [cache[0], cache[1], *rope]
    return pl.pallas_call(
        functools.partial(_diff_attn_kernel, latent=latent, tq=tq, seq=seq, past=past),
        out_shape=jax.ShapeDtypeStruct((nb * seq, BRANCH), BF16),
        grid=(nb, nq),
        in_specs=in_specs,
        out_specs=pl.BlockSpec((tq, BRANCH), lambda b, i: (b * nq + i, 0)),
        scratch_shapes=[pltpu.VMEM((256, seq + past), BF16),
                        pltpu.VMEM((seq + past, A_HEADS * LANES), BF16)],
        compiler_params=_params("arbitrary", "arbitrary"),
        name="diff_attn_lat" if latent else "diff_attn_ctx",
    )(*args)


def _win_attn_kernel(*refs, latent, tq, seq):
    if latent:
        (sink_ref, q_ref, k_ref, v_ref, ck_ref, cv_ref, rc_ref, rlo_ref, rhi_ref,
         o_ref, k_scr, v_scr, ck_scr, cv_scr) = refs
    else:
        sink_ref, q_ref, k_ref, v_ref, o_ref, k_scr, v_scr = refs
    qi = pl.program_id(1)
    nq = seq // tq
    shift = B_DIM // 4
    kvw = B_KV * B_DIM

    @pl.when(qi == 0)
    def _():
        k = k_ref[...]
        if latent:
            k = _rope(k, rc_ref[:, 0:kvw], rlo_ref[:, 0:kvw], rhi_ref[:, 0:kvw], shift)
            ck_scr[...] = ck_ref[...].T.astype(BF16)
            cv_scr[...] = cv_ref[...].astype(BF16)
        kt = k.T.astype(BF16)
        for j in range(nq):
            k_scr[j] = kt[:, j * tq:(j + 1) * tq]
        v_scr[...] = v_ref[...].astype(BF16)

    q = q_ref[...]
    if latent:
        r = pl.ds(pl.multiple_of(qi * tq, tq), tq)
        q = _rope(q, rc_ref[r, :], rlo_ref[r, :], rhi_ref[r, :], shift)
        near = (jnp.maximum(qi - 1, 0), qi, jnp.minimum(qi + 1, nq - 1))
        kl = jnp.concatenate([k_scr[j] for j in near], axis=1)
        vl = jnp.concatenate([v_scr[pl.ds(pl.multiple_of(j * tq, tq), tq), :] for j in near], axis=0)
    qb = q.astype(BF16)
    scale = B_DIM ** -0.5
    ratio = B_HEADS // B_KV
    rows = ratio * tq
    if latent:
        ii = lax.broadcasted_iota(jnp.int32, (rows, 3 * tq), 0) & (tq - 1)
        jj = lax.broadcasted_iota(jnp.int32, (rows, 3 * tq), 1)
        lo = jnp.where(qi > 0, 0, tq)
        hi = jnp.where(qi < nq - 1, 3 * tq, 2 * tq)
        valid = (jnp.abs(jj - tq - ii) <= WINDOW) & (jj >= lo) & (jj < hi)
    else:
        kl = k_scr[0]
        vl = v_scr[...]
    row_id = lax.broadcasted_iota(jnp.int32, (rows, 1), 0)
    head_of_row = (row_id - (row_id & (tq - 1))) // tq if ratio > 2 else (row_id >= tq).astype(jnp.int32)
    outs = []
    for g in range(B_KV):
        gsl = slice(g * B_DIM, (g + 1) * B_DIM)
        heads = range(g * ratio, (g + 1) * ratio)
        qg = jnp.concatenate([qb[:, h * B_DIM:(h + 1) * B_DIM] for h in heads], axis=0)
        snk = jnp.full((rows, 1), sink_ref[heads[0]], F32)
        for n, h in enumerate(heads[1:], start=1):
            snk = jnp.where(head_of_row == n, sink_ref[h], snk)
        s = _mm(qg, kl[gsl, :]) * scale
        if latent:
            s = jnp.where(valid, s, NEG)
            sc = _mm(qg, ck_scr[gsl, :]) * scale
            m = jnp.maximum(jnp.maximum(jnp.max(s, axis=1, keepdims=True),
                                        jnp.max(sc, axis=1, keepdims=True)), snk)
            pc = jnp.exp(sc - m)
        else:
            m = jnp.maximum(jnp.max(s, axis=1, keepdims=True), snk)
        p = jnp.exp(s - m)
        den = jnp.sum(p, axis=1, keepdims=True) + jnp.exp(snk - m)
        if latent:
            den = den + jnp.sum(pc, axis=1, keepdims=True)
        inv = 1.0 / den
        o = _mm((p * inv).astype(BF16), vl[:, gsl])
        if latent:
            o = o + _mm((pc * inv).astype(BF16), cv_scr[:, gsl])
        outs += [o[n * tq:(n + 1) * tq, :] for n in range(ratio)]
    o_ref[...] = jnp.concatenate(outs, axis=1).astype(BF16)


def _win_attn(proj, sink, row0, nb, seq, tq, cache=None, rope=None):
    latent = cache is not None
    nq = seq // tq
    rb = row0 // seq
    qb0 = row0 // tq
    kvw = B_KV * B_DIM
    in_specs = [
        pl.BlockSpec(memory_space=pltpu.SMEM),
        pl.BlockSpec((tq, 256), lambda b, i: (qb0 + b * nq + i, 3)),
        pl.BlockSpec((seq, kvw), lambda b, i: (rb + b, 8)),
        pl.BlockSpec((seq, kvw), lambda b, i: (rb + b, 9)),
    ]
    args = [sink, proj, proj, proj]
    scratch = [pltpu.VMEM((nq, kvw, tq), BF16), pltpu.VMEM((seq, kvw), BF16)]
    if latent:
        past = cache[0].shape[1]
        in_specs += [
            pl.BlockSpec((None, past, kvw), lambda b, i: (b, 0, 0)),
            pl.BlockSpec((None, past, kvw), lambda b, i: (b, 0, 0)),
        ] + [pl.BlockSpec((seq, 256), lambda b, i: (0, 0))] * 3
        args += [cache[0], cache[1], *rope]
        scratch += [pltpu.VMEM((kvw, past), BF16), pltpu.VMEM((past, kvw), BF16)]
    return pl.pallas_call(
        functools.partial(_win_attn_kernel, latent=latent, tq=tq, seq=seq),
        out_shape=jax.ShapeDtypeStruct((nb * seq, BRANCH), BF16),
        grid=(nb, nq),
        in_specs=in_specs,
        out_specs=pl.BlockSpec((tq, BRANCH), lambda b, i: (b * nq + i, 0)),
        scratch_shapes=scratch,
        compiler_params=_params("arbitrary", "arbitrary"),
        name="win_attn_lat" if latent else "win_attn_ctx",
    )(*args)


def _conv_silu(u, w, b):
    n = u.shape[0]
    rows = lax.broadcasted_iota(jnp.int32, u.shape, 0)
    up = jnp.where(rows == 0, 0.0, pltpu.roll(u, 1, 0))
    un = jnp.where(rows == n - 1, 0.0, pltpu.roll(u, n - 1, 0))
    return _silu(up * w[0:1, :] + u * w[1:2, :] + un * w[2:3, :] + b)


def _ssd_kernel(*refs, seq, has_h0):
    if has_h0:
        (cx_ref, cz_ref, cbc_ref, cdt_ref, cw_ref, cb_ref, dtb_ref, ac_ref, dv_ref, g_ref, h0_ref,
         y_ref, st_ref, xs, bcs, dts, ybuf, ybuf_b, hs) = refs
    else:
        (cx_ref, cz_ref, cbc_ref, cdt_ref, cw_ref, cb_ref, dtb_ref, ac_ref, dv_ref, g_ref,
         y_ref, st_ref, xs, bcs, dts, ybuf, ybuf_b, hs) = refs
    nc = seq // CHUNK
    xw = C_HEADS * C_P
    xs[...] = _conv_silu(cx_ref[...], cw_ref[:, 0:xw], cb_ref[:, 0:xw])
    bcs[...] = _conv_silu(cbc_ref[...], cw_ref[:, xw:2 * xw], cb_ref[:, xw:2 * xw])
    z = cdt_ref[...] + dtb_ref[...]
    dts[...] = jnp.maximum(z, 0.0) + jnp.log1p(jnp.exp(-jnp.abs(z)))
    if has_h0:
        hs[...] = h0_ref[...]
    else:
        hs[...] = jnp.zeros(hs.shape, F32)

    ri = lax.broadcasted_iota(jnp.int32, (CHUNK, CHUNK), 0)
    ci = lax.broadcasted_iota(jnp.int32, (CHUNK, CHUNK), 1)
    gw = C_GROUPS * C_N

    def chunk(c, d):
        r = pl.ds(pl.multiple_of(c * CHUNK, CHUNK), CHUNK)
        tri = (ri >= ci) if d == 0 else (ci >= ri)
        dt = dts[r, :]
        cs = _mm_exact_lhs(tri.astype(BF16), dt * ac_ref[...])
        cst = cs.T
        x = xs[r, :]
        bc = bcs[r, :]
        ys = []
        for g in range(C_GROUPS):
            bm = bc[:, g * C_N:(g + 1) * C_N]
            cb = bc[:, gw + g * C_N:gw + (g + 1) * C_N].astype(BF16)
            gram = _mm(cb, bm.astype(BF16), _NT)
            for hh in range(C_HEADS // C_GROUPS):
                h = g * (C_HEADS // C_GROUPS) + hh
                col = d * C_HEADS + h
                xb = (x[:, h * C_P:(h + 1) * C_P] * dt[:, col:col + 1]).astype(BF16)
                cc = cs[:, col:col + 1]
                dec = jnp.exp(jnp.where(tri, cc - cst[col:col + 1, :], NEG))
                tot = cc[CHUNK - 1:CHUNK, :] if d == 0 else cc[0:1, :]
                hin = hs[col]
                y = _mm((gram * dec).astype(BF16), xb)
                y = y + _mm(cb, hin.astype(BF16), _NT) * jnp.exp(cc)
                bd = (bm * jnp.exp(tot - cc)).astype(BF16)
                hs[col] = hin * jnp.exp(tot) + _mm(xb, bd, _TN)
                ys.append(y)
        return r, jnp.concatenate(ys, axis=1)

    def scan(t, carry):
        r, y = chunk(t, 0)
        ybuf[r, :] = y
        r, y = chunk(nc - 1 - t, 1)
        ybuf_b[r, :] = y
        return carry

    lax.fori_loop(0, nc, scan, 0)

    def finish(c, carry):
        r = pl.ds(pl.multiple_of(c * CHUNK, CHUNK), CHUNK)
        y = (ybuf[r, :] + ybuf_b[r, :]) + xs[r, :] * dv_ref[...]
        y = y * _silu(cz_ref[r, :])
        gl = xw // C_GROUPS
        parts = []
        for g in range(C_GROUPS):
            seg = y[:, g * gl:(g + 1) * gl]
            parts.append(seg * lax.rsqrt(jnp.mean(seg * seg, axis=1, keepdims=True) + EPS))
        y_ref[r, :] = (jnp.concatenate(parts, axis=1) * g_ref[...]).astype(BF16)
        return carry

    lax.fori_loop(0, nc, finish, 0)
    st_ref[...] = hs[...]


def _ssd(proj, conv_w, conv_b, dtb, acoef, dvec, g, row0, nb, seq, h0=None):
    has_h0 = h0 is not None
    rb = row0 // seq
    nst = 2 * C_HEADS
    in_specs = [
        pl.BlockSpec((seq, 256), lambda b: (rb + b, 5)),
        pl.BlockSpec((seq, 256), lambda b: (rb + b, 6)),
        pl.BlockSpec((seq, 256), lambda b: (rb + b, 7)),
        pl.BlockSpec((seq, 128), lambda b: (rb + b, 22)),
        pl.BlockSpec((3, 512), lambda b: (0, 0)),
        pl.BlockSpec((1, 512), lambda b: (0, 0)),
        pl.BlockSpec((1, 128), lambda b: (0, 0)),
        pl.BlockSpec((1, 128), lambda b: (0, 0)),
        pl.BlockSpec((1, 256), lambda b: (0, 0)),
        pl.BlockSpec((1, 256), lambda b: (0, 0)),
    ]
    args = [proj, proj, proj, proj, conv_w, conv_b, dtb, acoef, dvec, g]
    if has_h0:
        in_specs.append(pl.BlockSpec((None, nst, C_P, C_N), lambda b: (b, 0, 0, 0)))
        args.append(h0)
    return pl.pallas_call(
        functools.partial(_ssd_kernel, seq=seq, has_h0=has_h0),
        out_shape=(jax.ShapeDtypeStruct((nb * seq, BRANCH), BF16),
                   jax.ShapeDtypeStruct((nb, nst, C_P, C_N), F32)),
        grid=(nb,),
        in_specs=in_specs,
        out_specs=(pl.BlockSpec((seq, BRANCH), lambda b: (b, 0)),
                   pl.BlockSpec((None, nst, C_P, C_N), lambda b: (b, 0, 0, 0))),
        scratch_shapes=[pltpu.VMEM((seq, 256), F32), pltpu.VMEM((seq, 256), F32),
                        pltpu.VMEM((seq, 128), F32), pltpu.VMEM((seq, 256), F32),
                        pltpu.VMEM((seq, 256), F32), pltpu.VMEM((nst, C_P, C_N), F32)],
        compiler_params=_params("arbitrary"),
        name="ssd_lat" if has_h0 else "ssd_ctx",
    )(*args)


def _ret_kernel(*refs, seq, has_h0):
    if has_h0:
        lg_ref, q_ref, k_ref, v_ref, gt_ref, g_ref, h0_ref, y_ref, st_ref, ybuf, ybuf_b, hs = refs
    else:
        lg_ref, q_ref, k_ref, v_ref, gt_ref, g_ref, y_ref, st_ref, ybuf, ybuf_b, hs = refs
    nc = seq // CHUNK
    if has_h0:
        hs[...] = h0_ref[...]
    else:
        hs[...] = jnp.zeros(hs.shape, F32)
    ri = lax.broadcasted_iota(jnp.int32, (CHUNK, CHUNK), 0)
    ci = lax.broadcasted_iota(jnp.int32, (CHUNK, CHUNK), 1)
    pos = lax.broadcasted_iota(jnp.int32, (CHUNK, 1), 0).astype(F32)
    kscale = D_K ** -0.5

    def chunk(c, d):
        r = pl.ds(pl.multiple_of(c * CHUNK, CHUNK), CHUNK)
        q = q_ref[r, :]
        k = k_ref[r, :] * kscale
        v = v_ref[r, :]
        if d == 0:
            tri, dist = ri >= ci, (ri - ci).astype(F32)
            steps_in, steps_out = pos + 1.0, (CHUNK - 1.0) - pos
        else:
            tri, dist = ci >= ri, (ci - ri).astype(F32)
            steps_in, steps_out = CHUNK - pos, pos
        ys = []
        for h in range(D_HEADS):
            col = d * D_HEADS + h
            lg = lg_ref[col]
            dec = jnp.exp(jnp.where(tri, dist * lg, NEG))
            e_in = jnp.exp(steps_in * lg)
            e_tot = e_in[CHUNK - 1:CHUNK, :] if d == 0 else e_in[0:1, :]
            qb = q[:, h * D_K:(h + 1) * D_K].astype(BF16)
            km = k[:, h * D_K:(h + 1) * D_K]
            vb = v[:, h * D_V:(h + 1) * D_V].astype(BF16)
            hin = hs[col]
            y = _mm((_mm(qb, km.astype(BF16), _NT) * dec).astype(BF16), vb)
            y = y + _mm(qb, hin.astype(BF16), _NT) * e_in
            bd = (km * jnp.exp(steps_out * lg)).astype(BF16)
            hs[col] = hin * e_tot + _mm(vb, bd, _TN)
            ys.append(y)
        return r, ys

    def scan(t, carry):
        r, ys = chunk(t, 0)
        ybuf[r, :] = jnp.concatenate(ys, axis=1)
        r, ys = chunk(nc - 1 - t, 1)
        ybuf_b[r, :] = jnp.concatenate(ys, axis=1)
        return carry

    lax.fori_loop(0, nc, scan, 0)

    def finish(c, carry):
        r = pl.ds(pl.multiple_of(c * CHUNK, CHUNK), CHUNK)
        yf = ybuf[r, :] + ybuf_b[r, :]
        parts = []
        for h in range(D_HEADS):
            o = yf[:, h * D_V:(h + 1) * D_V]
            o = o - jnp.mean(o, axis=1, keepdims=True)
            parts.append(o * lax.rsqrt(jnp.mean(o * o, axis=1, keepdims=True) + EPS))
        y = (jnp.concatenate(parts, axis=1) * g_ref[...]) * _silu(gt_ref[r, :])
        y_ref[r, :] = y.astype(BF16)
        return carry

    lax.fori_loop(0, nc, finish, 0)
    st_ref[...] = hs[...]


def _ret(proj, log_g, g, row0, nb, seq, h0=None):
    has_h0 = h0 is not None
    rb = row0 // seq
    nst = 2 * D_HEADS
    in_specs = [
        pl.BlockSpec(memory_space=pltpu.SMEM),
        pl.BlockSpec((seq, 128), lambda b: (rb + b, 16)),
        pl.BlockSpec((seq, 128), lambda b: (rb + b, 17)),
        pl.BlockSpec((seq, 256), lambda b: (rb + b, 9)),
        pl.BlockSpec((seq, 256), lambda b: (rb + b, 10)),
        pl.BlockSpec((1, 256), lambda b: (0, 0)),
    ]
    args = [log_g, proj, proj, proj, proj, g]
    if has_h0:
        in_specs.append(pl.BlockSpec((None, nst, D_V, D_K), lambda b: (b, 0, 0, 0)))
        args.append(h0)
    return pl.pallas_call(
        functools.partial(_ret_kernel, seq=seq, has_h0=has_h0),
        out_shape=(jax.ShapeDtypeStruct((nb * seq, BRANCH), BF16),
                   jax.ShapeDtypeStruct((nb, nst, D_V, D_K), F32)),
        grid=(nb,),
        in_specs=in_specs,
        out_specs=(pl.BlockSpec((seq, BRANCH), lambda b: (b, 0)),
                   pl.BlockSpec((None, nst, D_V, D_K), lambda b: (b, 0, 0, 0))),
        scratch_shapes=[pltpu.VMEM((seq, 256), F32), pltpu.VMEM((seq, 256), F32),
                        pltpu.VMEM((nst, D_V, D_K), F32)],
        compiler_params=_params("arbitrary"),
        name="ret_lat" if has_h0 else "ret_ctx",
    )(*args)


def _route(sel, s):
    row = lambda a, e: a[e:e + 1, :]
    best = None
    grp = None
    for g in range(N_EXP_GROUPS):
        vals = [row(sel, g * EXP_PER_GROUP + j) for j in range(EXP_PER_GROUP)]
        score = None
        for a in range(EXP_PER_GROUP):
            for b in range(a + 1, EXP_PER_GROUP):
                pair = vals[a] + vals[b]
                score = pair if score is None else jnp.maximum(score, pair)
        if best is None:
            best, grp = score, jnp.zeros(score.shape, jnp.int32)
        else:
            better = score > best
            best = jnp.where(better, score, best)
            grp = jnp.where(better, g, grp)

    def pick(a, j):
        out = row(a, j)
        for g in range(1, N_EXP_GROUPS):
            out = jnp.where(grp == g, row(a, g * EXP_PER_GROUP + j), out)
        return out

    cand = [pick(sel, j) for j in range(EXP_PER_GROUP)]
    aff = [pick(s, j) for j in range(EXP_PER_GROUP)]

    def arg_first_max(vals):
        top, idx = vals[0], jnp.zeros(vals[0].shape, jnp.int32)
        for j in range(1, len(vals)):
            better = vals[j] > top
            top = jnp.where(better, vals[j], top)
            idx = jnp.where(better, j, idx)
        return idx

    def take(vals, idx):
        out = vals[0]
        for j in range(1, len(vals)):
            out = jnp.where(idx == j, vals[j], out)
        return out

    i1 = arg_first_max(cand)
    i2 = arg_first_max([jnp.where(i1 == j, -jnp.inf, cand[j]) for j in range(EXP_PER_GROUP)])
    w1, w2 = take(aff, i1), take(aff, i2)
    tot = w1 + w2
    ids = jnp.concatenate([grp * EXP_PER_GROUP + i1, grp * EXP_PER_GROUP + i2], axis=0)
    gates = jnp.concatenate([w1 / tot, w2 / tot], axis=0)
    return ids, gates


def _merge_kernel(*refs, n_ctx_tiles):
    (x_ref, mod_ref, wgl_ref, wbr_ref, wout_ref, lng_ref, lnb_ref, rw_ref, rb_ref, tri_ref) = refs[:10]
    br_refs = refs[10:10 + 2 * N_BRANCH]
    x1_ref, h2_ref, ids_ref, gates_ref, rank_ref, cnt_ref = refs[10 + 2 * N_BRANCH:]
    d = D_MODEL

    @pl.when(pl.program_id(0) == 0)
    def _():
        cnt_ref[...] = jnp.zeros(cnt_ref.shape, F32)

    is_ctx = pl.program_id(0) < n_ctx_tiles
    sub = tri_ref.shape[0]
    for part in range(x_ref.shape[0] // sub):
        rs = slice(part * sub, (part + 1) * sub)
        x = x_ref[rs, :]
        hb = (x * (1.0 + mod_ref[:, d:2 * d]) + mod_ref[:, 0:d]).astype(BF16)
        merged = None
        for k in range(N_BRANCH):
            gate = jax.nn.sigmoid(_mm(hb, wgl_ref[:, k * d:(k + 1) * d]))
            br_k = jnp.where(is_ctx, br_refs[2 * k][rs, :], br_refs[2 * k + 1][rs, :])
            up = _mm(br_k, wbr_ref[k * BRANCH:(k + 1) * BRANCH, :])
            merged = gate * up if merged is None else merged + gate * up
        mix = _mm(merged.astype(BF16), wout_ref[...])
        y = ALPHA * x + mod_ref[:, 2 * d:3 * d] * mix
        y = y - jnp.mean(y, axis=1, keepdims=True)
        x1 = (y * lax.rsqrt(jnp.mean(y * y, axis=1, keepdims=True) + EPS)) * lng_ref[...] + lnb_ref[...]
        x1_ref[rs, :] = x1
        h2 = x1 * (1.0 + mod_ref[:, 4 * d:5 * d]) + mod_ref[:, 3 * d:4 * d]
        _store_token_tiles(h2_ref, h2, base=part * sub * TOKEN_TILE_ROWS)
        s = jax.nn.sigmoid(_mm_f32(rw_ref[...], h2, _NT))
        ids, gates = _route(s + rb_ref[...], s)
        ids_ref[:, rs] = ids
        gates_ref[:, rs] = gates
        expert = lax.broadcasted_iota(jnp.int32, (N_EXP, sub), 0)
        hot = [(expert == ids[k:k + 1, :]).astype(F32) for k in range(2)]
        both = hot[0] + hot[1]
        incl = _mm(both.astype(BF16), tri_ref[...])
        before = cnt_ref[...] + (incl - both)
        rank_ref[:, rs] = jnp.concatenate(
            [jnp.sum(hk * before, axis=0, keepdims=True) for hk in hot], axis=0).astype(jnp.int32)
        cnt_ref[...] = cnt_ref[...] + incl[:, sub - 1:sub]


def _merge(x, mod_l, branches, w_gl, w_br, w_out, ln_g, ln_b, rw_t, rb, n_ctx, dec_seq):
    n_tok = x.shape[0]
    tm = MERGE_TILE
    nct = n_ctx // tm
    grp = functools.partial(_mod_group, tm=tm, n_ctx=n_ctx, dec_seq=dec_seq)
    full = lambda shape: pl.BlockSpec(shape, lambda i: (0,) * len(shape))
    pos = jnp.arange(TOK_TILE)
    tri = (pos[:, None] <= pos[None, :]).astype(BF16)
    br_specs = [pl.BlockSpec((tm, BRANCH), lambda i: (jnp.minimum(i, nct - 1), 0)),
                pl.BlockSpec((tm, BRANCH), lambda i: (jnp.maximum(i - nct, 0), 0))] * N_BRANCH
    br_args = [a for pair in branches for a in pair]
    return pl.pallas_call(
        functools.partial(_merge_kernel, n_ctx_tiles=nct),
        out_shape=(jax.ShapeDtypeStruct((n_tok, D_MODEL), F32),
                   jax.ShapeDtypeStruct((n_tok * TOKEN_TILE_ROWS, LANES), F32),
                   jax.ShapeDtypeStruct((2, n_tok), jnp.int32),
                   jax.ShapeDtypeStruct((2, n_tok), F32),
                   jax.ShapeDtypeStruct((2, n_tok), jnp.int32),
                   jax.ShapeDtypeStruct((N_EXP, 1), F32)),
        grid=(n_tok // tm,),
        in_specs=[
            pl.BlockSpec((tm, D_MODEL), lambda i: (i, 0)),
            pl.BlockSpec((None, 1, ADA_DIM), lambda i: (grp(i), 0, 0)),
            full((D_MODEL, N_BRANCH * D_MODEL)),
            full((N_BRANCH * BRANCH, D_MODEL)),
            full((D_MODEL, D_MODEL)),
            full((1, D_MODEL)),
            full((1, D_MODEL)),
            full((N_EXP, D_MODEL)),
            full((N_EXP, 1)),
            full((TOK_TILE, TOK_TILE)),
        ] + br_specs,
        out_specs=(pl.BlockSpec((tm, D_MODEL), lambda i: (i, 0)),
                   pl.BlockSpec((tm * TOKEN_TILE_ROWS, LANES), lambda i: (i, 0)),
                   pl.BlockSpec((2, tm), lambda i: (0, i)),
                   pl.BlockSpec((2, tm), lambda i: (0, i)),
                   pl.BlockSpec((2, tm), lambda i: (0, i)),
                   full((N_EXP, 1))),
        compiler_params=_params("arbitrary"),
        name="merge",
    )(x, mod_l, w_gl, w_br, w_out, ln_g, ln_b, rw_t, rb, tri, *br_args)


def _dispatch_kernel(dest_ref, pad_ref, h_hbm, x_hbm, hbuf, zbuf, in_sem, out_sem, pad_sem, *, n_tok):
    tr = TOKEN_TILE_ROWS
    rows = TOK_TILE * tr
    i = pl.program_id(0)
    last = pl.num_programs(0) - 1
    slot = lax.rem(i, 2)
    other = 1 - slot

    def load(tile, buf):
        return pltpu.make_async_copy(h_hbm.at[pl.ds(tile * rows, rows), :], hbuf.at[buf], in_sem.at[buf])

    def slot_copy(r, dst, buf):
        return pltpu.make_async_copy(hbuf.at[buf, pl.ds(r * tr, tr), :],
                                     x_hbm.at[pl.ds(dst * tr, tr), :], out_sem.at[buf])

    def drain(buf):
        for r in range(2 * TOK_TILE):
            slot_copy(0, 0, buf).wait()

    @pl.when(i == 0)
    def _():
        load(0, 0).start()

    load(i, slot).wait()

    @pl.when(i < last)
    def _():
        @pl.when(i >= 1)
        def _():
            drain(other)
        load(i + 1, other).start()

    base = i * TOK_TILE
    for r in range(TOK_TILE):
        slot_copy(r, dest_ref[base + r], slot).start()
        slot_copy(r, dest_ref[n_tok + base + r], slot).start()

    @pl.when(i == last)
    def _():
        zbuf[...] = jnp.zeros(zbuf.shape, F32)

        def zero_copy(dst):
            return pltpu.make_async_copy(zbuf, x_hbm.at[pl.ds(dst * tr, tr), :], pad_sem)

        n_pad = pad_ref.shape[0]

        def fill(j, carry):
            for u in range(PAD_UNROLL):
                zero_copy(pad_ref[j * PAD_UNROLL + u]).start()
            return carry

        lax.fori_loop(0, n_pad // PAD_UNROLL, fill, 0)
        drain(slot)

        @pl.when(i >= 1)
        def _():
            drain(other)

        def unfill(j, carry):
            for u in range(PAD_UNROLL):
                zero_copy(0).wait()
            return carry

        lax.fori_loop(0, n_pad // PAD_UNROLL, unfill, 0)


PAD_UNROLL = 8


def _moe_dispatch(h2_tiles, dest, pad_slots, n_slots):
    tr = TOKEN_TILE_ROWS
    n_tok = h2_tiles.shape[0] // tr
    assert pad_slots.shape[0] % PAD_UNROLL == 0
    any_spec = pl.BlockSpec(memory_space=pl.ANY)
    return pl.pallas_call(
        functools.partial(_dispatch_kernel, n_tok=n_tok),
        out_shape=jax.ShapeDtypeStruct((n_slots * tr, LANES), F32),
        grid_spec=pltpu.PrefetchScalarGridSpec(
            num_scalar_prefetch=2,
            grid=(n_tok // TOK_TILE,),
            in_specs=[any_spec],
            out_specs=any_spec,
            scratch_shapes=[pltpu.VMEM((2, TOK_TILE * tr, LANES), F32), pltpu.VMEM((tr, LANES), F32),
                            pltpu.SemaphoreType.DMA((2,)), pltpu.SemaphoreType.DMA((2,)),
                            pltpu.SemaphoreType.DMA(())],
        ),
        compiler_params=_params("arbitrary"),
        name="moe_dispatch",
    )(dest, pad_slots, h2_tiles)


def _moe_kernel(be_ref, nu_ref, x_ref, wg_ref, wu_ref, wd_ref, o_ref):
    del be_ref
    i = pl.program_id(0)

    @pl.when(i < nu_ref[0])
    def _():
        x = _load_token_tiles(x_ref, MOE_ROWS).astype(BF16)
        act = _silu(_mm(x, wg_ref[...].astype(BF16))) * _mm(x, wu_ref[...].astype(BF16))
        _store_token_tiles(o_ref, _mm(act.astype(BF16), wd_ref[...].astype(BF16)))

    @pl.when(i >= nu_ref[0])
    def _():
        o_ref[...] = jnp.zeros(o_ref.shape, F32)


def _moe_experts(x_tiles, blk_exp, n_used, wg, wu, wd, layer):
    n_blk = blk_exp.shape[0]
    blk = pl.BlockSpec((MOE_ROWS * TOKEN_TILE_ROWS, LANES), lambda i, be, nu: (i, 0))
    return pl.pallas_call(
        _moe_kernel,
        out_shape=jax.ShapeDtypeStruct(x_tiles.shape, F32),
        grid_spec=pltpu.PrefetchScalarGridSpec(
            num_scalar_prefetch=2,
            grid=(n_blk,),
            in_specs=[
                blk,
                pl.BlockSpec((None, None, D_MODEL, D_FF_EXP), lambda i, be, nu: (layer, be[i], 0, 0)),
                pl.BlockSpec((None, None, D_MODEL, D_FF_EXP), lambda i, be, nu: (layer, be[i], 0, 0)),
                pl.BlockSpec((None, None, D_FF_EXP, D_MODEL), lambda i, be, nu: (layer, be[i], 0, 0)),
            ],
            out_specs=blk,
        ),
        compiler_params=_params("arbitrary"),
        name="moe_experts",
    )(blk_exp, n_used, x_tiles, wg, wu, wd)


def _final_kernel(dest_ref, x1_ref, mod_ref, gt_ref, lng_ref, lnb_ref, y_hbm, *rest, n_ctx_tiles, n_tok):
    *o_refs, ybuf, sem = rest
    d = D_MODEL
    n = x1_ref.shape[0]
    tr = TOKEN_TILE_ROWS
    i = pl.program_id(0)
    slot = lax.rem(i, 2)

    def fetch_copy(src, k, r, buf):
        return pltpu.make_async_copy(y_hbm.at[pl.ds(src * tr, tr), :],
                                     ybuf.at[buf, k, pl.ds(r * tr, tr), :], sem.at[buf])

    def start_fetch(tile, buf):
        for k in range(2):
            for r in range(n):
                fetch_copy(dest_ref[k * n_tok + tile * n + r], k, r, buf).start()

    @pl.when(i == 0)
    def _():
        start_fetch(0, 0)

    for k in range(2):
        for r in range(n):
            fetch_copy(0, k, r, slot).wait()

    @pl.when(i + 1 < pl.num_programs(0))
    def _():
        start_fetch(i + 1, 1 - slot)

    gt = gt_ref[...]
    ffn = (_load_token_tiles(ybuf, n, lead=(slot, 0)) * gt[:, 0:1]
           + _load_token_tiles(ybuf, n, lead=(slot, 1)) * gt[:, 1:2])
    y = ALPHA * x1_ref[...] + mod_ref[:, 5 * d:6 * d] * ffn
    y = y - jnp.mean(y, axis=1, keepdims=True)
    out = (y * lax.rsqrt(jnp.mean(y * y, axis=1, keepdims=True) + EPS)) * lng_ref[...] + lnb_ref[...]
    if len(o_refs) == 1:
        o_refs[0][...] = out
    else:
        @pl.when(pl.program_id(0) < n_ctx_tiles)
        def _():
            o_refs[0][...] = out

        @pl.when(pl.program_id(0) >= n_ctx_tiles)
        def _():
            o_refs[1][...] = out


def _final(x1, mod_l, y_slots, dest, gates_t, ln_g, ln_b, n_ctx, dec_seq, split=False):
    n_tok = x1.shape[0]
    tm = TOK_TILE
    nt = n_tok // tm
    nct = n_ctx // tm
    grp = functools.partial(_mod_group, tm=tm, n_ctx=n_ctx, dec_seq=dec_seq)
    tile = pl.BlockSpec((tm, D_MODEL), lambda i, dst: (i, 0))
    vec = pl.BlockSpec((1, D_MODEL), lambda i, dst: (0, 0))
    if split:
        out_shape = (jax.ShapeDtypeStruct((n_ctx, D_MODEL), F32), jax.ShapeDtypeStruct((n_tok - n_ctx, D_MODEL), F32))
        out_specs = (pl.BlockSpec((tm, D_MODEL), lambda i, dst: (jnp.minimum(i, nct - 1), 0)),
                     pl.BlockSpec((tm, D_MODEL), lambda i, dst: (jnp.maximum(i - nct, 0), 0)))
    else:
        out_shape = jax.ShapeDtypeStruct((n_tok, D_MODEL), F32)
        out_specs = tile
    return pl.pallas_call(
        functools.partial(_final_kernel, n_ctx_tiles=nct, n_tok=n_tok),
        out_shape=out_shape,
        grid_spec=pltpu.PrefetchScalarGridSpec(
            num_scalar_prefetch=1,
            grid=(nt,),
            in_specs=[tile, pl.BlockSpec((None, 1, ADA_DIM), lambda i, dst: (grp(i), 0, 0)),
                      pl.BlockSpec((tm, 2), lambda i, dst: (i, 0)), vec, vec,
                      pl.BlockSpec(memory_space=pl.ANY)],
            out_specs=out_specs,
            scratch_shapes=[pltpu.VMEM((2, 2, tm * TOKEN_TILE_ROWS, LANES), F32),
                            pltpu.SemaphoreType.DMA((2,))],
        ),
        compiler_params=_params("arbitrary"),
        name="final_norm",
    )(dest, x1, mod_l, gates_t, ln_g, ln_b, y_slots)


def _rope_tables(seq, dim, width):
    nf = dim // 4
    t = jnp.arange(seq)
    pos = jnp.stack([t // GRID_W, t % GRID_W], axis=-1).astype(F32)
    inv = ROPE_BASE ** (-jnp.arange(nf, dtype=F32) / nf)
    ang = pos[:, :, None] * inv
    cos, sin = jnp.cos(ang), jnp.sin(ang)
    zero = jnp.zeros_like(sin)
    c = jnp.stack([cos, cos], axis=2).reshape(seq, dim)
    s_lo = jnp.stack([-sin, zero], axis=2).reshape(seq, dim)
    s_hi = jnp.stack([zero, sin], axis=2).reshape(seq, dim)
    rep = width // dim
    return tuple(jnp.tile(a, (1, rep)) for a in (c, s_lo, s_hi))


def _dispatch_plan(ids, rank, counts, n_tok):
    n_assign = 2 * n_tok
    flat_e = ids.reshape(n_assign)
    onehot = (flat_e[:, None] == jnp.arange(N_EXP, dtype=jnp.int32)[None, :]).astype(jnp.int32)
    counts = counts.reshape(N_EXP).astype(jnp.int32)
    padded = (counts + MOE_ROWS - 1) // MOE_ROWS * MOE_ROWS
    pad_end = jnp.cumsum(padded)
    pad_start = pad_end - padded
    dest = (jnp.sum(onehot * pad_start[None, :], axis=1) + rank.reshape(n_assign)).astype(jnp.int32)
    n_blk = n_assign // MOE_ROWS + N_EXP
    blk_start = jnp.arange(n_blk, dtype=jnp.int32) * MOE_ROWS
    blk_exp = jnp.sum((blk_start[:, None] >= pad_end[None, :]).astype(jnp.int32), axis=1)
    blk_exp = jnp.minimum(blk_exp, N_EXP - 1).astype(jnp.int32)
    n_used = (pad_end[-1] // MOE_ROWS).astype(jnp.int32).reshape(1)
    n_free = n_blk * MOE_ROWS - n_assign
    gap_end = jnp.cumsum(padded - counts)
    j = jnp.arange(n_free, dtype=jnp.int32)
    seg = jnp.sum((j[:, None] >= gap_end[None, :]).astype(jnp.int32), axis=1)
    seg_first_slot = jnp.concatenate([pad_start + counts, pad_end[-1:]])
    seg_first_j = jnp.concatenate([jnp.zeros((1,), jnp.int32), gap_end])
    seg_hot = (seg[:, None] == jnp.arange(N_EXP + 1, dtype=jnp.int32)[None, :]).astype(jnp.int32)
    free_slots = (jnp.sum(seg_hot * (seg_first_slot - seg_first_j)[None, :], axis=1) + j).astype(jnp.int32)
    return dest, free_slots, blk_exp, n_used


def kernel(x_prompt, x_sample, cache_diff_k, cache_diff_v, cache_win_k, cache_win_v, state_ssd, state_ret,
           c, c_ctx, w_ada, b_ada, w_in, diff_lambda, diff_norm_g, win_sink, conv_w, conv_b,
           ssd_A_log, ssd_dt_bias, ssd_D, ssd_norm_g, ret_decay_logit, ret_norm_g, w_branch, w_out,
           ln_g, ln_b, router_w, router_b, moe_w_gate, moe_w_up, moe_w_down):
    batch, seq, d = x_prompt.shape
    dec_batch, dec_seq, _ = x_sample.shape
    past = cache_diff_k.shape[2]
    n_ctx, n_lat = batch * seq, dec_batch * dec_seq
    n_tok = n_ctx + n_lat
    assert d == D_MODEL and n_ctx % dec_seq == 0 and seq % CHUNK == 0 and dec_seq % CHUNK == 0

    x = jnp.concatenate([x_prompt.reshape(n_ctx, d), x_sample.reshape(n_lat, d)], axis=0)

    n_mod = 1 + dec_batch
    n_mod_pad = -(-n_mod // 8) * 8
    cvec = jnp.concatenate([c_ctx[None, :], c, jnp.zeros((n_mod_pad - n_mod, d), F32)], axis=0)
    mod = _ada(cvec, w_ada, b_ada)

    rope_a = _rope_tables(dec_seq, A_QK, 256)
    rope_b = _rope_tables(dec_seq, B_DIM, 256)
    rw_t = router_w.T
    rb_col = router_b.reshape(N_EXP, 1)

    ctx_out = {k: [] for k in ('diff_k', 'diff_v', 'win_k', 'win_v', 'ssd', 'ret')}
    for l in range(DEPTH):
        mod_l = mod[l, :n_mod].reshape(n_mod, 1, ADA_DIM)
        wl = w_in[l]
        w_small = jnp.concatenate(
            [wl[:, :CDT_OFF], wl[:, CDT_OFF + 8:GATE_OFF], wl[:, CDT_OFF:CDT_OFF + 8],
             jnp.zeros((d, N_SMALL - GATE_OFF), F32)], axis=1).astype(BF16)
        w_gl = wl[:, GATE_OFF:].astype(BF16)
        proj = _inproj(x, mod_l, w_small, n_ctx, dec_seq)

        lam_init = 0.8 - 0.6 * math.exp(-0.3 * l)
        lv = diff_lambda[l]
        lam = jnp.exp(jnp.sum(lv[0] * lv[1])) - jnp.exp(jnp.sum(lv[2] * lv[3])) + lam_init
        diff_scal = jnp.stack([lam, jnp.asarray(1.0 - lam_init, F32)]).astype(F32)
        g_a = diff_norm_g[l].reshape(1, A_V)
        sink = win_sink[l]
        dtb = jnp.zeros((1, 128), F32).at[0, :8].set(ssd_dt_bias[l].reshape(8))
        acoef = jnp.zeros((1, 128), F32).at[0, :8].set(-jnp.exp(ssd_A_log[l]).reshape(8))
        dvec = jnp.repeat(ssd_D[l], C_P).reshape(1, 256)
        g_c = ssd_norm_g[l].reshape(1, 256)
        log_g = jax.nn.log_sigmoid(ret_decay_logit[l]).reshape(8)
        g_d = ret_norm_g[l].reshape(1, 256)
        cw = conv_w[l]
        cb = conv_b[l].reshape(1, 512)

        oa_c = _diff_attn(proj, diff_scal, g_a, 0, batch, seq, seq)
        ob_c = _win_attn(proj, sink, 0, batch, seq, seq)
        yc_c, st_c = _ssd(proj, cw, cb, dtb, acoef, dvec, g_c, 0, batch, seq)
        od_c, rt_c = _ret(proj, log_g, g_d, 0, batch, seq)
        cache_a = (cache_diff_k[:, l].reshape(dec_batch, past, 256), cache_diff_v[:, l].reshape(dec_batch, past, 256))
        cache_b = (cache_win_k[:, l].reshape(dec_batch, past, 128), cache_win_v[:, l].reshape(dec_batch, past, 128))
        oa_l = _diff_attn(proj, diff_scal, g_a, n_ctx, dec_batch, dec_seq, BLOCK, cache=cache_a, rope=rope_a)
        ob_l = _win_attn(proj, sink, n_ctx, dec_batch, dec_seq, BLOCK, cache=cache_b, rope=rope_b)
        yc_l, _ = _ssd(proj, cw, cb, dtb, acoef, dvec, g_c, n_ctx, dec_batch, dec_seq,
                       h0=state_ssd[:, l].reshape(dec_batch, 8, C_P, C_N))
        od_l, _ = _ret(proj, log_g, g_d, n_ctx, dec_batch, dec_seq,
                       h0=state_ret[:, l].reshape(dec_batch, 8, D_V, D_K))

        x1, h2, ids, gates, rank, counts = _merge(
            x, mod_l, ((oa_c, oa_l), (ob_c, ob_l), (yc_c, yc_l), (od_c, od_l)), w_gl, w_branch[l].reshape(N_BRANCH * BRANCH, d).astype(BF16),
            w_out[l].astype(BF16), ln_g[l, 0].reshape(1, d), ln_b[l, 0].reshape(1, d), rw_t, rb_col,
            n_ctx, dec_seq)

        dest, free_slots, blk_exp, n_used = _dispatch_plan(ids, rank, counts, n_tok)
        x_slots = _moe_dispatch(h2, dest, free_slots, blk_exp.shape[0] * MOE_ROWS)
        y_slots = _moe_experts(x_slots, blk_exp, n_used, moe_w_gate, moe_w_up, moe_w_down, l)
        x = _final(x1, mod_l, y_slots, dest, gates.T,
                   ln_g[l, 1].reshape(1, d), ln_b[l, 1].reshape(1, d), n_ctx, dec_seq,
                   split=(l == DEPTH - 1))

        pc = proj[:n_ctx]
        ctx_out['diff_k'].append(pc[:, 256:512].reshape(batch, seq, A_HEADS, 2, A_QK))
        ctx_out['diff_v'].append(pc[:, 512:768].reshape(batch, seq, A_HEADS, A_V))
        ctx_out['win_k'].append(pc[:, 1024:1152].reshape(batch, seq, B_KV, B_DIM))
        ctx_out['win_v'].append(pc[:, 1152:1280].reshape(batch, seq, B_KV, B_DIM))
        ctx_out['ssd'].append(st_c.reshape(batch, 2, C_HEADS, C_P, C_N))
        ctx_out['ret'].append(rt_c.reshape(batch, 2, D_HEADS, D_V, D_K))

    y_prompt = x[0].reshape(batch, seq, d)
    y_sample = x[1].reshape(dec_batch, dec_seq, d)
    stk = lambda k: jnp.stack(ctx_out[k], axis=1)
    return (y_prompt, y_sample, stk('diff_k'), stk('diff_v'), stk('win_k'), stk('win_v'), stk('ssd'), stk('ret'))
```

```python
import functools
import math

import jax
import jax.numpy as jnp
from jax import lax
from jax.experimental import pallas as pl
from jax.experimental.pallas import tpu as pltpu

F32 = jnp.float32
BF16 = jnp.bfloat16

D_MODEL = 1024
DEPTH = 4
GRID_W = 64
BLOCK = 128
WINDOW = 128
CHUNK = 128
A_HEADS, A_QK, A_V = 4, 32, 64
B_HEADS, B_KV, B_DIM = 4, 2, 64
C_HEADS, C_P, C_GROUPS, C_N = 4, 64, 2, 64
D_HEADS, D_K, D_V = 4, 32, 64
BRANCH = 256
N_BRANCH = 4
N_EXP = 16
N_EXP_GROUPS = 4
EXP_PER_GROUP = 4
D_FF_EXP = 512
ROPE_BASE = 10000.0
ALPHA = (2 * DEPTH) ** 0.25
EPS = 1e-5
ADA_DIM = 6 * D_MODEL
NEG = -1e30

N_SMALL = 23 * 128
GATE_OFF = 2824
CDT_OFF = 2048

VMEM_LIMIT = 52 * 1024 * 1024
MOE_ROWS = 512
DENSE_TILE = 512
TOK_TILE = 256
DIFF_TQ = 256

_NN = (((1,), (0,)), ((), ()))
_NT = (((1,), (1,)), ((), ()))
_TN = (((0,), (0,)), ((), ()))


def _params(*sem):
    return pltpu.CompilerParams(dimension_semantics=sem, vmem_limit_bytes=VMEM_LIMIT)


def _mm(a, b, dims=_NN):
    return lax.dot_general(a, b, dims, preferred_element_type=F32)


def _split(a):
    hi = a.astype(BF16)
    return hi, (a - hi.astype(F32)).astype(BF16)


def _mm_f32(a, b, dims=_NN):
    a_hi, a_lo = _split(a)
    b_hi, b_lo = _split(b)
    return (_mm(a_lo, b_hi, dims) + _mm(a_hi, b_lo, dims)) + _mm(a_hi, b_hi, dims)


def _mm_exact_lhs(m_bf, a):
    a1 = a.astype(BF16)
    r1 = a - a1.astype(F32)
    a2 = r1.astype(BF16)
    a3 = (r1 - a2.astype(F32)).astype(BF16)
    return (_mm(m_bf, a3) + _mm(m_bf, a2)) + _mm(m_bf, a1)


def _silu(x):
    return x * jax.nn.sigmoid(x)


LANES = 128
TOKEN_TILE_ROWS = D_MODEL // LANES


def _store_token_tiles(ref, val, lead=(), base=0):
    n = val.shape[0]
    for j in range(TOKEN_TILE_ROWS):
        ref[(*lead, pl.ds(base + j, n, stride=TOKEN_TILE_ROWS), slice(None))] = val[:, j * LANES:(j + 1) * LANES]


def _load_token_tiles(ref, n, lead=()):
    return jnp.concatenate(
        [ref[(*lead, pl.ds(j, n, stride=TOKEN_TILE_ROWS), slice(None))] for j in range(TOKEN_TILE_ROWS)], axis=1)


def _rope(x, c, s_lo, s_hi, shift):
    n = x.shape[1]
    return x * c + pltpu.roll(x, n - shift, 1) * s_lo + pltpu.roll(x, shift, 1) * s_hi


def _ada_kernel(c_ref, w_ref, b_ref, o_ref):
    c = c_ref[...]
    o_ref[...] = _mm_f32(_silu(c), w_ref[...]) + b_ref[...]


def _ada(cvec, w_ada, b_ada):
    rows = cvec.shape[0]
    tn = 1024
    return pl.pallas_call(
        _ada_kernel,
        out_shape=jax.ShapeDtypeStruct((DEPTH, rows, ADA_DIM), F32),
        grid=(DEPTH, ADA_DIM // tn),
        in_specs=[
            pl.BlockSpec((rows, D_MODEL), lambda l, j: (0, 0)),
            pl.BlockSpec((None, D_MODEL, tn), lambda l, j: (l, 0, j)),
            pl.BlockSpec((None, 1, tn), lambda l, j: (l, 0, j)),
        ],
        out_specs=pl.BlockSpec((None, rows, tn), lambda l, j: (l, 0, j)),
        compiler_params=_params("arbitrary", "arbitrary"),
        name="ada",
    )(cvec, w_ada, b_ada.reshape(DEPTH, 1, ADA_DIM))


def _inproj_kernel(x_ref, mod_ref, w_ref, o_ref):
    d = x_ref.shape[1]
    h = x_ref[...] * (1.0 + mod_ref[:, d:2 * d]) + mod_ref[:, 0:d]
    o_ref[...] = _mm(h.astype(BF16), w_ref[...])


def _mod_group(i, tm, n_ctx, dec_seq):
    row = i * tm
    return jnp.where(row < n_ctx, 0, 1 + lax.div(jnp.maximum(row - n_ctx, 0), dec_seq))


def _inproj(x, mod_l, w_small, n_ctx, dec_seq):
    n_tok = x.shape[0]
    tm = DENSE_TILE
    grp = functools.partial(_mod_group, tm=tm, n_ctx=n_ctx, dec_seq=dec_seq)
    return pl.pallas_call(
        _inproj_kernel,
        out_shape=jax.ShapeDtypeStruct((n_tok, N_SMALL), F32),
        grid=(n_tok // tm,),
        in_specs=[
            pl.BlockSpec((tm, D_MODEL), lambda i: (i, 0)),
            pl.BlockSpec((None, 1, ADA_DIM), lambda i: (grp(i), 0, 0)),
            pl.BlockSpec((D_MODEL, N_SMALL), lambda i: (0, 0)),
        ],
        out_specs=pl.BlockSpec((tm, N_SMALL), lambda i: (i, 0)),
        compiler_params=_params("arbitrary"),
        name="inproj",
    )(x, mod_l, w_small)


def _diff_attn_kernel(*refs, latent, tq, seq, past):
    if latent:
        (sc_ref, q_ref, k_ref, v_ref, g_ref, ck_ref, cv_ref, rc_ref, rlo_ref, rhi_ref,
         o_ref, kt_scr, v_scr) = refs
    else:
        sc_ref, q_ref, k_ref, v_ref, g_ref, o_ref, kt_scr, v_scr = refs
    qi = pl.program_id(1)
    shift = A_QK // 4

    @pl.when(qi == 0)
    def _():
        k = k_ref[...]
        if latent:
            k = _rope(k, rc_ref[...], rlo_ref[...], rhi_ref[...], shift)
        kt_scr[:, 0:seq] = k.T.astype(BF16)
        if latent:
            kt_scr[:, seq:seq + past] = ck_ref[...].T.astype(BF16)
        ones = jnp.ones((seq + past, LANES - A_V), BF16)
        for h in range(A_HEADS):
            hs = slice(h * A_V, (h + 1) * A_V)
            v_scr[0:seq, h * LANES:h * LANES + A_V] = v_ref[:, hs].astype(BF16)
            if latent:
                v_scr[seq:seq + past, h * LANES:h * LANES + A_V] = cv_ref[:, hs].astype(BF16)
            v_scr[:, h * LANES + A_V:(h + 1) * LANES] = ones

    q = q_ref[...]
    if latent:
        r = pl.ds(pl.multiple_of(qi * tq, tq), tq)
        q = _rope(q, rc_ref[r, :], rlo_ref[r, :], rhi_ref[r, :], shift)
    qb = q.astype(BF16)
    lam = sc_ref[0]
    post = sc_ref[1]
    c = (A_QK ** -0.5) * math.log2(math.e)
    outs = []
    for h in range(A_HEADS):
        probs = []
        for m in range(2):
            off = (h * 2 + m) * A_QK
            s = _mm(qb[:, off:off + A_QK], kt_scr[off:off + A_QK, :])
            probs.append(jnp.exp2(s * c - jnp.max(s, axis=1, keepdims=True) * c).astype(BF16))
        ov = _mm(jnp.concatenate(probs, axis=0), v_scr[:, h * LANES:(h + 1) * LANES])
        maps = [ov[m * tq:(m + 1) * tq, 0:A_V] * (1.0 / ov[m * tq:(m + 1) * tq, A_V:A_V + 1]) for m in range(2)]
        o = maps[0] - lam * maps[1]
        n = o * lax.rsqrt(jnp.mean(o * o, axis=1, keepdims=True) + EPS)
        outs.append((n * g_ref[...]) * post)
    o_ref[...] = jnp.concatenate(outs, axis=1).astype(BF16)


def _diff_attn(proj, scal, g, row0, nb, seq, tq, cache=None, rope=None):
    latent = cache is not None
    nq = seq // tq
    rb = row0 // seq
    qb0 = row0 // tq
    past = cache[0].shape[1] if latent else 0
    in_specs = [
        pl.BlockSpec(memory_space=pltpu.SMEM),
        pl.BlockSpec((tq, 256), lambda b, i: (qb0 + b * nq + i, 0)),
        pl.BlockSpec((seq, 256), lambda b, i: (rb + b, 1)),
        pl.BlockSpec((seq, 256), lambda b, i: (rb + b, 2)),
        pl.BlockSpec((1, A_V), lambda b, i: (0, 0)),
    ]
    args = [scal, proj, proj, proj, g]
    if latent:
        in_specs += [
            pl.BlockSpec((None, past, 256), lambda b, i: (b, 0, 0)),
            pl.BlockSpec((None, past, 256), lambda b, i: (b, 0, 0)),
        ] + [pl.BlockSpec((seq, 256), lambda b, i: (0, 0))] * 3
        args += [cache[0], cache[1], *rope]
    return pl.pallas_call(
        functools.partial(_diff_attn_kernel, latent=latent, tq=tq, seq=seq, past=past),
        out_shape=jax.ShapeDtypeStruct((nb * seq, BRANCH), BF16),
        grid=(nb, nq),
        in_specs=in_specs,
        out_specs=pl.BlockSpec((tq, BRANCH), lambda b, i: (b * nq + i, 0)),
        scratch_shapes=[pltpu.VMEM((256, seq + past), BF16),
                        pltpu.VMEM((seq + past, A_HEADS * LANES), BF16)],
        compiler_params=_params("arbitrary", "arbitrary"),
        name="diff_attn_lat" if latent else "diff_attn_ctx",
    )(*args)


def _win_attn_kernel(*refs, latent, tq, seq):
    if latent:
        (sink_ref, q_ref, k_ref, v_ref, ck_ref, cv_ref, rc_ref, rlo_ref, rhi_ref,
         o_ref, k_scr, v_scr, ck_scr, cv_scr) = refs
    else:
        sink_ref, q_ref, k_ref, v_ref, o_ref, k_scr, v_scr = refs
    qi = pl.program_id(1)
    nq = seq // tq
    shift = B_DIM // 4
    kvw = B_KV * B_DIM

    @pl.when(qi == 0)
    def _():
        k = k_ref[...]
        if latent:
            k = _rope(k, rc_ref[:, 0:kvw], rlo_ref[:, 0:kvw], rhi_ref[:, 0:kvw], shift)
            ck_scr[...] = ck_ref[...].T.astype(BF16)
            cv_scr[...] = cv_ref[...].astype(BF16)
        kt = k.T.astype(BF16)
        for j in range(nq):
            k_scr[j] = kt[:, j * tq:(j + 1) * tq]
        v_scr[...] = v_ref[...].astype(BF16)

    q = q_ref[...]
    if latent:
        r = pl.ds(pl.multiple_of(qi * tq, tq), tq)
        q = _rope(q, rc_ref[r, :], rlo_ref[r, :], rhi_ref[r, :], shift)
        near = (jnp.maximum(qi - 1, 0), qi, jnp.minimum(qi + 1, nq - 1))
        kl = jnp.concatenate([k_scr[j] for j in near], axis=1)
        vl = jnp.concatenate([v_scr[pl.ds(pl.multiple_of(j * tq, tq), tq), :] for j in near], axis=0)
    qb = q.astype(BF16)
    scale = B_DIM ** -0.5
    ratio = B_HEADS // B_KV
    rows = ratio * tq
    if latent:
        ii = lax.broadcasted_iota(jnp.int32, (rows, 3 * tq), 0) & (tq - 1)
        jj = lax.broadcasted_iota(jnp.int32, (rows, 3 * tq), 1)
        lo = jnp.where(qi > 0, 0, tq)
        hi = jnp.where(qi < nq - 1, 3 * tq, 2 * tq)
        valid = (jnp.abs(jj - tq - ii) <= WINDOW) & (jj >= lo) & (jj < hi)
    else:
        kl = k_scr[0]
        vl = v_scr[...]
    row_id = lax.broadcasted_iota(jnp.int32, (rows, 1), 0)
    head_of_row = (row_id - (row_id & (tq - 1))) // tq if ratio > 2 else (row_id >= tq).astype(jnp.int32)
    outs = []
    for g in range(B_KV):
        gsl = slice(g * B_DIM, (g + 1) * B_DIM)
        heads = range(g * ratio, (g + 1) * ratio)
        qg = jnp.concatenate([qb[:, h * B_DIM:(h + 1) * B_DIM] for h in heads], axis=0)
        snk = jnp.full((rows, 1), sink_ref[heads[0]], F32)
        for n, h in enumerate(heads[1:], start=1):
            snk = jnp.where(head_of_row == n, sink_ref[h], snk)
        s = _mm(qg, kl[gsl, :]) * scale
        if latent:
            s = jnp.where(valid, s, NEG)
            sc = _mm(qg, ck_scr[gsl, :]) * scale
            m = jnp.maximum(jnp.maximum(jnp.max(s, axis=1, keepdims=True),
                                        jnp.max(sc, axis=1, keepdims=True)), snk)
            pc = jnp.exp(sc - m)
        else:
            m = jnp.maximum(jnp.max(s, axis=1, keepdims=True), snk)
        p = jnp.exp(s - m)
        den = jnp.sum(p, axis=1, keepdims=True) + jnp.exp(snk - m)
        if latent:
            den = den + jnp.sum(pc, axis=1, keepdims=True)
        inv = 1.0 / den
        o = _mm((p * inv).astype(BF16), vl[:, gsl])
        if latent:
            o = o + _mm((pc * inv).astype(BF16), cv_scr[:, gsl])
        outs += [o[n * tq:(n + 1) * tq, :] for n in range(ratio)]
    o_ref[...] = jnp.concatenate(outs, axis=1).astype(BF16)


def _win_attn(proj, sink, row0, nb, seq, tq, cache=None, rope=None):
    latent = cache is not None
    nq = seq // tq
    rb = row0 // seq
    qb0 = row0 // tq
    kvw = B_KV * B_DIM
    in_specs = [
        pl.BlockSpec(memory_space=pltpu.SMEM),
        pl.BlockSpec((tq, 256), lambda b, i: (qb0 + b * nq + i, 3)),
        pl.BlockSpec((seq, kvw), lambda b, i: (rb + b, 8)),
        pl.BlockSpec((seq, kvw), lambda b, i: (rb + b, 9)),
    ]
    args = [sink, proj, proj, proj]
    scratch = [pltpu.VMEM((nq, kvw, tq), BF16), pltpu.VMEM((seq, kvw), BF16)]
    if latent:
        past = cache[0].shape[1]
        in_specs += [
            pl.BlockSpec((None, past, kvw), lambda b, i: (b, 0, 0)),
            pl.BlockSpec((None, past, kvw), lambda b, i: (b, 0, 0)),
        ] + [pl.BlockSpec((seq, 256), lambda b, i: (0, 0))] * 3
        args += [cache[0], cache[1], *rope]
        scratch += [pltpu.VMEM((kvw, past), BF16), pltpu.VMEM((past, kvw), BF16)]
    return pl.pallas_call(
        functools.partial(_win_attn_kernel, latent=latent, tq=tq, seq=seq),
        out_shape=jax.ShapeDtypeStruct((nb * seq, BRANCH), BF16),
        grid=(nb, nq),
        in_specs=in_specs,
        out_specs=pl.BlockSpec((tq, BRANCH), lambda b, i: (b * nq + i, 0)),
        scratch_shapes=scratch,
        compiler_params=_params("arbitrary", "arbitrary"),
        name="win_attn_lat" if latent else "win_attn_ctx",
    )(*args)


def _conv_silu(u, w, b):
    n = u.shape[0]
    rows = lax.broadcasted_iota(jnp.int32, u.shape, 0)
    up = jnp.where(rows == 0, 0.0, pltpu.roll(u, 1, 0))
    un = jnp.where(rows == n - 1, 0.0, pltpu.roll(u, n - 1, 0))
    return _silu(up * w[0:1, :] + u * w[1:2, :] + un * w[2:3, :] + b)


def _ssd_kernel(*refs, seq, has_h0):
    if has_h0:
        (cx_ref, cz_ref, cbc_ref, cdt_ref, cw_ref, cb_ref, dtb_ref, ac_ref, dv_ref, g_ref, h0_ref,
         y_ref, st_ref, xs, bcs, dts, ybuf, ybuf_b, hs) = refs
    else:
        (cx_ref, cz_ref, cbc_ref, cdt_ref, cw_ref, cb_ref, dtb_ref, ac_ref, dv_ref, g_ref,
         y_ref, st_ref, xs, bcs, dts, ybuf, ybuf_b, hs) = refs
    nc = seq // CHUNK
    xw = C_HEADS * C_P
    xs[...] = _conv_silu(cx_ref[...], cw_ref[:, 0:xw], cb_ref[:, 0:xw])
    bcs[...] = _conv_silu(cbc_ref[...], cw_ref[:, xw:2 * xw], cb_ref[:, xw:2 * xw])
    z = cdt_ref[...] + dtb_ref[...]
    dts[...] = jnp.maximum(z, 0.0) + jnp.log1p(jnp.exp(-jnp.abs(z)))
    if has_h0:
        hs[...] = h0_ref[...]
    else:
        hs[...] = jnp.zeros(hs.shape, F32)

    ri = lax.broadcasted_iota(jnp.int32, (CHUNK, CHUNK), 0)
    ci = lax.broadcasted_iota(jnp.int32, (CHUNK, CHUNK), 1)
    gw = C_GROUPS * C_N

    def chunk(c, d):
        r = pl.ds(pl.multiple_of(c * CHUNK, CHUNK), CHUNK)
        tri = (ri >= ci) if d == 0 else (ci >= ri)
        dt = dts[r, :]
        cs = _mm_exact_lhs(tri.astype(BF16), dt * ac_ref[...])
        cst = cs.T
        x = xs[r, :]
        bc = bcs[r, :]
        ys = []
        for g in range(C_GROUPS):
            bm = bc[:, g * C_N:(g + 1) * C_N]
            cb = bc[:, gw + g * C_N:gw + (g + 1) * C_N].astype(BF16)
            gram = _mm(cb, bm.astype(BF16), _NT)
            for hh in range(C_HEADS // C_GROUPS):
                h = g * (C_HEADS // C_GROUPS) + hh
                col = d * C_HEADS + h
                xb = (x[:, h * C_P:(h + 1) * C_P] * dt[:, col:col + 1]).astype(BF16)
                cc = cs[:, col:col + 1]
                dec = jnp.exp(jnp.where(tri, cc - cst[col:col + 1, :], NEG))
                tot = cc[CHUNK - 1:CHUNK, :] if d == 0 else cc[0:1, :]
                hin = hs[col]
                y = _mm((gram * dec).astype(BF16), xb)
                y = y + _mm(cb, hin.astype(BF16), _NT) * jnp.exp(cc)
                bd = (bm * jnp.exp(tot - cc)).astype(BF16)
                hs[col] = hin * jnp.exp(tot) + _mm(xb, bd, _TN)
                ys.append(y)
        return r, jnp.concatenate(ys, axis=1)

    def scan(t, carry):
        r, y = chunk(t, 0)
        ybuf[r, :] = y
        r, y = chunk(nc - 1 - t, 1)
        ybuf_b[r, :] = y
        return carry

    lax.fori_loop(0, nc, scan, 0)

    def finish(c, carry):
        r = pl.ds(pl.multiple_of(c * CHUNK, CHUNK), CHUNK)
        y = (ybuf[r, :] + ybuf_b[r, :]) + xs[r, :] * dv_ref[...]
        y = y * _silu(cz_ref[r, :])
        gl = xw // C_GROUPS
        parts = []
        for g in range(C_GROUPS):
            seg = y[:, g * gl:(g + 1) * gl]
            parts.append(seg * lax.rsqrt(jnp.mean(seg * seg, axis=1, keepdims=True) + EPS))
        y_ref[r, :] = (jnp.concatenate(parts, axis=1) * g_ref[...]).astype(BF16)
        return carry

    lax.fori_loop(0, nc, finish, 0)
    st_ref[...] = hs[...]


def _ssd(proj, conv_w, conv_b, dtb, acoef, dvec, g, row0, nb, seq, h0=None):
    has_h0 = h0 is not None
    rb = row0 // seq
    nst = 2 * C_HEADS
    in_specs = [
        pl.BlockSpec((seq, 256), lambda b: (rb + b, 5)),
        pl.BlockSpec((seq, 256), lambda b: (rb + b, 6)),
        pl.BlockSpec((seq, 256), lambda b: (rb + b, 7)),
        pl.BlockSpec((seq, 128), lambda b: (rb + b, 22)),
        pl.BlockSpec((3, 512), lambda b: (0, 0)),
        pl.BlockSpec((1, 512), lambda b: (0, 0)),
        pl.BlockSpec((1, 128), lambda b: (0, 0)),
        pl.BlockSpec((1, 128), lambda b: (0, 0)),
        pl.BlockSpec((1, 256), lambda b: (0, 0)),
        pl.BlockSpec((1, 256), lambda b: (0, 0)),
    ]
    args = [proj, proj, proj, proj, conv_w, conv_b, dtb, acoef, dvec, g]
    if has_h0:
        in_specs.append(pl.BlockSpec((None, nst, C_P, C_N), lambda b: (b, 0, 0, 0)))
        args.append(h0)
    return pl.pallas_call(
        functools.partial(_ssd_kernel, seq=seq, has_h0=has_h0),
        out_shape=(jax.ShapeDtypeStruct((nb * seq, BRANCH), BF16),
                   jax.ShapeDtypeStruct((nb, nst, C_P, C_N), F32)),
        grid=(nb,),
        in_specs=in_specs,
        out_specs=(pl.BlockSpec((seq, BRANCH), lambda b: (b, 0)),
                   pl.BlockSpec((None, nst, C_P, C_N), lambda b: (b, 0, 0, 0))),
        scratch_shapes=[pltpu.VMEM((seq, 256), F32), pltpu.VMEM((seq, 256), F32),
                        pltpu.VMEM((seq, 128), F32), pltpu.VMEM((seq, 256), F32),
                        pltpu.VMEM((seq, 256), F32), pltpu.VMEM((nst, C_P, C_N), F32)],
        compiler_params=_params("arbitrary"),
        name="ssd_lat" if has_h0 else "ssd_ctx",
    )(*args)


def _ret_kernel(*refs, seq, has_h0):
    if has_h0:
        lg_ref, q_ref, k_ref, v_ref, gt_ref, g_ref, h0_ref, y_ref, st_ref, ybuf, ybuf_b, hs = refs
    else:
        lg_ref, q_ref, k_ref, v_ref, gt_ref, g_ref, y_ref, st_ref, ybuf, ybuf_b, hs = refs
    nc = seq // CHUNK
    if has_h0:
        hs[...] = h0_ref[...]
    else:
        hs[...] = jnp.zeros(hs.shape, F32)
    ri = lax.broadcasted_iota(jnp.int32, (CHUNK, CHUNK), 0)
    ci = lax.broadcasted_iota(jnp.int32, (CHUNK, CHUNK), 1)
    pos = lax.broadcasted_iota(jnp.int32, (CHUNK, 1), 0).astype(F32)
    kscale = D_K ** -0.5

    def chunk(c, d):
        r = pl.ds(pl.multiple_of(c * CHUNK, CHUNK), CHUNK)
        q = q_ref[r, :]
        k = k_ref[r, :] * kscale
        v = v_ref[r, :]
        if d == 0:
            tri, dist = ri >= ci, (ri - ci).astype(F32)
            steps_in, steps_out = pos + 1.0, (CHUNK - 1.0) - pos
        else:
            tri, dist = ci >= ri, (ci - ri).astype(F32)
            steps_in, steps_out = CHUNK - pos, pos
        ys = []
        for h in range(D_HEADS):
            col = d * D_HEADS + h
            lg = lg_ref[col]
            dec = jnp.exp(jnp.where(tri, dist * lg, NEG))
            e_in = jnp.exp(steps_in * lg)
            e_tot = e_in[CHUNK - 1:CHUNK, :] if d == 0 else e_in[0:1, :]
            qb = q[:, h * D_K:(h + 1) * D_K].astype(BF16)
            km = k[:, h * D_K:(h + 1) * D_K]
            vb = v[:, h * D_V:(h + 1) * D_V].astype(BF16)
            hin = hs[col]
            y = _mm((_mm(qb, km.astype(BF16), _NT) * dec).astype(BF16), vb)
            y = y + _mm(qb, hin.astype(BF16), _NT) * e_in
            bd = (km * jnp.exp(steps_out * lg)).astype(BF16)
            hs[col] = hin * e_tot + _mm(vb, bd, _TN)
            ys.append(y)
        return r, ys

    def scan(t, carry):
        r, ys = chunk(t, 0)
        ybuf[r, :] = jnp.concatenate(ys, axis=1)
        r, ys = chunk(nc - 1 - t, 1)
        ybuf_b[r, :] = jnp.concatenate(ys, axis=1)
        return carry

    lax.fori_loop(0, nc, scan, 0)

    def finish(c, carry):
        r = pl.ds(pl.multiple_of(c * CHUNK, CHUNK), CHUNK)
        yf = ybuf[r, :] + ybuf_b[r, :]
        parts = []
        for h in range(D_HEADS):
            o = yf[:, h * D_V:(h + 1) * D_V]
            o = o - jnp.mean(o, axis=1, keepdims=True)
            parts.append(o * lax.rsqrt(jnp.mean(o * o, axis=1, keepdims=True) + EPS))
        y = (jnp.concatenate(parts, axis=1) * g_ref[...]) * _silu(gt_ref[r, :])
        y_ref[r, :] = y.astype(BF16)
        return carry

    lax.fori_loop(0, nc, finish, 0)
    st_ref[...] = hs[...]


def _ret(proj, log_g, g, row0, nb, seq, h0=None):
    has_h0 = h0 is not None
    rb = row0 // seq
    nst = 2 * D_HEADS
    in_specs = [
        pl.BlockSpec(memory_space=pltpu.SMEM),
        pl.BlockSpec((seq, 128), lambda b: (rb + b, 16)),
        pl.BlockSpec((seq, 128), lambda b: (rb + b, 17)),
        pl.BlockSpec((seq, 256), lambda b: (rb + b, 9)),
        pl.BlockSpec((seq, 256), lambda b: (rb + b, 10)),
        pl.BlockSpec((1, 256), lambda b: (0, 0)),
    ]
    args = [log_g, proj, proj, proj, proj, g]
    if has_h0:
        in_specs.append(pl.BlockSpec((None, nst, D_V, D_K), lambda b: (b, 0, 0, 0)))
        args.append(h0)
    return pl.pallas_call(
        functools.partial(_ret_kernel, seq=seq, has_h0=has_h0),
        out_shape=(jax.ShapeDtypeStruct((nb * seq, BRANCH), BF16),
                   jax.ShapeDtypeStruct((nb, nst, D_V, D_K), F32)),
        grid=(nb,),
        in_specs=in_specs,
        out_specs=(pl.BlockSpec((seq, BRANCH), lambda b: (b, 0)),
                   pl.BlockSpec((None, nst, D_V, D_K), lambda b: (b, 0, 0, 0))),
        scratch_shapes=[pltpu.VMEM((seq, 256), F32), pltpu.VMEM((seq, 256), F32),
                        pltpu.VMEM((nst, D_V, D_K), F32)],
        compiler_params=_params("arbitrary"),
        name="ret_lat" if has_h0 else "ret_ctx",
    )(*args)


def _route(sel, s):
    row = lambda a, e: a[e:e + 1, :]
    best = None
    grp = None
    for g in range(N_EXP_GROUPS):
        vals = [row(sel, g * EXP_PER_GROUP + j) for j in range(EXP_PER_GROUP)]
        score = None
        for a in range(EXP_PER_GROUP):
            for b in range(a + 1, EXP_PER_GROUP):
                pair = vals[a] + vals[b]
                score = pair if score is None else jnp.maximum(score, pair)
        if best is None:
            best, grp = score, jnp.zeros(score.shape, jnp.int32)
        else:
            better = score > best
            best = jnp.where(better, score, best)
            grp = jnp.where(better, g, grp)

    def pick(a, j):
        out = row(a, j)
        for g in range(1, N_EXP_GROUPS):
            out = jnp.where(grp == g, row(a, g * EXP_PER_GROUP + j), out)
        return out

    cand = [pick(sel, j) for j in range(EXP_PER_GROUP)]
    aff = [pick(s, j) for j in range(EXP_PER_GROUP)]

    def arg_first_max(vals):
        top, idx = vals[0], jnp.zeros(vals[0].shape, jnp.int32)
        for j in range(1, len(vals)):
            better = vals[j] > top
            top = jnp.where(better, vals[j], top)
            idx = jnp.where(better, j, idx)
        return idx

    def take(vals, idx):
        out = vals[0]
        for j in range(1, len(vals)):
            out = jnp.where(idx == j, vals[j], out)
        return out

    i1 = arg_first_max(cand)
    i2 = arg_first_max([jnp.where(i1 == j, -jnp.inf, cand[j]) for j in range(EXP_PER_GROUP)])
    w1, w2 = take(aff, i1), take(aff, i2)
    tot = w1 + w2
    ids = jnp.concatenate([grp * EXP_PER_GROUP + i1, grp * EXP_PER_GROUP + i2], axis=0)
    gates = jnp.concatenate([w1 / tot, w2 / tot], axis=0)
    return ids, gates


def _merge_kernel(*refs, n_ctx_tiles):
    (x_ref, mod_ref, wgl_ref, wbr_ref, wout_ref, lng_ref, lnb_ref, rw_ref, rb_ref, tri_ref) = refs[:10]
    br_refs = refs[10:10 + 2 * N_BRANCH]
    x1_ref, h2_ref, ids_ref, gates_ref, rank_ref, cnt_ref = refs[10 + 2 * N_BRANCH:]
    d = D_MODEL

    @pl.when(pl.program_id(0) == 0)
    def _():
        cnt_ref[...] = jnp.zeros(cnt_ref.shape, F32)

    is_ctx = pl.program_id(0) < n_ctx_tiles
    sub = tri_ref.shape[0]
    for part in range(x_ref.shape[0] // sub):
        rs = slice(part * sub, (part + 1) * sub)
        x = x_ref[rs, :]
        hb = (x * (1.0 + mod_ref[:, d:2 * d]) + mod_ref[:, 0:d]).astype(BF16)
        merged = None
        for k in range(N_BRANCH):
            gate = jax.nn.sigmoid(_mm(hb, wgl_ref[:, k * d:(k + 1) * d]))
            br_k = jnp.where(is_ctx, br_refs[2 * k][rs, :], br_refs[2 * k + 1][rs, :])
            up = _mm(br_k, wbr_ref[k * BRANCH:(k + 1) * BRANCH, :])
            merged = gate * up if merged is None else merged + gate * up
        mix = _mm(merged.astype(BF16), wout_ref[...])
        y = ALPHA * x + mod_ref[:, 2 * d:3 * d] * mix
        y = y - jnp.mean(y, axis=1, keepdims=True)
        x1 = (y * lax.rsqrt(jnp.mean(y * y, axis=1, keepdims=True) + EPS)) * lng_ref[...] + lnb_ref[...]
        x1_ref[rs, :] = x1
        h2 = x1 * (1.0 + mod_ref[:, 4 * d:5 * d]) + mod_ref[:, 3 * d:4 * d]
        _store_token_tiles(h2_ref, h2, base=part * sub * TOKEN_TILE_ROWS)
        s = jax.nn.sigmoid(_mm_f32(rw_ref[...], h2, _NT))
        ids, gates = _route(s + rb_ref[...], s)
        ids_ref[:, rs] = ids
        gates_ref[:, rs] = gates
        expert = lax.broadcasted_iota(jnp.int32, (N_EXP, sub), 0)
        hot = [(expert == ids[k:k + 1, :]).astype(F32) for k in range(2)]
        both = hot[0] + hot[1]
        incl = _mm(both.astype(BF16), tri_ref[...])
        before = cnt_ref[...] + (incl - both)
        rank_ref[:, rs] = jnp.concatenate(
            [jnp.sum(hk * before, axis=0, keepdims=True) for hk in hot], axis=0).astype(jnp.int32)
        cnt_ref[...] = cnt_ref[...] + incl[:, sub - 1:sub]


def _merge(x, mod_l, branches, w_gl, w_br, w_out, ln_g, ln_b, rw_t, rb, n_ctx, dec_seq):
    n_tok = x.shape[0]
    tm = DENSE_TILE
    nct = n_ctx // tm
    grp = functools.partial(_mod_group, tm=tm, n_ctx=n_ctx, dec_seq=dec_seq)
    full = lambda shape: pl.BlockSpec(shape, lambda i: (0,) * len(shape))
    pos = jnp.arange(tm)
    tri = (pos[:, None] <= pos[None, :]).astype(BF16)
    br_specs = [pl.BlockSpec((tm, BRANCH), lambda i: (jnp.minimum(i, nct - 1), 0)),
                pl.BlockSpec((tm, BRANCH), lambda i: (jnp.maximum(i - nct, 0), 0))] * N_BRANCH
    br_args = [a for pair in branches for a in pair]
    return pl.pallas_call(
        functools.partial(_merge_kernel, n_ctx_tiles=nct),
        out_shape=(jax.ShapeDtypeStruct((n_tok, D_MODEL), F32),
                   jax.ShapeDtypeStruct((n_tok * TOKEN_TILE_ROWS, LANES), F32),
                   jax.ShapeDtypeStruct((2, n_tok), jnp.int32),
                   jax.ShapeDtypeStruct((2, n_tok), F32),
                   jax.ShapeDtypeStruct((2, n_tok), jnp.int32),
                   jax.ShapeDtypeStruct((N_EXP, 1), F32)),
        grid=(n_tok // tm,),
        in_specs=[
            pl.BlockSpec((tm, D_MODEL), lambda i: (i, 0)),
            pl.BlockSpec((None, 1, ADA_DIM), lambda i: (grp(i), 0, 0)),
            full((D_MODEL, N_BRANCH * D_MODEL)),
            full((N_BRANCH * BRANCH, D_MODEL)),
            full((D_MODEL, D_MODEL)),
            full((1, D_MODEL)),
            full((1, D_MODEL)),
            full((N_EXP, D_MODEL)),
            full((N_EXP, 1)),
            full((tm, tm)),
        ] + br_specs,
        out_specs=(pl.BlockSpec((tm, D_MODEL), lambda i: (i, 0)),
                   pl.BlockSpec((tm * TOKEN_TILE_ROWS, LANES), lambda i: (i, 0)),
                   pl.BlockSpec((2, tm), lambda i: (0, i)),
                   pl.BlockSpec((2, tm), lambda i: (0, i)),
                   pl.BlockSpec((2, tm), lambda i: (0, i)),
                   full((N_EXP, 1))),
        compiler_params=_params("arbitrary"),
        name="merge",
    )(x, mod_l, w_gl, w_br, w_out, ln_g, ln_b, rw_t, rb, tri, *br_args)


def _dispatch_kernel(dest_ref, pad_ref, h_hbm, x_hbm, hbuf, zbuf, in_sem, out_sem, pad_sem, *, n_tok):
    tr = TOKEN_TILE_ROWS
    rows = TOK_TILE * tr
    i = pl.program_id(0)
    last = pl.num_programs(0) - 1
    slot = lax.rem(i, 2)
    other = 1 - slot

    def load(tile, buf):
        return pltpu.make_async_copy(h_hbm.at[pl.ds(tile * rows, rows), :], hbuf.at[buf], in_sem.at[buf])

    def slot_copy(r, dst, buf):
        return pltpu.make_async_copy(hbuf.at[buf, pl.ds(r * tr, tr), :],
                                     x_hbm.at[pl.ds(dst * tr, tr), :], out_sem.at[buf])

    def drain(buf):
        for r in range(2 * TOK_TILE):
            slot_copy(0, 0, buf).wait()

    @pl.when(i == 0)
    def _():
        load(0, 0).start()

    load(i, slot).wait()

    @pl.when(i < last)
    def _():
        @pl.when(i >= 1)
        def _():
            drain(other)
        load(i + 1, other).start()

    base = i * TOK_TILE
    for r in range(TOK_TILE):
        slot_copy(r, dest_ref[base + r], slot).start()
        slot_copy(r, dest_ref[n_tok + base + r], slot).start()

    @pl.when(i == last)
    def _():
        zbuf[...] = jnp.zeros(zbuf.shape, F32)

        def zero_copy(dst):
            return pltpu.make_async_copy(zbuf, x_hbm.at[pl.ds(dst * tr, tr), :], pad_sem)

        n_pad = pad_ref.shape[0]

        def fill(j, carry):
            for u in range(PAD_UNROLL):
                zero_copy(pad_ref[j * PAD_UNROLL + u]).start()
            return carry

        lax.fori_loop(0, n_pad // PAD_UNROLL, fill, 0)
        drain(slot)

        @pl.when(i >= 1)
        def _():
            drain(other)

        def unfill(j, carry):
            for u in range(PAD_UNROLL):
                zero_copy(0).wait()
            return carry

        lax.fori_loop(0, n_pad // PAD_UNROLL, unfill, 0)


PAD_UNROLL = 8


def _moe_dispatch(h2_tiles, dest, pad_slots, n_slots):
    tr = TOKEN_TILE_ROWS
    n_tok = h2_tiles.shape[0] // tr
    assert pad_slots.shape[0] % PAD_UNROLL == 0
    any_spec = pl.BlockSpec(memory_space=pl.ANY)
    return pl.pallas_call(
        functools.partial(_dispatch_kernel, n_tok=n_tok),
        out_shape=jax.ShapeDtypeStruct((n_slots * tr, LANES), F32),
        grid_spec=pltpu.PrefetchScalarGridSpec(
            num_scalar_prefetch=2,
            grid=(n_tok // TOK_TILE,),
            in_specs=[any_spec],
            out_specs=any_spec,
            scratch_shapes=[pltpu.VMEM((2, TOK_TILE * tr, LANES), F32), pltpu.VMEM((tr, LANES), F32),
                            pltpu.SemaphoreType.DMA((2,)), pltpu.SemaphoreType.DMA((2,)),
                            pltpu.SemaphoreType.DMA(())],
        ),
        compiler_params=_params("arbitrary"),
        name="moe_dispatch",
    )(dest, pad_slots, h2_tiles)


def _moe_kernel(be_ref, nu_ref, x_ref, wg_ref, wu_ref, wd_ref, o_ref):
    del be_ref
    i = pl.program_id(0)

    @pl.when(i < nu_ref[0])
    def _():
        x = _load_token_tiles(x_ref, MOE_ROWS).astype(BF16)
        act = _silu(_mm(x, wg_ref[...].astype(BF16))) * _mm(x, wu_ref[...].astype(BF16))
        _store_token_tiles(o_ref, _mm(act.astype(BF16), wd_ref[...].astype(BF16)))

    @pl.when(i >= nu_ref[0])
    def _():
        o_ref[...] = jnp.zeros(o_ref.shape, F32)


def _moe_experts(x_tiles, blk_exp, n_used, wg, wu, wd, layer):
    n_blk = blk_exp.shape[0]
    blk = pl.BlockSpec((MOE_ROWS * TOKEN_TILE_ROWS, LANES), lambda i, be, nu: (i, 0))
    return pl.pallas_call(
        _moe_kernel,
        out_shape=jax.ShapeDtypeStruct(x_tiles.shape, F32),
        grid_spec=pltpu.PrefetchScalarGridSpec(
            num_scalar_prefetch=2,
            grid=(n_blk,),
            in_specs=[
                blk,
                pl.BlockSpec((None, None, D_MODEL, D_FF_EXP), lambda i, be, nu: (layer, be[i], 0, 0)),
                pl.BlockSpec((None, None, D_MODEL, D_FF_EXP), lambda i, be, nu: (layer, be[i], 0, 0)),
                pl.BlockSpec((None, None, D_FF_EXP, D_MODEL), lambda i, be, nu: (layer, be[i], 0, 0)),
            ],
            out_specs=blk,
        ),
        compiler_params=_params("arbitrary"),
        name="moe_experts",
    )(blk_exp, n_used, x_tiles, wg, wu, wd)


def _final_kernel(dest_ref, x1_ref, mod_ref, gt_ref, lng_ref, lnb_ref, y_hbm, *rest, n_ctx_tiles, n_tok):
    *o_refs, ybuf, sem = rest
    d = D_MODEL
    n = x1_ref.shape[0]
    tr = TOKEN_TILE_ROWS
    i = pl.program_id(0)
    slot = lax.rem(i, 2)

    def fetch_copy(src, k, r, buf):
        return pltpu.make_async_copy(y_hbm.at[pl.ds(src * tr, tr), :],
                                     ybuf.at[buf, k, pl.ds(r * tr, tr), :], sem.at[buf])

    def start_fetch(tile, buf):
        for k in range(2):
            for r in range(n):
                fetch_copy(dest_ref[k * n_tok + tile * n + r], k, r, buf).start()

    @pl.when(i == 0)
    def _():
        start_fetch(0, 0)

    for k in range(2):
        for r in range(n):
            fetch_copy(0, k, r, slot).wait()

    @pl.when(i + 1 < pl.num_programs(0))
    def _():
        start_fetch(i + 1, 1 - slot)

    gt = gt_ref[...]
    ffn = (_load_token_tiles(ybuf, n, lead=(slot, 0)) * gt[:, 0:1]
           + _load_token_tiles(ybuf, n, lead=(slot, 1)) * gt[:, 1:2])
    y = ALPHA * x1_ref[...] + mod_ref[:, 5 * d:6 * d] * ffn
    y = y - jnp.mean(y, axis=1, keepdims=True)
    out = (y * lax.rsqrt(jnp.mean(y * y, axis=1, keepdims=True) + EPS)) * lng_ref[...] + lnb_ref[...]
    if len(o_refs) == 1:
        o_refs[0][...] = out
    else:
        @pl.when(pl.program_id(0) < n_ctx_tiles)
        def _():
            o_refs[0][...] = out

        @pl.when(pl.program_id(0) >= n_ctx_tiles)
        def _():
            o_refs[1][...] = out


def _final(x1, mod_l, y_slots, dest, gates_t, ln_g, ln_b, n_ctx, dec_seq, split=False):
    n_tok = x1.shape[0]
    tm = TOK_TILE
    nt = n_tok // tm
    nct = n_ctx // tm
    grp = functools.partial(_mod_group, tm=tm, n_ctx=n_ctx, dec_seq=dec_seq)
    tile = pl.BlockSpec((tm, D_MODEL), lambda i, dst: (i, 0))
    vec = pl.BlockSpec((1, D_MODEL), lambda i, dst: (0, 0))
    if split:
        out_shape = (jax.ShapeDtypeStruct((n_ctx, D_MODEL), F32), jax.ShapeDtypeStruct((n_tok - n_ctx, D_MODEL), F32))
        out_specs = (pl.BlockSpec((tm, D_MODEL), lambda i, dst: (jnp.minimum(i, nct - 1), 0)),
                     pl.BlockSpec((tm, D_MODEL), lambda i, dst: (jnp.maximum(i - nct, 0), 0)))
    else:
        out_shape = jax.ShapeDtypeStruct((n_tok, D_MODEL), F32)
        out_specs = tile
    return pl.pallas_call(
        functools.partial(_final_kernel, n_ctx_tiles=nct, n_tok=n_tok),
        out_shape=out_shape,
        grid_spec=pltpu.PrefetchScalarGridSpec(
            num_scalar_prefetch=1,
            grid=(nt,),
            in_specs=[tile, pl.BlockSpec((None, 1, ADA_DIM), lambda i, dst: (grp(i), 0, 0)),
                      pl.BlockSpec((tm, 2), lambda i, dst: (i, 0)), vec, vec,
                      pl.BlockSpec(memory_space=pl.ANY)],
            out_specs=out_specs,
            scratch_shapes=[pltpu.VMEM((2, 2, tm * TOKEN_TILE_ROWS, LANES), F32),
                            pltpu.SemaphoreType.DMA((2,))],
        ),
        compiler_params=_params("arbitrary"),
        name="final_norm",
    )(dest, x1, mod_l, gates_t, ln_g, ln_b, y_slots)


def _rope_tables(seq, dim, width):
    nf = dim // 4
    t = jnp.arange(seq)
    pos = jnp.stack([t // GRID_W, t % GRID_W], axis=-1).astype(F32)
    inv = ROPE_BASE ** (-jnp.arange(nf, dtype=F32) / nf)
    ang = pos[:, :, None] * inv
    cos, sin = jnp.cos(ang), jnp.sin(ang)
    zero = jnp.zeros_like(sin)
    c = jnp.stack([cos, cos], axis=2).reshape(seq, dim)
    s_lo = jnp.stack([-sin, zero], axis=2).reshape(seq, dim)
    s_hi = jnp.stack([zero, sin], axis=2).reshape(seq, dim)
    rep = width // dim
    return tuple(jnp.tile(a, (1, rep)) for a in (c, s_lo, s_hi))


def _dispatch_plan(ids, rank, counts, n_tok):
    n_assign = 2 * n_tok
    flat_e = ids.reshape(n_assign)
    onehot = (flat_e[:, None] == jnp.arange(N_EXP, dtype=jnp.int32)[None, :]).astype(jnp.int32)
    counts = counts.reshape(N_EXP).astype(jnp.int32)
    padded = (counts + MOE_ROWS - 1) // MOE_ROWS * MOE_ROWS
    pad_end = jnp.cumsum(padded)
    pad_start = pad_end - padded
    dest = (jnp.sum(onehot * pad_start[None, :], axis=1) + rank.reshape(n_assign)).astype(jnp.int32)
    n_blk = n_assign // MOE_ROWS + N_EXP
    blk_start = jnp.arange(n_blk, dtype=jnp.int32) * MOE_ROWS
    blk_exp = jnp.sum((blk_start[:, None] >= pad_end[None, :]).astype(jnp.int32), axis=1)
    blk_exp = jnp.minimum(blk_exp, N_EXP - 1).astype(jnp.int32)
    n_used = (pad_end[-1] // MOE_ROWS).astype(jnp.int32).reshape(1)
    n_free = n_blk * MOE_ROWS - n_assign
    gap_end = jnp.cumsum(padded - counts)
    j = jnp.arange(n_free, dtype=jnp.int32)
    seg = jnp.sum((j[:, None] >= gap_end[None, :]).astype(jnp.int32), axis=1)
    seg_first_slot = jnp.concatenate([pad_start + counts, pad_end[-1:]])
    seg_first_j = jnp.concatenate([jnp.zeros((1,), jnp.int32), gap_end])
    seg_hot = (seg[:, None] == jnp.arange(N_EXP + 1, dtype=jnp.int32)[None, :]).astype(jnp.int32)
    free_slots = (jnp.sum(seg_hot * (seg_first_slot - seg_first_j)[None, :], axis=1) + j).astype(jnp.int32)
    return dest, free_slots, blk_exp, n_used


def kernel(x_prompt, x_sample, cache_diff_k, cache_diff_v, cache_win_k, cache_win_v, state_ssd, state_ret,
           c, c_ctx, w_ada, b_ada, w_in, diff_lambda, diff_norm_g, win_sink, conv_w, conv_b,
           ssd_A_log, ssd_dt_bias, ssd_D, ssd_norm_g, ret_decay_logit, ret_norm_g, w_branch, w_out,
           ln_g, ln_b, router_w, router_b, moe_w_gate, moe_w_up, moe_w_down):
    batch, seq, d = x_prompt.shape
    dec_batch, dec_seq, _ = x_sample.shape
    past = cache_diff_k.shape[2]
    n_ctx, n_lat = batch * seq, dec_batch * dec_seq
    n_tok = n_ctx + n_lat
    assert d == D_MODEL and n_ctx % dec_seq == 0 and seq % CHUNK == 0 and dec_seq % CHUNK == 0

    x = jnp.concatenate([x_prompt.reshape(n_ctx, d), x_sample.reshape(n_lat, d)], axis=0)

    n_mod = 1 + dec_batch
    n_mod_pad = -(-n_mod // 8) * 8
    cvec = jnp.concatenate([c_ctx[None, :], c, jnp.zeros((n_mod_pad - n_mod, d), F32)], axis=0)
    mod = _ada(cvec, w_ada, b_ada)

    rope_a = _rope_tables(dec_seq, A_QK, 256)
    rope_b = _rope_tables(dec_seq, B_DIM, 256)
    rw_t = router_w.T
    rb_col = router_b.reshape(N_EXP, 1)

    ctx_out = {k: [] for k in ('diff_k', 'diff_v', 'win_k', 'win_v', 'ssd', 'ret')}
    for l in range(DEPTH):
        mod_l = mod[l, :n_mod].reshape(n_mod, 1, ADA_DIM)
        wl = w_in[l]
        w_small = jnp.concatenate(
            [wl[:, :CDT_OFF], wl[:, CDT_OFF + 8:GATE_OFF], wl[:, CDT_OFF:CDT_OFF + 8],
             jnp.zeros((d, N_SMALL - GATE_OFF), F32)], axis=1).astype(BF16)
        w_gl = wl[:, GATE_OFF:].astype(BF16)
        proj = _inproj(x, mod_l, w_small, n_ctx, dec_seq)

        lam_init = 0.8 - 0.6 * math.exp(-0.3 * l)
        lv = diff_lambda[l]
        lam = jnp.exp(jnp.sum(lv[0] * lv[1])) - jnp.exp(jnp.sum(lv[2] * lv[3])) + lam_init
        diff_scal = jnp.stack([lam, jnp.asarray(1.0 - lam_init, F32)]).astype(F32)
        g_a = diff_norm_g[l].reshape(1, A_V)
        sink = win_sink[l]
        dtb = jnp.zeros((1, 128), F32).at[0, :8].set(ssd_dt_bias[l].reshape(8))
        acoef = jnp.zeros((1, 128), F32).at[0, :8].set(-jnp.exp(ssd_A_log[l]).reshape(8))
        dvec = jnp.repeat(ssd_D[l], C_P).reshape(1, 256)
        g_c = ssd_norm_g[l].reshape(1, 256)
        log_g = jax.nn.log_sigmoid(ret_decay_logit[l]).reshape(8)
        g_d = ret_norm_g[l].reshape(1, 256)
        cw = conv_w[l]
        cb = conv_b[l].reshape(1, 512)

        oa_c = _diff_attn(proj, diff_scal, g_a, 0, batch, seq, seq)
        ob_c = _win_attn(proj, sink, 0, batch, seq, seq)
        yc_c, st_c = _ssd(proj, cw, cb, dtb, acoef, dvec, g_c, 0, batch, seq)
        od_c, rt_c = _ret(proj, log_g, g_d, 0, batch, seq)
        cache_a = (cache_diff_k[:, l].reshape(dec_batch, past, 256), cache_diff_v[:, l].reshape(dec_batch, past, 256))
        cache_b = (cache_win_k[:, l].reshape(dec_batch, past, 128), cache_win_v[:, l].reshape(dec_batch, past, 128))
        oa_l = _diff_attn(proj, diff_scal, g_a, n_ctx, dec_batch, dec_seq, DIFF_TQ, cache=cache_a, rope=rope_a)
        ob_l = _win_attn(proj, sink, n_ctx, dec_batch, dec_seq, BLOCK, cache=cache_b, rope=rope_b)
        yc_l, _ = _ssd(proj, cw, cb, dtb, acoef, dvec, g_c, n_ctx, dec_batch, dec_seq,
                       h0=state_ssd[:, l].reshape(dec_batch, 8, C_P, C_N))
        od_l, _ = _ret(proj, log_g, g_d, n_ctx, dec_batch, dec_seq,
                       h0=state_ret[:, l].reshape(dec_batch, 8, D_V, D_K))

        x1, h2, ids, gates, rank, counts = _merge(
            x, mod_l, ((oa_c, oa_l), (ob_c, ob_l), (yc_c, yc_l), (od_c, od_l)), w_gl, w_branch[l].reshape(N_BRANCH * BRANCH, d).astype(BF16),
            w_out[l].astype(BF16), ln_g[l, 0].reshape(1, d), ln_b[l, 0].reshape(1, d), rw_t, rb_col,
            n_ctx, dec_seq)

        dest, free_slots, blk_exp, n_used = _dispatch_plan(ids, rank, counts, n_tok)
        x_slots = _moe_dispatch(h2, dest, free_slots, blk_exp.shape[0] * MOE_ROWS)
        y_slots = _moe_experts(x_slots, blk_exp, n_used, moe_w_gate, moe_w_up, moe_w_down, l)
        x = _final(x1, mod_l, y_slots, dest, gates.T,
                   ln_g[l, 1].reshape(1, d), ln_b[l, 1].reshape(1, d), n_ctx, dec_seq,
                   split=(l == DEPTH - 1))

        pc = proj[:n_ctx]
        ctx_out['diff_k'].append(pc[:, 256:512].reshape(batch, seq, A_HEADS, 2, A_QK))
        ctx_out['diff_v'].append(pc[:, 512:768].reshape(batch, seq, A_HEADS, A_V))
        ctx_out['win_k'].append(pc[:, 1024:1152].reshape(batch, seq, B_KV, B_DIM))
        ctx_out['win_v'].append(pc[:, 1152:1280].reshape(batch, seq, B_KV, B_DIM))
        ctx_out['ssd'].append(st_c.reshape(batch, 2, C_HEADS, C_P, C_N))
        ctx_out['ret'].append(rt_c.reshape(batch, 2, D_HEADS, D_V, D_K))

    y_prompt = x[0].reshape(batch, seq, d)
    y_sample = x[1].reshape(dec_batch, dec_seq, d)
    stk = lambda k: jnp.stack(ctx_out[k], axis=1)
    return (y_prompt, y_sample, stk('diff_k'), stk('diff_v'), stk('win_k'), stk('win_v'), stk('ssd'), stk('ret'))
```

```python
import functools
import math

import jax
import jax.numpy as jnp
from jax import lax
from jax.experimental import pallas as pl
from jax.experimental.pallas import tpu as pltpu

F32 = jnp.float32
BF16 = jnp.bfloat16

D_MODEL = 1024
DEPTH = 4
GRID_W = 64
BLOCK = 128
WINDOW = 128
CHUNK = 128
A_HEADS, A_QK, A_V = 4, 32, 64
B_HEADS, B_KV, B_DIM = 4, 2, 64
C_HEADS, C_P, C_GROUPS, C_N = 4, 64, 2, 64
D_HEADS, D_K, D_V = 4, 32, 64
BRANCH = 256
N_BRANCH = 4
N_EXP = 16
N_EXP_GROUPS = 4
EXP_PER_GROUP = 4
D_FF_EXP = 512
ROPE_BASE = 10000.0
ALPHA = (2 * DEPTH) ** 0.25
EPS = 1e-5
ADA_DIM = 6 * D_MODEL
NEG = -1e30

N_SMALL = 23 * 128
GATE_OFF = 2824
CDT_OFF = 2048

VMEM_LIMIT = 52 * 1024 * 1024
MOE_ROWS = 512
DENSE_TILE = 512
TOK_TILE = 256
DIFF_TQ = 256

_NN = (((1,), (0,)), ((), ()))
_NT = (((1,), (1,)), ((), ()))
_TN = (((0,), (0,)), ((), ()))


def _params(*sem):
    return pltpu.CompilerParams(dimension_semantics=sem, vmem_limit_bytes=VMEM_LIMIT)


def _mm(a, b, dims=_NN):
    return lax.dot_general(a, b, dims, preferred_element_type=F32)


def _split(a):
    hi = a.astype(BF16)
    return hi, (a - hi.astype(F32)).astype(BF16)


def _mm_f32(a, b, dims=_NN):
    a_hi, a_lo = _split(a)
    b_hi, b_lo = _split(b)
    return (_mm(a_lo, b_hi, dims) + _mm(a_hi, b_lo, dims)) + _mm(a_hi, b_hi, dims)


def _mm_exact_lhs(m_bf, a):
    a1 = a.astype(BF16)
    r1 = a - a1.astype(F32)
    a2 = r1.astype(BF16)
    a3 = (r1 - a2.astype(F32)).astype(BF16)
    return (_mm(m_bf, a3) + _mm(m_bf, a2)) + _mm(m_bf, a1)


def _silu(x):
    return x * jax.nn.sigmoid(x)


LANES = 128
TOKEN_TILE_ROWS = D_MODEL // LANES


def _store_token_tiles(ref, val, lead=(), base=0):
    n = val.shape[0]
    for j in range(TOKEN_TILE_ROWS):
        ref[(*lead, pl.ds(base + j, n, stride=TOKEN_TILE_ROWS), slice(None))] = val[:, j * LANES:(j + 1) * LANES]


def _load_token_tiles(ref, n, lead=()):
    return jnp.concatenate(
        [ref[(*lead, pl.ds(j, n, stride=TOKEN_TILE_ROWS), slice(None))] for j in range(TOKEN_TILE_ROWS)], axis=1)


def _rope(x, c, s_lo, s_hi, shift):
    n = x.shape[1]
    return x * c + pltpu.roll(x, n - shift, 1) * s_lo + pltpu.roll(x, shift, 1) * s_hi


def _ada_kernel(c_ref, w_ref, b_ref, o_ref):
    c = c_ref[...]
    o_ref[...] = _mm_f32(_silu(c), w_ref[...]) + b_ref[...]


def _ada(cvec, w_ada, b_ada):
    rows = cvec.shape[0]
    tn = 1024
    return pl.pallas_call(
        _ada_kernel,
        out_shape=jax.ShapeDtypeStruct((DEPTH, rows, ADA_DIM), F32),
        grid=(DEPTH, ADA_DIM // tn),
        in_specs=[
            pl.BlockSpec((rows, D_MODEL), lambda l, j: (0, 0)),
            pl.BlockSpec((None, D_MODEL, tn), lambda l, j: (l, 0, j)),
            pl.BlockSpec((None, 1, tn), lambda l, j: (l, 0, j)),
        ],
        out_specs=pl.BlockSpec((None, rows, tn), lambda l, j: (l, 0, j)),
        compiler_params=_params("arbitrary", "arbitrary"),
        name="ada",
    )(cvec, w_ada, b_ada.reshape(DEPTH, 1, ADA_DIM))


def _inproj_kernel(x_ref, mod_ref, w_ref, o_ref):
    d = x_ref.shape[1]
    h = x_ref[...] * (1.0 + mod_ref[:, d:2 * d]) + mod_ref[:, 0:d]
    o_ref[...] = _mm(h.astype(BF16), w_ref[...])


def _mod_group(i, tm, n_ctx, dec_seq):
    row = i * tm
    return jnp.where(row < n_ctx, 0, 1 + lax.div(jnp.maximum(row - n_ctx, 0), dec_seq))


def _inproj(x, mod_l, w_small, n_ctx, dec_seq):
    n_tok = x.shape[0]
    tm = DENSE_TILE
    grp = functools.partial(_mod_group, tm=tm, n_ctx=n_ctx, dec_seq=dec_seq)
    return pl.pallas_call(
        _inproj_kernel,
        out_shape=jax.ShapeDtypeStruct((n_tok, N_SMALL), F32),
        grid=(n_tok // tm,),
        in_specs=[
            pl.BlockSpec((tm, D_MODEL), lambda i: (i, 0)),
            pl.BlockSpec((None, 1, ADA_DIM), lambda i: (grp(i), 0, 0)),
            pl.BlockSpec((D_MODEL, N_SMALL), lambda i: (0, 0)),
        ],
        out_specs=pl.BlockSpec((tm, N_SMALL), lambda i: (i, 0)),
        compiler_params=_params("arbitrary"),
        name="inproj",
    )(x, mod_l, w_small)


def _diff_attn_kernel(*refs, latent, tq, seq, past):
    if latent:
        (sc_ref, q_ref, k_ref, v_ref, g_ref, ck_ref, cv_ref, rc_ref, rlo_ref, rhi_ref,
         o_ref, kt_scr, v_scr) = refs
    else:
        sc_ref, q_ref, k_ref, v_ref, g_ref, o_ref, kt_scr, v_scr = refs
    qi = pl.program_id(1)
    shift = A_QK // 4

    @pl.when(qi == 0)
    def _():
        k = k_ref[...]
        if latent:
            k = _rope(k, rc_ref[...], rlo_ref[...], rhi_ref[...], shift)
        kt_scr[:, 0:seq] = k.T.astype(BF16)
        if latent:
            kt_scr[:, seq:seq + past] = ck_ref[...].T.astype(BF16)
        ones = jnp.ones((seq + past, LANES - A_V), BF16)
        for h in range(A_HEADS):
            hs = slice(h * A_V, (h + 1) * A_V)
            v_scr[0:seq, h * LANES:h * LANES + A_V] = v_ref[:, hs].astype(BF16)
            if latent:
                v_scr[seq:seq + past, h * LANES:h * LANES + A_V] = cv_ref[:, hs].astype(BF16)
            v_scr[:, h * LANES + A_V:(h + 1) * LANES] = ones

    q = q_ref[...]
    if latent:
        r = pl.ds(pl.multiple_of(qi * tq, tq), tq)
        q = _rope(q, rc_ref[r, :], rlo_ref[r, :], rhi_ref[r, :], shift)
    qb = q.astype(BF16)
    lam = sc_ref[0]
    post = sc_ref[1]
    c = (A_QK ** -0.5) * math.log2(math.e)
    outs = []
    for h in range(A_HEADS):
        probs = []
        for m in range(2):
            off = (h * 2 + m) * A_QK
            s = _mm(qb[:, off:off + A_QK], kt_scr[off:off + A_QK, :])
            probs.append(jnp.exp2(s * c - jnp.max(s, axis=1, keepdims=True) * c).astype(BF16))
        ov = _mm(jnp.concatenate(probs, axis=0), v_scr[:, h * LANES:(h + 1) * LANES])
        maps = [ov[m * tq:(m + 1) * tq, 0:A_V] * (1.0 / ov[m * tq:(m + 1) * tq, A_V:A_V + 1]) for m in range(2)]
        o = maps[0] - lam * maps[1]
        n = o * lax.rsqrt(jnp.mean(o * o, axis=1, keepdims=True) + EPS)
        outs.append((n * g_ref[...]) * post)
    o_ref[...] = jnp.concatenate(outs, axis=1).astype(BF16)


def _diff_attn(proj, scal, g, row0, nb, seq, tq, cache=None, rope=None):
    latent = cache is not None
    nq = seq // tq
    rb = row0 // seq
    qb0 = row0 // tq
    past = cache[0].shape[1] if latent else 0
    in_specs = [
        pl.BlockSpec(memory_space=pltpu.SMEM),
        pl.BlockSpec((tq, 256), lambda b, i: (qb0 + b * nq + i, 0)),
        pl.BlockSpec((seq, 256), lambda b, i: (rb + b, 1)),
        pl.BlockSpec((seq, 256), lambda b, i: (rb + b, 2)),
        pl.BlockSpec((1, A_V), lambda b, i: (0, 0)),
    ]
    args = [scal, proj, proj, proj, g]
    if latent:
        in_specs += [
            pl.BlockSpec((None, past, 256), lambda b, i: (b, 0, 0)),
            pl.BlockSpec((None, past, 256), lambda b, i: (b, 0, 0)),
        ] + [pl.BlockSpec((seq, 256), lambda b, i: (0, 0))] * 3
        args += [cache[0], cache[1], *rope]
    return pl.pallas_call(
        functools.partial(_diff_attn_kernel, latent=latent, tq=tq, seq=seq, past=past),
        out_shape=jax.ShapeDtypeStruct((nb * seq, BRANCH), BF16),
        grid=(nb, nq),
        in_specs=in_specs,
        out_specs=pl.BlockSpec((tq, BRANCH), lambda b, i: (b * nq + i, 0)),
        scratch_shapes=[pltpu.VMEM((256, seq + past), BF16),
                        pltpu.VMEM((seq + past, A_HEADS * LANES), BF16)],
        compiler_params=_params("arbitrary", "arbitrary"),
        name="diff_attn_lat" if latent else "diff_attn_ctx",
    )(*args)


def _win_attn_kernel(*refs, latent, tq, seq):
    if latent:
        (sink_ref, q_ref, k_ref, v_ref, ck_ref, cv_ref, rc_ref, rlo_ref, rhi_ref,
         o_ref, k_scr, v_scr, ck_scr, cv_scr) = refs
    else:
        sink_ref, q_ref, k_ref, v_ref, o_ref, k_scr, v_scr = refs
    qi = pl.program_id(1)
    nq = seq // tq
    shift = B_DIM // 4
    kvw = B_KV * B_DIM

    @pl.when(qi == 0)
    def _():
        k = k_ref[...]
        if latent:
            k = _rope(k, rc_ref[:, 0:kvw], rlo_ref[:, 0:kvw], rhi_ref[:, 0:kvw], shift)
            ck_scr[...] = ck_ref[...].T.astype(BF16)
            cv_scr[...] = cv_ref[...].astype(BF16)
        kt = k.T.astype(BF16)
        for j in range(nq):
            k_scr[j] = kt[:, j * tq:(j + 1) * tq]
        v_scr[...] = v_ref[...].astype(BF16)

    q = q_ref[...]
    if latent:
        r = pl.ds(pl.multiple_of(qi * tq, tq), tq)
        q = _rope(q, rc_ref[r, :], rlo_ref[r, :], rhi_ref[r, :], shift)
        near = (jnp.maximum(qi - 1, 0), qi, jnp.minimum(qi + 1, nq - 1))
        kl = jnp.concatenate([k_scr[j] for j in near], axis=1)
        vl = jnp.concatenate([v_scr[pl.ds(pl.multiple_of(j * tq, tq), tq), :] for j in near], axis=0)
    qb = q.astype(BF16)
    scale = B_DIM ** -0.5
    ratio = B_HEADS // B_KV
    rows = ratio * tq
    if latent:
        ii = lax.broadcasted_iota(jnp.int32, (rows, 3 * tq), 0) & (tq - 1)
        jj = lax.broadcasted_iota(jnp.int32, (rows, 3 * tq), 1)
        lo = jnp.where(qi > 0, 0, tq)
        hi = jnp.where(qi < nq - 1, 3 * tq, 2 * tq)
        valid = (jnp.abs(jj - tq - ii) <= WINDOW) & (jj >= lo) & (jj < hi)
    else:
        kl = k_scr[0]
        vl = v_scr[...]
    row_id = lax.broadcasted_iota(jnp.int32, (rows, 1), 0)
    head_of_row = (row_id - (row_id & (tq - 1))) // tq if ratio > 2 else (row_id >= tq).astype(jnp.int32)
    outs = []
    for g in range(B_KV):
        gsl = slice(g * B_DIM, (g + 1) * B_DIM)
        heads = range(g * ratio, (g + 1) * ratio)
        qg = jnp.concatenate([qb[:, h * B_DIM:(h + 1) * B_DIM] for h in heads], axis=0)
        snk = jnp.full((rows, 1), sink_ref[heads[0]], F32)
        for n, h in enumerate(heads[1:], start=1):
            snk = jnp.where(head_of_row == n, sink_ref[h], snk)
        s = _mm(qg, kl[gsl, :]) * scale
        if latent:
            s = jnp.where(valid, s, NEG)
            sc = _mm(qg, ck_scr[gsl, :]) * scale
            m = jnp.maximum(jnp.maximum(jnp.max(s, axis=1, keepdims=True),
                                        jnp.max(sc, axis=1, keepdims=True)), snk)
            pc = jnp.exp(sc - m)
        else:
            m = jnp.maximum(jnp.max(s, axis=1, keepdims=True), snk)
        p = jnp.exp(s - m)
        den = jnp.sum(p, axis=1, keepdims=True) + jnp.exp(snk - m)
        if latent:
            den = den + jnp.sum(pc, axis=1, keepdims=True)
        inv = 1.0 / den
        o = _mm((p * inv).astype(BF16), vl[:, gsl])
        if latent:
            o = o + _mm((pc * inv).astype(BF16), cv_scr[:, gsl])
        outs += [o[n * tq:(n + 1) * tq, :] for n in range(ratio)]
    o_ref[...] = jnp.concatenate(outs, axis=1).astype(BF16)


def _win_attn(proj, sink, row0, nb, seq, tq, cache=None, rope=None):
    latent = cache is not None
    nq = seq // tq
    rb = row0 // seq
    qb0 = row0 // tq
    kvw = B_KV * B_DIM
    in_specs = [
        pl.BlockSpec(memory_space=pltpu.SMEM),
        pl.BlockSpec((tq, 256), lambda b, i: (qb0 + b * nq + i, 3)),
        pl.BlockSpec((seq, kvw), lambda b, i: (rb + b, 8)),
        pl.BlockSpec((seq, kvw), lambda b, i: (rb + b, 9)),
    ]
    args = [sink, proj, proj, proj]
    scratch = [pltpu.VMEM((nq, kvw, tq), BF16), pltpu.VMEM((seq, kvw), BF16)]
    if latent:
        past = cache[0].shape[1]
        in_specs += [
            pl.BlockSpec((None, past, kvw), lambda b, i: (b, 0, 0)),
            pl.BlockSpec((None, past, kvw), lambda b, i: (b, 0, 0)),
        ] + [pl.BlockSpec((seq, 256), lambda b, i: (0, 0))] * 3
        args += [cache[0], cache[1], *rope]
        scratch += [pltpu.VMEM((kvw, past), BF16), pltpu.VMEM((past, kvw), BF16)]
    return pl.pallas_call(
        functools.partial(_win_attn_kernel, latent=latent, tq=tq, seq=seq),
        out_shape=jax.ShapeDtypeStruct((nb * seq, BRANCH), BF16),
        grid=(nb, nq),
        in_specs=in_specs,
        out_specs=pl.BlockSpec((tq, BRANCH), lambda b, i: (b * nq + i, 0)),
        scratch_shapes=scratch,
        compiler_params=_params("arbitrary", "arbitrary"),
        name="win_attn_lat" if latent else "win_attn_ctx",
    )(*args)


def _conv_silu(u, w, b):
    n = u.shape[0]
    rows = lax.broadcasted_iota(jnp.int32, u.shape, 0)
    up = jnp.where(rows == 0, 0.0, pltpu.roll(u, 1, 0))
    un = jnp.where(rows == n - 1, 0.0, pltpu.roll(u, n - 1, 0))
    return _silu(up * w[0:1, :] + u * w[1:2, :] + un * w[2:3, :] + b)


def _ssd_kernel(*refs, seq, has_h0):
    if has_h0:
        (cx_ref, cz_ref, cbc_ref, cdt_ref, cw_ref, cb_ref, dtb_ref, ac_ref, dv_ref, g_ref, h0_ref,
         y_ref, st_ref, xs, bcs, dts, ybuf, ybuf_b, hs) = refs
    else:
        (cx_ref, cz_ref, cbc_ref, cdt_ref, cw_ref, cb_ref, dtb_ref, ac_ref, dv_ref, g_ref,
         y_ref, st_ref, xs, bcs, dts, ybuf, ybuf_b, hs) = refs
    nc = seq // CHUNK
    xw = C_HEADS * C_P
    xs[...] = _conv_silu(cx_ref[...], cw_ref[:, 0:xw], cb_ref[:, 0:xw])
    bcs[...] = _conv_silu(cbc_ref[...], cw_ref[:, xw:2 * xw], cb_ref[:, xw:2 * xw])
    z = cdt_ref[...] + dtb_ref[...]
    dts[...] = jnp.maximum(z, 0.0) + jnp.log1p(jnp.exp(-jnp.abs(z)))
    if has_h0:
        hs[...] = h0_ref[...]
    else:
        hs[...] = jnp.zeros(hs.shape, F32)

    ri = lax.broadcasted_iota(jnp.int32, (CHUNK, CHUNK), 0)
    ci = lax.broadcasted_iota(jnp.int32, (CHUNK, CHUNK), 1)
    gw = C_GROUPS * C_N

    def chunk(c, d):
        r = pl.ds(pl.multiple_of(c * CHUNK, CHUNK), CHUNK)
        tri = (ri >= ci) if d == 0 else (ci >= ri)
        dt = dts[r, :]
        cs = _mm_exact_lhs(tri.astype(BF16), dt * ac_ref[...])
        cst = cs.T
        x = xs[r, :]
        bc = bcs[r, :]
        ys = []
        for g in range(C_GROUPS):
            bm = bc[:, g * C_N:(g + 1) * C_N]
            cb = bc[:, gw + g * C_N:gw + (g + 1) * C_N].astype(BF16)
            gram = _mm(cb, bm.astype(BF16), _NT)
            for hh in range(C_HEADS // C_GROUPS):
                h = g * (C_HEADS // C_GROUPS) + hh
                col = d * C_HEADS + h
                xb = (x[:, h * C_P:(h + 1) * C_P] * dt[:, col:col + 1]).astype(BF16)
                cc = jnp.broadcast_to(cs[:, col:col + 1], (CHUNK, CHUNK))
                dec = jnp.exp(jnp.where(tri, cc - cst[col:col + 1, :], NEG))
                tot = cc[CHUNK - 1:CHUNK, :] if d == 0 else cc[0:1, :]
                e_tot = jnp.exp(tot)
                hin = hs[col]
                y = _mm((gram * dec).astype(BF16), xb)
                y = y + _mm(cb, hin.astype(BF16), _NT) * jnp.exp(cc)[:, 0:C_P]
                bd = (bm * jnp.exp(tot - cc)[:, 0:C_N]).astype(BF16)
                hs[col] = hin * e_tot[:, 0:C_N] + _mm(xb, bd, _TN)
                ys.append(y)
        return r, jnp.concatenate(ys, axis=1)

    def scan(t, carry):
        r, y = chunk(t, 0)
        ybuf[r, :] = y
        r, y = chunk(nc - 1 - t, 1)
        ybuf_b[r, :] = y
        return carry

    lax.fori_loop(0, nc, scan, 0)

    def finish(c, carry):
        r = pl.ds(pl.multiple_of(c * CHUNK, CHUNK), CHUNK)
        y = (ybuf[r, :] + ybuf_b[r, :]) + xs[r, :] * dv_ref[...]
        y = y * _silu(cz_ref[r, :])
        gl = xw // C_GROUPS
        parts = []
        for g in range(C_GROUPS):
            seg = y[:, g * gl:(g + 1) * gl]
            parts.append(seg * lax.rsqrt(jnp.mean(seg * seg, axis=1, keepdims=True) + EPS))
        y_ref[r, :] = (jnp.concatenate(parts, axis=1) * g_ref[...]).astype(BF16)
        return carry

    lax.fori_loop(0, nc, finish, 0)
    st_ref[...] = hs[...]


def _ssd(proj, conv_w, conv_b, dtb, acoef, dvec, g, row0, nb, seq, h0=None):
    has_h0 = h0 is not None
    rb = row0 // seq
    nst = 2 * C_HEADS
    in_specs = [
        pl.BlockSpec((seq, 256), lambda b: (rb + b, 5)),
        pl.BlockSpec((seq, 256), lambda b: (rb + b, 6)),
        pl.BlockSpec((seq, 256), lambda b: (rb + b, 7)),
        pl.BlockSpec((seq, 128), lambda b: (rb + b, 22)),
        pl.BlockSpec((3, 512), lambda b: (0, 0)),
        pl.BlockSpec((1, 512), lambda b: (0, 0)),
        pl.BlockSpec((1, 128), lambda b: (0, 0)),
        pl.BlockSpec((1, 128), lambda b: (0, 0)),
        pl.BlockSpec((1, 256), lambda b: (0, 0)),
        pl.BlockSpec((1, 256), lambda b: (0, 0)),
    ]
    args = [proj, proj, proj, proj, conv_w, conv_b, dtb, acoef, dvec, g]
    if has_h0:
        in_specs.append(pl.BlockSpec((None, nst, C_P, C_N), lambda b: (b, 0, 0, 0)))
        args.append(h0)
    return pl.pallas_call(
        functools.partial(_ssd_kernel, seq=seq, has_h0=has_h0),
        out_shape=(jax.ShapeDtypeStruct((nb * seq, BRANCH), BF16),
                   jax.ShapeDtypeStruct((nb, nst, C_P, C_N), F32)),
        grid=(nb,),
        in_specs=in_specs,
        out_specs=(pl.BlockSpec((seq, BRANCH), lambda b: (b, 0)),
                   pl.BlockSpec((None, nst, C_P, C_N), lambda b: (b, 0, 0, 0))),
        scratch_shapes=[pltpu.VMEM((seq, 256), F32), pltpu.VMEM((seq, 256), F32),
                        pltpu.VMEM((seq, 128), F32), pltpu.VMEM((seq, 256), F32),
                        pltpu.VMEM((seq, 256), F32), pltpu.VMEM((nst, C_P, C_N), F32)],
        compiler_params=_params("arbitrary"),
        name="ssd_lat" if has_h0 else "ssd_ctx",
    )(*args)


def _ret_kernel(*refs, seq, has_h0):
    if has_h0:
        (lg_ref, q_ref, k_ref, v_ref, gt_ref, g_ref, h0_ref, y_ref, st_ref,
         ybuf, ybuf_b, hs, dec_scr, ein_scr, eout_scr) = refs
    else:
        (lg_ref, q_ref, k_ref, v_ref, gt_ref, g_ref, y_ref, st_ref,
         ybuf, ybuf_b, hs, dec_scr, ein_scr, eout_scr) = refs
    nc = seq // CHUNK
    if has_h0:
        hs[...] = h0_ref[...]
    else:
        hs[...] = jnp.zeros(hs.shape, F32)
    ri = lax.broadcasted_iota(jnp.int32, (CHUNK, CHUNK), 0)
    ci = lax.broadcasted_iota(jnp.int32, (CHUNK, CHUNK), 1)
    pos_v = lax.broadcasted_iota(jnp.int32, (CHUNK, D_V), 0).astype(F32)
    pos_k = lax.broadcasted_iota(jnp.int32, (CHUNK, D_K), 0).astype(F32)
    kscale = D_K ** -0.5
    for d in range(2):
        if d == 0:
            tri, dist = ri >= ci, (ri - ci).astype(F32)
            steps_in, steps_out = pos_v + 1.0, (CHUNK - 1.0) - pos_k
        else:
            tri, dist = ci >= ri, (ci - ri).astype(F32)
            steps_in, steps_out = CHUNK - pos_v, pos_k
        for h in range(D_HEADS):
            col = d * D_HEADS + h
            lg = lg_ref[col]
            dec_scr[col] = jnp.exp(jnp.where(tri, dist * lg, NEG))
            ein_scr[col] = jnp.exp(steps_in * lg)
            eout_scr[col] = jnp.exp(steps_out * lg)

    def chunk(c, d):
        r = pl.ds(pl.multiple_of(c * CHUNK, CHUNK), CHUNK)
        q = q_ref[r, :]
        k = k_ref[r, :] * kscale
        v = v_ref[r, :]
        ys = []
        for h in range(D_HEADS):
            col = d * D_HEADS + h
            e_in = ein_scr[col]
            e_tot = e_in[CHUNK - 1:CHUNK, :] if d == 0 else e_in[0:1, :]
            qb = q[:, h * D_K:(h + 1) * D_K].astype(BF16)
            km = k[:, h * D_K:(h + 1) * D_K]
            vb = v[:, h * D_V:(h + 1) * D_V].astype(BF16)
            hin = hs[col]
            y = _mm((_mm(qb, km.astype(BF16), _NT) * dec_scr[col]).astype(BF16), vb)
            y = y + _mm(qb, hin.astype(BF16), _NT) * e_in
            bd = (km * eout_scr[col]).astype(BF16)
            hs[col] = hin * e_tot[:, 0:D_K] + _mm(vb, bd, _TN)
            ys.append(y)
        return r, ys

    def scan(t, carry):
        r, ys = chunk(t, 0)
        ybuf[r, :] = jnp.concatenate(ys, axis=1)
        r, ys = chunk(nc - 1 - t, 1)
        ybuf_b[r, :] = jnp.concatenate(ys, axis=1)
        return carry

    lax.fori_loop(0, nc, scan, 0)

    def finish(c, carry):
        r = pl.ds(pl.multiple_of(c * CHUNK, CHUNK), CHUNK)
        yf = ybuf[r, :] + ybuf_b[r, :]
        parts = []
        for h in range(D_HEADS):
            o = yf[:, h * D_V:(h + 1) * D_V]
            o = o - jnp.mean(o, axis=1, keepdims=True)
            parts.append(o * lax.rsqrt(jnp.mean(o * o, axis=1, keepdims=True) + EPS))
        y = (jnp.concatenate(parts, axis=1) * g_ref[...]) * _silu(gt_ref[r, :])
        y_ref[r, :] = y.astype(BF16)
        return carry

    lax.fori_loop(0, nc, finish, 0)
    st_ref[...] = hs[...]


def _ret(proj, log_g, g, row0, nb, seq, h0=None):
    has_h0 = h0 is not None
    rb = row0 // seq
    nst = 2 * D_HEADS
    in_specs = [
        pl.BlockSpec(memory_space=pltpu.SMEM),
        pl.BlockSpec((seq, 128), lambda b: (rb + b, 16)),
        pl.BlockSpec((seq, 128), lambda b: (rb + b, 17)),
        pl.BlockSpec((seq, 256), lambda b: (rb + b, 9)),
        pl.BlockSpec((seq, 256), lambda b: (rb + b, 10)),
        pl.BlockSpec((1, 256), lambda b: (0, 0)),
    ]
    args = [log_g, proj, proj, proj, proj, g]
    if has_h0:
        in_specs.append(pl.BlockSpec((None, nst, D_V, D_K), lambda b: (b, 0, 0, 0)))
        args.append(h0)
    return pl.pallas_call(
        functools.partial(_ret_kernel, seq=seq, has_h0=has_h0),
        out_shape=(jax.ShapeDtypeStruct((nb * seq, BRANCH), BF16),
                   jax.ShapeDtypeStruct((nb, nst, D_V, D_K), F32)),
        grid=(nb,),
        in_specs=in_specs,
        out_specs=(pl.BlockSpec((seq, BRANCH), lambda b: (b, 0)),
                   pl.BlockSpec((None, nst, D_V, D_K), lambda b: (b, 0, 0, 0))),
        scratch_shapes=[pltpu.VMEM((seq, 256), F32), pltpu.VMEM((seq, 256), F32),
                        pltpu.VMEM((nst, D_V, D_K), F32), pltpu.VMEM((nst, CHUNK, CHUNK), F32),
                        pltpu.VMEM((nst, CHUNK, D_V), F32), pltpu.VMEM((nst, CHUNK, D_K), F32)],
        compiler_params=_params("arbitrary"),
        name="ret_lat" if has_h0 else "ret_ctx",
    )(*args)


def _route(sel, s):
    row = lambda a, e: a[e:e + 1, :]
    best = None
    grp = None
    for g in range(N_EXP_GROUPS):
        vals = [row(sel, g * EXP_PER_GROUP + j) for j in range(EXP_PER_GROUP)]
        score = None
        for a in range(EXP_PER_GROUP):
            for b in range(a + 1, EXP_PER_GROUP):
                pair = vals[a] + vals[b]
                score = pair if score is None else jnp.maximum(score, pair)
        if best is None:
            best, grp = score, jnp.zeros(score.shape, jnp.int32)
        else:
            better = score > best
            best = jnp.where(better, score, best)
            grp = jnp.where(better, g, grp)

    def pick(a, j):
        out = row(a, j)
        for g in range(1, N_EXP_GROUPS):
            out = jnp.where(grp == g, row(a, g * EXP_PER_GROUP + j), out)
        return out

    cand = [pick(sel, j) for j in range(EXP_PER_GROUP)]
    aff = [pick(s, j) for j in range(EXP_PER_GROUP)]

    def arg_first_max(vals):
        top, idx = vals[0], jnp.zeros(vals[0].shape, jnp.int32)
        for j in range(1, len(vals)):
            better = vals[j] > top
            top = jnp.where(better, vals[j], top)
            idx = jnp.where(better, j, idx)
        return idx

    def take(vals, idx):
        out = vals[0]
        for j in range(1, len(vals)):
            out = jnp.where(idx == j, vals[j], out)
        return out

    i1 = arg_first_max(cand)
    i2 = arg_first_max([jnp.where(i1 == j, -jnp.inf, cand[j]) for j in range(EXP_PER_GROUP)])
    w1, w2 = take(aff, i1), take(aff, i2)
    tot = w1 + w2
    ids = jnp.concatenate([grp * EXP_PER_GROUP + i1, grp * EXP_PER_GROUP + i2], axis=0)
    gates = jnp.concatenate([w1 / tot, w2 / tot], axis=0)
    return ids, gates


def _merge_kernel(*refs, n_ctx_tiles):
    (x_ref, mod_ref, wgl_ref, wbr_ref, wout_ref, lng_ref, lnb_ref, rw_ref, rb_ref, tri_ref) = refs[:10]
    br_refs = refs[10:10 + 2 * N_BRANCH]
    x1_ref, h2_ref, ids_ref, gates_ref, rank_ref, cnt_ref = refs[10 + 2 * N_BRANCH:]
    d = D_MODEL

    @pl.when(pl.program_id(0) == 0)
    def _():
        cnt_ref[...] = jnp.zeros(cnt_ref.shape, F32)

    is_ctx = pl.program_id(0) < n_ctx_tiles
    sub = tri_ref.shape[0]
    for part in range(x_ref.shape[0] // sub):
        rs = slice(part * sub, (part + 1) * sub)
        x = x_ref[rs, :]
        hb = (x * (1.0 + mod_ref[:, d:2 * d]) + mod_ref[:, 0:d]).astype(BF16)
        merged = None
        for k in range(N_BRANCH):
            gate = jax.nn.sigmoid(_mm(hb, wgl_ref[:, k * d:(k + 1) * d]))
            br_k = jnp.where(is_ctx, br_refs[2 * k][rs, :], br_refs[2 * k + 1][rs, :])
            up = _mm(br_k, wbr_ref[k * BRANCH:(k + 1) * BRANCH, :])
            merged = gate * up if merged is None else merged + gate * up
        mix = _mm(merged.astype(BF16), wout_ref[...])
        y = ALPHA * x + mod_ref[:, 2 * d:3 * d] * mix
        y = y - jnp.mean(y, axis=1, keepdims=True)
        x1 = (y * lax.rsqrt(jnp.mean(y * y, axis=1, keepdims=True) + EPS)) * lng_ref[...] + lnb_ref[...]
        x1_ref[rs, :] = x1
        h2 = x1 * (1.0 + mod_ref[:, 4 * d:5 * d]) + mod_ref[:, 3 * d:4 * d]
        _store_token_tiles(h2_ref, h2, base=part * sub * TOKEN_TILE_ROWS)
        s = jax.nn.sigmoid(_mm_f32(rw_ref[...], h2, _NT))
        ids, gates = _route(s + rb_ref[...], s)
        ids_ref[:, rs] = ids
        gates_ref[:, rs] = gates
        expert = lax.broadcasted_iota(jnp.int32, (N_EXP, sub), 0)
        hot = [(expert == ids[k:k + 1, :]).astype(F32) for k in range(2)]
        both = hot[0] + hot[1]
        incl = _mm(both.astype(BF16), tri_ref[...])
        before = cnt_ref[...] + (incl - both)
        rank_ref[:, rs] = jnp.concatenate(
            [jnp.sum(hk * before, axis=0, keepdims=True) for hk in hot], axis=0).astype(jnp.int32)
        cnt_ref[...] = cnt_ref[...] + incl[:, sub - 1:sub]


def _merge(x, mod_l, branches, w_gl, w_br, w_out, ln_g, ln_b, rw_t, rb, n_ctx, dec_seq):
    n_tok = x.shape[0]
    tm = DENSE_TILE
    nct = n_ctx // tm
    grp = functools.partial(_mod_group, tm=tm, n_ctx=n_ctx, dec_seq=dec_seq)
    full = lambda shape: pl.BlockSpec(shape, lambda i: (0,) * len(shape))
    pos = jnp.arange(tm)
    tri = (pos[:, None] <= pos[None, :]).astype(BF16)
    br_specs = [pl.BlockSpec((tm, BRANCH), lambda i: (jnp.minimum(i, nct - 1), 0)),
                pl.BlockSpec((tm, BRANCH), lambda i: (jnp.maximum(i - nct, 0), 0))] * N_BRANCH
    br_args = [a for pair in branches for a in pair]
    return pl.pallas_call(
        functools.partial(_merge_kernel, n_ctx_tiles=nct),
        out_shape=(jax.ShapeDtypeStruct((n_tok, D_MODEL), F32),
                   jax.ShapeDtypeStruct((n_tok * TOKEN_TILE_ROWS, LANES), F32),
                   jax.ShapeDtypeStruct((2, n_tok), jnp.int32),
                   jax.ShapeDtypeStruct((2, n_tok), F32),
                   jax.ShapeDtypeStruct((2, n_tok), jnp.int32),
                   jax.ShapeDtypeStruct((N_EXP, 1), F32)),
        grid=(n_tok // tm,),
        in_specs=[
            pl.BlockSpec((tm, D_MODEL), lambda i: (i, 0)),
            pl.BlockSpec((None, 1, ADA_DIM), lambda i: (grp(i), 0, 0)),
            full((D_MODEL, N_BRANCH * D_MODEL)),
            full((N_BRANCH * BRANCH, D_MODEL)),
            full((D_MODEL, D_MODEL)),
            full((1, D_MODEL)),
            full((1, D_MODEL)),
            full((N_EXP, D_MODEL)),
            full((N_EXP, 1)),
            full((tm, tm)),
        ] + br_specs,
        out_specs=(pl.BlockSpec((tm, D_MODEL), lambda i: (i, 0)),
                   pl.BlockSpec((tm * TOKEN_TILE_ROWS, LANES), lambda i: (i, 0)),
                   pl.BlockSpec((2, tm), lambda i: (0, i)),
                   pl.BlockSpec((2, tm), lambda i: (0, i)),
                   pl.BlockSpec((2, tm), lambda i: (0, i)),
                   full((N_EXP, 1))),
        compiler_params=_params("arbitrary"),
        name="merge",
    )(x, mod_l, w_gl, w_br, w_out, ln_g, ln_b, rw_t, rb, tri, *br_args)


def _dispatch_kernel(dest_ref, pad_ref, h_hbm, x_hbm, hbuf, zbuf, in_sem, out_sem, pad_sem, *, n_tok):
    tr = TOKEN_TILE_ROWS
    rows = TOK_TILE * tr
    i = pl.program_id(0)
    last = pl.num_programs(0) - 1
    slot = lax.rem(i, 2)
    other = 1 - slot

    def load(tile, buf):
        return pltpu.make_async_copy(h_hbm.at[pl.ds(tile * rows, rows), :], hbuf.at[buf], in_sem.at[buf])

    def slot_copy(r, dst, buf):
        return pltpu.make_async_copy(hbuf.at[buf, pl.ds(r * tr, tr), :],
                                     x_hbm.at[pl.ds(dst * tr, tr), :], out_sem.at[buf])

    def drain(buf):
        for r in range(2 * TOK_TILE):
            slot_copy(0, 0, buf).wait()

    @pl.when(i == 0)
    def _():
        load(0, 0).start()

    load(i, slot).wait()

    @pl.when(i < last)
    def _():
        @pl.when(i >= 1)
        def _():
            drain(other)
        load(i + 1, other).start()

    base = i * TOK_TILE
    for r in range(TOK_TILE):
        slot_copy(r, dest_ref[base + r], slot).start()
        slot_copy(r, dest_ref[n_tok + base + r], slot).start()

    @pl.when(i == last)
    def _():
        zbuf[...] = jnp.zeros(zbuf.shape, F32)

        def zero_copy(dst):
            return pltpu.make_async_copy(zbuf, x_hbm.at[pl.ds(dst * tr, tr), :], pad_sem)

        n_pad = pad_ref.shape[0]

        def fill(j, carry):
            for u in range(PAD_UNROLL):
                zero_copy(pad_ref[j * PAD_UNROLL + u]).start()
            return carry

        lax.fori_loop(0, n_pad // PAD_UNROLL, fill, 0)
        drain(slot)

        @pl.when(i >= 1)
        def _():
            drain(other)

        def unfill(j, carry):
            for u in range(PAD_UNROLL):
                zero_copy(0).wait()
            return carry

        lax.fori_loop(0, n_pad // PAD_UNROLL, unfill, 0)


PAD_UNROLL = 8


def _moe_dispatch(h2_tiles, dest, pad_slots, n_slots):
    tr = TOKEN_TILE_ROWS
    n_tok = h2_tiles.shape[0] // tr
    assert pad_slots.shape[0] % PAD_UNROLL == 0
    any_spec = pl.BlockSpec(memory_space=pl.ANY)
    return pl.pallas_call(
        functools.partial(_dispatch_kernel, n_tok=n_tok),
        out_shape=jax.ShapeDtypeStruct((n_slots * tr, LANES), F32),
        grid_spec=pltpu.PrefetchScalarGridSpec(
            num_scalar_prefetch=2,
            grid=(n_tok // TOK_TILE,),
            in_specs=[any_spec],
            out_specs=any_spec,
            scratch_shapes=[pltpu.VMEM((2, TOK_TILE * tr, LANES), F32), pltpu.VMEM((tr, LANES), F32),
                            pltpu.SemaphoreType.DMA((2,)), pltpu.SemaphoreType.DMA((2,)),
                            pltpu.SemaphoreType.DMA(())],
        ),
        compiler_params=_params("arbitrary"),
        name="moe_dispatch",
    )(dest, pad_slots, h2_tiles)


def _moe_kernel(be_ref, nu_ref, x_ref, wg_ref, wu_ref, wd_ref, o_ref):
    del be_ref
    i = pl.program_id(0)

    @pl.when(i < nu_ref[0])
    def _():
        x = _load_token_tiles(x_ref, MOE_ROWS).astype(BF16)
        act = _silu(_mm(x, wg_ref[...].astype(BF16))) * _mm(x, wu_ref[...].astype(BF16))
        _store_token_tiles(o_ref, _mm(act.astype(BF16), wd_ref[...].astype(BF16)))

    @pl.when(i >= nu_ref[0])
    def _():
        o_ref[...] = jnp.zeros(o_ref.shape, F32)


def _moe_experts(x_tiles, blk_exp, n_used, wg, wu, wd, layer):
    n_blk = blk_exp.shape[0]
    blk = pl.BlockSpec((MOE_ROWS * TOKEN_TILE_ROWS, LANES), lambda i, be, nu: (i, 0))
    return pl.pallas_call(
        _moe_kernel,
        out_shape=jax.ShapeDtypeStruct(x_tiles.shape, F32),
        grid_spec=pltpu.PrefetchScalarGridSpec(
            num_scalar_prefetch=2,
            grid=(n_blk,),
            in_specs=[
                blk,
                pl.BlockSpec((None, None, D_MODEL, D_FF_EXP), lambda i, be, nu: (layer, be[i], 0, 0)),
                pl.BlockSpec((None, None, D_MODEL, D_FF_EXP), lambda i, be, nu: (layer, be[i], 0, 0)),
                pl.BlockSpec((None, None, D_FF_EXP, D_MODEL), lambda i, be, nu: (layer, be[i], 0, 0)),
            ],
            out_specs=blk,
        ),
        compiler_params=_params("arbitrary"),
        name="moe_experts",
    )(blk_exp, n_used, x_tiles, wg, wu, wd)


def _final_kernel(dest_ref, x1_ref, mod_ref, gt_ref, lng_ref, lnb_ref, y_hbm, *rest, n_ctx_tiles, n_tok):
    *o_refs, ybuf, sem = rest
    d = D_MODEL
    n = x1_ref.shape[0]
    tr = TOKEN_TILE_ROWS
    i = pl.program_id(0)
    slot = lax.rem(i, 2)

    def fetch_copy(src, k, r, buf):
        return pltpu.make_async_copy(y_hbm.at[pl.ds(src * tr, tr), :],
                                     ybuf.at[buf, k, pl.ds(r * tr, tr), :], sem.at[buf])

    def start_fetch(tile, buf):
        for k in range(2):
            for r in range(n):
                fetch_copy(dest_ref[k * n_tok + tile * n + r], k, r, buf).start()

    @pl.when(i == 0)
    def _():
        start_fetch(0, 0)

    for k in range(2):
        for r in range(n):
            fetch_copy(0, k, r, slot).wait()

    @pl.when(i + 1 < pl.num_programs(0))
    def _():
        start_fetch(i + 1, 1 - slot)

    gt = gt_ref[...]
    ffn = (_load_token_tiles(ybuf, n, lead=(slot, 0)) * gt[:, 0:1]
           + _load_token_tiles(ybuf, n, lead=(slot, 1)) * gt[:, 1:2])
    y = ALPHA * x1_ref[...] + mod_ref[:, 5 * d:6 * d] * ffn
    y = y - jnp.mean(y, axis=1, keepdims=True)
    out = (y * lax.rsqrt(jnp.mean(y * y, axis=1, keepdims=True) + EPS)) * lng_ref[...] + lnb_ref[...]
    if len(o_refs) == 1:
        o_refs[0][...] = out
    else:
        @pl.when(pl.program_id(0) < n_ctx_tiles)
        def _():
            o_refs[0][...] = out

        @pl.when(pl.program_id(0) >= n_ctx_tiles)
        def _():
            o_refs[1][...] = out


def _final(x1, mod_l, y_slots, dest, gates_t, ln_g, ln_b, n_ctx, dec_seq, split=False):
    n_tok = x1.shape[0]
    tm = TOK_TILE
    nt = n_tok // tm
    nct = n_ctx // tm
    grp = functools.partial(_mod_group, tm=tm, n_ctx=n_ctx, dec_seq=dec_seq)
    tile = pl.BlockSpec((tm, D_MODEL), lambda i, dst: (i, 0))
    vec = pl.BlockSpec((1, D_MODEL), lambda i, dst: (0, 0))
    if split:
        out_shape = (jax.ShapeDtypeStruct((n_ctx, D_MODEL), F32), jax.ShapeDtypeStruct((n_tok - n_ctx, D_MODEL), F32))
        out_specs = (pl.BlockSpec((tm, D_MODEL), lambda i, dst: (jnp.minimum(i, nct - 1), 0)),
                     pl.BlockSpec((tm, D_MODEL), lambda i, dst: (jnp.maximum(i - nct, 0), 0)))
    else:
        out_shape = jax.ShapeDtypeStruct((n_tok, D_MODEL), F32)
        out_specs = tile
    return pl.pallas_call(
        functools.partial(_final_kernel, n_ctx_tiles=nct, n_tok=n_tok),
        out_shape=out_shape,
        grid_spec=pltpu.PrefetchScalarGridSpec(
            num_scalar_prefetch=1,
            grid=(nt,),
            in_specs=[tile, pl.BlockSpec((None, 1, ADA_DIM), lambda i, dst: (grp(i), 0, 0)),
                      pl.BlockSpec((tm, 2), lambda i, dst: (i, 0)), vec, vec,
                      pl.BlockSpec(memory_space=pl.ANY)],
            out_specs=out_specs,
            scratch_shapes=[pltpu.VMEM((2, 2, tm * TOKEN_TILE_ROWS, LANES), F32),
                            pltpu.SemaphoreType.DMA((2,))],
        ),
        compiler_params=_params("arbitrary"),
        name="final_norm",
    )(dest, x1, mod_l, gates_t, ln_g, ln_b, y_slots)


def _rope_tables(seq, dim, width):
    nf = dim // 4
    t = jnp.arange(seq)
    pos = jnp.stack([t // GRID_W, t % GRID_W], axis=-1).astype(F32)
    inv = ROPE_BASE ** (-jnp.arange(nf, dtype=F32) / nf)
    ang = pos[:, :, None] * inv
    cos, sin = jnp.cos(ang), jnp.sin(ang)
    zero = jnp.zeros_like(sin)
    c = jnp.stack([cos, cos], axis=2).reshape(seq, dim)
    s_lo = jnp.stack([-sin, zero], axis=2).reshape(seq, dim)
    s_hi = jnp.stack([zero, sin], axis=2).reshape(seq, dim)
    rep = width // dim
    return tuple(jnp.tile(a, (1, rep)) for a in (c, s_lo, s_hi))


def _dispatch_plan(ids, rank, counts, n_tok):
    n_assign = 2 * n_tok
    flat_e = ids.reshape(n_assign)
    onehot = (flat_e[:, None] == jnp.arange(N_EXP, dtype=jnp.int32)[None, :]).astype(jnp.int32)
    counts = counts.reshape(N_EXP).astype(jnp.int32)
    padded = (counts + MOE_ROWS - 1) // MOE_ROWS * MOE_ROWS
    pad_end = jnp.cumsum(padded)
    pad_start = pad_end - padded
    dest = (jnp.sum(onehot * pad_start[None, :], axis=1) + rank.reshape(n_assign)).astype(jnp.int32)
    n_blk = n_assign // MOE_ROWS + N_EXP
    blk_start = jnp.arange(n_blk, dtype=jnp.int32) * MOE_ROWS
    blk_exp = jnp.sum((blk_start[:, None] >= pad_end[None, :]).astype(jnp.int32), axis=1)
    blk_exp = jnp.minimum(blk_exp, N_EXP - 1).astype(jnp.int32)
    n_used = (pad_end[-1] // MOE_ROWS).astype(jnp.int32).reshape(1)
    n_free = n_blk * MOE_ROWS - n_assign
    gap_end = jnp.cumsum(padded - counts)
    j = jnp.arange(n_free, dtype=jnp.int32)
    seg = jnp.sum((j[:, None] >= gap_end[None, :]).astype(jnp.int32), axis=1)
    seg_first_slot = jnp.concatenate([pad_start + counts, pad_end[-1:]])
    seg_first_j = jnp.concatenate([jnp.zeros((1,), jnp.int32), gap_end])
    seg_hot = (seg[:, None] == jnp.arange(N_EXP + 1, dtype=jnp.int32)[None, :]).astype(jnp.int32)
    free_slots = (jnp.sum(seg_hot * (seg_first_slot - seg_first_j)[None, :], axis=1) + j).astype(jnp.int32)
    return dest, free_slots, blk_exp, n_used


def kernel(x_prompt, x_sample, cache_diff_k, cache_diff_v, cache_win_k, cache_win_v, state_ssd, state_ret,
           c, c_ctx, w_ada, b_ada, w_in, diff_lambda, diff_norm_g, win_sink, conv_w, conv_b,
           ssd_A_log, ssd_dt_bias, ssd_D, ssd_norm_g, ret_decay_logit, ret_norm_g, w_branch, w_out,
           ln_g, ln_b, router_w, router_b, moe_w_gate, moe_w_up, moe_w_down):
    batch, seq, d = x_prompt.shape
    dec_batch, dec_seq, _ = x_sample.shape
    past = cache_diff_k.shape[2]
    n_ctx, n_lat = batch * seq, dec_batch * dec_seq
    n_tok = n_ctx + n_lat
    assert d == D_MODEL and n_ctx % dec_seq == 0 and seq % CHUNK == 0 and dec_seq % CHUNK == 0

    x = jnp.concatenate([x_prompt.reshape(n_ctx, d), x_sample.reshape(n_lat, d)], axis=0)

    n_mod = 1 + dec_batch
    n_mod_pad = -(-n_mod // 8) * 8
    cvec = jnp.concatenate([c_ctx[None, :], c, jnp.zeros((n_mod_pad - n_mod, d), F32)], axis=0)
    mod = _ada(cvec, w_ada, b_ada)

    rope_a = _rope_tables(dec_seq, A_QK, 256)
    rope_b = _rope_tables(dec_seq, B_DIM, 256)
    rw_t = router_w.T
    rb_col = router_b.reshape(N_EXP, 1)

    ctx_out = {k: [] for k in ('diff_k', 'diff_v', 'win_k', 'win_v', 'ssd', 'ret')}
    for l in range(DEPTH):
        mod_l = mod[l, :n_mod].reshape(n_mod, 1, ADA_DIM)
        wl = w_in[l]
        w_small = jnp.concatenate(
            [wl[:, :CDT_OFF], wl[:, CDT_OFF + 8:GATE_OFF], wl[:, CDT_OFF:CDT_OFF + 8],
             jnp.zeros((d, N_SMALL - GATE_OFF), F32)], axis=1).astype(BF16)
        w_gl = wl[:, GATE_OFF:].astype(BF16)
        proj = _inproj(x, mod_l, w_small, n_ctx, dec_seq)

        lam_init = 0.8 - 0.6 * math.exp(-0.3 * l)
        lv = diff_lambda[l]
        lam = jnp.exp(jnp.sum(lv[0] * lv[1])) - jnp.exp(jnp.sum(lv[2] * lv[3])) + lam_init
        diff_scal = jnp.stack([lam, jnp.asarray(1.0 - lam_init, F32)]).astype(F32)
        g_a = diff_norm_g[l].reshape(1, A_V)
        sink = win_sink[l]
        dtb = jnp.zeros((1, 128), F32).at[0, :8].set(ssd_dt_bias[l].reshape(8))
        acoef = jnp.zeros((1, 128), F32).at[0, :8].set(-jnp.exp(ssd_A_log[l]).reshape(8))
        dvec = jnp.repeat(ssd_D[l], C_P).reshape(1, 256)
        g_c = ssd_norm_g[l].reshape(1, 256)
        log_g = jax.nn.log_sigmoid(ret_decay_logit[l]).reshape(8)
        g_d = ret_norm_g[l].reshape(1, 256)
        cw = conv_w[l]
        cb = conv_b[l].reshape(1, 512)

        oa_c = _diff_attn(proj, diff_scal, g_a, 0, batch, seq, seq)
        ob_c = _win_attn(proj, sink, 0, batch, seq, seq)
        yc_c, st_c = _ssd(proj, cw, cb, dtb, acoef, dvec, g_c, 0, batch, seq)
        od_c, rt_c = _ret(proj, log_g, g_d, 0, batch, seq)
        cache_a = (cache_diff_k[:, l].reshape(dec_batch, past, 256), cache_diff_v[:, l].reshape(dec_batch, past, 256))
        cache_b = (cache_win_k[:, l].reshape(dec_batch, past, 128), cache_win_v[:, l].reshape(dec_batch, past, 128))
        oa_l = _diff_attn(proj, diff_scal, g_a, n_ctx, dec_batch, dec_seq, DIFF_TQ, cache=cache_a, rope=rope_a)
        ob_l = _win_attn(proj, sink, n_ctx, dec_batch, dec_seq, BLOCK, cache=cache_b, rope=rope_b)
        yc_l, _ = _ssd(proj, cw, cb, dtb, acoef, dvec, g_c, n_ctx, dec_batch, dec_seq,
                       h0=state_ssd[:, l].reshape(dec_batch, 8, C_P, C_N))
        od_l, _ = _ret(proj, log_g, g_d, n_ctx, dec_batch, dec_seq,
                       h0=state_ret[:, l].reshape(dec_batch, 8, D_V, D_K))

        x1, h2, ids, gates, rank, counts = _merge(
            x, mod_l, ((oa_c, oa_l), (ob_c, ob_l), (yc_c, yc_l), (od_c, od_l)), w_gl, w_branch[l].reshape(N_BRANCH * BRANCH, d).astype(BF16),
            w_out[l].astype(BF16), ln_g[l, 0].reshape(1, d), ln_b[l, 0].reshape(1, d), rw_t, rb_col,
            n_ctx, dec_seq)

        dest, free_slots, blk_exp, n_used = _dispatch_plan(ids, rank, counts, n_tok)
        x_slots = _moe_dispatch(h2, dest, free_slots, blk_exp.shape[0] * MOE_ROWS)
        y_slots = _moe_experts(x_slots, blk_exp, n_used, moe_w_gate, moe_w_up, moe_w_down, l)
        x = _final(x1, mod_l, y_slots, dest, gates.T,
                   ln_g[l, 1].reshape(1, d), ln_b[l, 1].reshape(1, d), n_ctx, dec_seq,
                   split=(l == DEPTH - 1))

        pc = proj[:n_ctx]
        ctx_out['diff_k'].append(pc[:, 256:512].reshape(batch, seq, A_HEADS, 2, A_QK))
        ctx_out['diff_v'].append(pc[:, 512:768].reshape(batch, seq, A_HEADS, A_V))
        ctx_out['win_k'].append(pc[:, 1024:1152].reshape(batch, seq, B_KV, B_DIM))
        ctx_out['win_v'].append(pc[:, 1152:1280].reshape(batch, seq, B_KV, B_DIM))
        ctx_out['ssd'].append(st_c.reshape(batch, 2, C_HEADS, C_P, C_N))
        ctx_out['ret'].append(rt_c.reshape(batch, 2, D_HEADS, D_V, D_K))

    y_prompt = x[0].reshape(batch, seq, d)
    y_sample = x[1].reshape(dec_batch, dec_seq, d)
    stk = lambda k: jnp.stack(ctx_out[k], axis=1)
    return (y_prompt, y_sample, stk('diff_k'), stk('diff_v'), stk('win_k'), stk('win_v'), stk('ssd'), stk('ret'))
```

```python
import functools
import math

import jax
import jax.numpy as jnp
from jax import lax
from jax.experimental import pallas as pl
from jax.experimental.pallas import tpu as pltpu

F32 = jnp.float32
BF16 = jnp.bfloat16

D_MODEL = 1024
DEPTH = 4
GRID_W = 64
BLOCK = 128
WINDOW = 128
CHUNK = 128
A_HEADS, A_QK, A_V = 4, 32, 64
B_HEADS, B_KV, B_DIM = 4, 2, 64
C_HEADS, C_P, C_GROUPS, C_N = 4, 64, 2, 64
D_HEADS, D_K, D_V = 4, 32, 64
BRANCH = 256
N_BRANCH = 4
N_EXP = 16
N_EXP_GROUPS = 4
EXP_PER_GROUP = 4
D_FF_EXP = 512
ROPE_BASE = 10000.0
ALPHA = (2 * DEPTH) ** 0.25
EPS = 1e-5
ADA_DIM = 6 * D_MODEL
NEG = -1e30

N_SMALL = 23 * 128
GATE_OFF = 2824
CDT_OFF = 2048

VMEM_LIMIT = 52 * 1024 * 1024
MOE_ROWS = 512
DENSE_TILE = 512
TOK_TILE = 256
DIFF_TQ = 256

_NN = (((1,), (0,)), ((), ()))
_NT = (((1,), (1,)), ((), ()))
_TN = (((0,), (0,)), ((), ()))


def _params(*sem):
    return pltpu.CompilerParams(dimension_semantics=sem, vmem_limit_bytes=VMEM_LIMIT)


def _mm(a, b, dims=_NN):
    return lax.dot_general(a, b, dims, preferred_element_type=F32)


def _split(a):
    hi = a.astype(BF16)
    return hi, (a - hi.astype(F32)).astype(BF16)


def _mm_f32(a, b, dims=_NN):
    a_hi, a_lo = _split(a)
    b_hi, b_lo = _split(b)
    return (_mm(a_lo, b_hi, dims) + _mm(a_hi, b_lo, dims)) + _mm(a_hi, b_hi, dims)


def _mm_exact_lhs(m_bf, a):
    a1 = a.astype(BF16)
    r1 = a - a1.astype(F32)
    a2 = r1.astype(BF16)
    a3 = (r1 - a2.astype(F32)).astype(BF16)
    return (_mm(m_bf, a3) + _mm(m_bf, a2)) + _mm(m_bf, a1)


def _mm_exact_rhs(a, m_bf):
    a1 = a.astype(BF16)
    r1 = a - a1.astype(F32)
    a2 = r1.astype(BF16)
    a3 = (r1 - a2.astype(F32)).astype(BF16)
    return (_mm(a3, m_bf) + _mm(a2, m_bf)) + _mm(a1, m_bf)


def _silu(x):
    return x * jax.nn.sigmoid(x)


LANES = 128
TOKEN_TILE_ROWS = D_MODEL // LANES


def _store_token_tiles(ref, val, lead=(), base=0):
    n = val.shape[0]
    for j in range(TOKEN_TILE_ROWS):
        ref[(*lead, pl.ds(base + j, n, stride=TOKEN_TILE_ROWS), slice(None))] = val[:, j * LANES:(j + 1) * LANES]


def _load_token_tiles(ref, n, lead=()):
    return jnp.concatenate(
        [ref[(*lead, pl.ds(j, n, stride=TOKEN_TILE_ROWS), slice(None))] for j in range(TOKEN_TILE_ROWS)], axis=1)


def _rope(x, c, s_lo, s_hi, shift):
    n = x.shape[1]
    return x * c + pltpu.roll(x, n - shift, 1) * s_lo + pltpu.roll(x, shift, 1) * s_hi


def _ada_kernel(c_ref, w_ref, b_ref, o_ref):
    c = c_ref[...]
    o_ref[...] = _mm_f32(_silu(c), w_ref[...]) + b_ref[...]


def _ada(cvec, w_ada, b_ada):
    rows = cvec.shape[0]
    tn = 1024
    return pl.pallas_call(
        _ada_kernel,
        out_shape=jax.ShapeDtypeStruct((DEPTH, rows, ADA_DIM), F32),
        grid=(DEPTH, ADA_DIM // tn),
        in_specs=[
            pl.BlockSpec((rows, D_MODEL), lambda l, j: (0, 0)),
            pl.BlockSpec((None, D_MODEL, tn), lambda l, j: (l, 0, j)),
            pl.BlockSpec((None, 1, tn), lambda l, j: (l, 0, j)),
        ],
        out_specs=pl.BlockSpec((None, rows, tn), lambda l, j: (l, 0, j)),
        compiler_params=_params("arbitrary", "arbitrary"),
        name="ada",
    )(cvec, w_ada, b_ada.reshape(DEPTH, 1, ADA_DIM))


def _inproj_kernel(x_ref, mod_ref, w_ref, o_ref):
    d = x_ref.shape[1]
    h = x_ref[...] * (1.0 + mod_ref[:, d:2 * d]) + mod_ref[:, 0:d]
    o_ref[...] = _mm(h.astype(BF16), w_ref[...])


def _mod_group(i, tm, n_ctx, dec_seq):
    row = i * tm
    return jnp.where(row < n_ctx, 0, 1 + lax.div(jnp.maximum(row - n_ctx, 0), dec_seq))


def _inproj(x, mod_l, w_small, n_ctx, dec_seq):
    n_tok = x.shape[0]
    tm = DENSE_TILE
    grp = functools.partial(_mod_group, tm=tm, n_ctx=n_ctx, dec_seq=dec_seq)
    return pl.pallas_call(
        _inproj_kernel,
        out_shape=jax.ShapeDtypeStruct((n_tok, N_SMALL), F32),
        grid=(n_tok // tm,),
        in_specs=[
            pl.BlockSpec((tm, D_MODEL), lambda i: (i, 0)),
            pl.BlockSpec((None, 1, ADA_DIM), lambda i: (grp(i), 0, 0)),
            pl.BlockSpec((D_MODEL, N_SMALL), lambda i: (0, 0)),
        ],
        out_specs=pl.BlockSpec((tm, N_SMALL), lambda i: (i, 0)),
        compiler_params=_params("arbitrary"),
        name="inproj",
    )(x, mod_l, w_small)


def _diff_attn_kernel(*refs, latent, tq, seq, past):
    if latent:
        (sc_ref, q_ref, k_ref, v_ref, g_ref, ck_ref, cv_ref, rc_ref, rlo_ref, rhi_ref,
         o_ref, kt_scr, v_scr) = refs
    else:
        sc_ref, q_ref, k_ref, v_ref, g_ref, o_ref, kt_scr, v_scr = refs
    qi = pl.program_id(1)
    shift = A_QK // 4

    @pl.when(qi == 0)
    def _():
        k = k_ref[...]
        if latent:
            k = _rope(k, rc_ref[...], rlo_ref[...], rhi_ref[...], shift)
        kt_scr[:, 0:seq] = k.T.astype(BF16)
        if latent:
            kt_scr[:, seq:seq + past] = ck_ref[...].T.astype(BF16)
        ones = jnp.ones((seq + past, LANES - A_V), BF16)
        for h in range(A_HEADS):
            hs = slice(h * A_V, (h + 1) * A_V)
            v_scr[0:seq, h * LANES:h * LANES + A_V] = v_ref[:, hs].astype(BF16)
            if latent:
                v_scr[seq:seq + past, h * LANES:h * LANES + A_V] = cv_ref[:, hs].astype(BF16)
            v_scr[:, h * LANES + A_V:(h + 1) * LANES] = ones

    q = q_ref[...]
    if latent:
        r = pl.ds(pl.multiple_of(qi * tq, tq), tq)
        q = _rope(q, rc_ref[r, :], rlo_ref[r, :], rhi_ref[r, :], shift)
    qb = q.astype(BF16)
    lam = sc_ref[0]
    post = sc_ref[1]
    c = (A_QK ** -0.5) * math.log2(math.e)
    outs = []
    for h in range(A_HEADS):
        probs = []
        for m in range(2):
            off = (h * 2 + m) * A_QK
            s = _mm(qb[:, off:off + A_QK], kt_scr[off:off + A_QK, :])
            probs.append(jnp.exp2(s * c - jnp.max(s, axis=1, keepdims=True) * c).astype(BF16))
        ov = _mm(jnp.concatenate(probs, axis=0), v_scr[:, h * LANES:(h + 1) * LANES])
        maps = [ov[m * tq:(m + 1) * tq, 0:A_V] * (1.0 / ov[m * tq:(m + 1) * tq, A_V:A_V + 1]) for m in range(2)]
        o = maps[0] - lam * maps[1]
        n = o * lax.rsqrt(jnp.mean(o * o, axis=1, keepdims=True) + EPS)
        outs.append((n * g_ref[...]) * post)
    o_ref[...] = jnp.concatenate(outs, axis=1).astype(BF16)


def _diff_attn(proj, scal, g, row0, nb, seq, tq, cache=None, rope=None):
    latent = cache is not None
    nq = seq // tq
    rb = row0 // seq
    qb0 = row0 // tq
    past = cache[0].shape[1] if latent else 0
    in_specs = [
        pl.BlockSpec(memory_space=pltpu.SMEM),
        pl.BlockSpec((tq, 256), lambda b, i: (qb0 + b * nq + i, 0)),
        pl.BlockSpec((seq, 256), lambda b, i: (rb + b, 1)),
        pl.BlockSpec((seq, 256), lambda b, i: (rb + b, 2)),
        pl.BlockSpec((1, A_V), lambda b, i: (0, 0)),
    ]
    args = [scal, proj, proj, proj, g]
    if latent:
        in_specs += [
            pl.BlockSpec((None, past, 256), lambda b, i: (b, 0, 0)),
            pl.BlockSpec((None, past, 256), lambda b, i: (b, 0, 0)),
        ] + [pl.BlockSpec((seq, 256), lambda b, i: (0, 0))] * 3
        args += [cache[0], cache[1], *rope]
    return pl.pallas_call(
        functools.partial(_diff_attn_kernel, latent=latent, tq=tq, seq=seq, past=past),
        out_shape=jax.ShapeDtypeStruct((nb * seq, BRANCH), BF16),
        grid=(nb, nq),
        in_specs=in_specs,
        out_specs=pl.BlockSpec((tq, BRANCH), lambda b, i: (b * nq + i, 0)),
        scratch_shapes=[pltpu.VMEM((256, seq + past), BF16),
                        pltpu.VMEM((seq + past, A_HEADS * LANES), BF16)],
        compiler_params=_params("arbitrary", "arbitrary"),
        name="diff_attn_lat" if latent else "diff_attn_ctx",
    )(*args)


def _win_attn_kernel(*refs, latent, tq, seq):
    if latent:
        (sink_ref, q_ref, k_ref, v_ref, ck_ref, cv_ref, rc_ref, rlo_ref, rhi_ref,
         o_ref, k_scr, v_scr, ck_scr, cv_scr) = refs
    else:
        sink_ref, q_ref, k_ref, v_ref, o_ref, k_scr, v_scr = refs
    qi = pl.program_id(1)
    nq = seq // tq
    shift = B_DIM // 4
    kvw = B_KV * B_DIM

    @pl.when(qi == 0)
    def _():
        k = k_ref[...]
        if latent:
            k = _rope(k, rc_ref[:, 0:kvw], rlo_ref[:, 0:kvw], rhi_ref[:, 0:kvw], shift)
            ck_scr[...] = ck_ref[...].T.astype(BF16)
            cv_scr[...] = cv_ref[...].astype(BF16)
        kt = k.T.astype(BF16)
        for j in range(nq):
            k_scr[j] = kt[:, j * tq:(j + 1) * tq]
        v_scr[...] = v_ref[...].astype(BF16)

    q = q_ref[...]
    if latent:
        r = pl.ds(pl.multiple_of(qi * tq, tq), tq)
        q = _rope(q, rc_ref[r, :], rlo_ref[r, :], rhi_ref[r, :], shift)
        near = (jnp.maximum(qi - 1, 0), qi, jnp.minimum(qi + 1, nq - 1))
        kl = jnp.concatenate([k_scr[j] for j in near], axis=1)
        vl = jnp.concatenate([v_scr[pl.ds(pl.multiple_of(j * tq, tq), tq), :] for j in near], axis=0)
    qb = q.astype(BF16)
    scale = B_DIM ** -0.5
    ratio = B_HEADS // B_KV
    rows = ratio * tq
    if latent:
        ii = lax.broadcasted_iota(jnp.int32, (rows, 3 * tq), 0) & (tq - 1)
        jj = lax.broadcasted_iota(jnp.int32, (rows, 3 * tq), 1)
        lo = jnp.where(qi > 0, 0, tq)
        hi = jnp.where(qi < nq - 1, 3 * tq, 2 * tq)
        valid = (jnp.abs(jj - tq - ii) <= WINDOW) & (jj >= lo) & (jj < hi)
    else:
        kl = k_scr[0]
        vl = v_scr[...]
    row_id = lax.broadcasted_iota(jnp.int32, (rows, 1), 0)
    head_of_row = (row_id - (row_id & (tq - 1))) // tq if ratio > 2 else (row_id >= tq).astype(jnp.int32)
    outs = []
    for g in range(B_KV):
        gsl = slice(g * B_DIM, (g + 1) * B_DIM)
        heads = range(g * ratio, (g + 1) * ratio)
        qg = jnp.concatenate([qb[:, h * B_DIM:(h + 1) * B_DIM] for h in heads], axis=0)
        snk = jnp.full((rows, 1), sink_ref[heads[0]], F32)
        for n, h in enumerate(heads[1:], start=1):
            snk = jnp.where(head_of_row == n, sink_ref[h], snk)
        s = _mm(qg, kl[gsl, :]) * scale
        if latent:
            s = jnp.where(valid, s, NEG)
            sc = _mm(qg, ck_scr[gsl, :]) * scale
            m = jnp.maximum(jnp.maximum(jnp.max(s, axis=1, keepdims=True),
                                        jnp.max(sc, axis=1, keepdims=True)), snk)
            pc = jnp.exp(sc - m)
        else:
            m = jnp.maximum(jnp.max(s, axis=1, keepdims=True), snk)
        p = jnp.exp(s - m)
        den = jnp.sum(p, axis=1, keepdims=True) + jnp.exp(snk - m)
        if latent:
            den = den + jnp.sum(pc, axis=1, keepdims=True)
        inv = 1.0 / den
        o = _mm((p * inv).astype(BF16), vl[:, gsl])
        if latent:
            o = o + _mm((pc * inv).astype(BF16), cv_scr[:, gsl])
        outs += [o[n * tq:(n + 1) * tq, :] for n in range(ratio)]
    o_ref[...] = jnp.concatenate(outs, axis=1).astype(BF16)


def _win_attn(proj, sink, row0, nb, seq, tq, cache=None, rope=None):
    latent = cache is not None
    nq = seq // tq
    rb = row0 // seq
    qb0 = row0 // tq
    kvw = B_KV * B_DIM
    in_specs = [
        pl.BlockSpec(memory_space=pltpu.SMEM),
        pl.BlockSpec((tq, 256), lambda b, i: (qb0 + b * nq + i, 3)),
        pl.BlockSpec((seq, kvw), lambda b, i: (rb + b, 8)),
        pl.BlockSpec((seq, kvw), lambda b, i: (rb + b, 9)),
    ]
    args = [sink, proj, proj, proj]
    scratch = [pltpu.VMEM((nq, kvw, tq), BF16), pltpu.VMEM((seq, kvw), BF16)]
    if latent:
        past = cache[0].shape[1]
        in_specs += [
            pl.BlockSpec((None, past, kvw), lambda b, i: (b, 0, 0)),
            pl.BlockSpec((None, past, kvw), lambda b, i: (b, 0, 0)),
        ] + [pl.BlockSpec((seq, 256), lambda b, i: (0, 0))] * 3
        args += [cache[0], cache[1], *rope]
        scratch += [pltpu.VMEM((kvw, past), BF16), pltpu.VMEM((past, kvw), BF16)]
    return pl.pallas_call(
        functools.partial(_win_attn_kernel, latent=latent, tq=tq, seq=seq),
        out_shape=jax.ShapeDtypeStruct((nb * seq, BRANCH), BF16),
        grid=(nb, nq),
        in_specs=in_specs,
        out_specs=pl.BlockSpec((tq, BRANCH), lambda b, i: (b * nq + i, 0)),
        scratch_shapes=scratch,
        compiler_params=_params("arbitrary", "arbitrary"),
        name="win_attn_lat" if latent else "win_attn_ctx",
    )(*args)


def _conv_silu(u, w, b):
    n = u.shape[0]
    rows = lax.broadcasted_iota(jnp.int32, u.shape, 0)
    up = jnp.where(rows == 0, 0.0, pltpu.roll(u, 1, 0))
    un = jnp.where(rows == n - 1, 0.0, pltpu.roll(u, n - 1, 0))
    return _silu(up * w[0:1, :] + u * w[1:2, :] + un * w[2:3, :] + b)


def _ssd_kernel(*refs, seq, has_h0):
    if has_h0:
        (cx_ref, cz_ref, cbc_ref, cdt_ref, cw_ref, cb_ref, dtb_ref, ac_ref, dv_ref, g_ref, h0_ref,
         y_ref, st_ref, xs, bcs, dts, ybuf, ybuf_b, hs) = refs
    else:
        (cx_ref, cz_ref, cbc_ref, cdt_ref, cw_ref, cb_ref, dtb_ref, ac_ref, dv_ref, g_ref,
         y_ref, st_ref, xs, bcs, dts, ybuf, ybuf_b, hs) = refs
    nc = seq // CHUNK
    xw = C_HEADS * C_P
    xs[...] = _conv_silu(cx_ref[...], cw_ref[:, 0:xw], cb_ref[:, 0:xw])
    bcs[...] = _conv_silu(cbc_ref[...], cw_ref[:, xw:2 * xw], cb_ref[:, xw:2 * xw])
    z = cdt_ref[...] + dtb_ref[...]
    dts[...] = jnp.maximum(z, 0.0) + jnp.log1p(jnp.exp(-jnp.abs(z)))
    if has_h0:
        hs[...] = h0_ref[...]
    else:
        hs[...] = jnp.zeros(hs.shape, F32)

    ri = lax.broadcasted_iota(jnp.int32, (CHUNK, CHUNK), 0)
    ci = lax.broadcasted_iota(jnp.int32, (CHUNK, CHUNK), 1)
    gw = C_GROUPS * C_N

    def chunk(c, d):
        r = pl.ds(pl.multiple_of(c * CHUNK, CHUNK), CHUNK)
        tri = (ri >= ci) if d == 0 else (ci >= ri)
        dt = dts[r, :]
        cs = _mm_exact_lhs(tri.astype(BF16), dt * ac_ref[...])
        cst = cs.T
        x = xs[r, :]
        bc = bcs[r, :]
        ys = []
        for g in range(C_GROUPS):
            bm = bc[:, g * C_N:(g + 1) * C_N]
            cb = bc[:, gw + g * C_N:gw + (g + 1) * C_N].astype(BF16)
            gram = _mm(cb, bm.astype(BF16), _NT)
            for hh in range(C_HEADS // C_GROUPS):
                h = g * (C_HEADS // C_GROUPS) + hh
                col = d * C_HEADS + h
                xb = (x[:, h * C_P:(h + 1) * C_P] * dt[:, col:col + 1]).astype(BF16)
                cc = jnp.broadcast_to(cs[:, col:col + 1], (CHUNK, CHUNK))
                dec = jnp.exp(jnp.where(tri, cc - cst[col:col + 1, :], NEG))
                tot = cc[CHUNK - 1:CHUNK, :] if d == 0 else cc[0:1, :]
                e_tot = jnp.exp(tot)
                hin = hs[col]
                y = _mm((gram * dec).astype(BF16), xb)
                y = y + _mm(cb, hin.astype(BF16), _NT) * jnp.exp(cc)[:, 0:C_P]
                bd = (bm * jnp.exp(tot - cc)[:, 0:C_N]).astype(BF16)
                hs[col] = hin * e_tot[:, 0:C_N] + _mm(xb, bd, _TN)
                ys.append(y)
        return r, jnp.concatenate(ys, axis=1)

    def scan(t, carry):
        r, y = chunk(t, 0)
        ybuf[r, :] = y
        r, y = chunk(nc - 1 - t, 1)
        ybuf_b[r, :] = y
        return carry

    lax.fori_loop(0, nc, scan, 0)

    def finish(c, carry):
        r = pl.ds(pl.multiple_of(c * CHUNK, CHUNK), CHUNK)
        y = (ybuf[r, :] + ybuf_b[r, :]) + xs[r, :] * dv_ref[...]
        y = y * _silu(cz_ref[r, :])
        gl = xw // C_GROUPS
        parts = []
        for g in range(C_GROUPS):
            seg = y[:, g * gl:(g + 1) * gl]
            parts.append(seg * lax.rsqrt(jnp.mean(seg * seg, axis=1, keepdims=True) + EPS))
        y_ref[r, :] = (jnp.concatenate(parts, axis=1) * g_ref[...]).astype(BF16)
        return carry

    lax.fori_loop(0, nc, finish, 0)
    st_ref[...] = hs[...]


def _ssd(proj, conv_w, conv_b, dtb, acoef, dvec, g, row0, nb, seq, h0=None):
    has_h0 = h0 is not None
    rb = row0 // seq
    nst = 2 * C_HEADS
    in_specs = [
        pl.BlockSpec((seq, 256), lambda b: (rb + b, 5)),
        pl.BlockSpec((seq, 256), lambda b: (rb + b, 6)),
        pl.BlockSpec((seq, 256), lambda b: (rb + b, 7)),
        pl.BlockSpec((seq, 128), lambda b: (rb + b, 22)),
        pl.BlockSpec((3, 512), lambda b: (0, 0)),
        pl.BlockSpec((1, 512), lambda b: (0, 0)),
        pl.BlockSpec((1, 128), lambda b: (0, 0)),
        pl.BlockSpec((1, 128), lambda b: (0, 0)),
        pl.BlockSpec((1, 256), lambda b: (0, 0)),
        pl.BlockSpec((1, 256), lambda b: (0, 0)),
    ]
    args = [proj, proj, proj, proj, conv_w, conv_b, dtb, acoef, dvec, g]
    if has_h0:
        in_specs.append(pl.BlockSpec((None, nst, C_P, C_N), lambda b: (b, 0, 0, 0)))
        args.append(h0)
    return pl.pallas_call(
        functools.partial(_ssd_kernel, seq=seq, has_h0=has_h0),
        out_shape=(jax.ShapeDtypeStruct((nb * seq, BRANCH), BF16),
                   jax.ShapeDtypeStruct((nb, nst, C_P, C_N), F32)),
        grid=(nb,),
        in_specs=in_specs,
        out_specs=(pl.BlockSpec((seq, BRANCH), lambda b: (b, 0)),
                   pl.BlockSpec((None, nst, C_P, C_N), lambda b: (b, 0, 0, 0))),
        scratch_shapes=[pltpu.VMEM((seq, 256), F32), pltpu.VMEM((seq, 256), F32),
                        pltpu.VMEM((seq, 128), F32), pltpu.VMEM((seq, 256), F32),
                        pltpu.VMEM((seq, 256), F32), pltpu.VMEM((nst, C_P, C_N), F32)],
        compiler_params=_params("arbitrary"),
        name="ssd_lat" if has_h0 else "ssd_ctx",
    )(*args)


def _ret_kernel(*refs, seq, has_h0):
    if has_h0:
        (lg_ref, q_ref, k_ref, v_ref, gt_ref, g_ref, h0_ref, y_ref, st_ref,
         ybuf, ybuf_b, dec_scr, ein_scr, eout_scr, etot_scr) = refs
    else:
        (lg_ref, q_ref, k_ref, v_ref, gt_ref, g_ref, y_ref, st_ref,
         ybuf, ybuf_b, dec_scr, ein_scr, eout_scr, etot_scr) = refs
    nc = seq // CHUNK
    kw, vw = D_HEADS * D_K, D_HEADS * D_V
    if has_h0:
        st_ref[...] = h0_ref[...]
    else:
        st_ref[...] = jnp.zeros(st_ref.shape, F32)
    ri = lax.broadcasted_iota(jnp.int32, (CHUNK, CHUNK), 0)
    ci = lax.broadcasted_iota(jnp.int32, (CHUNK, CHUNK), 1)
    pos_v = lax.broadcasted_iota(jnp.int32, (CHUNK, D_V), 0).astype(F32)
    pos_k = lax.broadcasted_iota(jnp.int32, (CHUNK, D_K), 0).astype(F32)
    def same_block(shape, row_block, col_block):
        rows = lax.shift_right_logical(lax.broadcasted_iota(jnp.int32, shape, 0), row_block.bit_length() - 1)
        cols = lax.shift_right_logical(lax.broadcasted_iota(jnp.int32, shape, 1), col_block.bit_length() - 1)
        return rows == cols

    k_on_diag = same_block((kw, D_HEADS * CHUNK), D_K, CHUNK)
    v_on_diag = same_block((D_HEADS * CHUNK, vw), CHUNK, D_V)
    on_diag = same_block((kw, vw), D_K, D_V)
    head_avg = jnp.where(same_block((vw, vw), D_V, D_V), 1.0 / D_V, 0.0).astype(BF16)
    kscale = D_K ** -0.5
    for d in range(2):
        if d == 0:
            tri, dist = ri >= ci, (ri - ci).astype(F32)
            steps_in, steps_out = pos_v + 1.0, (CHUNK - 1.0) - pos_k
        else:
            tri, dist = ci >= ri, (ci - ri).astype(F32)
            steps_in, steps_out = CHUNK - pos_v, pos_k
        for h in range(D_HEADS):
            lg = lg_ref[d * D_HEADS + h]
            dec_scr[d, :, h * CHUNK:(h + 1) * CHUNK] = jnp.exp(jnp.where(tri, dist * lg, NEG))
            ein_scr[d, :, h * D_V:(h + 1) * D_V] = jnp.exp(steps_in * lg)
            eout_scr[d, :, h * D_K:(h + 1) * D_K] = jnp.exp(steps_out * lg)
            etot_scr[d, h * D_K:(h + 1) * D_K, :] = jnp.exp(jnp.full((D_K, vw), CHUNK * 1.0, F32) * lg)

    def chunk(c, d):
        r = pl.ds(pl.multiple_of(c * CHUNK, CHUNK), CHUNK)
        qb = q_ref[r, :].astype(BF16)
        k = k_ref[r, :] * kscale
        kb = k.astype(BF16)
        vb = v_ref[r, :].astype(BF16)
        state = st_ref[d]
        zero = jnp.zeros((), BF16)
        k_bd = jnp.where(k_on_diag, jnp.concatenate([kb.T] * D_HEADS, axis=1), zero)
        gram = _mm(qb, k_bd)
        v_bd = jnp.where(v_on_diag, jnp.concatenate([vb] * D_HEADS, axis=0), zero)
        y = _mm((gram * dec_scr[d]).astype(BF16), v_bd) + _mm(qb, state.astype(BF16)) * ein_scr[d]
        cross = _mm((k * eout_scr[d]).astype(BF16), vb, _TN)
        st_ref[d] = state * etot_scr[d] + jnp.where(on_diag, cross, 0.0)
        return r, y

    def scan(t, carry):
        r, y = chunk(t, 0)
        ybuf[r, :] = y
        r, y = chunk(nc - 1 - t, 1)
        ybuf_b[r, :] = y
        return carry

    lax.fori_loop(0, nc, scan, 0)

    def finish(c, carry):
        r = pl.ds(pl.multiple_of(c * CHUNK, CHUNK), CHUNK)
        yf = ybuf[r, :] + ybuf_b[r, :]
        o = yf - _mm_exact_rhs(yf, head_avg)
        y = o * lax.rsqrt(_mm_exact_rhs(o * o, head_avg) + EPS)
        y = (y * g_ref[...]) * _silu(gt_ref[r, :])
        y_ref[r, :] = y.astype(BF16)
        return carry

    lax.fori_loop(0, nc, finish, 0)


def _ret_state_to_blocks(state):
    b = state.shape[0]
    eye = jnp.eye(D_HEADS, dtype=state.dtype)
    return jnp.einsum('bdhpn,hg->bdhngp', state, eye).reshape(b, 2, D_HEADS * D_K, D_HEADS * D_V)


def _ret_blocks_to_state(blocks):
    b = blocks.shape[0]
    eye = jnp.eye(D_HEADS, dtype=blocks.dtype)
    return jnp.einsum('bdhngp,hg->bdhpn', blocks.reshape(b, 2, D_HEADS, D_K, D_HEADS, D_V), eye)


def _ret(proj, log_g, g, row0, nb, seq, h0=None):
    has_h0 = h0 is not None
    rb = row0 // seq
    nst = 2 * D_HEADS
    kw, vw = D_HEADS * D_K, D_HEADS * D_V
    in_specs = [
        pl.BlockSpec(memory_space=pltpu.SMEM),
        pl.BlockSpec((seq, 128), lambda b: (rb + b, 16)),
        pl.BlockSpec((seq, 128), lambda b: (rb + b, 17)),
        pl.BlockSpec((seq, 256), lambda b: (rb + b, 9)),
        pl.BlockSpec((seq, 256), lambda b: (rb + b, 10)),
        pl.BlockSpec((1, 256), lambda b: (0, 0)),
    ]
    args = [log_g, proj, proj, proj, proj, g]
    if has_h0:
        in_specs.append(pl.BlockSpec((None, 2, kw, vw), lambda b: (b, 0, 0, 0)))
        args.append(h0)
    return pl.pallas_call(
        functools.partial(_ret_kernel, seq=seq, has_h0=has_h0),
        out_shape=(jax.ShapeDtypeStruct((nb * seq, BRANCH), BF16),
                   jax.ShapeDtypeStruct((nb, 2, kw, vw), F32)),
        grid=(nb,),
        in_specs=in_specs,
        out_specs=(pl.BlockSpec((seq, BRANCH), lambda b: (b, 0)),
                   pl.BlockSpec((None, 2, kw, vw), lambda b: (b, 0, 0, 0))),
        scratch_shapes=[pltpu.VMEM((seq, 256), F32), pltpu.VMEM((seq, 256), F32),
                        pltpu.VMEM((2, CHUNK, D_HEADS * CHUNK), F32), pltpu.VMEM((2, CHUNK, vw), F32),
                        pltpu.VMEM((2, CHUNK, kw), F32), pltpu.VMEM((2, kw, vw), F32)],
        compiler_params=_params("arbitrary"),
        name="ret_lat" if has_h0 else "ret_ctx",
    )(*args)


def _route(sel, s):
    row = lambda a, e: a[e:e + 1, :]
    best = None
    grp = None
    for g in range(N_EXP_GROUPS):
        vals = [row(sel, g * EXP_PER_GROUP + j) for j in range(EXP_PER_GROUP)]
        score = None
        for a in range(EXP_PER_GROUP):
            for b in range(a + 1, EXP_PER_GROUP):
                pair = vals[a] + vals[b]
                score = pair if score is None else jnp.maximum(score, pair)
        if best is None:
            best, grp = score, jnp.zeros(score.shape, jnp.int32)
        else:
            better = score > best
            best = jnp.where(better, score, best)
            grp = jnp.where(better, g, grp)

    def pick(a, j):
        out = row(a, j)
        for g in range(1, N_EXP_GROUPS):
            out = jnp.where(grp == g, row(a, g * EXP_PER_GROUP + j), out)
        return out

    cand = [pick(sel, j) for j in range(EXP_PER_GROUP)]
    aff = [pick(s, j) for j in range(EXP_PER_GROUP)]

    def arg_first_max(vals):
        top, idx = vals[0], jnp.zeros(vals[0].shape, jnp.int32)
        for j in range(1, len(vals)):
            better = vals[j] > top
            top = jnp.where(better, vals[j], top)
            idx = jnp.where(better, j, idx)
        return idx

    def take(vals, idx):
        out = vals[0]
        for j in range(1, len(vals)):
            out = jnp.where(idx == j, vals[j], out)
        return out

    i1 = arg_first_max(cand)
    i2 = arg_first_max([jnp.where(i1 == j, -jnp.inf, cand[j]) for j in range(EXP_PER_GROUP)])
    w1, w2 = take(aff, i1), take(aff, i2)
    tot = w1 + w2
    ids = jnp.concatenate([grp * EXP_PER_GROUP + i1, grp * EXP_PER_GROUP + i2], axis=0)
    gates = jnp.concatenate([w1 / tot, w2 / tot], axis=0)
    return ids, gates


def _merge_kernel(*refs, n_ctx_tiles):
    (x_ref, mod_ref, wgl_ref, wbr_ref, wout_ref, lng_ref, lnb_ref, rw_ref, rb_ref, tri_ref) = refs[:10]
    br_refs = refs[10:10 + 2 * N_BRANCH]
    x1_ref, h2_ref, ids_ref, gates_ref, rank_ref, cnt_ref = refs[10 + 2 * N_BRANCH:]
    d = D_MODEL

    @pl.when(pl.program_id(0) == 0)
    def _():
        cnt_ref[...] = jnp.zeros(cnt_ref.shape, F32)

    is_ctx = pl.program_id(0) < n_ctx_tiles
    sub = tri_ref.shape[0]
    for part in range(x_ref.shape[0] // sub):
        rs = slice(part * sub, (part + 1) * sub)
        x = x_ref[rs, :]
        hb = (x * (1.0 + mod_ref[:, d:2 * d]) + mod_ref[:, 0:d]).astype(BF16)
        merged = None
        for k in range(N_BRANCH):
            gate = jax.nn.sigmoid(_mm(hb, wgl_ref[:, k * d:(k + 1) * d]))
            br_k = jnp.where(is_ctx, br_refs[2 * k][rs, :], br_refs[2 * k + 1][rs, :])
            up = _mm(br_k, wbr_ref[k * BRANCH:(k + 1) * BRANCH, :])
            merged = gate * up if merged is None else merged + gate * up
        mix = _mm(merged.astype(BF16), wout_ref[...])
        y = ALPHA * x + mod_ref[:, 2 * d:3 * d] * mix
        y = y - jnp.mean(y, axis=1, keepdims=True)
        x1 = (y * lax.rsqrt(jnp.mean(y * y, axis=1, keepdims=True) + EPS)) * lng_ref[...] + lnb_ref[...]
        x1_ref[rs, :] = x1
        h2 = x1 * (1.0 + mod_ref[:, 4 * d:5 * d]) + mod_ref[:, 3 * d:4 * d]
        _store_token_tiles(h2_ref, h2, base=part * sub * TOKEN_TILE_ROWS)
        s = jax.nn.sigmoid(_mm_f32(rw_ref[...], h2, _NT))
        ids, gates = _route(s + rb_ref[...], s)
        ids_ref[:, rs] = ids
        gates_ref[:, rs] = gates
        expert = lax.broadcasted_iota(jnp.int32, (N_EXP, sub), 0)
        hot = [(expert == ids[k:k + 1, :]).astype(F32) for k in range(2)]
        both = hot[0] + hot[1]
        incl = _mm(both.astype(BF16), tri_ref[...])
        before = cnt_ref[...] + (incl - both)
        rank_ref[:, rs] = jnp.concatenate(
            [jnp.sum(hk * before, axis=0, keepdims=True) for hk in hot], axis=0).astype(jnp.int32)
        cnt_ref[...] = cnt_ref[...] + incl[:, sub - 1:sub]


def _merge(x, mod_l, branches, w_gl, w_br, w_out, ln_g, ln_b, rw_t, rb, n_ctx, dec_seq):
    n_tok = x.shape[0]
    tm = DENSE_TILE
    nct = n_ctx // tm
    grp = functools.partial(_mod_group, tm=tm, n_ctx=n_ctx, dec_seq=dec_seq)
    full = lambda shape: pl.BlockSpec(shape, lambda i: (0,) * len(shape))
    pos = jnp.arange(tm)
    tri = (pos[:, None] <= pos[None, :]).astype(BF16)
    br_specs = [pl.BlockSpec((tm, BRANCH), lambda i: (jnp.minimum(i, nct - 1), 0)),
                pl.BlockSpec((tm, BRANCH), lambda i: (jnp.maximum(i - nct, 0), 0))] * N_BRANCH
    br_args = [a for pair in branches for a in pair]
    return pl.pallas_call(
        functools.partial(_merge_kernel, n_ctx_tiles=nct),
        out_shape=(jax.ShapeDtypeStruct((n_tok, D_MODEL), F32),
                   jax.ShapeDtypeStruct((n_tok * TOKEN_TILE_ROWS, LANES), F32),
                   jax.ShapeDtypeStruct((2, n_tok), jnp.int32),
                   jax.ShapeDtypeStruct((2, n_tok), F32),
                   jax.ShapeDtypeStruct((2, n_tok), jnp.int32),
                   jax.ShapeDtypeStruct((N_EXP, 1), F32)),
        grid=(n_tok // tm,),
        in_specs=[
            pl.BlockSpec((tm, D_MODEL), lambda i: (i, 0)),
            pl.BlockSpec((None, 1, ADA_DIM), lambda i: (grp(i), 0, 0)),
            full((D_MODEL, N_BRANCH * D_MODEL)),
            full((N_BRANCH * BRANCH, D_MODEL)),
            full((D_MODEL, D_MODEL)),
            full((1, D_MODEL)),
            full((1, D_MODEL)),
            full((N_EXP, D_MODEL)),
            full((N_EXP, 1)),
            full((tm, tm)),
        ] + br_specs,
        out_specs=(pl.BlockSpec((tm, D_MODEL), lambda i: (i, 0)),
                   pl.BlockSpec((tm * TOKEN_TILE_ROWS, LANES), lambda i: (i, 0)),
                   pl.BlockSpec((2, tm), lambda i: (0, i)),
                   pl.BlockSpec((2, tm), lambda i: (0, i)),
                   pl.BlockSpec((2, tm), lambda i: (0, i)),
                   full((N_EXP, 1))),
        compiler_params=_params("arbitrary"),
        name="merge",
    )(x, mod_l, w_gl, w_br, w_out, ln_g, ln_b, rw_t, rb, tri, *br_args)


def _dispatch_kernel(dest_ref, pad_ref, h_hbm, x_hbm, hbuf, zbuf, in_sem, out_sem, pad_sem, *, n_tok):
    tr = TOKEN_TILE_ROWS
    rows = TOK_TILE * tr
    i = pl.program_id(0)
    last = pl.num_programs(0) - 1
    slot = lax.rem(i, 2)
    other = 1 - slot

    def load(tile, buf):
        return pltpu.make_async_copy(h_hbm.at[pl.ds(tile * rows, rows), :], hbuf.at[buf], in_sem.at[buf])

    def slot_copy(r, dst, buf):
        return pltpu.make_async_copy(hbuf.at[buf, pl.ds(r * tr, tr), :],
                                     x_hbm.at[pl.ds(dst * tr, tr), :], out_sem.at[buf])

    def drain(buf):
        for r in range(2 * TOK_TILE):
            slot_copy(0, 0, buf).wait()

    @pl.when(i == 0)
    def _():
        load(0, 0).start()

    load(i, slot).wait()

    @pl.when(i < last)
    def _():
        @pl.when(i >= 1)
        def _():
            drain(other)
        load(i + 1, other).start()

    base = i * TOK_TILE
    for r in range(TOK_TILE):
        slot_copy(r, dest_ref[base + r], slot).start()
        slot_copy(r, dest_ref[n_tok + base + r], slot).start()

    @pl.when(i == last)
    def _():
        zbuf[...] = jnp.zeros(zbuf.shape, F32)

        def zero_copy(dst):
            return pltpu.make_async_copy(zbuf, x_hbm.at[pl.ds(dst * tr, tr), :], pad_sem)

        n_pad = pad_ref.shape[0]

        def fill(j, carry):
            for u in range(PAD_UNROLL):
                zero_copy(pad_ref[j * PAD_UNROLL + u]).start()
            return carry

        lax.fori_loop(0, n_pad // PAD_UNROLL, fill, 0)
        drain(slot)

        @pl.when(i >= 1)
        def _():
            drain(other)

        def unfill(j, carry):
            for u in range(PAD_UNROLL):
                zero_copy(0).wait()
            return carry

        lax.fori_loop(0, n_pad // PAD_UNROLL, unfill, 0)


PAD_UNROLL = 8


def _moe_dispatch(h2_tiles, dest, pad_slots, n_slots):
    tr = TOKEN_TILE_ROWS
    n_tok = h2_tiles.shape[0] // tr
    assert pad_slots.shape[0] % PAD_UNROLL == 0
    any_spec = pl.BlockSpec(memory_space=pl.ANY)
    return pl.pallas_call(
        functools.partial(_dispatch_kernel, n_tok=n_tok),
        out_shape=jax.ShapeDtypeStruct((n_slots * tr, LANES), F32),
        grid_spec=pltpu.PrefetchScalarGridSpec(
            num_scalar_prefetch=2,
            grid=(n_tok // TOK_TILE,),
            in_specs=[any_spec],
            out_specs=any_spec,
            scratch_shapes=[pltpu.VMEM((2, TOK_TILE * tr, LANES), F32), pltpu.VMEM((tr, LANES), F32),
                            pltpu.SemaphoreType.DMA((2,)), pltpu.SemaphoreType.DMA((2,)),
                            pltpu.SemaphoreType.DMA(())],
        ),
        compiler_params=_params("arbitrary"),
        name="moe_dispatch",
    )(dest, pad_slots, h2_tiles)


def _moe_kernel(be_ref, nu_ref, x_ref, wg_ref, wu_ref, wd_ref, o_ref):
    del be_ref
    i = pl.program_id(0)

    @pl.when(i < nu_ref[0])
    def _():
        x = _load_token_tiles(x_ref, MOE_ROWS).astype(BF16)
        act = _silu(_mm(x, wg_ref[...].astype(BF16))) * _mm(x, wu_ref[...].astype(BF16))
        _store_token_tiles(o_ref, _mm(act.astype(BF16), wd_ref[...].astype(BF16)))

    @pl.when(i >= nu_ref[0])
    def _():
        o_ref[...] = jnp.zeros(o_ref.shape, F32)


def _moe_experts(x_tiles, blk_exp, n_used, wg, wu, wd, layer):
    n_blk = blk_exp.shape[0]
    blk = pl.BlockSpec((MOE_ROWS * TOKEN_TILE_ROWS, LANES), lambda i, be, nu: (i, 0))
    return pl.pallas_call(
        _moe_kernel,
        out_shape=jax.ShapeDtypeStruct(x_tiles.shape, F32),
        grid_spec=pltpu.PrefetchScalarGridSpec(
            num_scalar_prefetch=2,
            grid=(n_blk,),
            in_specs=[
                blk,
                pl.BlockSpec((None, None, D_MODEL, D_FF_EXP), lambda i, be, nu: (layer, be[i], 0, 0)),
                pl.BlockSpec((None, None, D_MODEL, D_FF_EXP), lambda i, be, nu: (layer, be[i], 0, 0)),
                pl.BlockSpec((None, None, D_FF_EXP, D_MODEL), lambda i, be, nu: (layer, be[i], 0, 0)),
            ],
            out_specs=blk,
        ),
        compiler_params=_params("arbitrary"),
        name="moe_experts",
    )(blk_exp, n_used, x_tiles, wg, wu, wd)


def _final_kernel(dest_ref, x1_ref, mod_ref, gt_ref, lng_ref, lnb_ref, y_hbm, *rest, n_ctx_tiles, n_tok):
    *o_refs, ybuf, sem = rest
    d = D_MODEL
    n = x1_ref.shape[0]
    tr = TOKEN_TILE_ROWS
    i = pl.program_id(0)
    slot = lax.rem(i, 2)

    def fetch_copy(src, k, r, buf):
        return pltpu.make_async_copy(y_hbm.at[pl.ds(src * tr, tr), :],
                                     ybuf.at[buf, k, pl.ds(r * tr, tr), :], sem.at[buf])

    def start_fetch(tile, buf):
        for k in range(2):
            for r in range(n):
                fetch_copy(dest_ref[k * n_tok + tile * n + r], k, r, buf).start()

    @pl.when(i == 0)
    def _():
        start_fetch(0, 0)

    for k in range(2):
        for r in range(n):
            fetch_copy(0, k, r, slot).wait()

    @pl.when(i + 1 < pl.num_programs(0))
    def _():
        start_fetch(i + 1, 1 - slot)

    gt = gt_ref[...]
    ffn = (_load_token_tiles(ybuf, n, lead=(slot, 0)) * gt[:, 0:1]
           + _load_token_tiles(ybuf, n, lead=(slot, 1)) * gt[:, 1:2])
    y = ALPHA * x1_ref[...] + mod_ref[:, 5 * d:6 * d] * ffn
    y = y - jnp.mean(y, axis=1, keepdims=True)
    out = (y * lax.rsqrt(jnp.mean(y * y, axis=1, keepdims=True) + EPS)) * lng_ref[...] + lnb_ref[...]
    if len(o_refs) == 1:
        o_refs[0][...] = out
    else:
        @pl.when(pl.program_id(0) < n_ctx_tiles)
        def _():
            o_refs[0][...] = out

        @pl.when(pl.program_id(0) >= n_ctx_tiles)
        def _():
            o_refs[1][...] = out


def _final(x1, mod_l, y_slots, dest, gates_t, ln_g, ln_b, n_ctx, dec_seq, split=False):
    n_tok = x1.shape[0]
    tm = TOK_TILE
    nt = n_tok // tm
    nct = n_ctx // tm
    grp = functools.partial(_mod_group, tm=tm, n_ctx=n_ctx, dec_seq=dec_seq)
    tile = pl.BlockSpec((tm, D_MODEL), lambda i, dst: (i, 0))
    vec = pl.BlockSpec((1, D_MODEL), lambda i, dst: (0, 0))
    if split:
        out_shape = (jax.ShapeDtypeStruct((n_ctx, D_MODEL), F32), jax.ShapeDtypeStruct((n_tok - n_ctx, D_MODEL), F32))
        out_specs = (pl.BlockSpec((tm, D_MODEL), lambda i, dst: (jnp.minimum(i, nct - 1), 0)),
                     pl.BlockSpec((tm, D_MODEL), lambda i, dst: (jnp.maximum(i - nct, 0), 0)))
    else:
        out_shape = jax.ShapeDtypeStruct((n_tok, D_MODEL), F32)
        out_specs = tile
    return pl.pallas_call(
        functools.partial(_final_kernel, n_ctx_tiles=nct, n_tok=n_tok),
        out_shape=out_shape,
        grid_spec=pltpu.PrefetchScalarGridSpec(
            num_scalar_prefetch=1,
            grid=(nt,),
            in_specs=[tile, pl.BlockSpec((None, 1, ADA_DIM), lambda i, dst: (grp(i), 0, 0)),
                      pl.BlockSpec((tm, 2), lambda i, dst: (i, 0)), vec, vec,
                      pl.BlockSpec(memory_space=pl.ANY)],
            out_specs=out_specs,
            scratch_shapes=[pltpu.VMEM((2, 2, tm * TOKEN_TILE_ROWS, LANES), F32),
                            pltpu.SemaphoreType.DMA((2,))],
        ),
        compiler_params=_params("arbitrary"),
        name="final_norm",
    )(dest, x1, mod_l, gates_t, ln_g, ln_b, y_slots)


def _rope_tables(seq, dim, width):
    nf = dim // 4
    t = jnp.arange(seq)
    pos = jnp.stack([t // GRID_W, t % GRID_W], axis=-1).astype(F32)
    inv = ROPE_BASE ** (-jnp.arange(nf, dtype=F32) / nf)
    ang = pos[:, :, None] * inv
    cos, sin = jnp.cos(ang), jnp.sin(ang)
    zero = jnp.zeros_like(sin)
    c = jnp.stack([cos, cos], axis=2).reshape(seq, dim)
    s_lo = jnp.stack([-sin, zero], axis=2).reshape(seq, dim)
    s_hi = jnp.stack([zero, sin], axis=2).reshape(seq, dim)
    rep = width // dim
    return tuple(jnp.tile(a, (1, rep)) for a in (c, s_lo, s_hi))


def _dispatch_plan(ids, rank, counts, n_tok):
    n_assign = 2 * n_tok
    flat_e = ids.reshape(n_assign)
    onehot = (flat_e[:, None] == jnp.arange(N_EXP, dtype=jnp.int32)[None, :]).astype(jnp.int32)
    counts = counts.reshape(N_EXP).astype(jnp.int32)
    padded = (counts + MOE_ROWS - 1) // MOE_ROWS * MOE_ROWS
    pad_end = jnp.cumsum(padded)
    pad_start = pad_end - padded
    dest = (jnp.sum(onehot * pad_start[None, :], axis=1) + rank.reshape(n_assign)).astype(jnp.int32)
    n_blk = n_assign // MOE_ROWS + N_EXP
    blk_start = jnp.arange(n_blk, dtype=jnp.int32) * MOE_ROWS
    blk_exp = jnp.sum((blk_start[:, None] >= pad_end[None, :]).astype(jnp.int32), axis=1)
    blk_exp = jnp.minimum(blk_exp, N_EXP - 1).astype(jnp.int32)
    n_used = (pad_end[-1] // MOE_ROWS).astype(jnp.int32).reshape(1)
    n_free = n_blk * MOE_ROWS - n_assign
    gap_end = jnp.cumsum(padded - counts)
    j = jnp.arange(n_free, dtype=jnp.int32)
    seg = jnp.sum((j[:, None] >= gap_end[None, :]).astype(jnp.int32), axis=1)
    seg_first_slot = jnp.concatenate([pad_start + counts, pad_end[-1:]])
    seg_first_j = jnp.concatenate([jnp.zeros((1,), jnp.int32), gap_end])
    seg_hot = (seg[:, None] == jnp.arange(N_EXP + 1, dtype=jnp.int32)[None, :]).astype(jnp.int32)
    free_slots = (jnp.sum(seg_hot * (seg_first_slot - seg_first_j)[None, :], axis=1) + j).astype(jnp.int32)
    return dest, free_slots, blk_exp, n_used


def kernel(x_prompt, x_sample, cache_diff_k, cache_diff_v, cache_win_k, cache_win_v, state_ssd, state_ret,
           c, c_ctx, w_ada, b_ada, w_in, diff_lambda, diff_norm_g, win_sink, conv_w, conv_b,
           ssd_A_log, ssd_dt_bias, ssd_D, ssd_norm_g, ret_decay_logit, ret_norm_g, w_branch, w_out,
           ln_g, ln_b, router_w, router_b, moe_w_gate, moe_w_up, moe_w_down):
    batch, seq, d = x_prompt.shape
    dec_batch, dec_seq, _ = x_sample.shape
    past = cache_diff_k.shape[2]
    n_ctx, n_lat = batch * seq, dec_batch * dec_seq
    n_tok = n_ctx + n_lat
    assert d == D_MODEL and n_ctx % dec_seq == 0 and seq % CHUNK == 0 and dec_seq % CHUNK == 0

    x = jnp.concatenate([x_prompt.reshape(n_ctx, d), x_sample.reshape(n_lat, d)], axis=0)

    n_mod = 1 + dec_batch
    n_mod_pad = -(-n_mod // 8) * 8
    cvec = jnp.concatenate([c_ctx[None, :], c, jnp.zeros((n_mod_pad - n_mod, d), F32)], axis=0)
    mod = _ada(cvec, w_ada, b_ada)

    rope_a = _rope_tables(dec_seq, A_QK, 256)
    rope_b = _rope_tables(dec_seq, B_DIM, 256)
    rw_t = router_w.T
    rb_col = router_b.reshape(N_EXP, 1)

    ctx_out = {k: [] for k in ('diff_k', 'diff_v', 'win_k', 'win_v', 'ssd', 'ret')}
    for l in range(DEPTH):
        mod_l = mod[l, :n_mod].reshape(n_mod, 1, ADA_DIM)
        wl = w_in[l]
        w_small = jnp.concatenate(
            [wl[:, :CDT_OFF], wl[:, CDT_OFF + 8:GATE_OFF], wl[:, CDT_OFF:CDT_OFF + 8],
             jnp.zeros((d, N_SMALL - GATE_OFF), F32)], axis=1).astype(BF16)
        w_gl = wl[:, GATE_OFF:].astype(BF16)
        proj = _inproj(x, mod_l, w_small, n_ctx, dec_seq)

        lam_init = 0.8 - 0.6 * math.exp(-0.3 * l)
        lv = diff_lambda[l]
        lam = jnp.exp(jnp.sum(lv[0] * lv[1])) - jnp.exp(jnp.sum(lv[2] * lv[3])) + lam_init
        diff_scal = jnp.stack([lam, jnp.asarray(1.0 - lam_init, F32)]).astype(F32)
        g_a = diff_norm_g[l].reshape(1, A_V)
        sink = win_sink[l]
        dtb = jnp.zeros((1, 128), F32).at[0, :8].set(ssd_dt_bias[l].reshape(8))
        acoef = jnp.zeros((1, 128), F32).at[0, :8].set(-jnp.exp(ssd_A_log[l]).reshape(8))
        dvec = jnp.repeat(ssd_D[l], C_P).reshape(1, 256)
        g_c = ssd_norm_g[l].reshape(1, 256)
        log_g = jax.nn.log_sigmoid(ret_decay_logit[l]).reshape(8)
        g_d = ret_norm_g[l].reshape(1, 256)
        cw = conv_w[l]
        cb = conv_b[l].reshape(1, 512)

        oa_c = _diff_attn(proj, diff_scal, g_a, 0, batch, seq, seq)
        ob_c = _win_attn(proj, sink, 0, batch, seq, seq)
        yc_c, st_c = _ssd(proj, cw, cb, dtb, acoef, dvec, g_c, 0, batch, seq)
        od_c, rt_c = _ret(proj, log_g, g_d, 0, batch, seq)
        cache_a = (cache_diff_k[:, l].reshape(dec_batch, past, 256), cache_diff_v[:, l].reshape(dec_batch, past, 256))
        cache_b = (cache_win_k[:, l].reshape(dec_batch, past, 128), cache_win_v[:, l].reshape(dec_batch, past, 128))
        oa_l = _diff_attn(proj, diff_scal, g_a, n_ctx, dec_batch, dec_seq, DIFF_TQ, cache=cache_a, rope=rope_a)
        ob_l = _win_attn(proj, sink, n_ctx, dec_batch, dec_seq, BLOCK, cache=cache_b, rope=rope_b)
        yc_l, _ = _ssd(proj, cw, cb, dtb, acoef, dvec, g_c, n_ctx, dec_batch, dec_seq,
                       h0=state_ssd[:, l].reshape(dec_batch, 8, C_P, C_N))
        od_l, _ = _ret(proj, log_g, g_d, n_ctx, dec_batch, dec_seq,
                       h0=_ret_state_to_blocks(state_ret[:, l]))

        x1, h2, ids, gates, rank, counts = _merge(
            x, mod_l, ((oa_c, oa_l), (ob_c, ob_l), (yc_c, yc_l), (od_c, od_l)), w_gl, w_branch[l].reshape(N_BRANCH * BRANCH, d).astype(BF16),
            w_out[l].astype(BF16), ln_g[l, 0].reshape(1, d), ln_b[l, 0].reshape(1, d), rw_t, rb_col,
            n_ctx, dec_seq)

        dest, free_slots, blk_exp, n_used = _dispatch_plan(ids, rank, counts, n_tok)
        x_slots = _moe_dispatch(h2, dest, free_slots, blk_exp.shape[0] * MOE_ROWS)
        y_slots = _moe_experts(x_slots, blk_exp, n_used, moe_w_gate, moe_w_up, moe_w_down, l)
        x = _final(x1, mod_l, y_slots, dest, gates.T,
                   ln_g[l, 1].reshape(1, d), ln_b[l, 1].reshape(1, d), n_ctx, dec_seq,
                   split=(l == DEPTH - 1))

        pc = proj[:n_ctx]
        ctx_out['diff_k'].append(pc[:, 256:512].reshape(batch, seq, A_HEADS, 2, A_QK))
        ctx_out['diff_v'].append(pc[:, 512:768].reshape(batch, seq, A_HEADS, A_V))
        ctx_out['win_k'].append(pc[:, 1024:1152].reshape(batch, seq, B_KV, B_DIM))
        ctx_out['win_v'].append(pc[:, 1152:1280].reshape(batch, seq, B_KV, B_DIM))
        ctx_out['ssd'].append(st_c.reshape(batch, 2, C_HEADS, C_P, C_N))
        ctx_out['ret'].append(_ret_blocks_to_state(rt_c))

    y_prompt = x[0].reshape(batch, seq, d)
    y_sample = x[1].reshape(dec_batch, dec_seq, d)
    stk = lambda k: jnp.stack(ctx_out[k], axis=1)
    return (y_prompt, y_sample, stk('diff_k'), stk('diff_v'), stk('win_k'), stk('win_v'), stk('ssd'), stk('ret'))
```

```python
import functools
import math

import jax
import jax.numpy as jnp
from jax import lax
from jax.experimental import pallas as pl
from jax.experimental.pallas import tpu as pltpu

F32 = jnp.float32
BF16 = jnp.bfloat16

D_MODEL = 1024
DEPTH = 4
GRID_W = 64
BLOCK = 128
WINDOW = 128
CHUNK = 128
A_HEADS, A_QK, A_V = 4, 32, 64
B_HEADS, B_KV, B_DIM = 4, 2, 64
C_HEADS, C_P, C_GROUPS, C_N = 4, 64, 2, 64
D_HEADS, D_K, D_V = 4, 32, 64
BRANCH = 256
N_BRANCH = 4
N_EXP = 16
N_EXP_GROUPS = 4
EXP_PER_GROUP = 4
D_FF_EXP = 512
ROPE_BASE = 10000.0
ALPHA = (2 * DEPTH) ** 0.25
EPS = 1e-5
ADA_DIM = 6 * D_MODEL
NEG = -1e30

N_SMALL = 23 * 128
GATE_OFF = 2824
CDT_OFF = 2048

VMEM_LIMIT = 52 * 1024 * 1024
MOE_ROWS = 512
DENSE_TILE = 512
TOK_TILE = 256
DIFF_TQ = 256

_NN = (((1,), (0,)), ((), ()))
_NT = (((1,), (1,)), ((), ()))
_TN = (((0,), (0,)), ((), ()))


def _params(*sem):
    return pltpu.CompilerParams(dimension_semantics=sem, vmem_limit_bytes=VMEM_LIMIT)


def _mm(a, b, dims=_NN):
    return lax.dot_general(a, b, dims, preferred_element_type=F32)


def _split(a):
    hi = a.astype(BF16)
    return hi, (a - hi.astype(F32)).astype(BF16)


def _mm_f32(a, b, dims=_NN):
    a_hi, a_lo = _split(a)
    b_hi, b_lo = _split(b)
    return (_mm(a_lo, b_hi, dims) + _mm(a_hi, b_lo, dims)) + _mm(a_hi, b_hi, dims)


def _mm_exact_lhs(m_bf, a):
    a1 = a.astype(BF16)
    r1 = a - a1.astype(F32)
    a2 = r1.astype(BF16)
    a3 = (r1 - a2.astype(F32)).astype(BF16)
    return (_mm(m_bf, a3) + _mm(m_bf, a2)) + _mm(m_bf, a1)


def _mm_exact_rhs(a, m_bf):
    a1 = a.astype(BF16)
    r1 = a - a1.astype(F32)
    a2 = r1.astype(BF16)
    a3 = (r1 - a2.astype(F32)).astype(BF16)
    return (_mm(a3, m_bf) + _mm(a2, m_bf)) + _mm(a1, m_bf)


def _silu(x):
    return x * jax.nn.sigmoid(x)


LANES = 128
TOKEN_TILE_ROWS = D_MODEL // LANES


def _store_token_tiles(ref, val, lead=(), base=0):
    n = val.shape[0]
    for j in range(TOKEN_TILE_ROWS):
        ref[(*lead, pl.ds(base + j, n, stride=TOKEN_TILE_ROWS), slice(None))] = val[:, j * LANES:(j + 1) * LANES]


def _load_token_tiles(ref, n, lead=()):
    return jnp.concatenate(
        [ref[(*lead, pl.ds(j, n, stride=TOKEN_TILE_ROWS), slice(None))] for j in range(TOKEN_TILE_ROWS)], axis=1)


def _rope(x, c, s_lo, s_hi, shift):
    n = x.shape[1]
    return x * c + pltpu.roll(x, n - shift, 1) * s_lo + pltpu.roll(x, shift, 1) * s_hi


def _ada_kernel(c_ref, w_ref, b_ref, o_ref):
    c = c_ref[...]
    o_ref[...] = _mm_f32(_silu(c), w_ref[...]) + b_ref[...]


def _ada(cvec, w_ada, b_ada):
    rows = cvec.shape[0]
    tn = 1024
    return pl.pallas_call(
        _ada_kernel,
        out_shape=jax.ShapeDtypeStruct((DEPTH, rows, ADA_DIM), F32),
        grid=(DEPTH, ADA_DIM // tn),
        in_specs=[
            pl.BlockSpec((rows, D_MODEL), lambda l, j: (0, 0)),
            pl.BlockSpec((None, D_MODEL, tn), lambda l, j: (l, 0, j)),
            pl.BlockSpec((None, 1, tn), lambda l, j: (l, 0, j)),
        ],
        out_specs=pl.BlockSpec((None, rows, tn), lambda l, j: (l, 0, j)),
        compiler_params=_params("arbitrary", "arbitrary"),
        name="ada",
    )(cvec, w_ada, b_ada.reshape(DEPTH, 1, ADA_DIM))


def _inproj_kernel(x_ref, mod_ref, w_ref, o_ref):
    d = x_ref.shape[1]
    h = x_ref[...] * (1.0 + mod_ref[:, d:2 * d]) + mod_ref[:, 0:d]
    o_ref[...] = _mm(h.astype(BF16), w_ref[...])


def _mod_group(i, tm, n_ctx, dec_seq):
    row = i * tm
    return jnp.where(row < n_ctx, 0, 1 + lax.div(jnp.maximum(row - n_ctx, 0), dec_seq))


def _inproj(x, mod_l, w_small, n_ctx, dec_seq):
    n_tok = x.shape[0]
    tm = DENSE_TILE
    grp = functools.partial(_mod_group, tm=tm, n_ctx=n_ctx, dec_seq=dec_seq)
    return pl.pallas_call(
        _inproj_kernel,
        out_shape=jax.ShapeDtypeStruct((n_tok, N_SMALL), F32),
        grid=(n_tok // tm,),
        in_specs=[
            pl.BlockSpec((tm, D_MODEL), lambda i: (i, 0)),
            pl.BlockSpec((None, 1, ADA_DIM), lambda i: (grp(i), 0, 0)),
            pl.BlockSpec((D_MODEL, N_SMALL), lambda i: (0, 0)),
        ],
        out_specs=pl.BlockSpec((tm, N_SMALL), lambda i: (i, 0)),
        compiler_params=_params("arbitrary"),
        name="inproj",
    )(x, mod_l, w_small)


def _diff_attn_kernel(*refs, latent, tq, seq, past):
    if latent:
        (sc_ref, q_ref, k_ref, v_ref, g_ref, ck_ref, cv_ref, rc_ref, rlo_ref, rhi_ref,
         o_ref, kt_scr, v_scr) = refs
    else:
        sc_ref, q_ref, k_ref, v_ref, g_ref, o_ref, kt_scr, v_scr = refs
    qi = pl.program_id(1)
    shift = A_QK // 4

    @pl.when(qi == 0)
    def _():
        k = k_ref[...]
        if latent:
            k = _rope(k, rc_ref[...], rlo_ref[...], rhi_ref[...], shift)
        kt_scr[:, 0:seq] = k.T.astype(BF16)
        if latent:
            kt_scr[:, seq:seq + past] = ck_ref[...].T.astype(BF16)
        ones = jnp.ones((seq + past, LANES - A_V), BF16)
        for h in range(A_HEADS):
            hs = slice(h * A_V, (h + 1) * A_V)
            v_scr[0:seq, h * LANES:h * LANES + A_V] = v_ref[:, hs].astype(BF16)
            if latent:
                v_scr[seq:seq + past, h * LANES:h * LANES + A_V] = cv_ref[:, hs].astype(BF16)
            v_scr[:, h * LANES + A_V:(h + 1) * LANES] = ones

    q = q_ref[...]
    if latent:
        r = pl.ds(pl.multiple_of(qi * tq, tq), tq)
        q = _rope(q, rc_ref[r, :], rlo_ref[r, :], rhi_ref[r, :], shift)
    qb = q.astype(BF16)
    lam = sc_ref[0]
    post = sc_ref[1]
    c = (A_QK ** -0.5) * math.log2(math.e)
    outs = []
    for h in range(A_HEADS):
        probs = []
        for m in range(2):
            off = (h * 2 + m) * A_QK
            s = _mm(qb[:, off:off + A_QK], kt_scr[off:off + A_QK, :])
            probs.append(jnp.exp2(s * c - jnp.max(s, axis=1, keepdims=True) * c).astype(BF16))
        ov = _mm(jnp.concatenate(probs, axis=0), v_scr[:, h * LANES:(h + 1) * LANES])
        maps = [ov[m * tq:(m + 1) * tq, 0:A_V] * (1.0 / ov[m * tq:(m + 1) * tq, A_V:A_V + 1]) for m in range(2)]
        o = maps[0] - lam * maps[1]
        n = o * lax.rsqrt(jnp.mean(o * o, axis=1, keepdims=True) + EPS)
        outs.append((n * g_ref[...]) * post)
    o_ref[...] = jnp.concatenate(outs, axis=1).astype(BF16)


def _diff_attn(proj, scal, g, row0, nb, seq, tq, cache=None, rope=None):
    latent = cache is not None
    nq = seq // tq
    rb = row0 // seq
    qb0 = row0 // tq
    past = cache[0].shape[1] if latent else 0
    in_specs = [
        pl.BlockSpec(memory_space=pltpu.SMEM),
        pl.BlockSpec((tq, 256), lambda b, i: (qb0 + b * nq + i, 0)),
        pl.BlockSpec((seq, 256), lambda b, i: (rb + b, 1)),
        pl.BlockSpec((seq, 256), lambda b, i: (rb + b, 2)),
        pl.BlockSpec((1, A_V), lambda b, i: (0, 0)),
    ]
    args = [scal, proj, proj, proj, g]
    if latent:
        in_specs += [
            pl.BlockSpec((None, past, 256), lambda b, i: (b, 0, 0)),
            pl.BlockSpec((None, past, 256), lambda b, i: (b, 0, 0)),
        ] + [pl.BlockSpec((seq, 256), lambda b, i: (0, 0))] * 3
        args += [cache[0], cache[1], *rope]
    return pl.pallas_call(
        functools.partial(_diff_attn_kernel, latent=latent, tq=tq, seq=seq, past=past),
        out_shape=jax.ShapeDtypeStruct((nb * seq, BRANCH), BF16),
        grid=(nb, nq),
        in_specs=in_specs,
        out_specs=pl.BlockSpec((tq, BRANCH), lambda b, i: (b * nq + i, 0)),
        scratch_shapes=[pltpu.VMEM((256, seq + past), BF16),
                        pltpu.VMEM((seq + past, A_HEADS * LANES), BF16)],
        compiler_params=_params("arbitrary", "arbitrary"),
        name="diff_attn_lat" if latent else "diff_attn_ctx",
    )(*args)


def _win_attn_kernel(*refs, latent, tq, seq):
    if latent:
        (sink_ref, q_ref, k_ref, v_ref, ck_ref, cv_ref, rc_ref, rlo_ref, rhi_ref,
         o_ref, k_scr, v_scr, ck_scr, cv_scr) = refs
    else:
        sink_ref, q_ref, k_ref, v_ref, o_ref, k_scr, v_scr = refs
    qi = pl.program_id(1)
    nq = seq // tq
    shift = B_DIM // 4
    kvw = B_KV * B_DIM

    @pl.when(qi == 0)
    def _():
        k = k_ref[...]
        if latent:
            k = _rope(k, rc_ref[:, 0:kvw], rlo_ref[:, 0:kvw], rhi_ref[:, 0:kvw], shift)
            ck_scr[...] = ck_ref[...].T.astype(BF16)
            cv_scr[...] = cv_ref[...].astype(BF16)
        kt = k.T.astype(BF16)
        for j in range(nq):
            k_scr[j] = kt[:, j * tq:(j + 1) * tq]
        v_scr[...] = v_ref[...].astype(BF16)

    q = q_ref[...]
    if latent:
        r = pl.ds(pl.multiple_of(qi * tq, tq), tq)
        q = _rope(q, rc_ref[r, :], rlo_ref[r, :], rhi_ref[r, :], shift)
        near = (jnp.maximum(qi - 1, 0), qi, jnp.minimum(qi + 1, nq - 1))
        kl = jnp.concatenate([k_scr[j] for j in near], axis=1)
        vl = jnp.concatenate([v_scr[pl.ds(pl.multiple_of(j * tq, tq), tq), :] for j in near], axis=0)
    qb = q.astype(BF16)
    scale = B_DIM ** -0.5
    ratio = B_HEADS // B_KV
    rows = ratio * tq
    if latent:
        ii = lax.broadcasted_iota(jnp.int32, (rows, 3 * tq), 0) & (tq - 1)
        jj = lax.broadcasted_iota(jnp.int32, (rows, 3 * tq), 1)
        lo = jnp.where(qi > 0, 0, tq)
        hi = jnp.where(qi < nq - 1, 3 * tq, 2 * tq)
        valid = (jnp.abs(jj - tq - ii) <= WINDOW) & (jj >= lo) & (jj < hi)
    else:
        kl = k_scr[0]
        vl = v_scr[...]
    row_id = lax.broadcasted_iota(jnp.int32, (rows, 1), 0)
    head_of_row = (row_id - (row_id & (tq - 1))) // tq if ratio > 2 else (row_id >= tq).astype(jnp.int32)
    outs = []
    for g in range(B_KV):
        gsl = slice(g * B_DIM, (g + 1) * B_DIM)
        heads = range(g * ratio, (g + 1) * ratio)
        qg = jnp.concatenate([qb[:, h * B_DIM:(h + 1) * B_DIM] for h in heads], axis=0)
        snk = jnp.full((rows, 1), sink_ref[heads[0]], F32)
        for n, h in enumerate(heads[1:], start=1):
            snk = jnp.where(head_of_row == n, sink_ref[h], snk)
        s = _mm(qg, kl[gsl, :]) * scale
        if latent:
            s = jnp.where(valid, s, NEG)
            sc = _mm(qg, ck_scr[gsl, :]) * scale
            m = jnp.maximum(jnp.maximum(jnp.max(s, axis=1, keepdims=True),
                                        jnp.max(sc, axis=1, keepdims=True)), snk)
            pc = jnp.exp(sc - m)
        else:
            m = jnp.maximum(jnp.max(s, axis=1, keepdims=True), snk)
        p = jnp.exp(s - m)
        den = jnp.sum(p, axis=1, keepdims=True) + jnp.exp(snk - m)
        if latent:
            den = den + jnp.sum(pc, axis=1, keepdims=True)
        inv = 1.0 / den
        o = _mm((p * inv).astype(BF16), vl[:, gsl])
        if latent:
            o = o + _mm((pc * inv).astype(BF16), cv_scr[:, gsl])
        outs += [o[n * tq:(n + 1) * tq, :] for n in range(ratio)]
    o_ref[...] = jnp.concatenate(outs, axis=1).astype(BF16)


def _win_attn(proj, sink, row0, nb, seq, tq, cache=None, rope=None):
    latent = cache is not None
    nq = seq // tq
    rb = row0 // seq
    qb0 = row0 // tq
    kvw = B_KV * B_DIM
    in_specs = [
        pl.BlockSpec(memory_space=pltpu.SMEM),
        pl.BlockSpec((tq, 256), lambda b, i: (qb0 + b * nq + i, 3)),
        pl.BlockSpec((seq, kvw), lambda b, i: (rb + b, 8)),
        pl.BlockSpec((seq, kvw), lambda b, i: (rb + b, 9)),
    ]
    args = [sink, proj, proj, proj]
    scratch = [pltpu.VMEM((nq, kvw, tq), BF16), pltpu.VMEM((seq, kvw), BF16)]
    if latent:
        past = cache[0].shape[1]
        in_specs += [
            pl.BlockSpec((None, past, kvw), lambda b, i: (b, 0, 0)),
            pl.BlockSpec((None, past, kvw), lambda b, i: (b, 0, 0)),
        ] + [pl.BlockSpec((seq, 256), lambda b, i: (0, 0))] * 3
        args += [cache[0], cache[1], *rope]
        scratch += [pltpu.VMEM((kvw, past), BF16), pltpu.VMEM((past, kvw), BF16)]
    return pl.pallas_call(
        functools.partial(_win_attn_kernel, latent=latent, tq=tq, seq=seq),
        out_shape=jax.ShapeDtypeStruct((nb * seq, BRANCH), BF16),
        grid=(nb, nq),
        in_specs=in_specs,
        out_specs=pl.BlockSpec((tq, BRANCH), lambda b, i: (b * nq + i, 0)),
        scratch_shapes=scratch,
        compiler_params=_params("arbitrary", "arbitrary"),
        name="win_attn_lat" if latent else "win_attn_ctx",
    )(*args)


def _conv_silu(u, w, b):
    n = u.shape[0]
    rows = lax.broadcasted_iota(jnp.int32, u.shape, 0)
    up = jnp.where(rows == 0, 0.0, pltpu.roll(u, 1, 0))
    un = jnp.where(rows == n - 1, 0.0, pltpu.roll(u, n - 1, 0))
    return _silu(up * w[0:1, :] + u * w[1:2, :] + un * w[2:3, :] + b)


def _ssd_kernel(*refs, seq, has_h0):
    if has_h0:
        (cx_ref, cz_ref, cbc_ref, cdt_ref, cw_ref, cb_ref, dtb_ref, ac_ref, dv_ref, g_ref, h0_ref,
         y_ref, st_ref, xs, bcs, dts, ybuf, ybuf_b) = refs
    else:
        (cx_ref, cz_ref, cbc_ref, cdt_ref, cw_ref, cb_ref, dtb_ref, ac_ref, dv_ref, g_ref,
         y_ref, st_ref, xs, bcs, dts, ybuf, ybuf_b) = refs
    nc = seq // CHUNK
    xw = C_HEADS * C_P
    xs[...] = _conv_silu(cx_ref[...], cw_ref[:, 0:xw], cb_ref[:, 0:xw])
    bcs[...] = _conv_silu(cbc_ref[...], cw_ref[:, xw:2 * xw], cb_ref[:, xw:2 * xw])
    z = cdt_ref[...] + dtb_ref[...]
    dts[...] = jnp.maximum(z, 0.0) + jnp.log1p(jnp.exp(-jnp.abs(z)))
    if has_h0:
        st_ref[...] = h0_ref[...]
    else:
        st_ref[...] = jnp.zeros(st_ref.shape, F32)

    ri = lax.broadcasted_iota(jnp.int32, (CHUNK, CHUNK), 0)
    ci = lax.broadcasted_iota(jnp.int32, (CHUNK, CHUNK), 1)
    gw = C_GROUPS * C_N
    per_group = C_HEADS // C_GROUPS

    def same_block(shape, row_block, col_block):
        rows = lax.shift_right_logical(lax.broadcasted_iota(jnp.int32, shape, 0), row_block.bit_length() - 1)
        cols = lax.shift_right_logical(lax.broadcasted_iota(jnp.int32, shape, 1), col_block.bit_length() - 1)
        return rows == cols

    b_on_diag = same_block((gw, C_GROUPS * CHUNK), C_N, CHUNK)
    x_on_diag = same_block((C_HEADS * CHUNK, xw), CHUNK, C_P)
    st_on_diag = same_block((C_HEADS * C_N, xw), C_N, C_P)
    zero = jnp.zeros((), BF16)

    def per_head(group_cols):
        w = group_cols.shape[1] // C_GROUPS
        return jnp.concatenate([group_cols[:, (h // per_group) * w:(h // per_group + 1) * w]
                                for h in range(C_HEADS)], axis=1)

    def chunk(c, d):
        r = pl.ds(pl.multiple_of(c * CHUNK, CHUNK), CHUNK)
        tri = (ri >= ci) if d == 0 else (ci >= ri)
        dt = dts[r, :]
        cs = _mm_exact_lhs(tri.astype(BF16), dt * ac_ref[...])
        cst = cs.T
        bc = bcs[r, :]
        decs, e_ins, e_outs, e_tots, dt_cols = [], [], [], [], []
        for h in range(C_HEADS):
            col = d * C_HEADS + h
            cc = jnp.broadcast_to(cs[:, col:col + 1], (CHUNK, CHUNK))
            tot = cc[CHUNK - 1:CHUNK, :] if d == 0 else cc[0:1, :]
            decs.append(jnp.exp(jnp.where(tri, cc - cst[col:col + 1, :], NEG)))
            e_ins.append(jnp.exp(cc)[:, 0:C_P])
            e_outs.append(jnp.exp(tot - cc)[:, 0:C_N])
            e_tots.append(jnp.broadcast_to(jnp.exp(jnp.concatenate([tot] * (xw // CHUNK), axis=1)), (C_N, xw)))
            dt_cols.append(jnp.broadcast_to(dt[:, col:col + 1], (CHUNK, C_P)))
        xb = (xs[r, :] * jnp.concatenate(dt_cols, axis=1)).astype(BF16)
        bmat = bc[:, 0:gw]
        cb = bc[:, gw:2 * gw].astype(BF16)
        b_bd = jnp.where(b_on_diag, jnp.concatenate([bmat.astype(BF16).T] * C_GROUPS, axis=1), zero)
        gram = per_head(_mm(cb, b_bd))
        x_bd = jnp.where(x_on_diag, jnp.concatenate([xb] * C_HEADS, axis=0), zero)
        state = st_ref[d]
        y = _mm((gram * jnp.concatenate(decs, axis=1)).astype(BF16), x_bd)
        y = y + _mm(per_head(cb), state.astype(BF16)) * jnp.concatenate(e_ins, axis=1)
        bd = (per_head(bmat) * jnp.concatenate(e_outs, axis=1)).astype(BF16)
        cross = _mm(bd, xb, _TN)
        st_ref[d] = state * jnp.concatenate(e_tots, axis=0) + jnp.where(st_on_diag, cross, 0.0)
        return r, y

    def scan(t, carry):
        r, y = chunk(t, 0)
        ybuf[r, :] = y
        r, y = chunk(nc - 1 - t, 1)
        ybuf_b[r, :] = y
        return carry

    lax.fori_loop(0, nc, scan, 0)

    def finish(c, carry):
        r = pl.ds(pl.multiple_of(c * CHUNK, CHUNK), CHUNK)
        y = (ybuf[r, :] + ybuf_b[r, :]) + xs[r, :] * dv_ref[...]
        y = y * _silu(cz_ref[r, :])
        gl = xw // C_GROUPS
        parts = []
        for g in range(C_GROUPS):
            seg = y[:, g * gl:(g + 1) * gl]
            parts.append(seg * lax.rsqrt(jnp.mean(seg * seg, axis=1, keepdims=True) + EPS))
        y_ref[r, :] = (jnp.concatenate(parts, axis=1) * g_ref[...]).astype(BF16)
        return carry

    lax.fori_loop(0, nc, finish, 0)


def _ssd(proj, conv_w, conv_b, dtb, acoef, dvec, g, row0, nb, seq, h0=None):
    has_h0 = h0 is not None
    rb = row0 // seq
    st_shape = (2, C_HEADS * C_N, C_HEADS * C_P)
    in_specs = [
        pl.BlockSpec((seq, 256), lambda b: (rb + b, 5)),
        pl.BlockSpec((seq, 256), lambda b: (rb + b, 6)),
        pl.BlockSpec((seq, 256), lambda b: (rb + b, 7)),
        pl.BlockSpec((seq, 128), lambda b: (rb + b, 22)),
        pl.BlockSpec((3, 512), lambda b: (0, 0)),
        pl.BlockSpec((1, 512), lambda b: (0, 0)),
        pl.BlockSpec((1, 128), lambda b: (0, 0)),
        pl.BlockSpec((1, 128), lambda b: (0, 0)),
        pl.BlockSpec((1, 256), lambda b: (0, 0)),
        pl.BlockSpec((1, 256), lambda b: (0, 0)),
    ]
    args = [proj, proj, proj, proj, conv_w, conv_b, dtb, acoef, dvec, g]
    if has_h0:
        in_specs.append(pl.BlockSpec((None,) + st_shape, lambda b: (b, 0, 0, 0)))
        args.append(h0)
    return pl.pallas_call(
        functools.partial(_ssd_kernel, seq=seq, has_h0=has_h0),
        out_shape=(jax.ShapeDtypeStruct((nb * seq, BRANCH), BF16),
                   jax.ShapeDtypeStruct((nb,) + st_shape, F32)),
        grid=(nb,),
        in_specs=in_specs,
        out_specs=(pl.BlockSpec((seq, BRANCH), lambda b: (b, 0)),
                   pl.BlockSpec((None,) + st_shape, lambda b: (b, 0, 0, 0))),
        scratch_shapes=[pltpu.VMEM((seq, 256), F32), pltpu.VMEM((seq, 256), F32),
                        pltpu.VMEM((seq, 128), F32), pltpu.VMEM((seq, 256), F32),
                        pltpu.VMEM((seq, 256), F32)],
        compiler_params=_params("arbitrary"),
        name="ssd_lat" if has_h0 else "ssd_ctx",
    )(*args)


def _ret_kernel(*refs, seq, has_h0):
    if has_h0:
        (lg_ref, q_ref, k_ref, v_ref, gt_ref, g_ref, h0_ref, y_ref, st_ref,
         ybuf, ybuf_b, dec_scr, ein_scr, eout_scr, etot_scr) = refs
    else:
        (lg_ref, q_ref, k_ref, v_ref, gt_ref, g_ref, y_ref, st_ref,
         ybuf, ybuf_b, dec_scr, ein_scr, eout_scr, etot_scr) = refs
    nc = seq // CHUNK
    kw, vw = D_HEADS * D_K, D_HEADS * D_V
    if has_h0:
        st_ref[...] = h0_ref[...]
    else:
        st_ref[...] = jnp.zeros(st_ref.shape, F32)
    ri = lax.broadcasted_iota(jnp.int32, (CHUNK, CHUNK), 0)
    ci = lax.broadcasted_iota(jnp.int32, (CHUNK, CHUNK), 1)
    pos_v = lax.broadcasted_iota(jnp.int32, (CHUNK, D_V), 0).astype(F32)
    pos_k = lax.broadcasted_iota(jnp.int32, (CHUNK, D_K), 0).astype(F32)
    def same_block(shape, row_block, col_block):
        rows = lax.shift_right_logical(lax.broadcasted_iota(jnp.int32, shape, 0), row_block.bit_length() - 1)
        cols = lax.shift_right_logical(lax.broadcasted_iota(jnp.int32, shape, 1), col_block.bit_length() - 1)
        return rows == cols

    k_on_diag = same_block((kw, D_HEADS * CHUNK), D_K, CHUNK)
    v_on_diag = same_block((D_HEADS * CHUNK, vw), CHUNK, D_V)
    on_diag = same_block((kw, vw), D_K, D_V)
    head_avg = jnp.where(same_block((vw, vw), D_V, D_V), 1.0 / D_V, 0.0).astype(BF16)
    kscale = D_K ** -0.5
    for d in range(2):
        if d == 0:
            tri, dist = ri >= ci, (ri - ci).astype(F32)
            steps_in, steps_out = pos_v + 1.0, (CHUNK - 1.0) - pos_k
        else:
            tri, dist = ci >= ri, (ci - ri).astype(F32)
            steps_in, steps_out = CHUNK - pos_v, pos_k
        for h in range(D_HEADS):
            lg = lg_ref[d * D_HEADS + h]
            dec_scr[d, :, h * CHUNK:(h + 1) * CHUNK] = jnp.exp(jnp.where(tri, dist * lg, NEG))
            ein_scr[d, :, h * D_V:(h + 1) * D_V] = jnp.exp(steps_in * lg)
            eout_scr[d, :, h * D_K:(h + 1) * D_K] = jnp.exp(steps_out * lg)
            etot_scr[d, h * D_K:(h + 1) * D_K, :] = jnp.exp(jnp.full((D_K, vw), CHUNK * 1.0, F32) * lg)

    def chunk(c, d):
        r = pl.ds(pl.multiple_of(c * CHUNK, CHUNK), CHUNK)
        qb = q_ref[r, :].astype(BF16)
        k = k_ref[r, :] * kscale
        kb = k.astype(BF16)
        vb = v_ref[r, :].astype(BF16)
        state = st_ref[d]
        zero = jnp.zeros((), BF16)
        k_bd = jnp.where(k_on_diag, jnp.concatenate([kb.T] * D_HEADS, axis=1), zero)
        gram = _mm(qb, k_bd)
        v_bd = jnp.where(v_on_diag, jnp.concatenate([vb] * D_HEADS, axis=0), zero)
        y = _mm((gram * dec_scr[d]).astype(BF16), v_bd) + _mm(qb, state.astype(BF16)) * ein_scr[d]
        cross = _mm((k * eout_scr[d]).astype(BF16), vb, _TN)
        st_ref[d] = state * etot_scr[d] + jnp.where(on_diag, cross, 0.0)
        return r, y

    def scan(t, carry):
        r, y = chunk(t, 0)
        ybuf[r, :] = y
        r, y = chunk(nc - 1 - t, 1)
        ybuf_b[r, :] = y
        return carry

    lax.fori_loop(0, nc, scan, 0)

    def finish(c, carry):
        r = pl.ds(pl.multiple_of(c * CHUNK, CHUNK), CHUNK)
        yf = ybuf[r, :] + ybuf_b[r, :]
        o = yf - _mm_exact_rhs(yf, head_avg)
        y = o * lax.rsqrt(_mm_exact_rhs(o * o, head_avg) + EPS)
        y = (y * g_ref[...]) * _silu(gt_ref[r, :])
        y_ref[r, :] = y.astype(BF16)
        return carry

    lax.fori_loop(0, nc, finish, 0)


def _state_to_blocks(state):
    b, _, h, p, n = state.shape
    eye = jnp.eye(h, dtype=state.dtype)
    return jnp.einsum('bdhpn,hg->bdhngp', state, eye).reshape(b, 2, h * n, h * p)


def _blocks_to_state(blocks, h, p, n):
    b = blocks.shape[0]
    eye = jnp.eye(h, dtype=blocks.dtype)
    return jnp.einsum('bdhngp,hg->bdhpn', blocks.reshape(b, 2, h, n, h, p), eye)


def _ret(proj, log_g, g, row0, nb, seq, h0=None):
    has_h0 = h0 is not None
    rb = row0 // seq
    nst = 2 * D_HEADS
    kw, vw = D_HEADS * D_K, D_HEADS * D_V
    in_specs = [
        pl.BlockSpec(memory_space=pltpu.SMEM),
        pl.BlockSpec((seq, 128), lambda b: (rb + b, 16)),
        pl.BlockSpec((seq, 128), lambda b: (rb + b, 17)),
        pl.BlockSpec((seq, 256), lambda b: (rb + b, 9)),
        pl.BlockSpec((seq, 256), lambda b: (rb + b, 10)),
        pl.BlockSpec((1, 256), lambda b: (0, 0)),
    ]
    args = [log_g, proj, proj, proj, proj, g]
    if has_h0:
        in_specs.append(pl.BlockSpec((None, 2, kw, vw), lambda b: (b, 0, 0, 0)))
        args.append(h0)
    return pl.pallas_call(
        functools.partial(_ret_kernel, seq=seq, has_h0=has_h0),
        out_shape=(jax.ShapeDtypeStruct((nb * seq, BRANCH), BF16),
                   jax.ShapeDtypeStruct((nb, 2, kw, vw), F32)),
        grid=(nb,),
        in_specs=in_specs,
        out_specs=(pl.BlockSpec((seq, BRANCH), lambda b: (b, 0)),
                   pl.BlockSpec((None, 2, kw, vw), lambda b: (b, 0, 0, 0))),
        scratch_shapes=[pltpu.VMEM((seq, 256), F32), pltpu.VMEM((seq, 256), F32),
                        pltpu.VMEM((2, CHUNK, D_HEADS * CHUNK), F32), pltpu.VMEM((2, CHUNK, vw), F32),
                        pltpu.VMEM((2, CHUNK, kw), F32), pltpu.VMEM((2, kw, vw), F32)],
        compiler_params=_params("arbitrary"),
        name="ret_lat" if has_h0 else "ret_ctx",
    )(*args)


def _route(sel, s):
    row = lambda a, e: a[e:e + 1, :]
    best = None
    grp = None
    for g in range(N_EXP_GROUPS):
        vals = [row(sel, g * EXP_PER_GROUP + j) for j in range(EXP_PER_GROUP)]
        score = None
        for a in range(EXP_PER_GROUP):
            for b in range(a + 1, EXP_PER_GROUP):
                pair = vals[a] + vals[b]
                score = pair if score is None else jnp.maximum(score, pair)
        if best is None:
            best, grp = score, jnp.zeros(score.shape, jnp.int32)
        else:
            better = score > best
            best = jnp.where(better, score, best)
            grp = jnp.where(better, g, grp)

    def pick(a, j):
        out = row(a, j)
        for g in range(1, N_EXP_GROUPS):
            out = jnp.where(grp == g, row(a, g * EXP_PER_GROUP + j), out)
        return out

    cand = [pick(sel, j) for j in range(EXP_PER_GROUP)]
    aff = [pick(s, j) for j in range(EXP_PER_GROUP)]

    def arg_first_max(vals):
        top, idx = vals[0], jnp.zeros(vals[0].shape, jnp.int32)
        for j in range(1, len(vals)):
            better = vals[j] > top
            top = jnp.where(better, vals[j], top)
            idx = jnp.where(better, j, idx)
        return idx

    def take(vals, idx):
        out = vals[0]
        for j in range(1, len(vals)):
            out = jnp.where(idx == j, vals[j], out)
        return out

    i1 = arg_first_max(cand)
    i2 = arg_first_max([jnp.where(i1 == j, -jnp.inf, cand[j]) for j in range(EXP_PER_GROUP)])
    w1, w2 = take(aff, i1), take(aff, i2)
    tot = w1 + w2
    ids = jnp.concatenate([grp * EXP_PER_GROUP + i1, grp * EXP_PER_GROUP + i2], axis=0)
    gates = jnp.concatenate([w1 / tot, w2 / tot], axis=0)
    return ids, gates


def _merge_kernel(*refs, n_ctx_tiles):
    (x_ref, mod_ref, wgl_ref, wbr_ref, wout_ref, lng_ref, lnb_ref, rw_ref, rb_ref, tri_ref) = refs[:10]
    br_refs = refs[10:10 + 2 * N_BRANCH]
    x1_ref, h2_ref, ids_ref, gates_ref, rank_ref, cnt_ref = refs[10 + 2 * N_BRANCH:]
    d = D_MODEL

    @pl.when(pl.program_id(0) == 0)
    def _():
        cnt_ref[...] = jnp.zeros(cnt_ref.shape, F32)

    is_ctx = pl.program_id(0) < n_ctx_tiles
    sub = tri_ref.shape[0]
    for part in range(x_ref.shape[0] // sub):
        rs = slice(part * sub, (part + 1) * sub)
        x = x_ref[rs, :]
        hb = (x * (1.0 + mod_ref[:, d:2 * d]) + mod_ref[:, 0:d]).astype(BF16)
        merged = None
        for k in range(N_BRANCH):
            gate = jax.nn.sigmoid(_mm(hb, wgl_ref[:, k * d:(k + 1) * d]))
            br_k = jnp.where(is_ctx, br_refs[2 * k][rs, :], br_refs[2 * k + 1][rs, :])
            up = _mm(br_k, wbr_ref[k * BRANCH:(k + 1) * BRANCH, :])
            merged = gate * up if merged is None else merged + gate * up
        mix = _mm(merged.astype(BF16), wout_ref[...])
        y = ALPHA * x + mod_ref[:, 2 * d:3 * d] * mix
        y = y - jnp.mean(y, axis=1, keepdims=True)
        x1 = (y * lax.rsqrt(jnp.mean(y * y, axis=1, keepdims=True) + EPS)) * lng_ref[...] + lnb_ref[...]
        x1_ref[rs, :] = x1
        h2 = x1 * (1.0 + mod_ref[:, 4 * d:5 * d]) + mod_ref[:, 3 * d:4 * d]
        _store_token_tiles(h2_ref, h2, base=part * sub * TOKEN_TILE_ROWS)
        s = jax.nn.sigmoid(_mm_f32(rw_ref[...], h2, _NT))
        ids, gates = _route(s + rb_ref[...], s)
        ids_ref[:, rs] = ids
        gates_ref[:, rs] = gates
        expert = lax.broadcasted_iota(jnp.int32, (N_EXP, sub), 0)
        hot = [(expert == ids[k:k + 1, :]).astype(F32) for k in range(2)]
        both = hot[0] + hot[1]
        incl = _mm(both.astype(BF16), tri_ref[...])
        before = cnt_ref[...] + (incl - both)
        rank_ref[:, rs] = jnp.concatenate(
            [jnp.sum(hk * before, axis=0, keepdims=True) for hk in hot], axis=0).astype(jnp.int32)
        cnt_ref[...] = cnt_ref[...] + incl[:, sub - 1:sub]


def _merge(x, mod_l, branches, w_gl, w_br, w_out, ln_g, ln_b, rw_t, rb, n_ctx, dec_seq):
    n_tok = x.shape[0]
    tm = DENSE_TILE
    nct = n_ctx // tm
    grp = functools.partial(_mod_group, tm=tm, n_ctx=n_ctx, dec_seq=dec_seq)
    full = lambda shape: pl.BlockSpec(shape, lambda i: (0,) * len(shape))
    pos = jnp.arange(tm)
    tri = (pos[:, None] <= pos[None, :]).astype(BF16)
    br_specs = [pl.BlockSpec((tm, BRANCH), lambda i: (jnp.minimum(i, nct - 1), 0)),
                pl.BlockSpec((tm, BRANCH), lambda i: (jnp.maximum(i - nct, 0), 0))] * N_BRANCH
    br_args = [a for pair in branches for a in pair]
    return pl.pallas_call(
        functools.partial(_merge_kernel, n_ctx_tiles=nct),
        out_shape=(jax.ShapeDtypeStruct((n_tok, D_MODEL), F32),
                   jax.ShapeDtypeStruct((n_tok * TOKEN_TILE_ROWS, LANES), F32),
                   jax.ShapeDtypeStruct((2, n_tok), jnp.int32),
                   jax.ShapeDtypeStruct((2, n_tok), F32),
                   jax.ShapeDtypeStruct((2, n_tok), jnp.int32),
                   jax.ShapeDtypeStruct((N_EXP, 1), F32)),
        grid=(n_tok // tm,),
        in_specs=[
            pl.BlockSpec((tm, D_MODEL), lambda i: (i, 0)),
            pl.BlockSpec((None, 1, ADA_DIM), lambda i: (grp(i), 0, 0)),
            full((D_MODEL, N_BRANCH * D_MODEL)),
            full((N_BRANCH * BRANCH, D_MODEL)),
            full((D_MODEL, D_MODEL)),
            full((1, D_MODEL)),
            full((1, D_MODEL)),
            full((N_EXP, D_MODEL)),
            full((N_EXP, 1)),
            full((tm, tm)),
        ] + br_specs,
        out_specs=(pl.BlockSpec((tm, D_MODEL), lambda i: (i, 0)),
                   pl.BlockSpec((tm * TOKEN_TILE_ROWS, LANES), lambda i: (i, 0)),
                   pl.BlockSpec((2, tm), lambda i: (0, i)),
                   pl.BlockSpec((2, tm), lambda i: (0, i)),
                   pl.BlockSpec((2, tm), lambda i: (0, i)),
                   full((N_EXP, 1))),
        compiler_params=_params("arbitrary"),
        name="merge",
    )(x, mod_l, w_gl, w_br, w_out, ln_g, ln_b, rw_t, rb, tri, *br_args)


def _dispatch_kernel(dest_ref, pad_ref, h_hbm, x_hbm, hbuf, zbuf, in_sem, out_sem, pad_sem, *, n_tok):
    tr = TOKEN_TILE_ROWS
    rows = TOK_TILE * tr
    i = pl.program_id(0)
    last = pl.num_programs(0) - 1
    slot = lax.rem(i, 2)
    other = 1 - slot

    def load(tile, buf):
        return pltpu.make_async_copy(h_hbm.at[pl.ds(tile * rows, rows), :], hbuf.at[buf], in_sem.at[buf])

    def slot_copy(r, dst, buf):
        return pltpu.make_async_copy(hbuf.at[buf, pl.ds(r * tr, tr), :],
                                     x_hbm.at[pl.ds(dst * tr, tr), :], out_sem.at[buf])

    def drain(buf):
        for r in range(2 * TOK_TILE):
            slot_copy(0, 0, buf).wait()

    @pl.when(i == 0)
    def _():
        load(0, 0).start()

    load(i, slot).wait()

    @pl.when(i < last)
    def _():
        @pl.when(i >= 1)
        def _():
            drain(other)
        load(i + 1, other).start()

    base = i * TOK_TILE
    for r in range(TOK_TILE):
        slot_copy(r, dest_ref[base + r], slot).start()
        slot_copy(r, dest_ref[n_tok + base + r], slot).start()

    @pl.when(i == last)
    def _():
        zbuf[...] = jnp.zeros(zbuf.shape, F32)

        def zero_copy(dst):
            return pltpu.make_async_copy(zbuf, x_hbm.at[pl.ds(dst * tr, tr), :], pad_sem)

        n_pad = pad_ref.shape[0]

        def fill(j, carry):
            for u in range(PAD_UNROLL):
                zero_copy(pad_ref[j * PAD_UNROLL + u]).start()
            return carry

        lax.fori_loop(0, n_pad // PAD_UNROLL, fill, 0)
        drain(slot)

        @pl.when(i >= 1)
        def _():
            drain(other)

        def unfill(j, carry):
            for u in range(PAD_UNROLL):
                zero_copy(0).wait()
            return carry

        lax.fori_loop(0, n_pad // PAD_UNROLL, unfill, 0)


PAD_UNROLL = 8


def _moe_dispatch(h2_tiles, dest, pad_slots, n_slots):
    tr = TOKEN_TILE_ROWS
    n_tok = h2_tiles.shape[0] // tr
    assert pad_slots.shape[0] % PAD_UNROLL == 0
    any_spec = pl.BlockSpec(memory_space=pl.ANY)
    return pl.pallas_call(
        functools.partial(_dispatch_kernel, n_tok=n_tok),
        out_shape=jax.ShapeDtypeStruct((n_slots * tr, LANES), F32),
        grid_spec=pltpu.PrefetchScalarGridSpec(
            num_scalar_prefetch=2,
            grid=(n_tok // TOK_TILE,),
            in_specs=[any_spec],
            out_specs=any_spec,
            scratch_shapes=[pltpu.VMEM((2, TOK_TILE * tr, LANES), F32), pltpu.VMEM((tr, LANES), F32),
                            pltpu.SemaphoreType.DMA((2,)), pltpu.SemaphoreType.DMA((2,)),
                            pltpu.SemaphoreType.DMA(())],
        ),
        compiler_params=_params("arbitrary"),
        name="moe_dispatch",
    )(dest, pad_slots, h2_tiles)


def _moe_kernel(be_ref, nu_ref, x_ref, wg_ref, wu_ref, wd_ref, o_ref):
    del be_ref
    i = pl.program_id(0)

    @pl.when(i < nu_ref[0])
    def _():
        x = _load_token_tiles(x_ref, MOE_ROWS).astype(BF16)
        act = _silu(_mm(x, wg_ref[...].astype(BF16))) * _mm(x, wu_ref[...].astype(BF16))
        _store_token_tiles(o_ref, _mm(act.astype(BF16), wd_ref[...].astype(BF16)))

    @pl.when(i >= nu_ref[0])
    def _():
        o_ref[...] = jnp.zeros(o_ref.shape, F32)


def _moe_experts(x_tiles, blk_exp, n_used, wg, wu, wd, layer):
    n_blk = blk_exp.shape[0]
    blk = pl.BlockSpec((MOE_ROWS * TOKEN_TILE_ROWS, LANES), lambda i, be, nu: (i, 0))
    return pl.pallas_call(
        _moe_kernel,
        out_shape=jax.ShapeDtypeStruct(x_tiles.shape, F32),
        grid_spec=pltpu.PrefetchScalarGridSpec(
            num_scalar_prefetch=2,
            grid=(n_blk,),
            in_specs=[
                blk,
                pl.BlockSpec((None, None, D_MODEL, D_FF_EXP), lambda i, be, nu: (layer, be[i], 0, 0)),
                pl.BlockSpec((None, None, D_MODEL, D_FF_EXP), lambda i, be, nu: (layer, be[i], 0, 0)),
                pl.BlockSpec((None, None, D_FF_EXP, D_MODEL), lambda i, be, nu: (layer, be[i], 0, 0)),
            ],
            out_specs=blk,
        ),
        compiler_params=_params("arbitrary"),
        name="moe_experts",
    )(blk_exp, n_used, x_tiles, wg, wu, wd)


def _final_kernel(dest_ref, x1_ref, mod_ref, gt_ref, lng_ref, lnb_ref, y_hbm, *rest, n_ctx_tiles, n_tok):
    *o_refs, ybuf, sem = rest
    d = D_MODEL
    n = x1_ref.shape[0]
    tr = TOKEN_TILE_ROWS
    i = pl.program_id(0)
    slot = lax.rem(i, 2)

    def fetch_copy(src, k, r, buf):
        return pltpu.make_async_copy(y_hbm.at[pl.ds(src * tr, tr), :],
                                     ybuf.at[buf, k, pl.ds(r * tr, tr), :], sem.at[buf])

    def start_fetch(tile, buf):
        for k in range(2):
            for r in range(n):
                fetch_copy(dest_ref[k * n_tok + tile * n + r], k, r, buf).start()

    @pl.when(i == 0)
    def _():
        start_fetch(0, 0)

    for k in range(2):
        for r in range(n):
            fetch_copy(0, k, r, slot).wait()

    @pl.when(i + 1 < pl.num_programs(0))
    def _():
        start_fetch(i + 1, 1 - slot)

    gt = gt_ref[...]
    ffn = (_load_token_tiles(ybuf, n, lead=(slot, 0)) * gt[:, 0:1]
           + _load_token_tiles(ybuf, n, lead=(slot, 1)) * gt[:, 1:2])
    y = ALPHA * x1_ref[...] + mod_ref[:, 5 * d:6 * d] * ffn
    y = y - jnp.mean(y, axis=1, keepdims=True)
    out = (y * lax.rsqrt(jnp.mean(y * y, axis=1, keepdims=True) + EPS)) * lng_ref[...] + lnb_ref[...]
    if len(o_refs) == 1:
        o_refs[0][...] = out
    else:
        @pl.when(pl.program_id(0) < n_ctx_tiles)
        def _():
            o_refs[0][...] = out

        @pl.when(pl.program_id(0) >= n_ctx_tiles)
        def _():
            o_refs[1][...] = out


def _final(x1, mod_l, y_slots, dest, gates_t, ln_g, ln_b, n_ctx, dec_seq, split=False):
    n_tok = x1.shape[0]
    tm = TOK_TILE
    nt = n_tok // tm
    nct = n_ctx // tm
    grp = functools.partial(_mod_group, tm=tm, n_ctx=n_ctx, dec_seq=dec_seq)
    tile = pl.BlockSpec((tm, D_MODEL), lambda i, dst: (i, 0))
    vec = pl.BlockSpec((1, D_MODEL), lambda i, dst: (0, 0))
    if split:
        out_shape = (jax.ShapeDtypeStruct((n_ctx, D_MODEL), F32), jax.ShapeDtypeStruct((n_tok - n_ctx, D_MODEL), F32))
        out_specs = (pl.BlockSpec((tm, D_MODEL), lambda i, dst: (jnp.minimum(i, nct - 1), 0)),
                     pl.BlockSpec((tm, D_MODEL), lambda i, dst: (jnp.maximum(i - nct, 0), 0)))
    else:
        out_shape = jax.ShapeDtypeStruct((n_tok, D_MODEL), F32)
        out_specs = tile
    return pl.pallas_call(
        functools.partial(_final_kernel, n_ctx_tiles=nct, n_tok=n_tok),
        out_shape=out_shape,
        grid_spec=pltpu.PrefetchScalarGridSpec(
            num_scalar_prefetch=1,
            grid=(nt,),
            in_specs=[tile, pl.BlockSpec((None, 1, ADA_DIM), lambda i, dst: (grp(i), 0, 0)),
                      pl.BlockSpec((tm, 2), lambda i, dst: (i, 0)), vec, vec,
                      pl.BlockSpec(memory_space=pl.ANY)],
            out_specs=out_specs,
            scratch_shapes=[pltpu.VMEM((2, 2, tm * TOKEN_TILE_ROWS, LANES), F32),
                            pltpu.SemaphoreType.DMA((2,))],
        ),
        compiler_params=_params("arbitrary"),
        name="final_norm",
    )(dest, x1, mod_l, gates_t, ln_g, ln_b, y_slots)


def _rope_tables(seq, dim, width):
    nf = dim // 4
    t = jnp.arange(seq)
    pos = jnp.stack([t // GRID_W, t % GRID_W], axis=-1).astype(F32)
    inv = ROPE_BASE ** (-jnp.arange(nf, dtype=F32) / nf)
    ang = pos[:, :, None] * inv
    cos, sin = jnp.cos(ang), jnp.sin(ang)
    zero = jnp.zeros_like(sin)
    c = jnp.stack([cos, cos], axis=2).reshape(seq, dim)
    s_lo = jnp.stack([-sin, zero], axis=2).reshape(seq, dim)
    s_hi = jnp.stack([zero, sin], axis=2).reshape(seq, dim)
    rep = width // dim
    return tuple(jnp.tile(a, (1, rep)) for a in (c, s_lo, s_hi))


def _dispatch_plan(ids, rank, counts, n_tok):
    n_assign = 2 * n_tok
    flat_e = ids.reshape(n_assign)
    onehot = (flat_e[:, None] == jnp.arange(N_EXP, dtype=jnp.int32)[None, :]).astype(jnp.int32)
    counts = counts.reshape(N_EXP).astype(jnp.int32)
    padded = (counts + MOE_ROWS - 1) // MOE_ROWS * MOE_ROWS
    pad_end = jnp.cumsum(padded)
    pad_start = pad_end - padded
    dest = (jnp.sum(onehot * pad_start[None, :], axis=1) + rank.reshape(n_assign)).astype(jnp.int32)
    n_blk = n_assign // MOE_ROWS + N_EXP
    blk_start = jnp.arange(n_blk, dtype=jnp.int32) * MOE_ROWS
    blk_exp = jnp.sum((blk_start[:, None] >= pad_end[None, :]).astype(jnp.int32), axis=1)
    blk_exp = jnp.minimum(blk_exp, N_EXP - 1).astype(jnp.int32)
    n_used = (pad_end[-1] // MOE_ROWS).astype(jnp.int32).reshape(1)
    n_free = n_blk * MOE_ROWS - n_assign
    gap_end = jnp.cumsum(padded - counts)
    j = jnp.arange(n_free, dtype=jnp.int32)
    seg = jnp.sum((j[:, None] >= gap_end[None, :]).astype(jnp.int32), axis=1)
    seg_first_slot = jnp.concatenate([pad_start + counts, pad_end[-1:]])
    seg_first_j = jnp.concatenate([jnp.zeros((1,), jnp.int32), gap_end])
    seg_hot = (seg[:, None] == jnp.arange(N_EXP + 1, dtype=jnp.int32)[None, :]).astype(jnp.int32)
    free_slots = (jnp.sum(seg_hot * (seg_first_slot - seg_first_j)[None, :], axis=1) + j).astype(jnp.int32)
    return dest, free_slots, blk_exp, n_used


def kernel(x_prompt, x_sample, cache_diff_k, cache_diff_v, cache_win_k, cache_win_v, state_ssd, state_ret,
           c, c_ctx, w_ada, b_ada, w_in, diff_lambda, diff_norm_g, win_sink, conv_w, conv_b,
           ssd_A_log, ssd_dt_bias, ssd_D, ssd_norm_g, ret_decay_logit, ret_norm_g, w_branch, w_out,
           ln_g, ln_b, router_w, router_b, moe_w_gate, moe_w_up, moe_w_down):
    batch, seq, d = x_prompt.shape
    dec_batch, dec_seq, _ = x_sample.shape
    past = cache_diff_k.shape[2]
    n_ctx, n_lat = batch * seq, dec_batch * dec_seq
    n_tok = n_ctx + n_lat
    assert d == D_MODEL and n_ctx % dec_seq == 0 and seq % CHUNK == 0 and dec_seq % CHUNK == 0

    x = jnp.concatenate([x_prompt.reshape(n_ctx, d), x_sample.reshape(n_lat, d)], axis=0)

    n_mod = 1 + dec_batch
    n_mod_pad = -(-n_mod // 8) * 8
    cvec = jnp.concatenate([c_ctx[None, :], c, jnp.zeros((n_mod_pad - n_mod, d), F32)], axis=0)
    mod = _ada(cvec, w_ada, b_ada)

    rope_a = _rope_tables(dec_seq, A_QK, 256)
    rope_b = _rope_tables(dec_seq, B_DIM, 256)
    rw_t = router_w.T
    rb_col = router_b.reshape(N_EXP, 1)

    ctx_out = {k: [] for k in ('diff_k', 'diff_v', 'win_k', 'win_v', 'ssd', 'ret')}
    for l in range(DEPTH):
        mod_l = mod[l, :n_mod].reshape(n_mod, 1, ADA_DIM)
        wl = w_in[l]
        w_small = jnp.concatenate(
            [wl[:, :CDT_OFF], wl[:, CDT_OFF + 8:GATE_OFF], wl[:, CDT_OFF:CDT_OFF + 8],
             jnp.zeros((d, N_SMALL - GATE_OFF), F32)], axis=1).astype(BF16)
        w_gl = wl[:, GATE_OFF:].astype(BF16)
        proj = _inproj(x, mod_l, w_small, n_ctx, dec_seq)

        lam_init = 0.8 - 0.6 * math.exp(-0.3 * l)
        lv = diff_lambda[l]
        lam = jnp.exp(jnp.sum(lv[0] * lv[1])) - jnp.exp(jnp.sum(lv[2] * lv[3])) + lam_init
        diff_scal = jnp.stack([lam, jnp.asarray(1.0 - lam_init, F32)]).astype(F32)
        g_a = diff_norm_g[l].reshape(1, A_V)
        sink = win_sink[l]
        dtb = jnp.zeros((1, 128), F32).at[0, :8].set(ssd_dt_bias[l].reshape(8))
        acoef = jnp.zeros((1, 128), F32).at[0, :8].set(-jnp.exp(ssd_A_log[l]).reshape(8))
        dvec = jnp.repeat(ssd_D[l], C_P).reshape(1, 256)
        g_c = ssd_norm_g[l].reshape(1, 256)
        log_g = jax.nn.log_sigmoid(ret_decay_logit[l]).reshape(8)
        g_d = ret_norm_g[l].reshape(1, 256)
        cw = conv_w[l]
        cb = conv_b[l].reshape(1, 512)

        oa_c = _diff_attn(proj, diff_scal, g_a, 0, batch, seq, seq)
        ob_c = _win_attn(proj, sink, 0, batch, seq, seq)
        yc_c, st_c = _ssd(proj, cw, cb, dtb, acoef, dvec, g_c, 0, batch, seq)
        od_c, rt_c = _ret(proj, log_g, g_d, 0, batch, seq)
        cache_a = (cache_diff_k[:, l].reshape(dec_batch, past, 256), cache_diff_v[:, l].reshape(dec_batch, past, 256))
        cache_b = (cache_win_k[:, l].reshape(dec_batch, past, 128), cache_win_v[:, l].reshape(dec_batch, past, 128))
        oa_l = _diff_attn(proj, diff_scal, g_a, n_ctx, dec_batch, dec_seq, DIFF_TQ, cache=cache_a, rope=rope_a)
        ob_l = _win_attn(proj, sink, n_ctx, dec_batch, dec_seq, BLOCK, cache=cache_b, rope=rope_b)
        yc_l, _ = _ssd(proj, cw, cb, dtb, acoef, dvec, g_c, n_ctx, dec_batch, dec_seq,
                       h0=_state_to_blocks(state_ssd[:, l]))
        od_l, _ = _ret(proj, log_g, g_d, n_ctx, dec_batch, dec_seq,
                       h0=_state_to_blocks(state_ret[:, l]))

        x1, h2, ids, gates, rank, counts = _merge(
            x, mod_l, ((oa_c, oa_l), (ob_c, ob_l), (yc_c, yc_l), (od_c, od_l)), w_gl, w_branch[l].reshape(N_BRANCH * BRANCH, d).astype(BF16),
            w_out[l].astype(BF16), ln_g[l, 0].reshape(1, d), ln_b[l, 0].reshape(1, d), rw_t, rb_col,
            n_ctx, dec_seq)

        dest, free_slots, blk_exp, n_used = _dispatch_plan(ids, rank, counts, n_tok)
        x_slots = _moe_dispatch(h2, dest, free_slots, blk_exp.shape[0] * MOE_ROWS)
        y_slots = _moe_experts(x_slots, blk_exp, n_used, moe_w_gate, moe_w_up, moe_w_down, l)
        x = _final(x1, mod_l, y_slots, dest, gates.T,
                   ln_g[l, 1].reshape(1, d), ln_b[l, 1].reshape(1, d), n_ctx, dec_seq,
                   split=(l == DEPTH - 1))

        pc = proj[:n_ctx]
        ctx_out['diff_k'].append(pc[:, 256:512].reshape(batch, seq, A_HEADS, 2, A_QK))
        ctx_out['diff_v'].append(pc[:, 512:768].reshape(batch, seq, A_HEADS, A_V))
        ctx_out['win_k'].append(pc[:, 1024:1152].reshape(batch, seq, B_KV, B_DIM))
        ctx_out['win_v'].append(pc[:, 1152:1280].reshape(batch, seq, B_KV, B_DIM))
        ctx_out['ssd'].append(_blocks_to_state(st_c, C_HEADS, C_P, C_N))
        ctx_out['ret'].append(_blocks_to_state(rt_c, D_HEADS, D_V, D_K))

    y_prompt = x[0].reshape(batch, seq, d)
    y_sample = x[1].reshape(dec_batch, dec_seq, d)
    stk = lambda k: jnp.stack(ctx_out[k], axis=1)
    return (y_prompt, y_sample, stk('diff_k'), stk('diff_v'), stk('win_k'), stk('win_v'), stk('ssd'), stk('ret'))
```

```python
import functools
import math

import jax
import jax.numpy as jnp
from jax import lax
from jax.experimental import pallas as pl
from jax.experimental.pallas import tpu as pltpu

F32 = jnp.float32
BF16 = jnp.bfloat16

D_MODEL = 1024
DEPTH = 4
GRID_W = 64
BLOCK = 128
WINDOW = 128
CHUNK = 128
A_HEADS, A_QK, A_V = 4, 32, 64
B_HEADS, B_KV, B_DIM = 4, 2, 64
C_HEADS, C_P, C_GROUPS, C_N = 4, 64, 2, 64
D_HEADS, D_K, D_V = 4, 32, 64
BRANCH = 256
N_BRANCH = 4
N_EXP = 16
N_EXP_GROUPS = 4
EXP_PER_GROUP = 4
D_FF_EXP = 512
ROPE_BASE = 10000.0
ALPHA = (2 * DEPTH) ** 0.25
EPS = 1e-5
ADA_DIM = 6 * D_MODEL
NEG = -1e30

N_SMALL = 23 * 128
GATE_OFF = 2824
CDT_OFF = 2048

VMEM_LIMIT = 52 * 1024 * 1024
MOE_ROWS = 512
DENSE_TILE = 512
TOK_TILE = 256
DIFF_TQ = 256

_NN = (((1,), (0,)), ((), ()))
_NT = (((1,), (1,)), ((), ()))
_TN = (((0,), (0,)), ((), ()))


def _params(*sem):
    return pltpu.CompilerParams(dimension_semantics=sem, vmem_limit_bytes=VMEM_LIMIT)


def _mm(a, b, dims=_NN):
    return lax.dot_general(a, b, dims, preferred_element_type=F32)


def _split(a):
    hi = a.astype(BF16)
    return hi, (a - hi.astype(F32)).astype(BF16)


def _mm_f32(a, b, dims=_NN):
    a_hi, a_lo = _split(a)
    b_hi, b_lo = _split(b)
    return (_mm(a_lo, b_hi, dims) + _mm(a_hi, b_lo, dims)) + _mm(a_hi, b_hi, dims)


def _mm_exact_lhs(m_bf, a):
    a1 = a.astype(BF16)
    r1 = a - a1.astype(F32)
    a2 = r1.astype(BF16)
    a3 = (r1 - a2.astype(F32)).astype(BF16)
    return (_mm(m_bf, a3) + _mm(m_bf, a2)) + _mm(m_bf, a1)


def _mm_exact_rhs(a, m_bf):
    a1 = a.astype(BF16)
    r1 = a - a1.astype(F32)
    a2 = r1.astype(BF16)
    a3 = (r1 - a2.astype(F32)).astype(BF16)
    return (_mm(a3, m_bf) + _mm(a2, m_bf)) + _mm(a1, m_bf)


def _silu(x):
    return x * jax.nn.sigmoid(x)


LANES = 128
TOKEN_TILE_ROWS = D_MODEL // LANES


def _store_token_tiles(ref, val, lead=(), base=0):
    n = val.shape[0]
    for j in range(TOKEN_TILE_ROWS):
        ref[(*lead, pl.ds(base + j, n, stride=TOKEN_TILE_ROWS), slice(None))] = val[:, j * LANES:(j + 1) * LANES]


def _load_token_tiles(ref, n, lead=()):
    return jnp.concatenate(
        [ref[(*lead, pl.ds(j, n, stride=TOKEN_TILE_ROWS), slice(None))] for j in range(TOKEN_TILE_ROWS)], axis=1)


def _rope(x, c, s_lo, s_hi, shift):
    n = x.shape[1]
    return x * c + pltpu.roll(x, n - shift, 1) * s_lo + pltpu.roll(x, shift, 1) * s_hi


def _ada_kernel(c_ref, w_ref, b_ref, o_ref):
    c = c_ref[...]
    o_ref[...] = _mm_f32(_silu(c), w_ref[...]) + b_ref[...]


def _ada(cvec, w_ada, b_ada):
    rows = cvec.shape[0]
    tn = 1024
    return pl.pallas_call(
        _ada_kernel,
        out_shape=jax.ShapeDtypeStruct((DEPTH, rows, ADA_DIM), F32),
        grid=(DEPTH, ADA_DIM // tn),
        in_specs=[
            pl.BlockSpec((rows, D_MODEL), lambda l, j: (0, 0)),
            pl.BlockSpec((None, D_MODEL, tn), lambda l, j: (l, 0, j)),
            pl.BlockSpec((None, 1, tn), lambda l, j: (l, 0, j)),
        ],
        out_specs=pl.BlockSpec((None, rows, tn), lambda l, j: (l, 0, j)),
        compiler_params=_params("arbitrary", "arbitrary"),
        name="ada",
    )(cvec, w_ada, b_ada.reshape(DEPTH, 1, ADA_DIM))


def _inproj_kernel(x_ref, mod_ref, w_ref, o_ref):
    d = x_ref.shape[1]
    h = x_ref[...] * (1.0 + mod_ref[:, d:2 * d]) + mod_ref[:, 0:d]
    o_ref[...] = _mm(h.astype(BF16), w_ref[...])


def _mod_group(i, tm, n_ctx, dec_seq):
    row = i * tm
    return jnp.where(row < n_ctx, 0, 1 + lax.div(jnp.maximum(row - n_ctx, 0), dec_seq))


def _inproj(x, mod_l, w_small, n_ctx, dec_seq):
    n_tok = x.shape[0]
    tm = DENSE_TILE
    grp = functools.partial(_mod_group, tm=tm, n_ctx=n_ctx, dec_seq=dec_seq)
    return pl.pallas_call(
        _inproj_kernel,
        out_shape=jax.ShapeDtypeStruct((n_tok, N_SMALL), F32),
        grid=(n_tok // tm,),
        in_specs=[
            pl.BlockSpec((tm, D_MODEL), lambda i: (i, 0)),
            pl.BlockSpec((None, 1, ADA_DIM), lambda i: (grp(i), 0, 0)),
            pl.BlockSpec((D_MODEL, N_SMALL), lambda i: (0, 0)),
        ],
        out_specs=pl.BlockSpec((tm, N_SMALL), lambda i: (i, 0)),
        compiler_params=_params("arbitrary"),
        name="inproj",
    )(x, mod_l, w_small)


def _diff_attn_kernel(*refs, latent, tq, seq, past):
    if latent:
        (sc_ref, q_ref, k_ref, v_ref, g_ref, ck_ref, cv_ref, rc_ref, rlo_ref, rhi_ref,
         o_ref, kt_scr, v_scr) = refs
    else:
        sc_ref, q_ref, k_ref, v_ref, g_ref, o_ref, kt_scr, v_scr = refs
    qi = pl.program_id(1)
    shift = A_QK // 4

    @pl.when(qi == 0)
    def _():
        k = k_ref[...]
        if latent:
            k = _rope(k, rc_ref[...], rlo_ref[...], rhi_ref[...], shift)
        kt_scr[:, 0:seq] = k.T.astype(BF16)
        if latent:
            kt_scr[:, seq:seq + past] = ck_ref[...].T.astype(BF16)
        ones = jnp.ones((seq + past, LANES - A_V), BF16)
        for h in range(A_HEADS):
            hs = slice(h * A_V, (h + 1) * A_V)
            v_scr[0:seq, h * LANES:h * LANES + A_V] = v_ref[:, hs].astype(BF16)
            if latent:
                v_scr[seq:seq + past, h * LANES:h * LANES + A_V] = cv_ref[:, hs].astype(BF16)
            v_scr[:, h * LANES + A_V:(h + 1) * LANES] = ones

    q = q_ref[...]
    if latent:
        r = pl.ds(pl.multiple_of(qi * tq, tq), tq)
        q = _rope(q, rc_ref[r, :], rlo_ref[r, :], rhi_ref[r, :], shift)
    qb = q.astype(BF16)
    lam = sc_ref[0]
    post = sc_ref[1]
    c = (A_QK ** -0.5) * math.log2(math.e)
    outs = []
    for h in range(A_HEADS):
        probs = []
        for m in range(2):
            off = (h * 2 + m) * A_QK
            s = _mm(qb[:, off:off + A_QK], kt_scr[off:off + A_QK, :])
            probs.append(jnp.exp2(s * c - jnp.max(s, axis=1, keepdims=True) * c).astype(BF16))
        ov = _mm(jnp.concatenate(probs, axis=0), v_scr[:, h * LANES:(h + 1) * LANES])
        maps = [ov[m * tq:(m + 1) * tq, 0:A_V] * (1.0 / ov[m * tq:(m + 1) * tq, A_V:A_V + 1]) for m in range(2)]
        o = maps[0] - lam * maps[1]
        n = o * lax.rsqrt(jnp.mean(o * o, axis=1, keepdims=True) + EPS)
        outs.append((n * g_ref[...]) * post)
    o_ref[...] = jnp.concatenate(outs, axis=1).astype(BF16)


def _diff_attn(proj, scal, g, row0, nb, seq, tq, cache=None, rope=None):
    latent = cache is not None
    nq = seq // tq
    rb = row0 // seq
    qb0 = row0 // tq
    past = cache[0].shape[1] if latent else 0
    in_specs = [
        pl.BlockSpec(memory_space=pltpu.SMEM),
        pl.BlockSpec((tq, 256), lambda b, i: (qb0 + b * nq + i, 0)),
        pl.BlockSpec((seq, 256), lambda b, i: (rb + b, 1)),
        pl.BlockSpec((seq, 256), lambda b, i: (rb + b, 2)),
        pl.BlockSpec((1, A_V), lambda b, i: (0, 0)),
    ]
    args = [scal, proj, proj, proj, g]
    if latent:
        in_specs += [
            pl.BlockSpec((None, past, 256), lambda b, i: (b, 0, 0)),
            pl.BlockSpec((None, past, 256), lambda b, i: (b, 0, 0)),
        ] + [pl.BlockSpec((seq, 256), lambda b, i: (0, 0))] * 3
        args += [cache[0], cache[1], *rope]
    return pl.pallas_call(
        functools.partial(_diff_attn_kernel, latent=latent, tq=tq, seq=seq, past=past),
        out_shape=jax.ShapeDtypeStruct((nb * seq, BRANCH), BF16),
        grid=(nb, nq),
        in_specs=in_specs,
        out_specs=pl.BlockSpec((tq, BRANCH), lambda b, i: (b * nq + i, 0)),
        scratch_shapes=[pltpu.VMEM((256, seq + past), BF16),
                        pltpu.VMEM((seq + past, A_HEADS * LANES), BF16)],
        compiler_params=_params("arbitrary", "arbitrary"),
        name="diff_attn_lat" if latent else "diff_attn_ctx",
    )(*args)


def _win_attn_kernel(*refs, latent, tq, seq):
    if latent:
        (sink_ref, q_ref, k_ref, v_ref, ck_ref, cv_ref, rc_ref, rlo_ref, rhi_ref,
         o_ref, k_scr, v_scr, ck_scr, cv_scr) = refs
    else:
        sink_ref, q_ref, k_ref, v_ref, o_ref, k_scr, v_scr = refs
    qi = pl.program_id(1)
    nq = seq // tq
    shift = B_DIM // 4
    kvw = B_KV * B_DIM

    @pl.when(qi == 0)
    def _():
        k = k_ref[...]
        if latent:
            k = _rope(k, rc_ref[:, 0:kvw], rlo_ref[:, 0:kvw], rhi_ref[:, 0:kvw], shift)
            ck_scr[...] = ck_ref[...].T.astype(BF16)
            cv_scr[...] = cv_ref[...].astype(BF16)
        kt = k.T.astype(BF16)
        for j in range(nq):
            k_scr[j] = kt[:, j * tq:(j + 1) * tq]
        v_scr[...] = v_ref[...].astype(BF16)

    q = q_ref[...]
    if latent:
        r = pl.ds(pl.multiple_of(qi * tq, tq), tq)
        q = _rope(q, rc_ref[r, :], rlo_ref[r, :], rhi_ref[r, :], shift)
        near = (jnp.maximum(qi - 1, 0), qi, jnp.minimum(qi + 1, nq - 1))
        kl = jnp.concatenate([k_scr[j] for j in near], axis=1)
        vl = jnp.concatenate([v_scr[pl.ds(pl.multiple_of(j * tq, tq), tq), :] for j in near], axis=0)
    qb = q.astype(BF16)
    scale = B_DIM ** -0.5
    ratio = B_HEADS // B_KV
    rows = ratio * tq
    if latent:
        ii = lax.broadcasted_iota(jnp.int32, (rows, 3 * tq), 0) & (tq - 1)
        jj = lax.broadcasted_iota(jnp.int32, (rows, 3 * tq), 1)
        lo = jnp.where(qi > 0, 0, tq)
        hi = jnp.where(qi < nq - 1, 3 * tq, 2 * tq)
        valid = (jnp.abs(jj - tq - ii) <= WINDOW) & (jj >= lo) & (jj < hi)
    else:
        kl = k_scr[0]
        vl = v_scr[...]
    row_id = lax.broadcasted_iota(jnp.int32, (rows, 1), 0)
    head_of_row = (row_id - (row_id & (tq - 1))) // tq if ratio > 2 else (row_id >= tq).astype(jnp.int32)
    outs = []
    for g in range(B_KV):
        gsl = slice(g * B_DIM, (g + 1) * B_DIM)
        heads = range(g * ratio, (g + 1) * ratio)
        qg = jnp.concatenate([qb[:, h * B_DIM:(h + 1) * B_DIM] for h in heads], axis=0)
        snk = jnp.full((rows, 1), sink_ref[heads[0]], F32)
        for n, h in enumerate(heads[1:], start=1):
            snk = jnp.where(head_of_row == n, sink_ref[h], snk)
        s = _mm(qg, kl[gsl, :]) * scale
        if latent:
            s = jnp.where(valid, s, NEG)
            sc = _mm(qg, ck_scr[gsl, :]) * scale
            m = jnp.maximum(jnp.maximum(jnp.max(s, axis=1, keepdims=True),
                                        jnp.max(sc, axis=1, keepdims=True)), snk)
            pc = jnp.exp(sc - m)
        else:
            m = jnp.maximum(jnp.max(s, axis=1, keepdims=True), snk)
        p = jnp.exp(s - m)
        den = jnp.sum(p, axis=1, keepdims=True) + jnp.exp(snk - m)
        if latent:
            den = den + jnp.sum(pc, axis=1, keepdims=True)
        inv = 1.0 / den
        o = _mm((p * inv).astype(BF16), vl[:, gsl])
        if latent:
            o = o + _mm((pc * inv).astype(BF16), cv_scr[:, gsl])
        outs += [o[n * tq:(n + 1) * tq, :] for n in range(ratio)]
    o_ref[...] = jnp.concatenate(outs, axis=1).astype(BF16)


def _win_attn(proj, sink, row0, nb, seq, tq, cache=None, rope=None):
    latent = cache is not None
    nq = seq // tq
    rb = row0 // seq
    qb0 = row0 // tq
    kvw = B_KV * B_DIM
    in_specs = [
        pl.BlockSpec(memory_space=pltpu.SMEM),
        pl.BlockSpec((tq, 256), lambda b, i: (qb0 + b * nq + i, 3)),
        pl.BlockSpec((seq, kvw), lambda b, i: (rb + b, 8)),
        pl.BlockSpec((seq, kvw), lambda b, i: (rb + b, 9)),
    ]
    args = [sink, proj, proj, proj]
    scratch = [pltpu.VMEM((nq, kvw, tq), BF16), pltpu.VMEM((seq, kvw), BF16)]
    if latent:
        past = cache[0].shape[1]
        in_specs += [
            pl.BlockSpec((None, past, kvw), lambda b, i: (b, 0, 0)),
            pl.BlockSpec((None, past, kvw), lambda b, i: (b, 0, 0)),
        ] + [pl.BlockSpec((seq, 256), lambda b, i: (0, 0))] * 3
        args += [cache[0], cache[1], *rope]
        scratch += [pltpu.VMEM((kvw, past), BF16), pltpu.VMEM((past, kvw), BF16)]
    return pl.pallas_call(
        functools.partial(_win_attn_kernel, latent=latent, tq=tq, seq=seq),
        out_shape=jax.ShapeDtypeStruct((nb * seq, BRANCH), BF16),
        grid=(nb, nq),
        in_specs=in_specs,
        out_specs=pl.BlockSpec((tq, BRANCH), lambda b, i: (b * nq + i, 0)),
        scratch_shapes=scratch,
        compiler_params=_params("arbitrary", "arbitrary"),
        name="win_attn_lat" if latent else "win_attn_ctx",
    )(*args)


def _conv_silu(u, w, b):
    n = u.shape[0]
    rows = lax.broadcasted_iota(jnp.int32, u.shape, 0)
    up = jnp.where(rows == 0, 0.0, pltpu.roll(u, 1, 0))
    un = jnp.where(rows == n - 1, 0.0, pltpu.roll(u, n - 1, 0))
    return _silu(up * w[0:1, :] + u * w[1:2, :] + un * w[2:3, :] + b)


def _ssd_kernel(*refs, seq, has_h0):
    if has_h0:
        (cx_ref, cz_ref, cbc_ref, cdt_ref, cw_ref, cb_ref, dtb_ref, ac_ref, dv_ref, g_ref, h0_ref,
         y_ref, st_ref, xs, bcs, dts, ybuf, ybuf_b) = refs
    else:
        (cx_ref, cz_ref, cbc_ref, cdt_ref, cw_ref, cb_ref, dtb_ref, ac_ref, dv_ref, g_ref,
         y_ref, st_ref, xs, bcs, dts, ybuf, ybuf_b) = refs
    nc = seq // CHUNK
    xw = C_HEADS * C_P
    xs[...] = _conv_silu(cx_ref[...], cw_ref[:, 0:xw], cb_ref[:, 0:xw])
    bcs[...] = _conv_silu(cbc_ref[...], cw_ref[:, xw:2 * xw], cb_ref[:, xw:2 * xw])
    z = cdt_ref[...] + dtb_ref[...]
    dts[...] = jnp.maximum(z, 0.0) + jnp.log1p(jnp.exp(-jnp.abs(z)))
    if has_h0:
        st_ref[...] = h0_ref[...]
    else:
        st_ref[...] = jnp.zeros(st_ref.shape, F32)

    ri = lax.broadcasted_iota(jnp.int32, (CHUNK, CHUNK), 0)
    ci = lax.broadcasted_iota(jnp.int32, (CHUNK, CHUNK), 1)
    gw = C_GROUPS * C_N
    per_group = C_HEADS // C_GROUPS

    def same_block(shape, row_block, col_block):
        rows = lax.shift_right_logical(lax.broadcasted_iota(jnp.int32, shape, 0), row_block.bit_length() - 1)
        cols = lax.shift_right_logical(lax.broadcasted_iota(jnp.int32, shape, 1), col_block.bit_length() - 1)
        return rows == cols

    b_on_diag = same_block((gw, C_GROUPS * CHUNK), C_N, CHUNK)
    x_on_diag = same_block((C_HEADS * CHUNK, xw), CHUNK, C_P)
    st_on_diag = same_block((C_HEADS * C_N, xw), C_N, C_P)
    zero = jnp.zeros((), BF16)

    def per_head(group_cols):
        w = group_cols.shape[1] // C_GROUPS
        return jnp.concatenate([group_cols[:, (h // per_group) * w:(h // per_group + 1) * w]
                                for h in range(C_HEADS)], axis=1)

    def chunk(c, d):
        r = pl.ds(pl.multiple_of(c * CHUNK, CHUNK), CHUNK)
        tri = (ri >= ci) if d == 0 else (ci >= ri)
        dt = dts[r, :]
        cs = _mm_exact_lhs(tri.astype(BF16), dt * ac_ref[...])
        cst = cs.T
        bc = bcs[r, :]
        decs, e_ins, e_outs, e_tots, dt_cols = [], [], [], [], []
        for h in range(C_HEADS):
            col = d * C_HEADS + h
            cc = jnp.broadcast_to(cs[:, col:col + 1], (CHUNK, CHUNK))
            tot = cc[CHUNK - 1:CHUNK, :] if d == 0 else cc[0:1, :]
            decs.append(jnp.exp(jnp.where(tri, cc - cst[col:col + 1, :], NEG)))
            e_ins.append(jnp.exp(cc)[:, 0:C_P])
            e_outs.append(jnp.exp(tot - cc)[:, 0:C_N])
            e_tots.append(jnp.broadcast_to(jnp.exp(jnp.concatenate([tot] * (xw // CHUNK), axis=1)), (C_N, xw)))
            dt_cols.append(jnp.broadcast_to(dt[:, col:col + 1], (CHUNK, C_P)))
        xb = (xs[r, :] * jnp.concatenate(dt_cols, axis=1)).astype(BF16)
        bmat = bc[:, 0:gw]
        cb = bc[:, gw:2 * gw].astype(BF16)
        b_bd = jnp.where(b_on_diag, jnp.concatenate([bmat.astype(BF16).T] * C_GROUPS, axis=1), zero)
        gram = per_head(_mm(cb, b_bd))
        x_bd = jnp.where(x_on_diag, jnp.concatenate([xb] * C_HEADS, axis=0), zero)
        state = st_ref[d]
        y = _mm((gram * jnp.concatenate(decs, axis=1)).astype(BF16), x_bd)
        y = y + _mm(per_head(cb), state.astype(BF16)) * jnp.concatenate(e_ins, axis=1)
        bd = (per_head(bmat) * jnp.concatenate(e_outs, axis=1)).astype(BF16)
        cross = _mm(bd, xb, _TN)
        st_ref[d] = state * jnp.concatenate(e_tots, axis=0) + jnp.where(st_on_diag, cross, 0.0)
        return r, y

    def scan(t, carry):
        r, y = chunk(t, 0)
        ybuf[r, :] = y
        r, y = chunk(nc - 1 - t, 1)
        ybuf_b[r, :] = y
        return carry

    lax.fori_loop(0, nc, scan, 0)

    def finish(c, carry):
        r = pl.ds(pl.multiple_of(c * CHUNK, CHUNK), CHUNK)
        y = (ybuf[r, :] + ybuf_b[r, :]) + xs[r, :] * dv_ref[...]
        y = y * _silu(cz_ref[r, :])
        gl = xw // C_GROUPS
        parts = []
        for g in range(C_GROUPS):
            seg = y[:, g * gl:(g + 1) * gl]
            parts.append(seg * lax.rsqrt(jnp.mean(seg * seg, axis=1, keepdims=True) + EPS))
        y_ref[r, :] = (jnp.concatenate(parts, axis=1) * g_ref[...]).astype(BF16)
        return carry

    lax.fori_loop(0, nc, finish, 0)


def _ssd(proj, conv_w, conv_b, dtb, acoef, dvec, g, row0, nb, seq, h0=None):
    has_h0 = h0 is not None
    rb = row0 // seq
    st_shape = (2, C_HEADS * C_N, C_HEADS * C_P)
    in_specs = [
        pl.BlockSpec((seq, 256), lambda b: (rb + b, 5)),
        pl.BlockSpec((seq, 256), lambda b: (rb + b, 6)),
        pl.BlockSpec((seq, 256), lambda b: (rb + b, 7)),
        pl.BlockSpec((seq, 128), lambda b: (rb + b, 22)),
        pl.BlockSpec((3, 512), lambda b: (0, 0)),
        pl.BlockSpec((1, 512), lambda b: (0, 0)),
        pl.BlockSpec((1, 128), lambda b: (0, 0)),
        pl.BlockSpec((1, 128), lambda b: (0, 0)),
        pl.BlockSpec((1, 256), lambda b: (0, 0)),
        pl.BlockSpec((1, 256), lambda b: (0, 0)),
    ]
    args = [proj, proj, proj, proj, conv_w, conv_b, dtb, acoef, dvec, g]
    if has_h0:
        in_specs.append(pl.BlockSpec((None,) + st_shape, lambda b: (b, 0, 0, 0)))
        args.append(h0)
    return pl.pallas_call(
        functools.partial(_ssd_kernel, seq=seq, has_h0=has_h0),
        out_shape=(jax.ShapeDtypeStruct((nb * seq, BRANCH), BF16),
                   jax.ShapeDtypeStruct((nb,) + st_shape, F32)),
        grid=(nb,),
        in_specs=in_specs,
        out_specs=(pl.BlockSpec((seq, BRANCH), lambda b: (b, 0)),
                   pl.BlockSpec((None,) + st_shape, lambda b: (b, 0, 0, 0))),
        scratch_shapes=[pltpu.VMEM((seq, 256), F32), pltpu.VMEM((seq, 256), F32),
                        pltpu.VMEM((seq, 128), F32), pltpu.VMEM((seq, 256), F32),
                        pltpu.VMEM((seq, 256), F32)],
        compiler_params=_params("arbitrary"),
        name="ssd_lat" if has_h0 else "ssd_ctx",
    )(*args)


def _ret_kernel(*refs, seq, has_h0):
    if has_h0:
        (lg_ref, q_ref, k_ref, v_ref, gt_ref, g_ref, h0_ref, y_ref, st_ref,
         ybuf, ybuf_b, dec_scr, ein_scr, eout_scr, etot_scr) = refs
    else:
        (lg_ref, q_ref, k_ref, v_ref, gt_ref, g_ref, y_ref, st_ref,
         ybuf, ybuf_b, dec_scr, ein_scr, eout_scr, etot_scr) = refs
    nc = seq // CHUNK
    kw, vw = D_HEADS * D_K, D_HEADS * D_V
    if has_h0:
        st_ref[...] = h0_ref[...]
    else:
        st_ref[...] = jnp.zeros(st_ref.shape, F32)
    ri = lax.broadcasted_iota(jnp.int32, (CHUNK, CHUNK), 0)
    ci = lax.broadcasted_iota(jnp.int32, (CHUNK, CHUNK), 1)
    pos_v = lax.broadcasted_iota(jnp.int32, (CHUNK, D_V), 0).astype(F32)
    pos_k = lax.broadcasted_iota(jnp.int32, (CHUNK, D_K), 0).astype(F32)
    def same_block(shape, row_block, col_block):
        rows = lax.shift_right_logical(lax.broadcasted_iota(jnp.int32, shape, 0), row_block.bit_length() - 1)
        cols = lax.shift_right_logical(lax.broadcasted_iota(jnp.int32, shape, 1), col_block.bit_length() - 1)
        return rows == cols

    k_on_diag = same_block((kw, D_HEADS * CHUNK), D_K, CHUNK)
    v_on_diag = same_block((D_HEADS * CHUNK, vw), CHUNK, D_V)
    on_diag = same_block((kw, vw), D_K, D_V)
    head_avg = jnp.where(same_block((vw, vw), D_V, D_V), 1.0 / D_V, 0.0).astype(BF16)
    kscale = D_K ** -0.5
    for d in range(2):
        if d == 0:
            tri, dist = ri >= ci, (ri - ci).astype(F32)
            steps_in, steps_out = pos_v + 1.0, (CHUNK - 1.0) - pos_k
        else:
            tri, dist = ci >= ri, (ci - ri).astype(F32)
            steps_in, steps_out = CHUNK - pos_v, pos_k
        for h in range(D_HEADS):
            lg = lg_ref[d * D_HEADS + h]
            dec_scr[d, :, h * CHUNK:(h + 1) * CHUNK] = jnp.exp(jnp.where(tri, dist * lg, NEG))
            ein_scr[d, :, h * D_V:(h + 1) * D_V] = jnp.exp(steps_in * lg)
            eout_scr[d, :, h * D_K:(h + 1) * D_K] = jnp.exp(steps_out * lg)
            etot_scr[d, h * D_K:(h + 1) * D_K, :] = jnp.exp(jnp.full((D_K, vw), CHUNK * 1.0, F32) * lg)

    def chunk(c, d):
        r = pl.ds(pl.multiple_of(c * CHUNK, CHUNK), CHUNK)
        qb = q_ref[r, :].astype(BF16)
        k = k_ref[r, :] * kscale
        kb = k.astype(BF16)
        vb = v_ref[r, :].astype(BF16)
        state = st_ref[d]
        zero = jnp.zeros((), BF16)
        k_bd = jnp.where(k_on_diag, jnp.concatenate([kb.T] * D_HEADS, axis=1), zero)
        gram = _mm(qb, k_bd)
        v_bd = jnp.where(v_on_diag, jnp.concatenate([vb] * D_HEADS, axis=0), zero)
        y = _mm((gram * dec_scr[d]).astype(BF16), v_bd) + _mm(qb, state.astype(BF16)) * ein_scr[d]
        cross = _mm((k * eout_scr[d]).astype(BF16), vb, _TN)
        st_ref[d] = state * etot_scr[d] + jnp.where(on_diag, cross, 0.0)
        return r, y

    def scan(t, carry):
        r, y = chunk(t, 0)
        ybuf[r, :] = y
        r, y = chunk(nc - 1 - t, 1)
        ybuf_b[r, :] = y
        return carry

    lax.fori_loop(0, nc, scan, 0)

    def finish(c, carry):
        r = pl.ds(pl.multiple_of(c * CHUNK, CHUNK), CHUNK)
        yf = ybuf[r, :] + ybuf_b[r, :]
        o = yf - _mm_exact_rhs(yf, head_avg)
        y = o * lax.rsqrt(_mm_exact_rhs(o * o, head_avg) + EPS)
        y = (y * g_ref[...]) * _silu(gt_ref[r, :])
        y_ref[r, :] = y.astype(BF16)
        return carry

    lax.fori_loop(0, nc, finish, 0)


def _state_to_blocks(state):
    b, l, _, h, p, n = state.shape
    eye = jnp.eye(h, dtype=state.dtype)
    return jnp.einsum('bldhpn,hg->bldhngp', state, eye).reshape(b, l, 2, h * n, h * p)


def _blocks_to_state(blocks, h, p, n):
    b, l = blocks.shape[:2]
    eye = jnp.eye(h, dtype=blocks.dtype)
    return jnp.einsum('bldhngp,hg->bldhpn', blocks.reshape(b, l, 2, h, n, h, p), eye)


def _ret(proj, log_g, g, row0, nb, seq, h0=None):
    has_h0 = h0 is not None
    rb = row0 // seq
    nst = 2 * D_HEADS
    kw, vw = D_HEADS * D_K, D_HEADS * D_V
    in_specs = [
        pl.BlockSpec(memory_space=pltpu.SMEM),
        pl.BlockSpec((seq, 128), lambda b: (rb + b, 16)),
        pl.BlockSpec((seq, 128), lambda b: (rb + b, 17)),
        pl.BlockSpec((seq, 256), lambda b: (rb + b, 9)),
        pl.BlockSpec((seq, 256), lambda b: (rb + b, 10)),
        pl.BlockSpec((1, 256), lambda b: (0, 0)),
    ]
    args = [log_g, proj, proj, proj, proj, g]
    if has_h0:
        in_specs.append(pl.BlockSpec((None, 2, kw, vw), lambda b: (b, 0, 0, 0)))
        args.append(h0)
    return pl.pallas_call(
        functools.partial(_ret_kernel, seq=seq, has_h0=has_h0),
        out_shape=(jax.ShapeDtypeStruct((nb * seq, BRANCH), BF16),
                   jax.ShapeDtypeStruct((nb, 2, kw, vw), F32)),
        grid=(nb,),
        in_specs=in_specs,
        out_specs=(pl.BlockSpec((seq, BRANCH), lambda b: (b, 0)),
                   pl.BlockSpec((None, 2, kw, vw), lambda b: (b, 0, 0, 0))),
        scratch_shapes=[pltpu.VMEM((seq, 256), F32), pltpu.VMEM((seq, 256), F32),
                        pltpu.VMEM((2, CHUNK, D_HEADS * CHUNK), F32), pltpu.VMEM((2, CHUNK, vw), F32),
                        pltpu.VMEM((2, CHUNK, kw), F32), pltpu.VMEM((2, kw, vw), F32)],
        compiler_params=_params("arbitrary"),
        name="ret_lat" if has_h0 else "ret_ctx",
    )(*args)


def _route(sel, s):
    row = lambda a, e: a[e:e + 1, :]
    best = None
    grp = None
    for g in range(N_EXP_GROUPS):
        vals = [row(sel, g * EXP_PER_GROUP + j) for j in range(EXP_PER_GROUP)]
        score = None
        for a in range(EXP_PER_GROUP):
            for b in range(a + 1, EXP_PER_GROUP):
                pair = vals[a] + vals[b]
                score = pair if score is None else jnp.maximum(score, pair)
        if best is None:
            best, grp = score, jnp.zeros(score.shape, jnp.int32)
        else:
            better = score > best
            best = jnp.where(better, score, best)
            grp = jnp.where(better, g, grp)

    def pick(a, j):
        out = row(a, j)
        for g in range(1, N_EXP_GROUPS):
            out = jnp.where(grp == g, row(a, g * EXP_PER_GROUP + j), out)
        return out

    cand = [pick(sel, j) for j in range(EXP_PER_GROUP)]
    aff = [pick(s, j) for j in range(EXP_PER_GROUP)]

    def arg_first_max(vals):
        top, idx = vals[0], jnp.zeros(vals[0].shape, jnp.int32)
        for j in range(1, len(vals)):
            better = vals[j] > top
            top = jnp.where(better, vals[j], top)
            idx = jnp.where(better, j, idx)
        return idx

    def take(vals, idx):
        out = vals[0]
        for j in range(1, len(vals)):
            out = jnp.where(idx == j, vals[j], out)
        return out

    i1 = arg_first_max(cand)
    i2 = arg_first_max([jnp.where(i1 == j, -jnp.inf, cand[j]) for j in range(EXP_PER_GROUP)])
    w1, w2 = take(aff, i1), take(aff, i2)
    tot = w1 + w2
    ids = jnp.concatenate([grp * EXP_PER_GROUP + i1, grp * EXP_PER_GROUP + i2], axis=0)
    gates = jnp.concatenate([w1 / tot, w2 / tot], axis=0)
    return ids, gates


def _merge_kernel(*refs, n_ctx_tiles):
    (x_ref, mod_ref, wgl_ref, wbr_ref, wout_ref, lng_ref, lnb_ref, rw_ref, rb_ref, tri_ref) = refs[:10]
    br_refs = refs[10:10 + 2 * N_BRANCH]
    x1_ref, h2_ref, ids_ref, gates_ref, rank_ref, cnt_ref = refs[10 + 2 * N_BRANCH:]
    d = D_MODEL

    @pl.when(pl.program_id(0) == 0)
    def _():
        cnt_ref[...] = jnp.zeros(cnt_ref.shape, F32)

    is_ctx = pl.program_id(0) < n_ctx_tiles
    sub = tri_ref.shape[0]
    for part in range(x_ref.shape[0] // sub):
        rs = slice(part * sub, (part + 1) * sub)
        x = x_ref[rs, :]
        hb = (x * (1.0 + mod_ref[:, d:2 * d]) + mod_ref[:, 0:d]).astype(BF16)
        merged = None
        for k in range(N_BRANCH):
            gate = jax.nn.sigmoid(_mm(hb, wgl_ref[:, k * d:(k + 1) * d]))
            br_k = jnp.where(is_ctx, br_refs[2 * k][rs, :], br_refs[2 * k + 1][rs, :])
            up = _mm(br_k, wbr_ref[k * BRANCH:(k + 1) * BRANCH, :])
            merged = gate * up if merged is None else merged + gate * up
        mix = _mm(merged.astype(BF16), wout_ref[...])
        y = ALPHA * x + mod_ref[:, 2 * d:3 * d] * mix
        y = y - jnp.mean(y, axis=1, keepdims=True)
        x1 = (y * lax.rsqrt(jnp.mean(y * y, axis=1, keepdims=True) + EPS)) * lng_ref[...] + lnb_ref[...]
        x1_ref[rs, :] = x1
        h2 = x1 * (1.0 + mod_ref[:, 4 * d:5 * d]) + mod_ref[:, 3 * d:4 * d]
        _store_token_tiles(h2_ref, h2, base=part * sub * TOKEN_TILE_ROWS)
        s = jax.nn.sigmoid(_mm_f32(rw_ref[...], h2, _NT))
        ids, gates = _route(s + rb_ref[...], s)
        ids_ref[:, rs] = ids
        gates_ref[:, rs] = gates
        expert = lax.broadcasted_iota(jnp.int32, (N_EXP, sub), 0)
        hot = [(expert == ids[k:k + 1, :]).astype(F32) for k in range(2)]
        both = hot[0] + hot[1]
        incl = _mm(both.astype(BF16), tri_ref[...])
        before = cnt_ref[...] + (incl - both)
        rank_ref[:, rs] = jnp.concatenate(
            [jnp.sum(hk * before, axis=0, keepdims=True) for hk in hot], axis=0).astype(jnp.int32)
        cnt_ref[...] = cnt_ref[...] + incl[:, sub - 1:sub]


def _merge(x, mod_l, branches, w_gl, w_br, w_out, ln_g, ln_b, rw_t, rb, n_ctx, dec_seq):
    n_tok = x.shape[0]
    tm = DENSE_TILE
    nct = n_ctx // tm
    grp = functools.partial(_mod_group, tm=tm, n_ctx=n_ctx, dec_seq=dec_seq)
    full = lambda shape: pl.BlockSpec(shape, lambda i: (0,) * len(shape))
    pos = jnp.arange(tm)
    tri = (pos[:, None] <= pos[None, :]).astype(BF16)
    br_specs = [pl.BlockSpec((tm, BRANCH), lambda i: (jnp.minimum(i, nct - 1), 0)),
                pl.BlockSpec((tm, BRANCH), lambda i: (jnp.maximum(i - nct, 0), 0))] * N_BRANCH
    br_args = [a for pair in branches for a in pair]
    return pl.pallas_call(
        functools.partial(_merge_kernel, n_ctx_tiles=nct),
        out_shape=(jax.ShapeDtypeStruct((n_tok, D_MODEL), F32),
                   jax.ShapeDtypeStruct((n_tok * TOKEN_TILE_ROWS, LANES), F32),
                   jax.ShapeDtypeStruct((2, n_tok), jnp.int32),
                   jax.ShapeDtypeStruct((2, n_tok), F32),
                   jax.ShapeDtypeStruct((2, n_tok), jnp.int32),
                   jax.ShapeDtypeStruct((N_EXP, 1), F32)),
        grid=(n_tok // tm,),
        in_specs=[
            pl.BlockSpec((tm, D_MODEL), lambda i: (i, 0)),
            pl.BlockSpec((None, 1, ADA_DIM), lambda i: (grp(i), 0, 0)),
            full((D_MODEL, N_BRANCH * D_MODEL)),
            full((N_BRANCH * BRANCH, D_MODEL)),
            full((D_MODEL, D_MODEL)),
            full((1, D_MODEL)),
            full((1, D_MODEL)),
            full((N_EXP, D_MODEL)),
            full((N_EXP, 1)),
            full((tm, tm)),
        ] + br_specs,
        out_specs=(pl.BlockSpec((tm, D_MODEL), lambda i: (i, 0)),
                   pl.BlockSpec((tm * TOKEN_TILE_ROWS, LANES), lambda i: (i, 0)),
                   pl.BlockSpec((2, tm), lambda i: (0, i)),
                   pl.BlockSpec((2, tm), lambda i: (0, i)),
                   pl.BlockSpec((2, tm), lambda i: (0, i)),
                   full((N_EXP, 1))),
        compiler_params=_params("arbitrary"),
        name="merge",
    )(x, mod_l, w_gl, w_br, w_out, ln_g, ln_b, rw_t, rb, tri, *br_args)


def _dispatch_kernel(dest_ref, pad_ref, h_hbm, x_hbm, hbuf, zbuf, in_sem, out_sem, pad_sem, *, n_tok):
    tr = TOKEN_TILE_ROWS
    rows = TOK_TILE * tr
    i = pl.program_id(0)
    last = pl.num_programs(0) - 1
    slot = lax.rem(i, 2)
    other = 1 - slot

    def load(tile, buf):
        return pltpu.make_async_copy(h_hbm.at[pl.ds(tile * rows, rows), :], hbuf.at[buf], in_sem.at[buf])

    def slot_copy(r, dst, buf):
        return pltpu.make_async_copy(hbuf.at[buf, pl.ds(r * tr, tr), :],
                                     x_hbm.at[pl.ds(dst * tr, tr), :], out_sem.at[buf])

    def drain(buf):
        for r in range(2 * TOK_TILE):
            slot_copy(0, 0, buf).wait()

    @pl.when(i == 0)
    def _():
        load(0, 0).start()

    load(i, slot).wait()

    @pl.when(i < last)
    def _():
        @pl.when(i >= 1)
        def _():
            drain(other)
        load(i + 1, other).start()

    base = i * TOK_TILE
    for r in range(TOK_TILE):
        slot_copy(r, dest_ref[base + r], slot).start()
        slot_copy(r, dest_ref[n_tok + base + r], slot).start()

    @pl.when(i == last)
    def _():
        zbuf[...] = jnp.zeros(zbuf.shape, F32)

        def zero_copy(dst):
            return pltpu.make_async_copy(zbuf, x_hbm.at[pl.ds(dst * tr, tr), :], pad_sem)

        n_pad = pad_ref.shape[0]

        def fill(j, carry):
            for u in range(PAD_UNROLL):
                zero_copy(pad_ref[j * PAD_UNROLL + u]).start()
            return carry

        lax.fori_loop(0, n_pad // PAD_UNROLL, fill, 0)
        drain(slot)

        @pl.when(i >= 1)
        def _():
            drain(other)

        def unfill(j, carry):
            for u in range(PAD_UNROLL):
                zero_copy(0).wait()
            return carry

        lax.fori_loop(0, n_pad // PAD_UNROLL, unfill, 0)


PAD_UNROLL = 8


def _moe_dispatch(h2_tiles, dest, pad_slots, n_slots):
    tr = TOKEN_TILE_ROWS
    n_tok = h2_tiles.shape[0] // tr
    assert pad_slots.shape[0] % PAD_UNROLL == 0
    any_spec = pl.BlockSpec(memory_space=pl.ANY)
    return pl.pallas_call(
        functools.partial(_dispatch_kernel, n_tok=n_tok),
        out_shape=jax.ShapeDtypeStruct((n_slots * tr, LANES), F32),
        grid_spec=pltpu.PrefetchScalarGridSpec(
            num_scalar_prefetch=2,
            grid=(n_tok // TOK_TILE,),
            in_specs=[any_spec],
            out_specs=any_spec,
            scratch_shapes=[pltpu.VMEM((2, TOK_TILE * tr, LANES), F32), pltpu.VMEM((tr, LANES), F32),
                            pltpu.SemaphoreType.DMA((2,)), pltpu.SemaphoreType.DMA((2,)),
                            pltpu.SemaphoreType.DMA(())],
        ),
        compiler_params=_params("arbitrary"),
        name="moe_dispatch",
    )(dest, pad_slots, h2_tiles)


def _moe_kernel(be_ref, nu_ref, x_ref, wg_ref, wu_ref, wd_ref, o_ref):
    del be_ref
    i = pl.program_id(0)

    @pl.when(i < nu_ref[0])
    def _():
        x = _load_token_tiles(x_ref, MOE_ROWS).astype(BF16)
        act = _silu(_mm(x, wg_ref[...].astype(BF16))) * _mm(x, wu_ref[...].astype(BF16))
        _store_token_tiles(o_ref, _mm(act.astype(BF16), wd_ref[...].astype(BF16)))

    @pl.when(i >= nu_ref[0])
    def _():
        o_ref[...] = jnp.zeros(o_ref.shape, F32)


def _moe_experts(x_tiles, blk_exp, n_used, wg, wu, wd, layer):
    n_blk = blk_exp.shape[0]
    blk = pl.BlockSpec((MOE_ROWS * TOKEN_TILE_ROWS, LANES), lambda i, be, nu: (i, 0))
    return pl.pallas_call(
        _moe_kernel,
        out_shape=jax.ShapeDtypeStruct(x_tiles.shape, F32),
        grid_spec=pltpu.PrefetchScalarGridSpec(
            num_scalar_prefetch=2,
            grid=(n_blk,),
            in_specs=[
                blk,
                pl.BlockSpec((None, None, D_MODEL, D_FF_EXP), lambda i, be, nu: (layer, be[i], 0, 0)),
                pl.BlockSpec((None, None, D_MODEL, D_FF_EXP), lambda i, be, nu: (layer, be[i], 0, 0)),
                pl.BlockSpec((None, None, D_FF_EXP, D_MODEL), lambda i, be, nu: (layer, be[i], 0, 0)),
            ],
            out_specs=blk,
        ),
        compiler_params=_params("arbitrary"),
        name="moe_experts",
    )(blk_exp, n_used, x_tiles, wg, wu, wd)


def _final_kernel(dest_ref, x1_ref, mod_ref, gt_ref, lng_ref, lnb_ref, y_hbm, *rest, n_ctx_tiles, n_tok):
    *o_refs, ybuf, sem = rest
    d = D_MODEL
    n = x1_ref.shape[0]
    tr = TOKEN_TILE_ROWS
    i = pl.program_id(0)
    slot = lax.rem(i, 2)

    def fetch_copy(src, k, r, buf):
        return pltpu.make_async_copy(y_hbm.at[pl.ds(src * tr, tr), :],
                                     ybuf.at[buf, k, pl.ds(r * tr, tr), :], sem.at[buf])

    def start_fetch(tile, buf):
        for k in range(2):
            for r in range(n):
                fetch_copy(dest_ref[k * n_tok + tile * n + r], k, r, buf).start()

    @pl.when(i == 0)
    def _():
        start_fetch(0, 0)

    for k in range(2):
        for r in range(n):
            fetch_copy(0, k, r, slot).wait()

    @pl.when(i + 1 < pl.num_programs(0))
    def _():
        start_fetch(i + 1, 1 - slot)

    gt = gt_ref[...]
    ffn = (_load_token_tiles(ybuf, n, lead=(slot, 0)) * gt[:, 0:1]
           + _load_token_tiles(ybuf, n, lead=(slot, 1)) * gt[:, 1:2])
    y = ALPHA * x1_ref[...] + mod_ref[:, 5 * d:6 * d] * ffn
    y = y - jnp.mean(y, axis=1, keepdims=True)
    out = (y * lax.rsqrt(jnp.mean(y * y, axis=1, keepdims=True) + EPS)) * lng_ref[...] + lnb_ref[...]
    if len(o_refs) == 1:
        o_refs[0][...] = out
    else:
        @pl.when(pl.program_id(0) < n_ctx_tiles)
        def _():
            o_refs[0][...] = out

        @pl.when(pl.program_id(0) >= n_ctx_tiles)
        def _():
            o_refs[1][...] = out


def _final(x1, mod_l, y_slots, dest, gates_t, ln_g, ln_b, n_ctx, dec_seq, split=False):
    n_tok = x1.shape[0]
    tm = TOK_TILE
    nt = n_tok // tm
    nct = n_ctx // tm
    grp = functools.partial(_mod_group, tm=tm, n_ctx=n_ctx, dec_seq=dec_seq)
    tile = pl.BlockSpec((tm, D_MODEL), lambda i, dst: (i, 0))
    vec = pl.BlockSpec((1, D_MODEL), lambda i, dst: (0, 0))
    if split:
        out_shape = (jax.ShapeDtypeStruct((n_ctx, D_MODEL), F32), jax.ShapeDtypeStruct((n_tok - n_ctx, D_MODEL), F32))
        out_specs = (pl.BlockSpec((tm, D_MODEL), lambda i, dst: (jnp.minimum(i, nct - 1), 0)),
                     pl.BlockSpec((tm, D_MODEL), lambda i, dst: (jnp.maximum(i - nct, 0), 0)))
    else:
        out_shape = jax.ShapeDtypeStruct((n_tok, D_MODEL), F32)
        out_specs = tile
    return pl.pallas_call(
        functools.partial(_final_kernel, n_ctx_tiles=nct, n_tok=n_tok),
        out_shape=out_shape,
        grid_spec=pltpu.PrefetchScalarGridSpec(
            num_scalar_prefetch=1,
            grid=(nt,),
            in_specs=[tile, pl.BlockSpec((None, 1, ADA_DIM), lambda i, dst: (grp(i), 0, 0)),
                      pl.BlockSpec((tm, 2), lambda i, dst: (i, 0)), vec, vec,
                      pl.BlockSpec(memory_space=pl.ANY)],
            out_specs=out_specs,
            scratch_shapes=[pltpu.VMEM((2, 2, tm * TOKEN_TILE_ROWS, LANES), F32),
                            pltpu.SemaphoreType.DMA((2,))],
        ),
        compiler_params=_params("arbitrary"),
        name="final_norm",
    )(dest, x1, mod_l, gates_t, ln_g, ln_b, y_slots)


def _rope_tables(seq, dim, width):
    nf = dim // 4
    t = jnp.arange(seq)
    pos = jnp.stack([t // GRID_W, t % GRID_W], axis=-1).astype(F32)
    inv = ROPE_BASE ** (-jnp.arange(nf, dtype=F32) / nf)
    ang = pos[:, :, None] * inv
    cos, sin = jnp.cos(ang), jnp.sin(ang)
    zero = jnp.zeros_like(sin)
    c = jnp.stack([cos, cos], axis=2).reshape(seq, dim)
    s_lo = jnp.stack([-sin, zero], axis=2).reshape(seq, dim)
    s_hi = jnp.stack([zero, sin], axis=2).reshape(seq, dim)
    rep = width // dim
    return tuple(jnp.tile(a, (1, rep)) for a in (c, s_lo, s_hi))


def _dispatch_plan(ids, rank, counts, n_tok):
    n_assign = 2 * n_tok
    flat_e = ids.reshape(n_assign)
    onehot = (flat_e[:, None] == jnp.arange(N_EXP, dtype=jnp.int32)[None, :]).astype(jnp.int32)
    counts = counts.reshape(N_EXP).astype(jnp.int32)
    padded = (counts + MOE_ROWS - 1) // MOE_ROWS * MOE_ROWS
    pad_end = jnp.cumsum(padded)
    pad_start = pad_end - padded
    dest = (jnp.sum(onehot * pad_start[None, :], axis=1) + rank.reshape(n_assign)).astype(jnp.int32)
    n_blk = n_assign // MOE_ROWS + N_EXP
    blk_start = jnp.arange(n_blk, dtype=jnp.int32) * MOE_ROWS
    blk_exp = jnp.sum((blk_start[:, None] >= pad_end[None, :]).astype(jnp.int32), axis=1)
    blk_exp = jnp.minimum(blk_exp, N_EXP - 1).astype(jnp.int32)
    n_used = (pad_end[-1] // MOE_ROWS).astype(jnp.int32).reshape(1)
    n_free = n_blk * MOE_ROWS - n_assign
    gap_end = jnp.cumsum(padded - counts)
    j = jnp.arange(n_free, dtype=jnp.int32)
    seg = jnp.sum((j[:, None] >= gap_end[None, :]).astype(jnp.int32), axis=1)
    seg_first_slot = jnp.concatenate([pad_start + counts, pad_end[-1:]])
    seg_first_j = jnp.concatenate([jnp.zeros((1,), jnp.int32), gap_end])
    seg_hot = (seg[:, None] == jnp.arange(N_EXP + 1, dtype=jnp.int32)[None, :]).astype(jnp.int32)
    free_slots = (jnp.sum(seg_hot * (seg_first_slot - seg_first_j)[None, :], axis=1) + j).astype(jnp.int32)
    return dest, free_slots, blk_exp, n_used


def kernel(x_prompt, x_sample, cache_diff_k, cache_diff_v, cache_win_k, cache_win_v, state_ssd, state_ret,
           c, c_ctx, w_ada, b_ada, w_in, diff_lambda, diff_norm_g, win_sink, conv_w, conv_b,
           ssd_A_log, ssd_dt_bias, ssd_D, ssd_norm_g, ret_decay_logit, ret_norm_g, w_branch, w_out,
           ln_g, ln_b, router_w, router_b, moe_w_gate, moe_w_up, moe_w_down):
    batch, seq, d = x_prompt.shape
    dec_batch, dec_seq, _ = x_sample.shape
    past = cache_diff_k.shape[2]
    n_ctx, n_lat = batch * seq, dec_batch * dec_seq
    n_tok = n_ctx + n_lat
    assert d == D_MODEL and n_ctx % dec_seq == 0 and seq % CHUNK == 0 and dec_seq % CHUNK == 0

    x = jnp.concatenate([x_prompt.reshape(n_ctx, d), x_sample.reshape(n_lat, d)], axis=0)

    n_mod = 1 + dec_batch
    n_mod_pad = -(-n_mod // 8) * 8
    cvec = jnp.concatenate([c_ctx[None, :], c, jnp.zeros((n_mod_pad - n_mod, d), F32)], axis=0)
    mod = _ada(cvec, w_ada, b_ada)

    rope_a = _rope_tables(dec_seq, A_QK, 256)
    rope_b = _rope_tables(dec_seq, B_DIM, 256)
    rw_t = router_w.T
    rb_col = router_b.reshape(N_EXP, 1)
    ssd_h0 = _state_to_blocks(state_ssd)
    ret_h0 = _state_to_blocks(state_ret)

    ctx_out = {k: [] for k in ('diff_k', 'diff_v', 'win_k', 'win_v', 'ssd', 'ret')}
    for l in range(DEPTH):
        mod_l = mod[l, :n_mod].reshape(n_mod, 1, ADA_DIM)
        wl = w_in[l]
        w_small = jnp.concatenate(
            [wl[:, :CDT_OFF], wl[:, CDT_OFF + 8:GATE_OFF], wl[:, CDT_OFF:CDT_OFF + 8],
             jnp.zeros((d, N_SMALL - GATE_OFF), F32)], axis=1).astype(BF16)
        w_gl = wl[:, GATE_OFF:].astype(BF16)
        proj = _inproj(x, mod_l, w_small, n_ctx, dec_seq)

        lam_init = 0.8 - 0.6 * math.exp(-0.3 * l)
        lv = diff_lambda[l]
        lam = jnp.exp(jnp.sum(lv[0] * lv[1])) - jnp.exp(jnp.sum(lv[2] * lv[3])) + lam_init
        diff_scal = jnp.stack([lam, jnp.asarray(1.0 - lam_init, F32)]).astype(F32)
        g_a = diff_norm_g[l].reshape(1, A_V)
        sink = win_sink[l]
        dtb = jnp.zeros((1, 128), F32).at[0, :8].set(ssd_dt_bias[l].reshape(8))
        acoef = jnp.zeros((1, 128), F32).at[0, :8].set(-jnp.exp(ssd_A_log[l]).reshape(8))
        dvec = jnp.repeat(ssd_D[l], C_P).reshape(1, 256)
        g_c = ssd_norm_g[l].reshape(1, 256)
        log_g = jax.nn.log_sigmoid(ret_decay_logit[l]).reshape(8)
        g_d = ret_norm_g[l].reshape(1, 256)
        cw = conv_w[l]
        cb = conv_b[l].reshape(1, 512)

        oa_c = _diff_attn(proj, diff_scal, g_a, 0, batch, seq, seq)
        ob_c = _win_attn(proj, sink, 0, batch, seq, seq)
        yc_c, st_c = _ssd(proj, cw, cb, dtb, acoef, dvec, g_c, 0, batch, seq)
        od_c, rt_c = _ret(proj, log_g, g_d, 0, batch, seq)
        cache_a = (cache_diff_k[:, l].reshape(dec_batch, past, 256), cache_diff_v[:, l].reshape(dec_batch, past, 256))
        cache_b = (cache_win_k[:, l].reshape(dec_batch, past, 128), cache_win_v[:, l].reshape(dec_batch, past, 128))
        oa_l = _diff_attn(proj, diff_scal, g_a, n_ctx, dec_batch, dec_seq, DIFF_TQ, cache=cache_a, rope=rope_a)
        ob_l = _win_attn(proj, sink, n_ctx, dec_batch, dec_seq, BLOCK, cache=cache_b, rope=rope_b)
        yc_l, _ = _ssd(proj, cw, cb, dtb, acoef, dvec, g_c, n_ctx, dec_batch, dec_seq,
                       h0=ssd_h0[:, l])
        od_l, _ = _ret(proj, log_g, g_d, n_ctx, dec_batch, dec_seq, h0=ret_h0[:, l])

        x1, h2, ids, gates, rank, counts = _merge(
            x, mod_l, ((oa_c, oa_l), (ob_c, ob_l), (yc_c, yc_l), (od_c, od_l)), w_gl, w_branch[l].reshape(N_BRANCH * BRANCH, d).astype(BF16),
            w_out[l].astype(BF16), ln_g[l, 0].reshape(1, d), ln_b[l, 0].reshape(1, d), rw_t, rb_col,
            n_ctx, dec_seq)

        dest, free_slots, blk_exp, n_used = _dispatch_plan(ids, rank, counts, n_tok)
        x_slots = _moe_dispatch(h2, dest, free_slots, blk_exp.shape[0] * MOE_ROWS)
        y_slots = _moe_experts(x_slots, blk_exp, n_used, moe_w_gate, moe_w_up, moe_w_down, l)
        x = _final(x1, mod_l, y_slots, dest, gates.T,
                   ln_g[l, 1].reshape(1, d), ln_b[l, 1].reshape(1, d), n_ctx, dec_seq,
                   split=(l == DEPTH - 1))

        pc = proj[:n_ctx]
        ctx_out['diff_k'].append(pc[:, 256:512].reshape(batch, seq, A_HEADS, 2, A_QK))
        ctx_out['diff_v'].append(pc[:, 512:768].reshape(batch, seq, A_HEADS, A_V))
        ctx_out['win_k'].append(pc[:, 1024:1152].reshape(batch, seq, B_KV, B_DIM))
        ctx_out['win_v'].append(pc[:, 1152:1280].reshape(batch, seq, B_KV, B_DIM))
        ctx_out['ssd'].append(st_c)
        ctx_out['ret'].append(rt_c)

    y_prompt = x[0].reshape(batch, seq, d)
    y_sample = x[1].reshape(dec_batch, dec_seq, d)
    stk = lambda k: jnp.stack(ctx_out[k], axis=1)
    return (y_prompt, y_sample, stk('diff_k'), stk('diff_v'), stk('win_k'), stk('win_v'),
            _blocks_to_state(stk('ssd'), C_HEADS, C_P, C_N), _blocks_to_state(stk('ret'), D_HEADS, D_V, D_K))
```

```python
import functools
import math

import jax
import jax.numpy as jnp
from jax import lax
from jax.experimental import pallas as pl
from jax.experimental.pallas import tpu as pltpu

F32 = jnp.float32
BF16 = jnp.bfloat16

D_MODEL = 1024
DEPTH = 4
GRID_W = 64
BLOCK = 128
WINDOW = 128
CHUNK = 128
A_HEADS, A_QK, A_V = 4, 32, 64
B_HEADS, B_KV, B_DIM = 4, 2, 64
C_HEADS, C_P, C_GROUPS, C_N = 4, 64, 2, 64
D_HEADS, D_K, D_V = 4, 32, 64
BRANCH = 256
N_BRANCH = 4
N_EXP = 16
N_EXP_GROUPS = 4
EXP_PER_GROUP = 4
D_FF_EXP = 512
ROPE_BASE = 10000.0
ALPHA = (2 * DEPTH) ** 0.25
EPS = 1e-5
ADA_DIM = 6 * D_MODEL
NEG = -1e30

N_SMALL = 23 * 128
GATE_OFF = 2824
CDT_OFF = 2048

VMEM_LIMIT = 52 * 1024 * 1024
MOE_ROWS = 512
DENSE_TILE = 512
TOK_TILE = 512
DIFF_TQ = 256

_NN = (((1,), (0,)), ((), ()))
_NT = (((1,), (1,)), ((), ()))
_TN = (((0,), (0,)), ((), ()))


def _params(*sem):
    return pltpu.CompilerParams(dimension_semantics=sem, vmem_limit_bytes=VMEM_LIMIT)


def _mm(a, b, dims=_NN):
    return lax.dot_general(a, b, dims, preferred_element_type=F32)


def _split(a):
    hi = a.astype(BF16)
    return hi, (a - hi.astype(F32)).astype(BF16)


def _mm_f32(a, b, dims=_NN):
    a_hi, a_lo = _split(a)
    b_hi, b_lo = _split(b)
    return (_mm(a_lo, b_hi, dims) + _mm(a_hi, b_lo, dims)) + _mm(a_hi, b_hi, dims)


def _mm_exact_lhs(m_bf, a):
    a1 = a.astype(BF16)
    r1 = a - a1.astype(F32)
    a2 = r1.astype(BF16)
    a3 = (r1 - a2.astype(F32)).astype(BF16)
    return (_mm(m_bf, a3) + _mm(m_bf, a2)) + _mm(m_bf, a1)


def _mm_exact_rhs(a, m_bf):
    a1 = a.astype(BF16)
    r1 = a - a1.astype(F32)
    a2 = r1.astype(BF16)
    a3 = (r1 - a2.astype(F32)).astype(BF16)
    return (_mm(a3, m_bf) + _mm(a2, m_bf)) + _mm(a1, m_bf)


def _silu(x):
    return x * jax.nn.sigmoid(x)


LANES = 128
TOKEN_TILE_ROWS = D_MODEL // LANES


def _store_token_tiles(ref, val, lead=(), base=0):
    n = val.shape[0]
    for j in range(TOKEN_TILE_ROWS):
        ref[(*lead, pl.ds(base + j, n, stride=TOKEN_TILE_ROWS), slice(None))] = val[:, j * LANES:(j + 1) * LANES]


def _load_token_tiles(ref, n, lead=()):
    return jnp.concatenate(
        [ref[(*lead, pl.ds(j, n, stride=TOKEN_TILE_ROWS), slice(None))] for j in range(TOKEN_TILE_ROWS)], axis=1)


def _rope(x, c, s_lo, s_hi, shift):
    n = x.shape[1]
    return x * c + pltpu.roll(x, n - shift, 1) * s_lo + pltpu.roll(x, shift, 1) * s_hi


def _ada_kernel(c_ref, w_ref, b_ref, o_ref):
    c = c_ref[...]
    o_ref[...] = _mm_f32(_silu(c), w_ref[...]) + b_ref[...]


def _ada(cvec, w_ada, b_ada):
    rows = cvec.shape[0]
    tn = 1024
    return pl.pallas_call(
        _ada_kernel,
        out_shape=jax.ShapeDtypeStruct((DEPTH, rows, ADA_DIM), F32),
        grid=(DEPTH, ADA_DIM // tn),
        in_specs=[
            pl.BlockSpec((rows, D_MODEL), lambda l, j: (0, 0)),
            pl.BlockSpec((None, D_MODEL, tn), lambda l, j: (l, 0, j)),
            pl.BlockSpec((None, 1, tn), lambda l, j: (l, 0, j)),
        ],
        out_specs=pl.BlockSpec((None, rows, tn), lambda l, j: (l, 0, j)),
        compiler_params=_params("arbitrary", "arbitrary"),
        name="ada",
    )(cvec, w_ada, b_ada.reshape(DEPTH, 1, ADA_DIM))


def _inproj_kernel(x_ref, mod_ref, w_ref, o_ref):
    d = x_ref.shape[1]
    h = x_ref[...] * (1.0 + mod_ref[:, d:2 * d]) + mod_ref[:, 0:d]
    o_ref[...] = _mm(h.astype(BF16), w_ref[...])


def _mod_group(i, tm, n_ctx, dec_seq):
    row = i * tm
    return jnp.where(row < n_ctx, 0, 1 + lax.div(jnp.maximum(row - n_ctx, 0), dec_seq))


def _inproj(x, mod_l, w_small, n_ctx, dec_seq):
    n_tok = x.shape[0]
    tm = DENSE_TILE
    grp = functools.partial(_mod_group, tm=tm, n_ctx=n_ctx, dec_seq=dec_seq)
    return pl.pallas_call(
        _inproj_kernel,
        out_shape=jax.ShapeDtypeStruct((n_tok, N_SMALL), F32),
        grid=(n_tok // tm,),
        in_specs=[
            pl.BlockSpec((tm, D_MODEL), lambda i: (i, 0)),
            pl.BlockSpec((None, 1, ADA_DIM), lambda i: (grp(i), 0, 0)),
            pl.BlockSpec((D_MODEL, N_SMALL), lambda i: (0, 0)),
        ],
        out_specs=pl.BlockSpec((tm, N_SMALL), lambda i: (i, 0)),
        compiler_params=_params("arbitrary"),
        name="inproj",
    )(x, mod_l, w_small)


def _diff_attn_kernel(*refs, latent, tq, seq, past):
    if latent:
        (sc_ref, q_ref, k_ref, v_ref, g_ref, ck_ref, cv_ref, rc_ref, rlo_ref, rhi_ref,
         o_ref, kt_scr, v_scr) = refs
    else:
        sc_ref, q_ref, k_ref, v_ref, g_ref, o_ref, kt_scr, v_scr = refs
    qi = pl.program_id(1)
    shift = A_QK // 4

    @pl.when(qi == 0)
    def _():
        k = k_ref[...]
        if latent:
            k = _rope(k, rc_ref[...], rlo_ref[...], rhi_ref[...], shift)
        kt_scr[:, 0:seq] = k.T.astype(BF16)
        if latent:
            kt_scr[:, seq:seq + past] = ck_ref[...].T.astype(BF16)
        ones = jnp.ones((seq + past, LANES - A_V), BF16)
        for h in range(A_HEADS):
            hs = slice(h * A_V, (h + 1) * A_V)
            v_scr[0:seq, h * LANES:h * LANES + A_V] = v_ref[:, hs].astype(BF16)
            if latent:
                v_scr[seq:seq + past, h * LANES:h * LANES + A_V] = cv_ref[:, hs].astype(BF16)
            v_scr[:, h * LANES + A_V:(h + 1) * LANES] = ones

    q = q_ref[...]
    if latent:
        r = pl.ds(pl.multiple_of(qi * tq, tq), tq)
        q = _rope(q, rc_ref[r, :], rlo_ref[r, :], rhi_ref[r, :], shift)
    qb = q.astype(BF16)
    lam = sc_ref[0]
    post = sc_ref[1]
    c = (A_QK ** -0.5) * math.log2(math.e)
    outs = []
    for h in range(A_HEADS):
        probs = []
        for m in range(2):
            off = (h * 2 + m) * A_QK
            s = _mm(qb[:, off:off + A_QK], kt_scr[off:off + A_QK, :])
            probs.append(jnp.exp2(s * c - jnp.max(s, axis=1, keepdims=True) * c).astype(BF16))
        ov = _mm(jnp.concatenate(probs, axis=0), v_scr[:, h * LANES:(h + 1) * LANES])
        maps = [ov[m * tq:(m + 1) * tq, 0:A_V] * (1.0 / ov[m * tq:(m + 1) * tq, A_V:A_V + 1]) for m in range(2)]
        o = maps[0] - lam * maps[1]
        n = o * lax.rsqrt(jnp.mean(o * o, axis=1, keepdims=True) + EPS)
        outs.append((n * g_ref[...]) * post)
    o_ref[...] = jnp.concatenate(outs, axis=1).astype(BF16)


def _diff_attn(proj, scal, g, row0, nb, seq, tq, cache=None, rope=None):
    latent = cache is not None
    nq = seq // tq
    rb = row0 // seq
    qb0 = row0 // tq
    past = cache[0].shape[1] if latent else 0
    in_specs = [
        pl.BlockSpec(memory_space=pltpu.SMEM),
        pl.BlockSpec((tq, 256), lambda b, i: (qb0 + b * nq + i, 0)),
        pl.BlockSpec((seq, 256), lambda b, i: (rb + b, 1)),
        pl.BlockSpec((seq, 256), lambda b, i: (rb + b, 2)),
        pl.BlockSpec((1, A_V), lambda b, i: (0, 0)),
    ]
    args = [scal, proj, proj, proj, g]
    if latent:
        in_specs += [
            pl.BlockSpec((None, past, 256), lambda b, i: (b, 0, 0)),
            pl.BlockSpec((None, past, 256), lambda b, i: (b, 0, 0)),
        ] + [pl.BlockSpec((seq, 256), lambda b, i: (0, 0))] * 3
        args += [cache[0], cache[1], *rope]
    return pl.pallas_call(
        functools.partial(_diff_attn_kernel, latent=latent, tq=tq, seq=seq, past=past),
        out_shape=jax.ShapeDtypeStruct((nb * seq, BRANCH), BF16),
        grid=(nb, nq),
        in_specs=in_specs,
        out_specs=pl.BlockSpec((tq, BRANCH), lambda b, i: (b * nq + i, 0)),
        scratch_shapes=[pltpu.VMEM((256, seq + past), BF16),
                        pltpu.VMEM((seq + past, A_HEADS * LANES), BF16)],
        compiler_params=_params("arbitrary", "arbitrary"),
        name="diff_attn_lat" if latent else "diff_attn_ctx",
    )(*args)


def _win_attn_kernel(*refs, latent, tq, seq):
    if latent:
        (sink_ref, q_ref, k_ref, v_ref, ck_ref, cv_ref, rc_ref, rlo_ref, rhi_ref,
         o_ref, k_scr, v_scr, ck_scr, cv_scr) = refs
    else:
        sink_ref, q_ref, k_ref, v_ref, o_ref, k_scr, v_scr = refs
    qi = pl.program_id(1)
    nq = seq // tq
    shift = B_DIM // 4
    kvw = B_KV * B_DIM

    @pl.when(qi == 0)
    def _():
        k = k_ref[...]
        if latent:
            k = _rope(k, rc_ref[:, 0:kvw], rlo_ref[:, 0:kvw], rhi_ref[:, 0:kvw], shift)
            ck_scr[...] = ck_ref[...].T.astype(BF16)
            cv_scr[...] = cv_ref[...].astype(BF16)
        kt = k.T.astype(BF16)
        for j in range(nq):
            k_scr[j] = kt[:, j * tq:(j + 1) * tq]
        v_scr[...] = v_ref[...].astype(BF16)

    q = q_ref[...]
    if latent:
        r = pl.ds(pl.multiple_of(qi * tq, tq), tq)
        q = _rope(q, rc_ref[r, :], rlo_ref[r, :], rhi_ref[r, :], shift)
        near = (jnp.maximum(qi - 1, 0), qi, jnp.minimum(qi + 1, nq - 1))
        kl = jnp.concatenate([k_scr[j] for j in near], axis=1)
        vl = jnp.concatenate([v_scr[pl.ds(pl.multiple_of(j * tq, tq), tq), :] for j in near], axis=0)
    qb = q.astype(BF16)
    scale = B_DIM ** -0.5
    ratio = B_HEADS // B_KV
    rows = ratio * tq
    if latent:
        ii = lax.broadcasted_iota(jnp.int32, (rows, 3 * tq), 0) & (tq - 1)
        jj = lax.broadcasted_iota(jnp.int32, (rows, 3 * tq), 1)
        lo = jnp.where(qi > 0, 0, tq)
        hi = jnp.where(qi < nq - 1, 3 * tq, 2 * tq)
        valid = (jnp.abs(jj - tq - ii) <= WINDOW) & (jj >= lo) & (jj < hi)
    else:
        kl = k_scr[0]
        vl = v_scr[...]
    row_id = lax.broadcasted_iota(jnp.int32, (rows, 1), 0)
    head_of_row = (row_id - (row_id & (tq - 1))) // tq if ratio > 2 else (row_id >= tq).astype(jnp.int32)
    outs = []
    for g in range(B_KV):
        gsl = slice(g * B_DIM, (g + 1) * B_DIM)
        heads = range(g * ratio, (g + 1) * ratio)
        qg = jnp.concatenate([qb[:, h * B_DIM:(h + 1) * B_DIM] for h in heads], axis=0)
        snk = jnp.full((rows, 1), sink_ref[heads[0]], F32)
        for n, h in enumerate(heads[1:], start=1):
            snk = jnp.where(head_of_row == n, sink_ref[h], snk)
        s = _mm(qg, kl[gsl, :]) * scale
        if latent:
            s = jnp.where(valid, s, NEG)
            sc = _mm(qg, ck_scr[gsl, :]) * scale
            m = jnp.maximum(jnp.maximum(jnp.max(s, axis=1, keepdims=True),
                                        jnp.max(sc, axis=1, keepdims=True)), snk)
            pc = jnp.exp(sc - m)
        else:
            m = jnp.maximum(jnp.max(s, axis=1, keepdims=True), snk)
        p = jnp.exp(s - m)
        den = jnp.sum(p, axis=1, keepdims=True) + jnp.exp(snk - m)
        if latent:
            den = den + jnp.sum(pc, axis=1, keepdims=True)
        inv = 1.0 / den
        o = _mm((p * inv).astype(BF16), vl[:, gsl])
        if latent:
            o = o + _mm((pc * inv).astype(BF16), cv_scr[:, gsl])
        outs += [o[n * tq:(n + 1) * tq, :] for n in range(ratio)]
    o_ref[...] = jnp.concatenate(outs, axis=1).astype(BF16)


def _win_attn(proj, sink, row0, nb, seq, tq, cache=None, rope=None):
    latent = cache is not None
    nq = seq // tq
    rb = row0 // seq
    qb0 = row0 // tq
    kvw = B_KV * B_DIM
    in_specs = [
        pl.BlockSpec(memory_space=pltpu.SMEM),
        pl.BlockSpec((tq, 256), lambda b, i: (qb0 + b * nq + i, 3)),
        pl.BlockSpec((seq, kvw), lambda b, i: (rb + b, 8)),
        pl.BlockSpec((seq, kvw), lambda b, i: (rb + b, 9)),
    ]
    args = [sink, proj, proj, proj]
    scratch = [pltpu.VMEM((nq, kvw, tq), BF16), pltpu.VMEM((seq, kvw), BF16)]
    if latent:
        past = cache[0].shape[1]
        in_specs += [
            pl.BlockSpec((None, past, kvw), lambda b, i: (b, 0, 0)),
            pl.BlockSpec((None, past, kvw), lambda b, i: (b, 0, 0)),
        ] + [pl.BlockSpec((seq, 256), lambda b, i: (0, 0))] * 3
        args += [cache[0], cache[1], *rope]
        scratch += [pltpu.VMEM((kvw, past), BF16), pltpu.VMEM((past, kvw), BF16)]
    return pl.pallas_call(
        functools.partial(_win_attn_kernel, latent=latent, tq=tq, seq=seq),
        out_shape=jax.ShapeDtypeStruct((nb * seq, BRANCH), BF16),
        grid=(nb, nq),
        in_specs=in_specs,
        out_specs=pl.BlockSpec((tq, BRANCH), lambda b, i: (b * nq + i, 0)),
        scratch_shapes=scratch,
        compiler_params=_params("arbitrary", "arbitrary"),
        name="win_attn_lat" if latent else "win_attn_ctx",
    )(*args)


def _conv_silu(u, w, b):
    n = u.shape[0]
    rows = lax.broadcasted_iota(jnp.int32, u.shape, 0)
    up = jnp.where(rows == 0, 0.0, pltpu.roll(u, 1, 0))
    un = jnp.where(rows == n - 1, 0.0, pltpu.roll(u, n - 1, 0))
    return _silu(up * w[0:1, :] + u * w[1:2, :] + un * w[2:3, :] + b)


def _ssd_kernel(*refs, seq, has_h0):
    if has_h0:
        (cx_ref, cz_ref, cbc_ref, cdt_ref, cw_ref, cb_ref, dtb_ref, ac_ref, dv_ref, g_ref, h0_ref,
         y_ref, st_ref, xs, bcs, dts, ybuf, ybuf_b) = refs
    else:
        (cx_ref, cz_ref, cbc_ref, cdt_ref, cw_ref, cb_ref, dtb_ref, ac_ref, dv_ref, g_ref,
         y_ref, st_ref, xs, bcs, dts, ybuf, ybuf_b) = refs
    nc = seq // CHUNK
    xw = C_HEADS * C_P
    xs[...] = _conv_silu(cx_ref[...], cw_ref[:, 0:xw], cb_ref[:, 0:xw])
    bcs[...] = _conv_silu(cbc_ref[...], cw_ref[:, xw:2 * xw], cb_ref[:, xw:2 * xw])
    z = cdt_ref[...] + dtb_ref[...]
    dts[...] = jnp.maximum(z, 0.0) + jnp.log1p(jnp.exp(-jnp.abs(z)))
    if has_h0:
        st_ref[...] = h0_ref[...]
    else:
        st_ref[...] = jnp.zeros(st_ref.shape, F32)

    ri = lax.broadcasted_iota(jnp.int32, (CHUNK, CHUNK), 0)
    ci = lax.broadcasted_iota(jnp.int32, (CHUNK, CHUNK), 1)
    gw = C_GROUPS * C_N
    per_group = C_HEADS // C_GROUPS

    def same_block(shape, row_block, col_block):
        rows = lax.shift_right_logical(lax.broadcasted_iota(jnp.int32, shape, 0), row_block.bit_length() - 1)
        cols = lax.shift_right_logical(lax.broadcasted_iota(jnp.int32, shape, 1), col_block.bit_length() - 1)
        return rows == cols

    b_on_diag = same_block((gw, C_GROUPS * CHUNK), C_N, CHUNK)
    x_on_diag = same_block((C_HEADS * CHUNK, xw), CHUNK, C_P)
    st_on_diag = same_block((C_HEADS * C_N, xw), C_N, C_P)
    zero = jnp.zeros((), BF16)

    def per_head(group_cols):
        w = group_cols.shape[1] // C_GROUPS
        return jnp.concatenate([group_cols[:, (h // per_group) * w:(h // per_group + 1) * w]
                                for h in range(C_HEADS)], axis=1)

    def chunk(c, d):
        r = pl.ds(pl.multiple_of(c * CHUNK, CHUNK), CHUNK)
        tri = (ri >= ci) if d == 0 else (ci >= ri)
        dt = dts[r, :]
        cs = _mm_exact_lhs(tri.astype(BF16), dt * ac_ref[...])
        cst = cs.T
        bc = bcs[r, :]
        decs, e_ins, e_outs, e_tots, dt_cols = [], [], [], [], []
        for h in range(C_HEADS):
            col = d * C_HEADS + h
            cc = jnp.broadcast_to(cs[:, col:col + 1], (CHUNK, CHUNK))
            tot = cc[CHUNK - 1:CHUNK, :] if d == 0 else cc[0:1, :]
            decs.append(jnp.exp(jnp.where(tri, cc - cst[col:col + 1, :], NEG)))
            e_ins.append(jnp.exp(cc)[:, 0:C_P])
            e_outs.append(jnp.exp(tot - cc)[:, 0:C_N])
            e_tots.append(jnp.broadcast_to(jnp.exp(jnp.concatenate([tot] * (xw // CHUNK), axis=1)), (C_N, xw)))
            dt_cols.append(jnp.broadcast_to(dt[:, col:col + 1], (CHUNK, C_P)))
        xb = (xs[r, :] * jnp.concatenate(dt_cols, axis=1)).astype(BF16)
        bmat = bc[:, 0:gw]
        cb = bc[:, gw:2 * gw].astype(BF16)
        b_bd = jnp.where(b_on_diag, jnp.concatenate([bmat.astype(BF16).T] * C_GROUPS, axis=1), zero)
        gram = per_head(_mm(cb, b_bd))
        x_bd = jnp.where(x_on_diag, jnp.concatenate([xb] * C_HEADS, axis=0), zero)
        state = st_ref[d]
        y = _mm((gram * jnp.concatenate(decs, axis=1)).astype(BF16), x_bd)
        y = y + _mm(per_head(cb), state.astype(BF16)) * jnp.concatenate(e_ins, axis=1)
        bd = (per_head(bmat) * jnp.concatenate(e_outs, axis=1)).astype(BF16)
        cross = _mm(bd, xb, _TN)
        st_ref[d] = state * jnp.concatenate(e_tots, axis=0) + jnp.where(st_on_diag, cross, 0.0)
        return r, y

    def scan(t, carry):
        r, y = chunk(t, 0)
        ybuf[r, :] = y
        r, y = chunk(nc - 1 - t, 1)
        ybuf_b[r, :] = y
        return carry

    lax.fori_loop(0, nc, scan, 0)

    def finish(c, carry):
        r = pl.ds(pl.multiple_of(c * CHUNK, CHUNK), CHUNK)
        y = (ybuf[r, :] + ybuf_b[r, :]) + xs[r, :] * dv_ref[...]
        y = y * _silu(cz_ref[r, :])
        gl = xw // C_GROUPS
        parts = []
        for g in range(C_GROUPS):
            seg = y[:, g * gl:(g + 1) * gl]
            parts.append(seg * lax.rsqrt(jnp.mean(seg * seg, axis=1, keepdims=True) + EPS))
        y_ref[r, :] = (jnp.concatenate(parts, axis=1) * g_ref[...]).astype(BF16)
        return carry

    lax.fori_loop(0, nc, finish, 0)


def _ssd(proj, conv_w, conv_b, dtb, acoef, dvec, g, row0, nb, seq, h0=None):
    has_h0 = h0 is not None
    rb = row0 // seq
    st_shape = (2, C_HEADS * C_N, C_HEADS * C_P)
    in_specs = [
        pl.BlockSpec((seq, 256), lambda b: (rb + b, 5)),
        pl.BlockSpec((seq, 256), lambda b: (rb + b, 6)),
        pl.BlockSpec((seq, 256), lambda b: (rb + b, 7)),
        pl.BlockSpec((seq, 128), lambda b: (rb + b, 22)),
        pl.BlockSpec((3, 512), lambda b: (0, 0)),
        pl.BlockSpec((1, 512), lambda b: (0, 0)),
        pl.BlockSpec((1, 128), lambda b: (0, 0)),
        pl.BlockSpec((1, 128), lambda b: (0, 0)),
        pl.BlockSpec((1, 256), lambda b: (0, 0)),
        pl.BlockSpec((1, 256), lambda b: (0, 0)),
    ]
    args = [proj, proj, proj, proj, conv_w, conv_b, dtb, acoef, dvec, g]
    if has_h0:
        in_specs.append(pl.BlockSpec((None,) + st_shape, lambda b: (b, 0, 0, 0)))
        args.append(h0)
    return pl.pallas_call(
        functools.partial(_ssd_kernel, seq=seq, has_h0=has_h0),
        out_shape=(jax.ShapeDtypeStruct((nb * seq, BRANCH), BF16),
                   jax.ShapeDtypeStruct((nb,) + st_shape, F32)),
        grid=(nb,),
        in_specs=in_specs,
        out_specs=(pl.BlockSpec((seq, BRANCH), lambda b: (b, 0)),
                   pl.BlockSpec((None,) + st_shape, lambda b: (b, 0, 0, 0))),
        scratch_shapes=[pltpu.VMEM((seq, 256), F32), pltpu.VMEM((seq, 256), F32),
                        pltpu.VMEM((seq, 128), F32), pltpu.VMEM((seq, 256), F32),
                        pltpu.VMEM((seq, 256), F32)],
        compiler_params=_params("arbitrary"),
        name="ssd_lat" if has_h0 else "ssd_ctx",
    )(*args)


def _ret_kernel(*refs, seq, has_h0):
    if has_h0:
        (lg_ref, q_ref, k_ref, v_ref, gt_ref, g_ref, h0_ref, y_ref, st_ref,
         ybuf, ybuf_b, dec_scr, ein_scr, eout_scr, etot_scr) = refs
    else:
        (lg_ref, q_ref, k_ref, v_ref, gt_ref, g_ref, y_ref, st_ref,
         ybuf, ybuf_b, dec_scr, ein_scr, eout_scr, etot_scr) = refs
    nc = seq // CHUNK
    kw, vw = D_HEADS * D_K, D_HEADS * D_V
    if has_h0:
        st_ref[...] = h0_ref[...]
    else:
        st_ref[...] = jnp.zeros(st_ref.shape, F32)
    ri = lax.broadcasted_iota(jnp.int32, (CHUNK, CHUNK), 0)
    ci = lax.broadcasted_iota(jnp.int32, (CHUNK, CHUNK), 1)
    pos_v = lax.broadcasted_iota(jnp.int32, (CHUNK, D_V), 0).astype(F32)
    pos_k = lax.broadcasted_iota(jnp.int32, (CHUNK, D_K), 0).astype(F32)
    def same_block(shape, row_block, col_block):
        rows = lax.shift_right_logical(lax.broadcasted_iota(jnp.int32, shape, 0), row_block.bit_length() - 1)
        cols = lax.shift_right_logical(lax.broadcasted_iota(jnp.int32, shape, 1), col_block.bit_length() - 1)
        return rows == cols

    k_on_diag = same_block((kw, D_HEADS * CHUNK), D_K, CHUNK)
    v_on_diag = same_block((D_HEADS * CHUNK, vw), CHUNK, D_V)
    on_diag = same_block((kw, vw), D_K, D_V)
    head_avg = jnp.where(same_block((vw, vw), D_V, D_V), 1.0 / D_V, 0.0).astype(BF16)
    kscale = D_K ** -0.5
    for d in range(2):
        if d == 0:
            tri, dist = ri >= ci, (ri - ci).astype(F32)
            steps_in, steps_out = pos_v + 1.0, (CHUNK - 1.0) - pos_k
        else:
            tri, dist = ci >= ri, (ci - ri).astype(F32)
            steps_in, steps_out = CHUNK - pos_v, pos_k
        for h in range(D_HEADS):
            lg = lg_ref[d * D_HEADS + h]
            dec_scr[d, :, h * CHUNK:(h + 1) * CHUNK] = jnp.exp(jnp.where(tri, dist * lg, NEG))
            ein_scr[d, :, h * D_V:(h + 1) * D_V] = jnp.exp(steps_in * lg)
            eout_scr[d, :, h * D_K:(h + 1) * D_K] = jnp.exp(steps_out * lg)
            etot_scr[d, h * D_K:(h + 1) * D_K, :] = jnp.exp(jnp.full((D_K, vw), CHUNK * 1.0, F32) * lg)

    def chunk(c, d):
        r = pl.ds(pl.multiple_of(c * CHUNK, CHUNK), CHUNK)
        qb = q_ref[r, :].astype(BF16)
        k = k_ref[r, :] * kscale
        kb = k.astype(BF16)
        vb = v_ref[r, :].astype(BF16)
        state = st_ref[d]
        zero = jnp.zeros((), BF16)
        k_bd = jnp.where(k_on_diag, jnp.concatenate([kb.T] * D_HEADS, axis=1), zero)
        gram = _mm(qb, k_bd)
        v_bd = jnp.where(v_on_diag, jnp.concatenate([vb] * D_HEADS, axis=0), zero)
        y = _mm((gram * dec_scr[d]).astype(BF16), v_bd) + _mm(qb, state.astype(BF16)) * ein_scr[d]
        cross = _mm((k * eout_scr[d]).astype(BF16), vb, _TN)
        st_ref[d] = state * etot_scr[d] + jnp.where(on_diag, cross, 0.0)
        return r, y

    def scan(t, carry):
        r, y = chunk(t, 0)
        ybuf[r, :] = y
        r, y = chunk(nc - 1 - t, 1)
        ybuf_b[r, :] = y
        return carry

    lax.fori_loop(0, nc, scan, 0)

    def finish(c, carry):
        r = pl.ds(pl.multiple_of(c * CHUNK, CHUNK), CHUNK)
        yf = ybuf[r, :] + ybuf_b[r, :]
        o = yf - _mm_exact_rhs(yf, head_avg)
        y = o * lax.rsqrt(_mm_exact_rhs(o * o, head_avg) + EPS)
        y = (y * g_ref[...]) * _silu(gt_ref[r, :])
        y_ref[r, :] = y.astype(BF16)
        return carry

    lax.fori_loop(0, nc, finish, 0)


def _state_to_blocks(state):
    b, _, h, p, n = state.shape
    eye = jnp.eye(h, dtype=state.dtype)
    return jnp.einsum('bdhpn,hg->bdhngp', state, eye).reshape(b, 2, h * n, h * p)


def _blocks_to_state(blocks, h, p, n):
    b = blocks.shape[0]
    eye = jnp.eye(h, dtype=blocks.dtype)
    return jnp.einsum('bdhngp,hg->bdhpn', blocks.reshape(b, 2, h, n, h, p), eye)


def _ret(proj, log_g, g, row0, nb, seq, h0=None):
    has_h0 = h0 is not None
    rb = row0 // seq
    nst = 2 * D_HEADS
    kw, vw = D_HEADS * D_K, D_HEADS * D_V
    in_specs = [
        pl.BlockSpec(memory_space=pltpu.SMEM),
        pl.BlockSpec((seq, 128), lambda b: (rb + b, 16)),
        pl.BlockSpec((seq, 128), lambda b: (rb + b, 17)),
        pl.BlockSpec((seq, 256), lambda b: (rb + b, 9)),
        pl.BlockSpec((seq, 256), lambda b: (rb + b, 10)),
        pl.BlockSpec((1, 256), lambda b: (0, 0)),
    ]
    args = [log_g, proj, proj, proj, proj, g]
    if has_h0:
        in_specs.append(pl.BlockSpec((None, 2, kw, vw), lambda b: (b, 0, 0, 0)))
        args.append(h0)
    return pl.pallas_call(
        functools.partial(_ret_kernel, seq=seq, has_h0=has_h0),
        out_shape=(jax.ShapeDtypeStruct((nb * seq, BRANCH), BF16),
                   jax.ShapeDtypeStruct((nb, 2, kw, vw), F32)),
        grid=(nb,),
        in_specs=in_specs,
        out_specs=(pl.BlockSpec((seq, BRANCH), lambda b: (b, 0)),
                   pl.BlockSpec((None, 2, kw, vw), lambda b: (b, 0, 0, 0))),
        scratch_shapes=[pltpu.VMEM((seq, 256), F32), pltpu.VMEM((seq, 256), F32),
                        pltpu.VMEM((2, CHUNK, D_HEADS * CHUNK), F32), pltpu.VMEM((2, CHUNK, vw), F32),
                        pltpu.VMEM((2, CHUNK, kw), F32), pltpu.VMEM((2, kw, vw), F32)],
        compiler_params=_params("arbitrary"),
        name="ret_lat" if has_h0 else "ret_ctx",
    )(*args)


def _route(sel, s):
    row = lambda a, e: a[e:e + 1, :]
    best = None
    grp = None
    for g in range(N_EXP_GROUPS):
        vals = [row(sel, g * EXP_PER_GROUP + j) for j in range(EXP_PER_GROUP)]
        score = None
        for a in range(EXP_PER_GROUP):
            for b in range(a + 1, EXP_PER_GROUP):
                pair = vals[a] + vals[b]
                score = pair if score is None else jnp.maximum(score, pair)
        if best is None:
            best, grp = score, jnp.zeros(score.shape, jnp.int32)
        else:
            better = score > best
            best = jnp.where(better, score, best)
            grp = jnp.where(better, g, grp)

    def pick(a, j):
        out = row(a, j)
        for g in range(1, N_EXP_GROUPS):
            out = jnp.where(grp == g, row(a, g * EXP_PER_GROUP + j), out)
        return out

    cand = [pick(sel, j) for j in range(EXP_PER_GROUP)]
    aff = [pick(s, j) for j in range(EXP_PER_GROUP)]

    def arg_first_max(vals):
        top, idx = vals[0], jnp.zeros(vals[0].shape, jnp.int32)
        for j in range(1, len(vals)):
            better = vals[j] > top
            top = jnp.where(better, vals[j], top)
            idx = jnp.where(better, j, idx)
        return idx

    def take(vals, idx):
        out = vals[0]
        for j in range(1, len(vals)):
            out = jnp.where(idx == j, vals[j], out)
        return out

    i1 = arg_first_max(cand)
    i2 = arg_first_max([jnp.where(i1 == j, -jnp.inf, cand[j]) for j in range(EXP_PER_GROUP)])
    w1, w2 = take(aff, i1), take(aff, i2)
    tot = w1 + w2
    ids = jnp.concatenate([grp * EXP_PER_GROUP + i1, grp * EXP_PER_GROUP + i2], axis=0)
    gates = jnp.concatenate([w1 / tot, w2 / tot], axis=0)
    return ids, gates


def _merge_kernel(*refs, n_ctx_tiles):
    (x_ref, mod_ref, wgl_ref, wbr_ref, wout_ref, lng_ref, lnb_ref, rw_ref, rb_ref, tri_ref) = refs[:10]
    br_refs = refs[10:10 + 2 * N_BRANCH]
    x1_ref, h2_ref, ids_ref, gates_ref, rank_ref, cnt_ref = refs[10 + 2 * N_BRANCH:]
    d = D_MODEL

    @pl.when(pl.program_id(0) == 0)
    def _():
        cnt_ref[...] = jnp.zeros(cnt_ref.shape, F32)

    is_ctx = pl.program_id(0) < n_ctx_tiles
    sub = tri_ref.shape[0]
    for part in range(x_ref.shape[0] // sub):
        rs = slice(part * sub, (part + 1) * sub)
        x = x_ref[rs, :]
        hb = (x * (1.0 + mod_ref[:, d:2 * d]) + mod_ref[:, 0:d]).astype(BF16)
        merged = None
        for k in range(N_BRANCH):
            gate = jax.nn.sigmoid(_mm(hb, wgl_ref[:, k * d:(k + 1) * d]))
            br_k = jnp.where(is_ctx, br_refs[2 * k][rs, :], br_refs[2 * k + 1][rs, :])
            up = _mm(br_k, wbr_ref[k * BRANCH:(k + 1) * BRANCH, :])
            merged = gate * up if merged is None else merged + gate * up
        mix = _mm(merged.astype(BF16), wout_ref[...])
        y = ALPHA * x + mod_ref[:, 2 * d:3 * d] * mix
        y = y - jnp.mean(y, axis=1, keepdims=True)
        x1 = (y * lax.rsqrt(jnp.mean(y * y, axis=1, keepdims=True) + EPS)) * lng_ref[...] + lnb_ref[...]
        x1_ref[rs, :] = x1
        h2 = x1 * (1.0 + mod_ref[:, 4 * d:5 * d]) + mod_ref[:, 3 * d:4 * d]
        _store_token_tiles(h2_ref, h2, base=part * sub * TOKEN_TILE_ROWS)
        s = jax.nn.sigmoid(_mm_f32(rw_ref[...], h2, _NT))
        ids, gates = _route(s + rb_ref[...], s)
        ids_ref[:, rs] = ids
        gates_ref[:, rs] = gates
        expert = lax.broadcasted_iota(jnp.int32, (N_EXP, sub), 0)
        hot = [(expert == ids[k:k + 1, :]).astype(F32) for k in range(2)]
        both = hot[0] + hot[1]
        incl = _mm(both.astype(BF16), tri_ref[...])
        before = cnt_ref[...] + (incl - both)
        rank_ref[:, rs] = jnp.concatenate(
            [jnp.sum(hk * before, axis=0, keepdims=True) for hk in hot], axis=0).astype(jnp.int32)
        cnt_ref[...] = cnt_ref[...] + incl[:, sub - 1:sub]


def _merge(x, mod_l, branches, w_gl, w_br, w_out, ln_g, ln_b, rw_t, rb, n_ctx, dec_seq):
    n_tok = x.shape[0]
    tm = DENSE_TILE
    nct = n_ctx // tm
    grp = functools.partial(_mod_group, tm=tm, n_ctx=n_ctx, dec_seq=dec_seq)
    full = lambda shape: pl.BlockSpec(shape, lambda i: (0,) * len(shape))
    pos = jnp.arange(tm)
    tri = (pos[:, None] <= pos[None, :]).astype(BF16)
    br_specs = [pl.BlockSpec((tm, BRANCH), lambda i: (jnp.minimum(i, nct - 1), 0)),
                pl.BlockSpec((tm, BRANCH), lambda i: (jnp.maximum(i - nct, 0), 0))] * N_BRANCH
    br_args = [a for pair in branches for a in pair]
    return pl.pallas_call(
        functools.partial(_merge_kernel, n_ctx_tiles=nct),
        out_shape=(jax.ShapeDtypeStruct((n_tok, D_MODEL), F32),
                   jax.ShapeDtypeStruct((n_tok * TOKEN_TILE_ROWS, LANES), F32),
                   jax.ShapeDtypeStruct((2, n_tok), jnp.int32),
                   jax.ShapeDtypeStruct((2, n_tok), F32),
                   jax.ShapeDtypeStruct((2, n_tok), jnp.int32),
                   jax.ShapeDtypeStruct((N_EXP, 1), F32)),
        grid=(n_tok // tm,),
        in_specs=[
            pl.BlockSpec((tm, D_MODEL), lambda i: (i, 0)),
            pl.BlockSpec((None, 1, ADA_DIM), lambda i: (grp(i), 0, 0)),
            full((D_MODEL, N_BRANCH * D_MODEL)),
            full((N_BRANCH * BRANCH, D_MODEL)),
            full((D_MODEL, D_MODEL)),
            full((1, D_MODEL)),
            full((1, D_MODEL)),
            full((N_EXP, D_MODEL)),
            full((N_EXP, 1)),
            full((tm, tm)),
        ] + br_specs,
        out_specs=(pl.BlockSpec((tm, D_MODEL), lambda i: (i, 0)),
                   pl.BlockSpec((tm * TOKEN_TILE_ROWS, LANES), lambda i: (i, 0)),
                   pl.BlockSpec((2, tm), lambda i: (0, i)),
                   pl.BlockSpec((2, tm), lambda i: (0, i)),
                   pl.BlockSpec((2, tm), lambda i: (0, i)),
                   full((N_EXP, 1))),
        compiler_params=_params("arbitrary"),
        name="merge",
    )(x, mod_l, w_gl, w_br, w_out, ln_g, ln_b, rw_t, rb, tri, *br_args)


def _dispatch_kernel(dest_ref, pad_ref, h_hbm, x_hbm, hbuf, zbuf, in_sem, out_sem, pad_sem, *, n_tok):
    tr = TOKEN_TILE_ROWS
    rows = TOK_TILE * tr
    i = pl.program_id(0)
    last = pl.num_programs(0) - 1
    slot = lax.rem(i, 2)
    other = 1 - slot

    def load(tile, buf):
        return pltpu.make_async_copy(h_hbm.at[pl.ds(tile * rows, rows), :], hbuf.at[buf], in_sem.at[buf])

    def slot_copy(r, dst, buf):
        return pltpu.make_async_copy(hbuf.at[buf, pl.ds(r * tr, tr), :],
                                     x_hbm.at[pl.ds(dst * tr, tr), :], out_sem.at[buf])

    def drain(buf):
        for r in range(2 * TOK_TILE):
            slot_copy(0, 0, buf).wait()

    @pl.when(i == 0)
    def _():
        load(0, 0).start()

    load(i, slot).wait()

    @pl.when(i < last)
    def _():
        @pl.when(i >= 1)
        def _():
            drain(other)
        load(i + 1, other).start()

    base = i * TOK_TILE
    for r in range(TOK_TILE):
        slot_copy(r, dest_ref[base + r], slot).start()
        slot_copy(r, dest_ref[n_tok + base + r], slot).start()

    @pl.when(i == last)
    def _():
        zbuf[...] = jnp.zeros(zbuf.shape, F32)

        def zero_copy(dst):
            return pltpu.make_async_copy(zbuf, x_hbm.at[pl.ds(dst * tr, tr), :], pad_sem)

        n_pad = pad_ref.shape[0]

        def fill(j, carry):
            for u in range(PAD_UNROLL):
                zero_copy(pad_ref[j * PAD_UNROLL + u]).start()
            return carry

        lax.fori_loop(0, n_pad // PAD_UNROLL, fill, 0)
        drain(slot)

        @pl.when(i >= 1)
        def _():
            drain(other)

        def unfill(j, carry):
            for u in range(PAD_UNROLL):
                zero_copy(0).wait()
            return carry

        lax.fori_loop(0, n_pad // PAD_UNROLL, unfill, 0)


PAD_UNROLL = 8


def _moe_dispatch(h2_tiles, dest, pad_slots, n_slots):
    tr = TOKEN_TILE_ROWS
    n_tok = h2_tiles.shape[0] // tr
    assert pad_slots.shape[0] % PAD_UNROLL == 0
    any_spec = pl.BlockSpec(memory_space=pl.ANY)
    return pl.pallas_call(
        functools.partial(_dispatch_kernel, n_tok=n_tok),
        out_shape=jax.ShapeDtypeStruct((n_slots * tr, LANES), F32),
        grid_spec=pltpu.PrefetchScalarGridSpec(
            num_scalar_prefetch=2,
            grid=(n_tok // TOK_TILE,),
            in_specs=[any_spec],
            out_specs=any_spec,
            scratch_shapes=[pltpu.VMEM((2, TOK_TILE * tr, LANES), F32), pltpu.VMEM((tr, LANES), F32),
                            pltpu.SemaphoreType.DMA((2,)), pltpu.SemaphoreType.DMA((2,)),
                            pltpu.SemaphoreType.DMA(())],
        ),
        compiler_params=_params("arbitrary"),
        name="moe_dispatch",
    )(dest, pad_slots, h2_tiles)


def _moe_kernel(be_ref, nu_ref, x_ref, wg_ref, wu_ref, wd_ref, o_ref):
    del be_ref
    i = pl.program_id(0)

    @pl.when(i < nu_ref[0])
    def _():
        x = _load_token_tiles(x_ref, MOE_ROWS).astype(BF16)
        act = _silu(_mm(x, wg_ref[...].astype(BF16))) * _mm(x, wu_ref[...].astype(BF16))
        _store_token_tiles(o_ref, _mm(act.astype(BF16), wd_ref[...].astype(BF16)))

    @pl.when(i >= nu_ref[0])
    def _():
        o_ref[...] = jnp.zeros(o_ref.shape, F32)


def _moe_experts(x_tiles, blk_exp, n_used, wg, wu, wd, layer):
    n_blk = blk_exp.shape[0]
    blk = pl.BlockSpec((MOE_ROWS * TOKEN_TILE_ROWS, LANES), lambda i, be, nu: (i, 0))
    return pl.pallas_call(
        _moe_kernel,
        out_shape=jax.ShapeDtypeStruct(x_tiles.shape, F32),
        grid_spec=pltpu.PrefetchScalarGridSpec(
            num_scalar_prefetch=2,
            grid=(n_blk,),
            in_specs=[
                blk,
                pl.BlockSpec((None, None, D_MODEL, D_FF_EXP), lambda i, be, nu: (layer, be[i], 0, 0)),
                pl.BlockSpec((None, None, D_MODEL, D_FF_EXP), lambda i, be, nu: (layer, be[i], 0, 0)),
                pl.BlockSpec((None, None, D_FF_EXP, D_MODEL), lambda i, be, nu: (layer, be[i], 0, 0)),
            ],
            out_specs=blk,
        ),
        compiler_params=_params("arbitrary"),
        name="moe_experts",
    )(blk_exp, n_used, x_tiles, wg, wu, wd)


def _final_kernel(dest_ref, x1_ref, mod_ref, gt_ref, lng_ref, lnb_ref, y_hbm, *rest, n_ctx_tiles, n_tok):
    *o_refs, ybuf, sem = rest
    d = D_MODEL
    n = x1_ref.shape[0]
    tr = TOKEN_TILE_ROWS
    i = pl.program_id(0)
    slot = lax.rem(i, 2)

    def fetch_copy(src, k, r, buf):
        return pltpu.make_async_copy(y_hbm.at[pl.ds(src * tr, tr), :],
                                     ybuf.at[buf, k, pl.ds(r * tr, tr), :], sem.at[buf])

    def start_fetch(tile, buf):
        for k in range(2):
            for r in range(n):
                fetch_copy(dest_ref[k * n_tok + tile * n + r], k, r, buf).start()

    @pl.when(i == 0)
    def _():
        start_fetch(0, 0)

    for k in range(2):
        for r in range(n):
            fetch_copy(0, k, r, slot).wait()

    @pl.when(i + 1 < pl.num_programs(0))
    def _():
        start_fetch(i + 1, 1 - slot)

    gt = gt_ref[...]
    ffn = (_load_token_tiles(ybuf, n, lead=(slot, 0)) * gt[:, 0:1]
           + _load_token_tiles(ybuf, n, lead=(slot, 1)) * gt[:, 1:2])
    y = ALPHA * x1_ref[...] + mod_ref[:, 5 * d:6 * d] * ffn
    y = y - jnp.mean(y, axis=1, keepdims=True)
    out = (y * lax.rsqrt(jnp.mean(y * y, axis=1, keepdims=True) + EPS)) * lng_ref[...] + lnb_ref[...]
    if len(o_refs) == 1:
        o_refs[0][...] = out
    else:
        @pl.when(pl.program_id(0) < n_ctx_tiles)
        def _():
            o_refs[0][...] = out

        @pl.when(pl.program_id(0) >= n_ctx_tiles)
        def _():
            o_refs[1][...] = out


def _final(x1, mod_l, y_slots, dest, gates_t, ln_g, ln_b, n_ctx, dec_seq, split=False):
    n_tok = x1.shape[0]
    tm = TOK_TILE
    nt = n_tok // tm
    nct = n_ctx // tm
    grp = functools.partial(_mod_group, tm=tm, n_ctx=n_ctx, dec_seq=dec_seq)
    tile = pl.BlockSpec((tm, D_MODEL), lambda i, dst: (i, 0))
    vec = pl.BlockSpec((1, D_MODEL), lambda i, dst: (0, 0))
    if split:
        out_shape = (jax.ShapeDtypeStruct((n_ctx, D_MODEL), F32), jax.ShapeDtypeStruct((n_tok - n_ctx, D_MODEL), F32))
        out_specs = (pl.BlockSpec((tm, D_MODEL), lambda i, dst: (jnp.minimum(i, nct - 1), 0)),
                     pl.BlockSpec((tm, D_MODEL), lambda i, dst: (jnp.maximum(i - nct, 0), 0)))
    else:
        out_shape = jax.ShapeDtypeStruct((n_tok, D_MODEL), F32)
        out_specs = tile
    return pl.pallas_call(
        functools.partial(_final_kernel, n_ctx_tiles=nct, n_tok=n_tok),
        out_shape=out_shape,
        grid_spec=pltpu.PrefetchScalarGridSpec(
            num_scalar_prefetch=1,
            grid=(nt,),
            in_specs=[tile, pl.BlockSpec((None, 1, ADA_DIM), lambda i, dst: (grp(i), 0, 0)),
                      pl.BlockSpec((tm, 2), lambda i, dst: (i, 0)), vec, vec,
                      pl.BlockSpec(memory_space=pl.ANY)],
            out_specs=out_specs,
            scratch_shapes=[pltpu.VMEM((2, 2, tm * TOKEN_TILE_ROWS, LANES), F32),
                            pltpu.SemaphoreType.DMA((2,))],
        ),
        compiler_params=_params("arbitrary"),
        name="final_norm",
    )(dest, x1, mod_l, gates_t, ln_g, ln_b, y_slots)


def _rope_tables(seq, dim, width):
    nf = dim // 4
    t = jnp.arange(seq)
    pos = jnp.stack([t // GRID_W, t % GRID_W], axis=-1).astype(F32)
    inv = ROPE_BASE ** (-jnp.arange(nf, dtype=F32) / nf)
    ang = pos[:, :, None] * inv
    cos, sin = jnp.cos(ang), jnp.sin(ang)
    zero = jnp.zeros_like(sin)
    c = jnp.stack([cos, cos], axis=2).reshape(seq, dim)
    s_lo = jnp.stack([-sin, zero], axis=2).reshape(seq, dim)
    s_hi = jnp.stack([zero, sin], axis=2).reshape(seq, dim)
    rep = width // dim
    return tuple(jnp.tile(a, (1, rep)) for a in (c, s_lo, s_hi))


def _dispatch_plan(ids, rank, counts, n_tok):
    n_assign = 2 * n_tok
    flat_e = ids.reshape(n_assign)
    onehot = (flat_e[:, None] == jnp.arange(N_EXP, dtype=jnp.int32)[None, :]).astype(jnp.int32)
    counts = counts.reshape(N_EXP).astype(jnp.int32)
    padded = (counts + MOE_ROWS - 1) // MOE_ROWS * MOE_ROWS
    pad_end = jnp.cumsum(padded)
    pad_start = pad_end - padded
    dest = (jnp.sum(onehot * pad_start[None, :], axis=1) + rank.reshape(n_assign)).astype(jnp.int32)
    n_blk = n_assign // MOE_ROWS + N_EXP
    blk_start = jnp.arange(n_blk, dtype=jnp.int32) * MOE_ROWS
    blk_exp = jnp.sum((blk_start[:, None] >= pad_end[None, :]).astype(jnp.int32), axis=1)
    blk_exp = jnp.minimum(blk_exp, N_EXP - 1).astype(jnp.int32)
    n_used = (pad_end[-1] // MOE_ROWS).astype(jnp.int32).reshape(1)
    n_free = n_blk * MOE_ROWS - n_assign
    gap_end = jnp.cumsum(padded - counts)
    j = jnp.arange(n_free, dtype=jnp.int32)
    seg = jnp.sum((j[:, None] >= gap_end[None, :]).astype(jnp.int32), axis=1)
    seg_first_slot = jnp.concatenate([pad_start + counts, pad_end[-1:]])
    seg_first_j = jnp.concatenate([jnp.zeros((1,), jnp.int32), gap_end])
    seg_hot = (seg[:, None] == jnp.arange(N_EXP + 1, dtype=jnp.int32)[None, :]).astype(jnp.int32)
    free_slots = (jnp.sum(seg_hot * (seg_first_slot - seg_first_j)[None, :], axis=1) + j).astype(jnp.int32)
    return dest, free_slots, blk_exp, n_used


def kernel(x_prompt, x_sample, cache_diff_k, cache_diff_v, cache_win_k, cache_win_v, state_ssd, state_ret,
           c, c_ctx, w_ada, b_ada, w_in, diff_lambda, diff_norm_g, win_sink, conv_w, conv_b,
           ssd_A_log, ssd_dt_bias, ssd_D, ssd_norm_g, ret_decay_logit, ret_norm_g, w_branch, w_out,
           ln_g, ln_b, router_w, router_b, moe_w_gate, moe_w_up, moe_w_down):
    batch, seq, d = x_prompt.shape
    dec_batch, dec_seq, _ = x_sample.shape
    past = cache_diff_k.shape[2]
    n_ctx, n_lat = batch * seq, dec_batch * dec_seq
    n_tok = n_ctx + n_lat
    assert d == D_MODEL and n_ctx % dec_seq == 0 and seq % CHUNK == 0 and dec_seq % CHUNK == 0

    x = jnp.concatenate([x_prompt.reshape(n_ctx, d), x_sample.reshape(n_lat, d)], axis=0)

    n_mod = 1 + dec_batch
    n_mod_pad = -(-n_mod // 8) * 8
    cvec = jnp.concatenate([c_ctx[None, :], c, jnp.zeros((n_mod_pad - n_mod, d), F32)], axis=0)
    mod = _ada(cvec, w_ada, b_ada)

    rope_a = _rope_tables(dec_seq, A_QK, 256)
    rope_b = _rope_tables(dec_seq, B_DIM, 256)
    rw_t = router_w.T
    rb_col = router_b.reshape(N_EXP, 1)

    ctx_out = {k: [] for k in ('diff_k', 'diff_v', 'win_k', 'win_v', 'ssd', 'ret')}
    for l in range(DEPTH):
        mod_l = mod[l, :n_mod].reshape(n_mod, 1, ADA_DIM)
        wl = w_in[l]
        w_small = jnp.concatenate(
            [wl[:, :CDT_OFF], wl[:, CDT_OFF + 8:GATE_OFF], wl[:, CDT_OFF:CDT_OFF + 8],
             jnp.zeros((d, N_SMALL - GATE_OFF), F32)], axis=1).astype(BF16)
        w_gl = wl[:, GATE_OFF:].astype(BF16)
        proj = _inproj(x, mod_l, w_small, n_ctx, dec_seq)

        lam_init = 0.8 - 0.6 * math.exp(-0.3 * l)
        lv = diff_lambda[l]
        lam = jnp.exp(jnp.sum(lv[0] * lv[1])) - jnp.exp(jnp.sum(lv[2] * lv[3])) + lam_init
        diff_scal = jnp.stack([lam, jnp.asarray(1.0 - lam_init, F32)]).astype(F32)
        g_a = diff_norm_g[l].reshape(1, A_V)
        sink = win_sink[l]
        dtb = jnp.zeros((1, 128), F32).at[0, :8].set(ssd_dt_bias[l].reshape(8))
        acoef = jnp.zeros((1, 128), F32).at[0, :8].set(-jnp.exp(ssd_A_log[l]).reshape(8))
        dvec = jnp.repeat(ssd_D[l], C_P).reshape(1, 256)
        g_c = ssd_norm_g[l].reshape(1, 256)
        log_g = jax.nn.log_sigmoid(ret_decay_logit[l]).reshape(8)
        g_d = ret_norm_g[l].reshape(1, 256)
        cw = conv_w[l]
        cb = conv_b[l].reshape(1, 512)

        oa_c = _diff_attn(proj, diff_scal, g_a, 0, batch, seq, seq)
        ob_c = _win_attn(proj, sink, 0, batch, seq, seq)
        yc_c, st_c = _ssd(proj, cw, cb, dtb, acoef, dvec, g_c, 0, batch, seq)
        od_c, rt_c = _ret(proj, log_g, g_d, 0, batch, seq)
        cache_a = (cache_diff_k[:, l].reshape(dec_batch, past, 256), cache_diff_v[:, l].reshape(dec_batch, past, 256))
        cache_b = (cache_win_k[:, l].reshape(dec_batch, past, 128), cache_win_v[:, l].reshape(dec_batch, past, 128))
        oa_l = _diff_attn(proj, diff_scal, g_a, n_ctx, dec_batch, dec_seq, DIFF_TQ, cache=cache_a, rope=rope_a)
        ob_l = _win_attn(proj, sink, n_ctx, dec_batch, dec_seq, BLOCK, cache=cache_b, rope=rope_b)
        yc_l, _ = _ssd(proj, cw, cb, dtb, acoef, dvec, g_c, n_ctx, dec_batch, dec_seq,
                       h0=_state_to_blocks(state_ssd[:, l]))
        od_l, _ = _ret(proj, log_g, g_d, n_ctx, dec_batch, dec_seq,
                       h0=_state_to_blocks(state_ret[:, l]))

        x1, h2, ids, gates, rank, counts = _merge(
            x, mod_l, ((oa_c, oa_l), (ob_c, ob_l), (yc_c, yc_l), (od_c, od_l)), w_gl, w_branch[l].reshape(N_BRANCH * BRANCH, d).astype(BF16),
            w_out[l].astype(BF16), ln_g[l, 0].reshape(1, d), ln_b[l, 0].reshape(1, d), rw_t, rb_col,
            n_ctx, dec_seq)

        dest, free_slots, blk_exp, n_used = _dispatch_plan(ids, rank, counts, n_tok)
        x_slots = _moe_dispatch(h2, dest, free_slots, blk_exp.shape[0] * MOE_ROWS)
        y_slots = _moe_experts(x_slots, blk_exp, n_used, moe_w_gate, moe_w_up, moe_w_down, l)
        x = _final(x1, mod_l, y_slots, dest, gates.T,
                   ln_g[l, 1].reshape(1, d), ln_b[l, 1].reshape(1, d), n_ctx, dec_seq,
                   split=(l == DEPTH - 1))

        pc = proj[:n_ctx]
        ctx_out['diff_k'].append(pc[:, 256:512].reshape(batch, seq, A_HEADS, 2, A_QK))
        ctx_out['diff_v'].append(pc[:, 512:768].reshape(batch, seq, A_HEADS, A_V))
        ctx_out['win_k'].append(pc[:, 1024:1152].reshape(batch, seq, B_KV, B_DIM))
        ctx_out['win_v'].append(pc[:, 1152:1280].reshape(batch, seq, B_KV, B_DIM))
        ctx_out['ssd'].append(_blocks_to_state(st_c, C_HEADS, C_P, C_N))
        ctx_out['ret'].append(_blocks_to_state(rt_c, D_HEADS, D_V, D_K))

    y_prompt = x[0].reshape(batch, seq, d)
    y_sample = x[1].reshape(dec_batch, dec_seq, d)
    stk = lambda k: jnp.stack(ctx_out[k], axis=1)
    return (y_prompt, y_sample, stk('diff_k'), stk('diff_v'), stk('win_k'), stk('win_v'), stk('ssd'), stk('ret'))
```
